```python
import jax, jax.numpy as jnp
from jax import lax
import numpy as np

D_MODEL = 1024
BATCH = 8
SEQ = 8192
DEPTH = 2

CHUNK = 64
N_MEM = 256
CONV_WIDTH = 4
D_LRU = D_MODEL
LRU_BLOCKS = 8
LRU_BLOCK = D_LRU // LRU_BLOCKS
LRU_C = 8.0
D_SSD = 2 * D_MODEL
SSD_HEAD_DIM = 64
SSD_HEADS = D_SSD // SSD_HEAD_DIM
SSD_GROUPS = 4
SSD_HEADS_PER_GROUP = SSD_HEADS // SSD_GROUPS
SSD_STATE = 128
D_BC = SSD_GROUPS * SSD_STATE
D_XBC = D_SSD + 2 * D_BC
XA_HEADS = 4
XA_HEAD_DIM = 256
D_XA = XA_HEADS * XA_HEAD_DIM
N_BRANCH = 3
D_FF = ((8 * D_MODEL // 3 + 255) // 256) * 256
ALPHA = (2 * DEPTH) ** 0.25
BETA = (8 * DEPTH) ** -0.25
EPS = 1e-5

_SPLITS = (D_LRU, D_LRU, D_SSD, D_XBC, SSD_HEADS, D_XA, N_BRANCH * D_MODEL)
N_IN = sum(_SPLITS)
_OFFSETS = tuple(sum(_SPLITS[:i + 1]) for i in range(len(_SPLITS) - 1))

kernel_name = 'hybrid_rglru_ssd_memxattn_deepnorm'


def layer_norm(x, g, b):
    xf = x.astype(jnp.float32)
    mu = jnp.mean(xf, axis=-1, keepdims=True)
    var = jnp.mean(jnp.square(xf - mu), axis=-1, keepdims=True)
    return ((xf - mu) * lax.rsqrt(var + EPS) * g + b).astype(x.dtype)


def causal_depthwise_conv(x, w, b):
    c = x.shape[-1]
    y = lax.conv_general_dilated(
        x, w[:, None, :].astype(x.dtype), window_strides=(1,),
        padding=[(CONV_WIDTH - 1, 0)], dimension_numbers=('NWC', 'WIO', 'NWC'),
        feature_group_count=c)
    return y + b


def rg_lru(x, w_a, b_a, w_i, b_i, lam):
    xf = x.astype(jnp.float32)
    xb = xf.reshape(*xf.shape[:-1], LRU_BLOCKS, LRU_BLOCK)
    r = jax.nn.sigmoid(jnp.einsum('bsnk,nkj->bsnj', xb, w_a.astype(jnp.float32)).reshape(xf.shape) + b_a)
    i = jax.nn.sigmoid(jnp.einsum('bsnk,nkj->bsnj', xb, w_i.astype(jnp.float32)).reshape(xf.shape) + b_i)
    log_a = -LRU_C * r * jax.nn.softplus(-lam.astype(jnp.float32))
    a = jnp.exp(log_a)
    u = jnp.sqrt(-jnp.expm1(2.0 * log_a)) * (i * xf)

    def combine(lhs, rhs):
        a1, b1 = lhs
        a2, b2 = rhs
        return a1 * a2, a2 * b1 + b2

    _, h = lax.associative_scan(combine, (a, u), axis=1)
    return h


def ssd_chunked(xs, dt, a, bm, cm):
    bsz, s = xs.shape[:2]
    nc = s // CHUNK
    g, k, p, n = SSD_GROUPS, SSD_HEADS_PER_GROUP, SSD_HEAD_DIM, SSD_STATE
    x_c = (xs * dt[..., None]).reshape(bsz, nc, CHUNK, g, k, p)
    da = (dt * a).reshape(bsz, nc, CHUNK, g, k)
    b_c = bm.reshape(bsz, nc, CHUNK, g, n)
    c_c = cm.reshape(bsz, nc, CHUNK, g, n)
    cs = jnp.cumsum(da, axis=2)
    idx = jnp.arange(CHUNK)
    causal = (idx[:, None] >= idx[None, :])[:, :, None, None]
    seg = cs[:, :, :, None] - cs[:, :, None, :]
    decay = jnp.exp(jnp.where(causal, seg, -jnp.inf))
    cb = jnp.einsum('bclgn,bcsgn->bclsg', c_c, b_c)
    y_diag = jnp.einsum('bclsgk,bcsgkp->bclgkp', cb[..., None] * decay, x_c)
    decay_end = jnp.exp(cs[:, :, -1:] - cs)
    states = jnp.einsum('bclgn,bclgkp->bcgkpn', b_c, x_c * decay_end[..., None])
    chunk_decay = jnp.exp(cs[:, :, -1])

    def step(h, inp):
        st, dec = inp
        return h * dec[..., None, None] + st, h

    h0 = jnp.zeros((bsz, g, k, p, n), jnp.float32)
    _, prev = lax.scan(step, h0, (jnp.moveaxis(states, 1, 0), jnp.moveaxis(chunk_decay, 1, 0)))
    prev = jnp.moveaxis(prev, 0, 1)
    y_off = jnp.einsum('bclgn,bcgkpn->bclgkp', c_c, prev) * jnp.exp(cs)[..., None]
    return (y_diag + y_off).reshape(bsz, s, SSD_HEADS, p)


def ssd_branch(z, xbc, dt_raw, conv_w, conv_b, dt_bias, a_log, d_skip, norm_w):
    bsz, s = z.shape[:2]
    xbc = jax.nn.silu(causal_depthwise_conv(xbc, conv_w, conv_b)).astype(jnp.float32)
    xs, bm, cm = jnp.split(xbc, [D_SSD, D_SSD + D_BC], axis=-1)
    xs = xs.reshape(bsz, s, SSD_HEADS, SSD_HEAD_DIM)
    bm = bm.reshape(bsz, s, SSD_GROUPS, SSD_STATE)
    cm = cm.reshape(bsz, s, SSD_GROUPS, SSD_STATE)
    dt = jax.nn.softplus(dt_raw.astype(jnp.float32) + dt_bias)
    a = -jnp.exp(a_log.astype(jnp.float32))
    y = ssd_chunked(xs, dt, a, bm, cm) + d_skip[:, None] * xs
    y = y.reshape(bsz, s, D_SSD) * jax.nn.silu(z.astype(jnp.float32))
    yg = y.reshape(bsz, s, SSD_GROUPS, D_SSD // SSD_GROUPS)
    yg = yg * lax.rsqrt(jnp.mean(jnp.square(yg), axis=-1, keepdims=True) + EPS)
    return (yg.reshape(bsz, s, D_SSD) * norm_w).astype(z.dtype)


def memory_cross_attention(q, mem, w_kv):
    bsz, s = q.shape[:2]
    m = mem.shape[1]
    k, v = jnp.split(mem @ w_kv, 2, axis=-1)
    q = q.reshape(bsz, s, XA_HEADS, XA_HEAD_DIM)
    k = k.reshape(bsz, m, XA_HEADS, XA_HEAD_DIM)
    v = v.reshape(bsz, m, XA_HEADS, XA_HEAD_DIM)
    scores = jnp.einsum('bshd,bmhd->bhsm', q, k).astype(jnp.float32) * (XA_HEAD_DIM ** -0.5)
    probs = jax.nn.softmax(scores, axis=-1).astype(v.dtype)
    return jnp.einsum('bhsm,bmhd->bshd', probs, v).reshape(bsz, s, D_XA)


def hybrid_mixer(x, mem, w_in, b_gate, lru_conv_w, lru_conv_b, lru_w_a, lru_b_a, lru_w_i, lru_b_i,
                 lru_lambda, ssd_conv_w, ssd_conv_b, ssd_dt_bias, ssd_a_log, ssd_d, ssd_norm_w,
                 mem_w_kv, w_br_lru, w_br_ssd, w_br_xa, w_out):
    bsz, s = x.shape[:2]
    proj = x @ w_in
    lru_x, lru_gate, ssd_z, ssd_xbc, ssd_dt, xa_q, gate_logits = jnp.split(proj, _OFFSETS, axis=-1)
    h = rg_lru(causal_depthwise_conv(lru_x, lru_conv_w, lru_conv_b), lru_w_a, lru_b_a, lru_w_i, lru_b_i, lru_lambda)
    y_lru = (jax.nn.gelu(lru_gate.astype(jnp.float32)) * h).astype(x.dtype)
    y_ssd = ssd_branch(ssd_z, ssd_xbc, ssd_dt, ssd_conv_w, ssd_conv_b, ssd_dt_bias, ssd_a_log, ssd_d, ssd_norm_w)
    y_xa = memory_cross_attention(xa_q, mem, mem_w_kv)
    gates = jax.nn.sigmoid(gate_logits.reshape(bsz, s, N_BRANCH, D_MODEL) + b_gate)
    merged = (gates[:, :, 0] * (y_lru @ w_br_lru)
              + gates[:, :, 1] * (y_ssd @ w_br_ssd)
              + gates[:, :, 2] * (y_xa @ w_br_xa))
    return merged @ w_out


def swiglu(x, w_in, w_down):
    gate, up = jnp.split(x @ w_in, 2, axis=-1)
    return (jax.nn.silu(gate) * up) @ w_down


def _fwd_setup_inputs(seed: int = 0) -> dict:
    key = jax.random.key(seed)
    ks = jax.random.split(key, 32)

    def nrm(k, shape, scale):
        return jax.random.normal(k, shape, jnp.float32) * scale

    a0 = jax.random.uniform(ks[10], (DEPTH, D_LRU), jnp.float32, 0.9, 0.999)
    root = a0 ** (1.0 / LRU_C)
    lru_lambda = jnp.log(root) - jnp.log1p(-root)
    dt0 = jnp.exp(jax.random.uniform(ks[13], (DEPTH, SSD_HEADS), jnp.float32, np.log(0.001), np.log(0.1)))
    ssd_dt_bias = dt0 + jnp.log(-jnp.expm1(-dt0))
    ssd_a_log = jnp.log(jax.random.uniform(ks[14], (DEPTH, SSD_HEADS), jnp.float32, 1.0, 16.0))
    return {
        'x': nrm(ks[0], (BATCH, SEQ, D_MODEL), 1.0),
        'mem': nrm(ks[1], (BATCH, N_MEM, D_MODEL), 1.0),
        'w_in': nrm(ks[2], (DEPTH, D_MODEL, N_IN), D_MODEL ** -0.5),
        'b_gate': nrm(ks[3], (DEPTH, N_BRANCH, D_MODEL), 0.1),
        'lru_conv_w': nrm(ks[4], (DEPTH, CONV_WIDTH, D_LRU), CONV_WIDTH ** -0.5),
        'lru_conv_b': nrm(ks[5], (DEPTH, D_LRU), 0.02),
        'lru_w_a': nrm(ks[6], (DEPTH, LRU_BLOCKS, LRU_BLOCK, LRU_BLOCK), LRU_BLOCK ** -0.5),
        'lru_b_a': nrm(ks[7], (DEPTH, D_LRU), 0.02),
        'lru_w_i': nrm(ks[8], (DEPTH, LRU_BLOCKS, LRU_BLOCK, LRU_BLOCK), LRU_BLOCK ** -0.5),
        'lru_b_i': nrm(ks[9], (DEPTH, D_LRU), 0.02),
        'lru_lambda': lru_lambda,
        'ssd_conv_w': nrm(ks[11], (DEPTH, CONV_WIDTH, D_XBC), CONV_WIDTH ** -0.5),
        'ssd_conv_b': nrm(ks[12], (DEPTH, D_XBC), 0.02),
        'ssd_dt_bias': ssd_dt_bias,
        'ssd_a_log': ssd_a_log,
        'ssd_d': 1.0 + nrm(ks[15], (DEPTH, SSD_HEADS), 0.02),
        'ssd_norm_w': 1.0 + nrm(ks[16], (DEPTH, D_SSD), 0.02),
        'mem_w_kv': nrm(ks[17], (DEPTH, D_MODEL, 2 * D_XA), D_MODEL ** -0.5),
        'w_br_lru': nrm(ks[18], (DEPTH, D_LRU, D_MODEL), D_LRU ** -0.5),
        'w_br_ssd': nrm(ks[19], (DEPTH, D_SSD, D_MODEL), D_SSD ** -0.5),
        'w_br_xa': nrm(ks[20], (DEPTH, D_XA, D_MODEL), D_XA ** -0.5),
        'w_out': nrm(ks[21], (DEPTH, D_MODEL, D_MODEL), BETA * D_MODEL ** -0.5),
        'ln1_g': 1.0 + nrm(ks[22], (DEPTH, D_MODEL), 0.02),
        'ln1_b': nrm(ks[23], (DEPTH, D_MODEL), 0.02),
        'ffn_w_in': nrm(ks[24], (DEPTH, D_MODEL, 2 * D_FF), D_MODEL ** -0.5),
        'ffn_w_down': nrm(ks[25], (DEPTH, D_FF, D_MODEL), BETA * D_FF ** -0.5),
        'ln2_g': 1.0 + nrm(ks[26], (DEPTH, D_MODEL), 0.02),
        'ln2_b': nrm(ks[27], (DEPTH, D_MODEL), 0.02),
    }


def _fwd_reference(x, mem, w_in, b_gate, lru_conv_w, lru_conv_b, lru_w_a, lru_b_a, lru_w_i, lru_b_i,
              lru_lambda, ssd_conv_w, ssd_conv_b, ssd_dt_bias, ssd_a_log, ssd_d, ssd_norm_w,
              mem_w_kv, w_br_lru, w_br_ssd, w_br_xa, w_out, ln1_g, ln1_b, ffn_w_in, ffn_w_down,
              ln2_g, ln2_b):
    for l in range(DEPTH):
        mix = hybrid_mixer(x, mem, w_in[l], b_gate[l], lru_conv_w[l], lru_conv_b[l], lru_w_a[l], lru_b_a[l],
                           lru_w_i[l], lru_b_i[l], lru_lambda[l], ssd_conv_w[l], ssd_conv_b[l], ssd_dt_bias[l],
                           ssd_a_log[l], ssd_d[l], ssd_norm_w[l], mem_w_kv[l], w_br_lru[l], w_br_ssd[l],
                           w_br_xa[l], w_out[l])
        x = layer_norm(ALPHA * x + mix, ln1_g[l], ln1_b[l])
        x = layer_norm(ALPHA * x + swiglu(x, ffn_w_in[l], ffn_w_down[l]), ln2_g[l], ln2_b[l])
    return x


import jax as _jax
import jax.numpy as _jnp

TWIN_FORMAT = 'train_step'
FWD_PARAMS = ['x', 'mem', 'w_in', 'b_gate', 'lru_conv_w', 'lru_conv_b', 'lru_w_a', 'lru_b_a', 'lru_w_i', 'lru_b_i', 'lru_lambda', 'ssd_conv_w', 'ssd_conv_b', 'ssd_dt_bias', 'ssd_a_log', 'ssd_d', 'ssd_norm_w', 'mem_w_kv', 'w_br_lru', 'w_br_ssd', 'w_br_xa', 'w_out', 'ln1_g', 'ln1_b', 'ffn_w_in', 'ffn_w_down', 'ln2_g', 'ln2_b']
TWIN_WEIGHTS = ['w_in', 'b_gate', 'lru_conv_w', 'lru_conv_b', 'lru_w_a', 'lru_b_a', 'lru_w_i', 'lru_b_i', 'lru_lambda', 'ssd_conv_w', 'ssd_conv_b', 'ssd_dt_bias', 'ssd_a_log', 'ssd_d', 'ssd_norm_w', 'mem_w_kv', 'w_br_lru', 'w_br_ssd', 'w_br_xa', 'w_out', 'ln1_g', 'ln1_b', 'ffn_w_in', 'ffn_w_down', 'ln2_g', 'ln2_b']
TWIN_DIFF_INPUT = 'x'
TWIN_INPUTS = ['x', 'mem', 'w_in', 'b_gate', 'lru_conv_w', 'lru_conv_b', 'lru_w_a', 'lru_b_a', 'lru_w_i', 'lru_b_i', 'lru_lambda', 'ssd_conv_w', 'ssd_conv_b', 'ssd_dt_bias', 'ssd_a_log', 'ssd_d', 'ssd_norm_w', 'mem_w_kv', 'w_br_lru', 'w_br_ssd', 'w_br_xa', 'w_out', 'ln1_g', 'ln1_b', 'ffn_w_in', 'ffn_w_down', 'ln2_g', 'ln2_b', 'loss_target', 'm_w_in', 'm_b_gate', 'm_lru_conv_w', 'm_lru_conv_b', 'm_lru_w_a', 'm_lru_b_a', 'm_lru_w_i', 'm_lru_b_i', 'm_lru_lambda', 'm_ssd_conv_w', 'm_ssd_conv_b', 'm_ssd_dt_bias', 'm_ssd_a_log', 'm_ssd_d', 'm_ssd_norm_w', 'm_mem_w_kv', 'm_w_br_lru', 'm_w_br_ssd', 'm_w_br_xa', 'm_w_out', 'm_ln1_g', 'm_ln1_b', 'm_ffn_w_in', 'm_ffn_w_down', 'm_ln2_g', 'm_ln2_b', 'v_w_in', 'v_b_gate', 'v_lru_conv_w', 'v_lru_conv_b', 'v_lru_w_a', 'v_lru_b_a', 'v_lru_w_i', 'v_lru_b_i', 'v_lru_lambda', 'v_ssd_conv_w', 'v_ssd_conv_b', 'v_ssd_dt_bias', 'v_ssd_a_log', 'v_ssd_d', 'v_ssd_norm_w', 'v_mem_w_kv', 'v_w_br_lru', 'v_w_br_ssd', 'v_w_br_xa', 'v_w_out', 'v_ln1_g', 'v_ln1_b', 'v_ffn_w_in', 'v_ffn_w_down', 'v_ln2_g', 'v_ln2_b']
TWIN_OUTPUTS = ['loss', 'grad_x', 'grad_w_in', 'grad_b_gate', 'grad_lru_conv_w', 'grad_lru_conv_b', 'grad_lru_w_a', 'grad_lru_b_a', 'grad_lru_w_i', 'grad_lru_b_i', 'grad_lru_lambda', 'grad_ssd_conv_w', 'grad_ssd_conv_b', 'grad_ssd_dt_bias', 'grad_ssd_a_log', 'grad_ssd_d', 'grad_ssd_norm_w', 'grad_mem_w_kv', 'grad_w_br_lru', 'grad_w_br_ssd', 'grad_w_br_xa', 'grad_w_out', 'grad_ln1_g', 'grad_ln1_b', 'grad_ffn_w_in', 'grad_ffn_w_down', 'grad_ln2_g', 'grad_ln2_b', 'delta_w_in', 'delta_b_gate', 'delta_lru_conv_w', 'delta_lru_conv_b', 'delta_lru_w_a', 'delta_lru_b_a', 'delta_lru_w_i', 'delta_lru_b_i', 'delta_lru_lambda', 'delta_ssd_conv_w', 'delta_ssd_conv_b', 'delta_ssd_dt_bias', 'delta_ssd_a_log', 'delta_ssd_d', 'delta_ssd_norm_w', 'delta_mem_w_kv', 'delta_w_br_lru', 'delta_w_br_ssd', 'delta_w_br_xa', 'delta_w_out', 'delta_ln1_g', 'delta_ln1_b', 'delta_ffn_w_in', 'delta_ffn_w_down', 'delta_ln2_g', 'delta_ln2_b', 'new_m_w_in', 'new_m_b_gate', 'new_m_lru_conv_w', 'new_m_lru_conv_b', 'new_m_lru_w_a', 'new_m_lru_b_a', 'new_m_lru_w_i', 'new_m_lru_b_i', 'new_m_lru_lambda', 'new_m_ssd_conv_w', 'new_m_ssd_conv_b', 'new_m_ssd_dt_bias', 'new_m_ssd_a_log', 'new_m_ssd_d', 'new_m_ssd_norm_w', 'new_m_mem_w_kv', 'new_m_w_br_lru', 'new_m_w_br_ssd', 'new_m_w_br_xa', 'new_m_w_out', 'new_m_ln1_g', 'new_m_ln1_b', 'new_m_ffn_w_in', 'new_m_ffn_w_down', 'new_m_ln2_g', 'new_m_ln2_b', 'new_v_w_in', 'new_v_b_gate', 'new_v_lru_conv_w', 'new_v_lru_conv_b', 'new_v_lru_w_a', 'new_v_lru_b_a', 'new_v_lru_w_i', 'new_v_lru_b_i', 'new_v_lru_lambda', 'new_v_ssd_conv_w', 'new_v_ssd_conv_b', 'new_v_ssd_dt_bias', 'new_v_ssd_a_log', 'new_v_ssd_d', 'new_v_ssd_norm_w', 'new_v_mem_w_kv', 'new_v_w_br_lru', 'new_v_w_br_ssd', 'new_v_w_br_xa', 'new_v_w_out', 'new_v_ln1_g', 'new_v_ln1_b', 'new_v_ffn_w_in', 'new_v_ffn_w_down', 'new_v_ln2_g', 'new_v_ln2_b']
TWIN_LEAF_KINDS = {'loss': 'loss', 'grad_x': 'grad_x', 'grad_w_in': 'grad_w', 'grad_b_gate': 'grad_w', 'grad_lru_conv_w': 'grad_w', 'grad_lru_conv_b': 'grad_w', 'grad_lru_w_a': 'grad_w', 'grad_lru_b_a': 'grad_w', 'grad_lru_w_i': 'grad_w', 'grad_lru_b_i': 'grad_w', 'grad_lru_lambda': 'grad_w', 'grad_ssd_conv_w': 'grad_w', 'grad_ssd_conv_b': 'grad_w', 'grad_ssd_dt_bias': 'grad_w', 'grad_ssd_a_log': 'grad_w', 'grad_ssd_d': 'grad_w', 'grad_ssd_norm_w': 'grad_w', 'grad_mem_w_kv': 'grad_w', 'grad_w_br_lru': 'grad_w', 'grad_w_br_ssd': 'grad_w', 'grad_w_br_xa': 'grad_w', 'grad_w_out': 'grad_w', 'grad_ln1_g': 'grad_w', 'grad_ln1_b': 'grad_w', 'grad_ffn_w_in': 'grad_w', 'grad_ffn_w_down': 'grad_w', 'grad_ln2_g': 'grad_w', 'grad_ln2_b': 'grad_w', 'delta_w_in': 'delta_w', 'delta_b_gate': 'delta_w', 'delta_lru_conv_w': 'delta_w', 'delta_lru_conv_b': 'delta_w', 'delta_lru_w_a': 'delta_w', 'delta_lru_b_a': 'delta_w', 'delta_lru_w_i': 'delta_w', 'delta_lru_b_i': 'delta_w', 'delta_lru_lambda': 'delta_w', 'delta_ssd_conv_w': 'delta_w', 'delta_ssd_conv_b': 'delta_w', 'delta_ssd_dt_bias': 'delta_w', 'delta_ssd_a_log': 'delta_w', 'delta_ssd_d': 'delta_w', 'delta_ssd_norm_w': 'delta_w', 'delta_mem_w_kv': 'delta_w', 'delta_w_br_lru': 'delta_w', 'delta_w_br_ssd': 'delta_w', 'delta_w_br_xa': 'delta_w', 'delta_w_out': 'delta_w', 'delta_ln1_g': 'delta_w', 'delta_ln1_b': 'delta_w', 'delta_ffn_w_in': 'delta_w', 'delta_ffn_w_down': 'delta_w', 'delta_ln2_g': 'delta_w', 'delta_ln2_b': 'delta_w', 'new_m_w_in': 'new_m', 'new_m_b_gate': 'new_m', 'new_m_lru_conv_w': 'new_m', 'new_m_lru_conv_b': 'new_m', 'new_m_lru_w_a': 'new_m', 'new_m_lru_b_a': 'new_m', 'new_m_lru_w_i': 'new_m', 'new_m_lru_b_i': 'new_m', 'new_m_lru_lambda': 'new_m', 'new_m_ssd_conv_w': 'new_m', 'new_m_ssd_conv_b': 'new_m', 'new_m_ssd_dt_bias': 'new_m', 'new_m_ssd_a_log': 'new_m', 'new_m_ssd_d': 'new_m', 'new_m_ssd_norm_w': 'new_m', 'new_m_mem_w_kv': 'new_m', 'new_m_w_br_lru': 'new_m', 'new_m_w_br_ssd': 'new_m', 'new_m_w_br_xa': 'new_m', 'new_m_w_out': 'new_m', 'new_m_ln1_g': 'new_m', 'new_m_ln1_b': 'new_m', 'new_m_ffn_w_in': 'new_m', 'new_m_ffn_w_down': 'new_m', 'new_m_ln2_g': 'new_m', 'new_m_ln2_b': 'new_m', 'new_v_w_in': 'new_v', 'new_v_b_gate': 'new_v', 'new_v_lru_conv_w': 'new_v', 'new_v_lru_conv_b': 'new_v', 'new_v_lru_w_a': 'new_v', 'new_v_lru_b_a': 'new_v', 'new_v_lru_w_i': 'new_v', 'new_v_lru_b_i': 'new_v', 'new_v_lru_lambda': 'new_v', 'new_v_ssd_conv_w': 'new_v', 'new_v_ssd_conv_b': 'new_v', 'new_v_ssd_dt_bias': 'new_v', 'new_v_ssd_a_log': 'new_v', 'new_v_ssd_d': 'new_v', 'new_v_ssd_norm_w': 'new_v', 'new_v_mem_w_kv': 'new_v', 'new_v_w_br_lru': 'new_v', 'new_v_w_br_ssd': 'new_v', 'new_v_w_br_xa': 'new_v', 'new_v_w_out': 'new_v', 'new_v_ln1_g': 'new_v', 'new_v_ln1_b': 'new_v', 'new_v_ffn_w_in': 'new_v', 'new_v_ffn_w_down': 'new_v', 'new_v_ln2_g': 'new_v', 'new_v_ln2_b': 'new_v'}


def _forward(args):
    return _fwd_reference(*[args[k] for k in FWD_PARAMS])


def _output_shape():
    def fwd():
        inp = _fwd_setup_inputs(0)
        return _fwd_reference(*[inp[k] for k in FWD_PARAMS])
    out = _jax.eval_shape(fwd)
    return out.shape, out.dtype

N_MICROBATCH = 1
ADAM_LR = 0.001
ADAM_B1 = 0.9
ADAM_B2 = 0.999
ADAM_EPS = 1e-08
ADAM_WD = 0.01
ADAM_STEP = 10
PER_EXAMPLE_BATCH_AXIS = {'x': 0, 'mem': 0, 'loss_target': 0}
SHARED_INPUTS = []
_WEIGHT_DTYPES = {'w_in': _jnp.float32, 'b_gate': _jnp.float32, 'lru_conv_w': _jnp.float32, 'lru_conv_b': _jnp.float32, 'lru_w_a': _jnp.float32, 'lru_b_a': _jnp.float32, 'lru_w_i': _jnp.float32, 'lru_b_i': _jnp.float32, 'lru_lambda': _jnp.float32, 'ssd_conv_w': _jnp.float32, 'ssd_conv_b': _jnp.float32, 'ssd_dt_bias': _jnp.float32, 'ssd_a_log': _jnp.float32, 'ssd_d': _jnp.float32, 'ssd_norm_w': _jnp.float32, 'mem_w_kv': _jnp.float32, 'w_br_lru': _jnp.float32, 'w_br_ssd': _jnp.float32, 'w_br_xa': _jnp.float32, 'w_out': _jnp.float32, 'ln1_g': _jnp.float32, 'ln1_b': _jnp.float32, 'ffn_w_in': _jnp.float32, 'ffn_w_down': _jnp.float32, 'ln2_g': _jnp.float32, 'ln2_b': _jnp.float32}
MOMENT_SCALE = {'w_in': 2.545278e-02, 'b_gate': 1.372102e-02, 'lru_conv_w': 2.525628e-02, 'lru_conv_b': 3.654265e-01, 'lru_w_a': 9.857326e-03, 'lru_b_a': 7.447642e-03, 'lru_w_i': 1.809920e-02, 'lru_b_i': 8.872850e-03, 'lru_lambda': 1.309867e-02, 'ssd_conv_w': 3.107587e-02, 'ssd_conv_b': 4.892152e-02, 'ssd_dt_bias': 9.456714e-02, 'ssd_a_log': 2.405317e-01, 'ssd_d': 2.017140e-01, 'ssd_norm_w': 4.037457e-02, 'mem_w_kv': 5.604815e-03, 'w_br_lru': 2.770538e-02, 'w_br_ssd': 5.399146e-02, 'w_br_xa': 6.023742e-03, 'w_out': 1.204904e-01, 'ln1_g': 1.900791e+00, 'ln1_b': 9.890052e-01, 'ffn_w_in': 3.260586e-02, 'ffn_w_down': 1.065131e-01, 'ln2_g': 4.535706e+01, 'ln2_b': 2.626337e+00}


def _to_microbatches(a, axis):
    t = _jnp.moveaxis(a, axis, 0)
    t = t.reshape((N_MICROBATCH, t.shape[0] // N_MICROBATCH) + t.shape[1:])
    return _jnp.moveaxis(t, 1, axis + 1)


def setup_inputs(seed: int = 0) -> dict:
    inp = _fwd_setup_inputs(seed)
    key = _jax.random.fold_in(_jax.random.key(seed), 7919)
    shape, _ = _output_shape()
    out = dict(inp)
    out["loss_target"] = _jax.random.normal(_jax.random.fold_in(key, 0), shape, _jnp.float32)
    for i, name in enumerate(TWIN_WEIGHTS):
        w = inp[name].astype(_jnp.float32)
        if MOMENT_SCALE is None:
            s = _jnp.sqrt(_jnp.mean(_jnp.square(w)) + 1e-30)
        else:
            s = MOMENT_SCALE[name]
        km, kv = _jax.random.split(_jax.random.fold_in(key, i + 1))
        out[name] = w
        out["m_" + name] = s * _jax.random.normal(km, w.shape, _jnp.float32)
        out["v_" + name] = (s * s) * _jax.random.uniform(kv, w.shape, _jnp.float32, 0.5, 1.5)
    if N_MICROBATCH > 1:
        for name, axis in PER_EXAMPLE_BATCH_AXIS.items():
            out[name] = _to_microbatches(out[name], axis)
    return {'x': out['x'], 'mem': out['mem'], 'w_in': out['w_in'], 'b_gate': out['b_gate'], 'lru_conv_w': out['lru_conv_w'], 'lru_conv_b': out['lru_conv_b'], 'lru_w_a': out['lru_w_a'], 'lru_b_a': out['lru_b_a'], 'lru_w_i': out['lru_w_i'], 'lru_b_i': out['lru_b_i'], 'lru_lambda': out['lru_lambda'], 'ssd_conv_w': out['ssd_conv_w'], 'ssd_conv_b': out['ssd_conv_b'], 'ssd_dt_bias': out['ssd_dt_bias'], 'ssd_a_log': out['ssd_a_log'], 'ssd_d': out['ssd_d'], 'ssd_norm_w': out['ssd_norm_w'], 'mem_w_kv': out['mem_w_kv'], 'w_br_lru': out['w_br_lru'], 'w_br_ssd': out['w_br_ssd'], 'w_br_xa': out['w_br_xa'], 'w_out': out['w_out'], 'ln1_g': out['ln1_g'], 'ln1_b': out['ln1_b'], 'ffn_w_in': out['ffn_w_in'], 'ffn_w_down': out['ffn_w_down'], 'ln2_g': out['ln2_g'], 'ln2_b': out['ln2_b'], 'loss_target': out['loss_target'], 'm_w_in': out['m_w_in'], 'm_b_gate': out['m_b_gate'], 'm_lru_conv_w': out['m_lru_conv_w'], 'm_lru_conv_b': out['m_lru_conv_b'], 'm_lru_w_a': out['m_lru_w_a'], 'm_lru_b_a': out['m_lru_b_a'], 'm_lru_w_i': out['m_lru_w_i'], 'm_lru_b_i': out['m_lru_b_i'], 'm_lru_lambda': out['m_lru_lambda'], 'm_ssd_conv_w': out['m_ssd_conv_w'], 'm_ssd_conv_b': out['m_ssd_conv_b'], 'm_ssd_dt_bias': out['m_ssd_dt_bias'], 'm_ssd_a_log': out['m_ssd_a_log'], 'm_ssd_d': out['m_ssd_d'], 'm_ssd_norm_w': out['m_ssd_norm_w'], 'm_mem_w_kv': out['m_mem_w_kv'], 'm_w_br_lru': out['m_w_br_lru'], 'm_w_br_ssd': out['m_w_br_ssd'], 'm_w_br_xa': out['m_w_br_xa'], 'm_w_out': out['m_w_out'], 'm_ln1_g': out['m_ln1_g'], 'm_ln1_b': out['m_ln1_b'], 'm_ffn_w_in': out['m_ffn_w_in'], 'm_ffn_w_down': out['m_ffn_w_down'], 'm_ln2_g': out['m_ln2_g'], 'm_ln2_b': out['m_ln2_b'], 'v_w_in': out['v_w_in'], 'v_b_gate': out['v_b_gate'], 'v_lru_conv_w': out['v_lru_conv_w'], 'v_lru_conv_b': out['v_lru_conv_b'], 'v_lru_w_a': out['v_lru_w_a'], 'v_lru_b_a': out['v_lru_b_a'], 'v_lru_w_i': out['v_lru_w_i'], 'v_lru_b_i': out['v_lru_b_i'], 'v_lru_lambda': out['v_lru_lambda'], 'v_ssd_conv_w': out['v_ssd_conv_w'], 'v_ssd_conv_b': out['v_ssd_conv_b'], 'v_ssd_dt_bias': out['v_ssd_dt_bias'], 'v_ssd_a_log': out['v_ssd_a_log'], 'v_ssd_d': out['v_ssd_d'], 'v_ssd_norm_w': out['v_ssd_norm_w'], 'v_mem_w_kv': out['v_mem_w_kv'], 'v_w_br_lru': out['v_w_br_lru'], 'v_w_br_ssd': out['v_w_br_ssd'], 'v_w_br_xa': out['v_w_br_xa'], 'v_w_out': out['v_w_out'], 'v_ln1_g': out['v_ln1_g'], 'v_ln1_b': out['v_ln1_b'], 'v_ffn_w_in': out['v_ffn_w_in'], 'v_ffn_w_down': out['v_ffn_w_down'], 'v_ln2_g': out['v_ln2_g'], 'v_ln2_b': out['v_ln2_b']}


def _loss(weights, diff, rest, loss_target):
    with _jax.named_scope("forward"):
        args = {**rest, TWIN_DIFF_INPUT: diff, **{k: w.astype(_WEIGHT_DTYPES[k]) for k, w in weights.items()}}
        y = _forward(args)
    with _jax.named_scope("loss_head"):
        err = _jnp.square(y.astype(_jnp.float32) - loss_target)
        return 0.5 * _jnp.sum(_jnp.mean(err, axis=-1)) if err.ndim else 0.5 * err


def _adamw(w, g, m, v):
    m = ADAM_B1 * m + (1.0 - ADAM_B1) * g
    v = ADAM_B2 * v + (1.0 - ADAM_B2) * _jnp.square(g)
    m_hat = m / (1.0 - ADAM_B1 ** ADAM_STEP)
    v_hat = v / (1.0 - ADAM_B2 ** ADAM_STEP)
    delta = -ADAM_LR * (m_hat / (_jnp.sqrt(v_hat) + ADAM_EPS) + ADAM_WD * w)
    return delta, m, v


def reference(x, mem, w_in, b_gate, lru_conv_w, lru_conv_b, lru_w_a, lru_b_a, lru_w_i, lru_b_i, lru_lambda, ssd_conv_w, ssd_conv_b, ssd_dt_bias, ssd_a_log, ssd_d, ssd_norm_w, mem_w_kv, w_br_lru, w_br_ssd, w_br_xa, w_out, ln1_g, ln1_b, ffn_w_in, ffn_w_down, ln2_g, ln2_b, loss_target, m_w_in, m_b_gate, m_lru_conv_w, m_lru_conv_b, m_lru_w_a, m_lru_b_a, m_lru_w_i, m_lru_b_i, m_lru_lambda, m_ssd_conv_w, m_ssd_conv_b, m_ssd_dt_bias, m_ssd_a_log, m_ssd_d, m_ssd_norm_w, m_mem_w_kv, m_w_br_lru, m_w_br_ssd, m_w_br_xa, m_w_out, m_ln1_g, m_ln1_b, m_ffn_w_in, m_ffn_w_down, m_ln2_g, m_ln2_b, v_w_in, v_b_gate, v_lru_conv_w, v_lru_conv_b, v_lru_w_a, v_lru_b_a, v_lru_w_i, v_lru_b_i, v_lru_lambda, v_ssd_conv_w, v_ssd_conv_b, v_ssd_dt_bias, v_ssd_a_log, v_ssd_d, v_ssd_norm_w, v_mem_w_kv, v_w_br_lru, v_w_br_ssd, v_w_br_xa, v_w_out, v_ln1_g, v_ln1_b, v_ffn_w_in, v_ffn_w_down, v_ln2_g, v_ln2_b):
    given = dict(x=x, mem=mem, w_in=w_in, b_gate=b_gate, lru_conv_w=lru_conv_w, lru_conv_b=lru_conv_b, lru_w_a=lru_w_a, lru_b_a=lru_b_a, lru_w_i=lru_w_i, lru_b_i=lru_b_i, lru_lambda=lru_lambda, ssd_conv_w=ssd_conv_w, ssd_conv_b=ssd_conv_b, ssd_dt_bias=ssd_dt_bias, ssd_a_log=ssd_a_log, ssd_d=ssd_d, ssd_norm_w=ssd_norm_w, mem_w_kv=mem_w_kv, w_br_lru=w_br_lru, w_br_ssd=w_br_ssd, w_br_xa=w_br_xa, w_out=w_out, ln1_g=ln1_g, ln1_b=ln1_b, ffn_w_in=ffn_w_in, ffn_w_down=ffn_w_down, ln2_g=ln2_g, ln2_b=ln2_b, loss_target=loss_target, m_w_in=m_w_in, m_b_gate=m_b_gate, m_lru_conv_w=m_lru_conv_w, m_lru_conv_b=m_lru_conv_b, m_lru_w_a=m_lru_w_a, m_lru_b_a=m_lru_b_a, m_lru_w_i=m_lru_w_i, m_lru_b_i=m_lru_b_i, m_lru_lambda=m_lru_lambda, m_ssd_conv_w=m_ssd_conv_w, m_ssd_conv_b=m_ssd_conv_b, m_ssd_dt_bias=m_ssd_dt_bias, m_ssd_a_log=m_ssd_a_log, m_ssd_d=m_ssd_d, m_ssd_norm_w=m_ssd_norm_w, m_mem_w_kv=m_mem_w_kv, m_w_br_lru=m_w_br_lru, m_w_br_ssd=m_w_br_ssd, m_w_br_xa=m_w_br_xa, m_w_out=m_w_out, m_ln1_g=m_ln1_g, m_ln1_b=m_ln1_b, m_ffn_w_in=m_ffn_w_in, m_ffn_w_down=m_ffn_w_down, m_ln2_g=m_ln2_g, m_ln2_b=m_ln2_b, v_w_in=v_w_in, v_b_gate=v_b_gate, v_lru_conv_w=v_lru_conv_w, v_lru_conv_b=v_lru_conv_b, v_lru_w_a=v_lru_w_a, v_lru_b_a=v_lru_b_a, v_lru_w_i=v_lru_w_i, v_lru_b_i=v_lru_b_i, v_lru_lambda=v_lru_lambda, v_ssd_conv_w=v_ssd_conv_w, v_ssd_conv_b=v_ssd_conv_b, v_ssd_dt_bias=v_ssd_dt_bias, v_ssd_a_log=v_ssd_a_log, v_ssd_d=v_ssd_d, v_ssd_norm_w=v_ssd_norm_w, v_mem_w_kv=v_mem_w_kv, v_w_br_lru=v_w_br_lru, v_w_br_ssd=v_w_br_ssd, v_w_br_xa=v_w_br_xa, v_w_out=v_w_out, v_ln1_g=v_ln1_g, v_ln1_b=v_ln1_b, v_ffn_w_in=v_ffn_w_in, v_ffn_w_down=v_ffn_w_down, v_ln2_g=v_ln2_g, v_ln2_b=v_ln2_b)
    weights = {n: given[n] for n in TWIN_WEIGHTS}
    shared = {n: given[n] for n in SHARED_INPUTS}
    per_example = {n: given[n] for n in ['x', 'mem']}
    grad_fn = _jax.value_and_grad(_loss, argnums=(0, 1))

    def one_microbatch(ex, loss_target):
        ex = dict(ex)
        diff = ex.pop(TWIN_DIFF_INPUT)
        return grad_fn(weights, diff, {**shared, **ex}, loss_target)

    if N_MICROBATCH == 1:
        loss, (grad_w, grad_x) = one_microbatch(per_example, given["loss_target"])
    else:
        def body(carry, xs):
            loss_sum, grad_sum = carry
            l_k, (gw_k, gx_k) = one_microbatch(xs[0], xs[1])
            with _jax.named_scope("update"):
                return (loss_sum + l_k, _jax.tree.map(_jnp.add, grad_sum, gw_k)), gx_k

        init = (_jnp.zeros((), _jnp.float32), _jax.tree.map(_jnp.zeros_like, weights))
        (loss, grad_w), grad_x = _jax.lax.scan(body, init, (per_example, given["loss_target"]))
    with _jax.named_scope("update"):
        delta_w, new_m, new_v = {}, {}, {}
        for n in TWIN_WEIGHTS:
            delta_w[n], new_m[n], new_v[n] = _adamw(weights[n], grad_w[n], given["m_" + n], given["v_" + n])
    return (loss, grad_x, *[grad_w[n] for n in TWIN_WEIGHTS], *[delta_w[n] for n in TWIN_WEIGHTS],
            *[new_m[n] for n in TWIN_WEIGHTS], *[new_v[n] for n in TWIN_WEIGHTS])
```

```python
import functools
import math

import jax
import jax.numpy as jnp
from jax import lax
from jax.experimental import pallas as pl
from jax.experimental.pallas import tpu as pltpu

F32, BF16 = jnp.float32, jnp.bfloat16
MESH = pl.DeviceIdType.MESH
VMEM_LIMIT_BYTES = 56 * 2**20
HALO = 16

D_MODEL = 1024
DEPTH = 2
CHUNK = 64
N_MEM = 256
LRU_BLOCKS = 8
LRU_BLOCK = 128
LRU_C = 8.0
D_SSD = 2048
SSD_HEADS = 32
SSD_HEAD_DIM = 64
SSD_GROUPS = 4
SSD_STATE = 128
D_BC = SSD_GROUPS * SSD_STATE
D_XBC = D_SSD + 2 * D_BC
XA_HEADS = 4
XA_HEAD_DIM = 256
D_FF = 2816
ALPHA = (2 * DEPTH) ** 0.25
EPS = 1e-5
N_IN = 11296
N_PROJ = 11264
DT_PAD = 128

ADAM_LR, ADAM_B1, ADAM_B2, ADAM_EPS, ADAM_WD, ADAM_STEP = 0.001, 0.9, 0.999, 1e-08, 0.01, 10

CB_XBC = (3072, 0)
CB_XS, CB_BM, CB_CM = (2048, 0), (512, 4), (512, 5)
CB_LOGITS = (3072, 1)
CB_G0, CB_G1, CB_G2 = (1024, 3), (1024, 4), (1024, 5)
CB_Z = (2048, 3)
CB_LRU_X, CB_LRU_GATE, CB_XA_Q = (1024, 8), (1024, 9), (1024, 10)


def _cparams(sem):
    return pltpu.CompilerParams(dimension_semantics=sem, vmem_limit_bytes=VMEM_LIMIT_BYTES)


def _pcall(name, body, n, ins, outs, scratch=(), reverse=False, aliases=None):
    def ridx(i):
        return (n - 1 - i) if reverse else i

    in_specs, args = [], []
    for sp in ins:
        kind, arr = sp[0], sp[1]
        if kind == "row":
            _, _, tile, width, cb = sp
            in_specs.append(pl.BlockSpec((tile, width), lambda i, cb=cb: (ridx(i), cb)))
        elif kind == "prev":
            _, _, tile, width, cb = sp
            t = tile // HALO
            in_specs.append(pl.BlockSpec((HALO, width), lambda i, cb=cb, t=t: (jnp.maximum(ridx(i) * t - 1, 0), cb)))
        elif kind == "next":
            _, _, tile, width, cb = sp
            t = tile // HALO
            last = arr.shape[0] // HALO - 1
            in_specs.append(pl.BlockSpec((HALO, width), lambda i, cb=cb, t=t, last=last: (jnp.minimum((ridx(i) + 1) * t, last), cb)))
        elif kind == "lead":
            nd = arr.ndim
            in_specs.append(pl.BlockSpec((1,) + arr.shape[1:], lambda i, nd=nd: (ridx(i),) + (0,) * (nd - 1)))
        elif kind == "full":
            nd = arr.ndim
            in_specs.append(pl.BlockSpec(arr.shape, lambda i, nd=nd: (0,) * nd))
        elif kind == "any":
            in_specs.append(pl.BlockSpec(memory_space=pl.ANY))
        else:
            raise ValueError(kind)
        args.append(arr)
    out_specs, out_shape = [], []
    for sp in outs:
        kind = sp[0]
        if kind == "row":
            _, rows, cols, dtype, tile, width, cb = sp
            out_shape.append(jax.ShapeDtypeStruct((rows, cols), dtype))
            out_specs.append(pl.BlockSpec((tile, width), lambda i, cb=cb: (ridx(i), cb)))
        elif kind == "lead":
            _, shape, dtype = sp
            nd = len(shape)
            out_shape.append(jax.ShapeDtypeStruct(shape, dtype))
            out_specs.append(pl.BlockSpec((1,) + tuple(shape[1:]), lambda i, nd=nd: (ridx(i),) + (0,) * (nd - 1)))
        elif kind == "full":
            _, shape, dtype = sp
            nd = len(shape)
            out_shape.append(jax.ShapeDtypeStruct(shape, dtype))
            out_specs.append(pl.BlockSpec(tuple(shape), lambda i, nd=nd: (0,) * nd))
        else:
            raise ValueError(kind)
    res = pl.pallas_call(
        body, name=name, grid=(n,), in_specs=in_specs, out_specs=out_specs, out_shape=out_shape,
        scratch_shapes=list(scratch), input_output_aliases=aliases or {},
        compiler_params=_cparams(("arbitrary",)),
    )(*args)
    return res


def _pick(n, cands):
    for c in cands:
        if n % c == 0:
            return c
    return n


def _mm(name, a, b, mode, out_dtype, add=None):
    if mode == "nn":
        (M, K), (K2, N) = a.shape, b.shape
    elif mode == "nt":
        (M, K), (N, K2) = a.shape, b.shape
    else:
        (K, M), (K2, N) = a.shape, b.shape
    assert K == K2, (name, a.shape, b.shape)
    tm = _pick(M, (512, 256, 128))
    tn = _pick(N, (1024, 512, 256, 128))
    if mode == "tn":
        tk = _pick(K, (1024, 512, 256))
    else:
        tk = K if K <= 2816 else _pick(K, (1024, 512, 256, 128))
    nk = K // tk
    has_add = add is not None

    def body(*refs):
        if has_add:
            a_ref, b_ref, add_ref, o_ref, acc_ref = refs
        else:
            a_ref, b_ref, o_ref, acc_ref = refs
        k = pl.program_id(2)
        av = a_ref[...].astype(BF16)
        bv = b_ref[...].astype(BF16)
        if mode == "nn":
            p = jnp.dot(av, bv, preferred_element_type=F32)
        elif mode == "nt":
            p = lax.dot_general(av, bv, (((1,), (1,)), ((), ())), preferred_element_type=F32)
        else:
            p = lax.dot_general(av, bv, (((0,), (0,)), ((), ())), preferred_element_type=F32)

        def fin(v):
            if has_add:
                v = v + add_ref[...].astype(F32)
            o_ref[...] = v.astype(out_dtype)

        if nk == 1:
            fin(p)
        else:
            @pl.when(k == 0)
            def _():
                acc_ref[...] = p

            @pl.when(k > 0)
            def _():
                acc_ref[...] += p

            @pl.when(k == nk - 1)
            def _():
                fin(acc_ref[...])

    if mode == "nn":
        specs = [pl.BlockSpec((tm, tk), lambda i, j, k: (i, k)), pl.BlockSpec((tk, tn), lambda i, j, k: (k, j))]
    elif mode == "nt":
        specs = [pl.BlockSpec((tm, tk), lambda i, j, k: (i, k)), pl.BlockSpec((tn, tk), lambda i, j, k: (j, k))]
    else:
        specs = [pl.BlockSpec((tk, tm), lambda i, j, k: (k, i)), pl.BlockSpec((tk, tn), lambda i, j, k: (k, j))]
    args = [a, b]
    if has_add:
        specs.append(pl.BlockSpec((tm, tn), lambda i, j, k: (i, j)))
        args.append(add)
    acc_shape = (tm, tn) if nk > 1 else (8, 128)
    return pl.pallas_call(
        body, name=name, grid=(M // tm, N // tn, nk), in_specs=specs,
        out_specs=pl.BlockSpec((tm, tn), lambda i, j, k: (i, j)),
        out_shape=jax.ShapeDtypeStruct((M, N), out_dtype),
        scratch_shapes=[pltpu.VMEM(acc_shape, F32)],
        compiler_params=_cparams(("parallel", "parallel", "arbitrary")),
    )(*args)


def _sigmoid(x):
    return 1.0 / (1.0 + jnp.exp(-x))


def _silu(x):
    return x * _sigmoid(x)


def _dsilu(x):
    s = _sigmoid(x)
    return s * (1.0 + x * (1.0 - s))


def _softplus(x):
    return jnp.maximum(x, 0.0) + jnp.log(1.0 + jnp.exp(-jnp.abs(x)))


_GELU_C = math.sqrt(2.0 / math.pi)


def _gelu(x):
    return 0.5 * x * (1.0 + jnp.tanh(_GELU_C * (x + 0.044715 * x * x * x)))


def _dgelu(x):
    t = jnp.tanh(_GELU_C * (x + 0.044715 * x * x * x))
    return 0.5 * (1.0 + t) + 0.5 * x * (1.0 - t * t) * _GELU_C * (1.0 + 3.0 * 0.044715 * x * x)


def _acc(ref, i, val):
    @pl.when(i == 0)
    def _():
        ref[...] = val

    @pl.when(i > 0)
    def _():
        ref[...] += val


def _rows(shape):
    return lax.broadcasted_iota(jnp.int32, shape, 0)


def _shift_down(x, k, halo8, first):
    r = pltpu.roll(x, k, 0)
    h = pltpu.roll(halo8, k, 0)
    h = jnp.where(first, 0.0, h)
    head = jnp.where(_rows(h.shape) < k, h, r[:8])
    return jnp.concatenate([head, r[8:]], axis=0)


def _shift_up(x, k, halo8, last):
    T = x.shape[0]
    r = pltpu.roll(x, T - k, 0)
    h = pltpu.roll(halo8, 8 - k, 0)
    h = jnp.where(last, 0.0, h)
    tail = jnp.where(_rows(h.shape) >= 8 - k, h, r[T - 8:])
    return jnp.concatenate([r[:T - 8], tail], axis=0)


def _ln_fwd(name, a, b, g, beta, tile):
    S, Dm = a.shape
    n = S // tile

    def body(a_ref, b_ref, g_ref, be_ref, y_ref, xh_ref, rs_ref):
        z = ALPHA * a_ref[...] + b_ref[...].astype(F32)
        mu = jnp.mean(z, axis=-1, keepdims=True)
        zc = z - mu
        var = jnp.mean(zc * zc, axis=-1, keepdims=True)
        rstd = lax.rsqrt(var + EPS)
        xh = zc * rstd
        y_ref[...] = xh * g_ref[...] + be_ref[...]
        xh_ref[...] = xh
        rs_ref[...] = rstd

    return _pcall(name, body, n,
                  [("row", a, tile, Dm, 0), ("row", b, tile, Dm, 0), ("full", g), ("full", beta)],
                  [("row", S, Dm, F32, tile, Dm, 0), ("row", S, Dm, F32, tile, Dm, 0), ("row", S, 1, F32, tile, 1, 0)])


def _ln_bwd(name, dys, coefs, xh, rstd, g, tile):
    S, Dm = xh.shape
    n = S // tile
    nd = len(dys)

    def body(*refs):
        dy_refs = refs[:nd]
        xh_ref, rs_ref, g_ref, dz_ref, dg_ref, db_ref = refs[nd:]
        i = pl.program_id(0)
        dy = coefs[0] * dy_refs[0][...].astype(F32)
        for k in range(1, nd):
            dy = dy + coefs[k] * dy_refs[k][...].astype(F32)
        xh_v = xh_ref[...]
        dxh = dy * g_ref[...]
        m1 = jnp.mean(dxh, axis=-1, keepdims=True)
        m2 = jnp.mean(dxh * xh_v, axis=-1, keepdims=True)
        dz_ref[...] = rs_ref[...] * (dxh - m1 - xh_v * m2)
        _acc(dg_ref, i, jnp.sum(dy * xh_v, axis=0, keepdims=True))
        _acc(db_ref, i, jnp.sum(dy, axis=0, keepdims=True))

    ins = [("row", d, tile, Dm, 0) for d in dys]
    ins += [("row", xh, tile, Dm, 0), ("row", rstd, tile, 1, 0), ("full", g)]
    return _pcall(name, body, n, ins,
                  [("row", S, Dm, F32, tile, Dm, 0), ("full", (1, Dm), F32), ("full", (1, Dm), F32)])


def _loss_fwd_bwd(name, y, target, tile):
    S, Dm = y.shape
    n = S // tile

    def body(y_ref, t_ref, dy_ref, l_ref):
        i = pl.program_id(0)
        err = y_ref[...] - t_ref[...]
        dy_ref[...] = err * (1.0 / Dm)
        part = jnp.sum(jnp.sum(err * err, axis=-1, keepdims=True), axis=0, keepdims=True) * (0.5 / Dm)
        _acc(l_ref, i, part)

    return _pcall(name, body, n, [("row", y, tile, Dm, 0), ("row", target, tile, Dm, 0)],
                  [("row", S, Dm, F32, tile, Dm, 0), ("full", (1, 1), F32)])


def _swiglu_fwd(name, gu, tile):
    S = gu.shape[0]
    n = S // tile

    def body(g_ref, u_ref, o_ref):
        o_ref[...] = (_silu(g_ref[...].astype(F32)) * u_ref[...].astype(F32)).astype(BF16)

    return _pcall(name, body, n, [("row", gu, tile, D_FF, 0), ("row", gu, tile, D_FF, 1)],
                  [("row", S, D_FF, BF16, tile, D_FF, 0)])[0]


def _swiglu_bwd(name, gu, dh, tile):
    S = gu.shape[0]
    n = S // tile

    def body(g_ref, u_ref, dh_ref, o_ref):
        gv = g_ref[...].astype(F32)
        uv = u_ref[...].astype(F32)
        dv = dh_ref[...].astype(F32)
        dg = dv * uv * _dsilu(gv)
        du = dv * _silu(gv)
        o_ref[...] = jnp.concatenate([dg, du], axis=1).astype(BF16)

    return _pcall(name, body, n, [("row", gu, tile, D_FF, 0), ("row", gu, tile, D_FF, 1), ("row", dh, tile, D_FF, 0)],
                  [("row", S, 2 * D_FF, BF16, tile, 2 * D_FF, 0)])[0]


def _merge_fwd(name, proj, b_gate, pl_, ps_, px_, tile):
    S = proj.shape[0]
    n = S // tile
    Dm = D_MODEL

    def body(l0, l1, l2, bg, p0, p1, p2, o_ref):
        bgv = bg[...]
        acc = _sigmoid(l0[...].astype(F32) + bgv[0:1]) * p0[...].astype(F32)
        acc = acc + _sigmoid(l1[...].astype(F32) + bgv[1:2]) * p1[...].astype(F32)
        acc = acc + _sigmoid(l2[...].astype(F32) + bgv[2:3]) * p2[...].astype(F32)
        o_ref[...] = acc.astype(BF16)

    ins = [("row", proj, tile, *CB_G0), ("row", proj, tile, *CB_G1), ("row", proj, tile, *CB_G2), ("full", b_gate),
           ("row", pl_, tile, Dm, 0), ("row", ps_, tile, Dm, 0), ("row", px_, tile, Dm, 0)]
    return _pcall(name, body, n, ins, [("row", S, Dm, BF16, tile, Dm, 0)])[0]


def _merge_bwd(name, proj, b_gate, pl_, ps_, px_, dmerged, tile):
    S = proj.shape[0]
    n = S // tile
    Dm = D_MODEL

    def body(l0, l1, l2, bg, p0, p1, p2, dm_ref, dproj_ref, d0, d1, d2, db0, db1, db2):
        i = pl.program_id(0)
        bgv = bg[...]
        dm = dm_ref[...].astype(F32)
        dls = []
        for k, (lr, pr, dr, dbr) in enumerate(((l0, p0, d0, db0), (l1, p1, d1, db1), (l2, p2, d2, db2))):
            gk = _sigmoid(lr[...].astype(F32) + bgv[k:k + 1])
            dr[...] = (dm * gk).astype(BF16)
            dl = dm * pr[...].astype(F32) * gk * (1.0 - gk)
            _acc(dbr, i, jnp.sum(dl, axis=0, keepdims=True))
            dls.append(dl)
        dproj_ref[...] = jnp.concatenate(dls, axis=1).astype(BF16)

    ins = [("row", proj, tile, *CB_G0), ("row", proj, tile, *CB_G1), ("row", proj, tile, *CB_G2), ("full", b_gate),
           ("row", pl_, tile, Dm, 0), ("row", ps_, tile, Dm, 0), ("row", px_, tile, Dm, 0), ("row", dmerged, tile, Dm, 0)]
    outs = [("row", S, N_PROJ, BF16, tile, *CB_LOGITS)] + [("row", S, Dm, BF16, tile, Dm, 0)] * 3 + [("full", (1, Dm), F32)] * 3
    return _pcall(name, body, n, ins, outs)


def _conv_taps(xf, halo8, first, w):
    out = xf * w[3:4]
    for k in (1, 2, 3):
        out = out + _shift_down(xf, k, halo8, first) * w[3 - k:4 - k]
    return out


def _conv_fwd(name, src, cb, w, b, act, out_dtype, tile):
    S = src.shape[0]
    C = cb[0]
    n = S // tile

    def body(x_ref, p_ref, w_ref, b_ref, o_ref):
        i = pl.program_id(0)
        xf = x_ref[...].astype(F32)
        halo8 = p_ref[...].astype(F32)[HALO - 8:]
        pre = _conv_taps(xf, halo8, i == 0, w_ref[...]) + b_ref[...]
        o_ref[...] = (_silu(pre) if act else pre).astype(out_dtype)

    return _pcall(name, body, n, [("row", src, tile, *cb), ("prev", src, tile, *cb), ("full", w), ("full", b)],
                  [("row", S, C, out_dtype, tile, C, 0)])[0]


def _conv_act_bwd(name, src, cb, w, b, dout, tile):
    S = src.shape[0]
    C = cb[0]
    n = S // tile

    def body(x_ref, p_ref, w_ref, b_ref, d_ref, o_ref):
        i = pl.program_id(0)
        xf = x_ref[...].astype(F32)
        halo8 = p_ref[...].astype(F32)[HALO - 8:]
        pre = _conv_taps(xf, halo8, i == 0, w_ref[...]) + b_ref[...]
        o_ref[...] = d_ref[...].astype(F32) * _dsilu(pre)

    return _pcall(name, body, n, [("row", src, tile, *cb), ("prev", src, tile, *cb), ("full", w), ("full", b),
                                  ("row", dout, tile, C, 0)],
                  [("row", S, C, F32, tile, C, 0)])[0]


def _conv_bwd(name, src, cb, w, dpre, dproj, tile):
    S = src.shape[0]
    C = cb[0]
    n = S // tile

    def body(x_ref, p_ref, w_ref, d_ref, nx_ref, buf_ref, dx_ref, dw0, dw1, dw2, dw3, db_ref):
        i = pl.program_id(0)
        first, last = i == 0, i == n - 1
        xf = x_ref[...].astype(F32)
        halo8 = p_ref[...].astype(F32)[HALO - 8:]
        dv = d_ref[...]
        nx8 = nx_ref[...][:8]
        wv = w_ref[...]
        dx = dv * wv[3:4]
        for k in (1, 2, 3):
            dx = dx + _shift_up(dv, k, nx8, last) * wv[3 - k:4 - k]
        dx_ref[...] = dx.astype(BF16)
        _acc(dw3, i, jnp.sum(dv * xf, axis=0, keepdims=True))
        for k, dwr in ((1, dw2), (2, dw1), (3, dw0)):
            _acc(dwr, i, jnp.sum(dv * _shift_down(xf, k, halo8, first), axis=0, keepdims=True))
        _acc(db_ref, i, jnp.sum(dv, axis=0, keepdims=True))

    ins = [("row", src, tile, *cb), ("prev", src, tile, *cb), ("full", w), ("row", dpre, tile, C, 0),
           ("next", dpre, tile, C, 0), ("any", dproj)]
    outs = [("row", S, N_PROJ, BF16, tile, *cb)] + [("full", (1, C), F32)] * 5
    return _pcall(name, body, n, ins, outs, aliases={5: 0})


def _lru_gates(xc, wa_ref, wi_ref, ba, bi, lam):
    xb = xc.astype(BF16)
    pa, pi_ = [], []
    for nb in range(LRU_BLOCKS):
        sl = slice(nb * LRU_BLOCK, (nb + 1) * LRU_BLOCK)
        pa.append(jnp.dot(xb[:, sl], wa_ref[nb], preferred_element_type=F32))
        pi_.append(jnp.dot(xb[:, sl], wi_ref[nb], preferred_element_type=F32))
    r = _sigmoid(jnp.concatenate(pa, axis=1) + ba)
    ig = _sigmoid(jnp.concatenate(pi_, axis=1) + bi)
    sp = _softplus(-lam)
    a = jnp.exp(-LRU_C * r * sp)
    m = jnp.sqrt(1.0 - a * a)
    return xb, r, ig, sp, a, m


def _lru_fwd(name, xc, proj, wa, wi, ba, bi, lam, tile):
    S = xc.shape[0]
    n = S // tile
    C = D_MODEL

    def body(xc_ref, gate_ref, wa_ref, wi_ref, ba_ref, bi_ref, lam_ref, y_ref, h_ref, carry):
        i = pl.program_id(0)

        @pl.when(i == 0)
        def _():
            carry[...] = jnp.zeros_like(carry)

        xcv = xc_ref[...]
        _, r, ig, sp, a, m = _lru_gates(xcv, wa_ref, wi_ref, ba_ref[...], bi_ref[...], lam_ref[...])
        u = m * (ig * xcv)
        rows = _rows(a.shape)
        d = 1
        while d < tile:
            keep = rows >= d
            a_s = jnp.where(keep, pltpu.roll(a, d, 0), 1.0)
            u_s = jnp.where(keep, pltpu.roll(u, d, 0), 0.0)
            u = a * u_s + u
            a = a * a_s
            d *= 2
        h = u + a * carry[0:1, :]
        h_ref[...] = h
        carry[0:1, :] = h_ref[pl.ds(tile - 1, 1), :]
        y_ref[...] = (_gelu(gate_ref[...].astype(F32)) * h).astype(BF16)

    ins = [("row", xc, tile, C, 0), ("row", proj, tile, *CB_LRU_GATE), ("full", wa), ("full", wi),
           ("full", ba), ("full", bi), ("full", lam)]
    return _pcall(name, body, n, ins, [("row", S, C, BF16, tile, C, 0), ("row", S, C, F32, tile, C, 0)],
                  scratch=[pltpu.VMEM((8, C), F32)])


def _lru_bwd(name, dy, xc, proj, h, wa, wi, ba, bi, lam, dproj, tile):
    S = xc.shape[0]
    n = S // tile
    C = D_MODEL

    def body(dy_ref, xc_ref, gate_ref, h_ref, hp_ref, wa_ref, wi_ref, ba_ref, bi_ref, lam_ref, buf_ref,
             dg_ref, dxc_ref, dwa_ref, dwi_ref, dba_ref, dbi_ref, dlam_ref, carry):
        i = pl.program_id(0)
        first_tile = i == n - 1

        @pl.when(i == 0)
        def _():
            carry[...] = jnp.zeros_like(carry)

        xcv = xc_ref[...]
        lamv = lam_ref[...]
        xb, r, ig, sp, a, m = _lru_gates(xcv, wa_ref, wi_ref, ba_ref[...], bi_ref[...], lamv)
        hv = h_ref[...]
        gv = gate_ref[...].astype(F32)
        dyv = dy_ref[...].astype(F32)
        dg_ref[...] = (dyv * hv * _dgelu(gv)).astype(BF16)
        v = dyv * _gelu(gv)
        rows = _rows(a.shape)
        bcoef = jnp.where(rows == tile - 1, 1.0, pltpu.roll(a, tile - 1, 0))
        d = 1
        while d < tile:
            keep = rows < tile - d
            b_s = jnp.where(keep, pltpu.roll(bcoef, tile - d, 0), 1.0)
            v_s = jnp.where(keep, pltpu.roll(v, tile - d, 0), 0.0)
            v = v + bcoef * v_s
            bcoef = bcoef * b_s
            d *= 2
        dH = v + bcoef * carry[0:1, :]
        dxc_ref[...] = dH
        carry[0:1, :] = dxc_ref[pl.ds(0, 1), :] * a[0:1, :]
        halo8 = hp_ref[...][HALO - 8:]
        hprev = _shift_down(hv, 1, halo8, first_tile)
        da = dH * hprev
        ix = ig * xcv
        dm = dH * ix
        di = dH * m * xcv
        dxc = dH * m * ig
        da = da - dm * a / m
        dla = da * a
        dr = dla * (-LRU_C) * sp
        _acc(dlam_ref, i, jnp.sum(dla * (-LRU_C) * r, axis=0, keepdims=True) * (-_sigmoid(-lamv)))
        dpa = dr * r * (1.0 - r)
        dpi = di * ig * (1.0 - ig)
        _acc(dba_ref, i, jnp.sum(dpa, axis=0, keepdims=True))
        _acc(dbi_ref, i, jnp.sum(dpi, axis=0, keepdims=True))
        dpab, dpib = dpa.astype(BF16), dpi.astype(BF16)
        back = []
        for nb in range(LRU_BLOCKS):
            sl = slice(nb * LRU_BLOCK, (nb + 1) * LRU_BLOCK)
            back.append(lax.dot_general(dpab[:, sl], wa_ref[nb], (((1,), (1,)), ((), ())), preferred_element_type=F32)
                        + lax.dot_general(dpib[:, sl], wi_ref[nb], (((1,), (1,)), ((), ())), preferred_element_type=F32))
            ga = lax.dot_general(xb[:, sl], dpab[:, sl], (((0,), (0,)), ((), ())), preferred_element_type=F32)
            gi = lax.dot_general(xb[:, sl], dpib[:, sl], (((0,), (0,)), ((), ())), preferred_element_type=F32)

            @pl.when(i == 0)
            def _(ga=ga, gi=gi, nb=nb):
                dwa_ref[nb] = ga
                dwi_ref[nb] = gi

            @pl.when(i > 0)
            def _(ga=ga, gi=gi, nb=nb):
                dwa_ref[nb] += ga
                dwi_ref[nb] += gi

        dxc_ref[...] = dxc + jnp.concatenate(back, axis=1)

    ins = [("row", dy, tile, C, 0), ("row", xc, tile, C, 0), ("row", proj, tile, *CB_LRU_GATE), ("row", h, tile, C, 0),
           ("prev", h, tile, C, 0), ("full", wa), ("full", wi), ("full", ba), ("full", bi), ("full", lam), ("any", dproj)]
    outs = [("row", S, N_PROJ, BF16, tile, *CB_LRU_GATE), ("row", S, C, F32, tile, C, 0),
            ("full", (LRU_BLOCKS, LRU_BLOCK, LRU_BLOCK), F32), ("full", (LRU_BLOCKS, LRU_BLOCK, LRU_BLOCK), F32),
            ("full", (1, C), F32), ("full", (1, C), F32), ("full", (1, C), F32)]
    return _pcall(name, body, n, ins, outs, scratch=[pltpu.VMEM((8, C), F32)], reverse=True, aliases={10: 0})


def _split3(x):
    h = x.astype(BF16)
    r = x - h.astype(F32)
    m = r.astype(BF16)
    lo = (r - m.astype(F32)).astype(BF16)
    return h, m, lo


def _dot01_r(x, e):
    h, m, lo = _split3(x)
    return (jnp.dot(h, e, preferred_element_type=F32) + jnp.dot(m, e, preferred_element_type=F32)
            + jnp.dot(lo, e, preferred_element_type=F32))


def _dot01_l(e, x):
    h, m, lo = _split3(x)
    return (jnp.dot(e, h, preferred_element_type=F32) + jnp.dot(e, m, preferred_element_type=F32)
            + jnp.dot(e, lo, preferred_element_type=F32))


def _ssd_consts():
    hh = lax.broadcasted_iota(jnp.int32, (DT_PAD, D_SSD), 0)
    cc = lax.broadcasted_iota(jnp.int32, (DT_PAD, D_SSD), 1)
    e = (cc // SSD_HEAD_DIM == hh).astype(BF16)
    li = lax.broadcasted_iota(jnp.int32, (CHUNK, CHUNK), 0)
    si = lax.broadcasted_iota(jnp.int32, (CHUNK, CHUNK), 1)
    ltri = (li >= si).astype(BF16)
    l4 = lax.broadcasted_iota(jnp.int32, (CHUNK, 4 * CHUNK), 0)
    s4 = lax.broadcasted_iota(jnp.int32, (CHUNK, 4 * CHUNK), 1) % CHUNK
    itile = (l4 == s4).astype(F32)
    causal = (l4 >= s4).astype(F32)
    j4 = lax.broadcasted_iota(jnp.int32, (8, 4 * CHUNK), 0)
    c4 = lax.broadcasted_iota(jnp.int32, (8, 4 * CHUNK), 1) // CHUNK
    hmask = (j4 == c4).astype(F32)
    return e, e.T, ltri, ltri.T, itile, causal, hmask


def _ssd_chunk_common(xs_ref, bm_ref, cm_ref, dt_ref, dtb_ref, a_ref, e_ref, ltri_ref):
    xs = xs_ref[...].astype(F32)
    raw = dt_ref[...] + dtb_ref[...]
    dtv = _softplus(raw)
    da = dtv * a_ref[...]
    cs = _dot01_l(ltri_ref[...], da)
    e = e_ref[...]
    dte = _dot01_r(dtv, e)
    ce = _dot01_r(cs, e)
    xdt = xs * dte
    cle = ce[CHUNK - 1:CHUNK, :]
    dend = jnp.exp(cle - ce)
    ecs = jnp.exp(ce)
    return xs, raw, dtv, cs, dte, ce, xdt, cle, dend, ecs


def _quad_terms(ce_q, cb4, itile, causal):
    cr = jnp.sum(ce_q * itile, axis=0, keepdims=True)
    seg = ce_q - cr
    dec = jnp.where(causal > 0.0, jnp.exp(jnp.minimum(seg, 0.0)), 0.0)
    return dec, cb4 * dec


def _block_diag4(xq, hmask):
    return jnp.concatenate([xq * hmask[j:j + 1].astype(xq.dtype) for j in range(4)], axis=0)


def _ssd_fwd(name, xbc, dt_raw, dt_bias, a_neg, d_exp, consts):
    S = xbc.shape[0]
    nc = S // CHUNK
    e, et, ltri, ltri_t, itile, causal, hmask = consts

    def body(xs_ref, bm_ref, cm_ref, dt_ref, dtb_ref, a_ref, dex_ref, e_ref, ltri_ref, it_ref, ca_ref, hm_ref,
             y_ref, prev_ref, hst):
        i = pl.program_id(0)

        @pl.when(i == 0)
        def _():
            hst[...] = jnp.zeros_like(hst)

        xs, raw, dtv, cs, dte, ce, xdt, cle, dend, ecs = _ssd_chunk_common(
            xs_ref, bm_ref, cm_ref, dt_ref, dtb_ref, a_ref, e_ref, ltri_ref)
        xdtb = xdt.astype(BF16)
        xst = (xdt * dend).astype(BF16)
        ecl = jnp.exp(cle)
        itile_v, causal_v, hmask_v = it_ref[...], ca_ref[...], hm_ref[...]
        bm = bm_ref[...]
        cm = cm_ref[...]
        dskip = dex_ref[...] * xs
        for g in range(SSD_GROUPS):
            gs = slice(g * 512, (g + 1) * 512)
            ns = slice(g * SSD_STATE, (g + 1) * SSD_STATE)
            bm_g, cm_g = bm[:, ns], cm[:, ns]
            hprev = hst[g]
            hprev_b = hprev.astype(BF16)
            prev_ref[0, g] = hprev_b
            yoff = jnp.dot(cm_g, hprev_b, preferred_element_type=F32) * ecs[:, gs]
            st = lax.dot_general(bm_g, xst[:, gs], (((0,), (0,)), ((), ())), preferred_element_type=F32)
            hst[g] = hprev * ecl[:, gs] + st
            b4 = jnp.concatenate([bm_g] * 4, axis=0)
            cb4 = lax.dot_general(cm_g, b4, (((1,), (1,)), ((), ())), preferred_element_type=F32)
            for q in range(2):
                cols = slice(g * 512 + q * 256, g * 512 + (q + 1) * 256)
                _, mq = _quad_terms(ce[:, cols], cb4, itile_v, causal_v)
                xbd = _block_diag4(xdtb[:, cols], hmask_v)
                ydiag = jnp.dot(mq.astype(BF16), xbd, preferred_element_type=F32)
                y_ref[:, cols] = ydiag + yoff[:, q * 256:(q + 1) * 256] + dskip[:, cols]

    ins = [("row", xbc, CHUNK, *CB_XS), ("row", xbc, CHUNK, *CB_BM), ("row", xbc, CHUNK, *CB_CM),
           ("row", dt_raw, CHUNK, DT_PAD, 0), ("full", dt_bias), ("full", a_neg), ("full", d_exp),
           ("full", e), ("full", ltri), ("full", itile), ("full", causal), ("full", hmask)]
    outs = [("row", S, D_SSD, F32, CHUNK, D_SSD, 0), ("lead", (nc, SSD_GROUPS, SSD_STATE, 512), BF16)]
    return _pcall(name, body, nc, ins, outs, scratch=[pltpu.VMEM((SSD_GROUPS, SSD_STATE, 512), F32)])


def _ssd_bwd(name, xbc, dt_raw, dt_bias, a_neg, d_exp, prev, dy, consts):
    S = xbc.shape[0]
    nc = S // CHUNK
    e, et, ltri, ltri_t, itile, causal, hmask = consts

    def body(xs_ref, bm_ref, cm_ref, dt_ref, dtb_ref, a_ref, dex_ref, prev_ref, dy_ref,
             e_ref, et_ref, ltri_ref, ltt_ref, it_ref, ca_ref, hm_ref,
             dx_ref, ddt_ref, da_ref, dbias_ref, dd_ref, dh, dce_ref, dxdt_ref):
        i = pl.program_id(0)

        @pl.when(i == 0)
        def _():
            dh[...] = jnp.zeros_like(dh)

        xs, raw, dtv, cs, dte, ce, xdt, cle, dend, ecs = _ssd_chunk_common(
            xs_ref, bm_ref, cm_ref, dt_ref, dtb_ref, a_ref, e_ref, ltri_ref)
        xdtb = xdt.astype(BF16)
        xst = (xdt * dend).astype(BF16)
        ecl = jnp.exp(cle)
        itile_v, causal_v, hmask_v = it_ref[...], ca_ref[...], hm_ref[...]
        bm = bm_ref[...]
        cm = cm_ref[...]
        dyv = dy_ref[...]
        last_row = _rows((CHUNK, 512)) == CHUNK - 1
        for g in range(SSD_GROUPS):
            gs = slice(g * 512, (g + 1) * 512)
            ns = slice(g * SSD_STATE, (g + 1) * SSD_STATE)
            bm_g, cm_g = bm[:, ns], cm[:, ns]
            hprev_b = prev_ref[0, g]
            dhn = dh[g]
            dhn_b = dhn.astype(BF16)
            dy_g = dyv[:, gs]
            ecs_g, dend_g, xdt_g, ecl_g = ecs[:, gs], dend[:, gs], xdt[:, gs], ecl[:, gs]
            z = jnp.dot(cm_g, hprev_b, preferred_element_type=F32)
            dz = dy_g * ecs_g
            dzb = dz.astype(BF16)
            dce_g = dz * z
            dcm_g = lax.dot_general(dzb, hprev_b, (((1,), (1,)), ((), ())), preferred_element_type=F32)
            dprev = lax.dot_general(cm_g, dzb, (((0,), (0,)), ((), ())), preferred_element_type=F32) + dhn * ecl_g
            dcl = jnp.sum(dhn * hprev_b.astype(F32), axis=0, keepdims=True) * ecl_g
            gmat = jnp.dot(bm_g, dhn_b, preferred_element_type=F32)
            dbm_g = lax.dot_general(xst[:, gs], dhn_b, (((1,), (1,)), ((), ())), preferred_element_type=F32)
            dxdt_g = gmat * dend_g
            t = gmat * xdt_g * dend_g
            dce_g = dce_g - t
            dcl = dcl + jnp.sum(t, axis=0, keepdims=True)
            dce_g = dce_g + jnp.where(last_row, dcl, 0.0)
            dh[g] = dprev
            b4 = jnp.concatenate([bm_g] * 4, axis=0)
            cb4 = lax.dot_general(cm_g, b4, (((1,), (1,)), ((), ())), preferred_element_type=F32)
            for q in range(2):
                qs = slice(q * 256, (q + 1) * 256)
                cols = slice(g * 512 + q * 256, g * 512 + (q + 1) * 256)
                dec, mq = _quad_terms(ce[:, cols], cb4, itile_v, causal_v)
                mqb = mq.astype(BF16)
                xbd = _block_diag4(xdtb[:, cols], hmask_v)
                dyq = dy_g[:, qs].astype(BF16)
                dm = lax.dot_general(dyq, xbd, (((1,), (1,)), ((), ())), preferred_element_type=F32)
                rmat = lax.dot_general(mqb, dyq, (((0,), (0,)), ((), ())), preferred_element_type=F32)
                dxq = rmat[0:64] * hmask_v[0:1]
                for j in range(1, 4):
                    dxq = dxq + rmat[64 * j:64 * (j + 1)] * hmask_v[j:j + 1]
                tq = dm * dec
                tqb = tq.astype(BF16)
                dcm_g = dcm_g + jnp.dot(tqb, b4, preferred_element_type=F32)
                rb = lax.dot_general(tqb, cm_g, (((0,), (0,)), ((), ())), preferred_element_type=F32)
                dbm_g = dbm_g + rb[0:64] + rb[64:128] + rb[128:192] + rb[192:256]
                dseg = tq * cb4
                colsum = jnp.sum(dseg, axis=0, keepdims=True)
                dce_ref[:, cols] = dce_g[:, qs] + dseg - itile_v * colsum
                dxdt_ref[:, cols] = dxdt_g[:, qs] + dxq
            dx_ref[:, D_SSD + g * SSD_STATE:D_SSD + (g + 1) * SSD_STATE] = dbm_g
            dx_ref[:, D_SSD + D_BC + g * SSD_STATE:D_SSD + D_BC + (g + 1) * SSD_STATE] = dcm_g
        dxdt = dxdt_ref[...]
        dexv = dex_ref[...]
        dx_ref[:, 0:D_SSD] = dxdt * dte + dyv * dexv
        _acc(dd_ref, i, jnp.sum(dyv * xs, axis=0, keepdims=True))
        etv = et_ref[...]
        dcs = _dot01_r(dce_ref[...], etv)
        dda = _dot01_l(ltt_ref[...], dcs)
        av = a_ref[...]
        ddtv = dda * av + _dot01_r(dxdt * xs, etv)
        _acc(da_ref, i, jnp.sum(dda * dtv, axis=0, keepdims=True))
        draw = ddtv * _sigmoid(raw)
        ddt_ref[...] = draw.astype(BF16)
        _acc(dbias_ref, i, jnp.sum(draw, axis=0, keepdims=True))

    ins = [("row", xbc, CHUNK, *CB_XS), ("row", xbc, CHUNK, *CB_BM), ("row", xbc, CHUNK, *CB_CM),
           ("row", dt_raw, CHUNK, DT_PAD, 0), ("full", dt_bias), ("full", a_neg), ("full", d_exp),
           ("lead", prev), ("row", dy, CHUNK, D_SSD, 0),
           ("full", e), ("full", et), ("full", ltri), ("full", ltri_t), ("full", itile), ("full", causal), ("full", hmask)]
    outs = [("row", S, D_XBC, F32, CHUNK, D_XBC, 0), ("row", S, DT_PAD, BF16, CHUNK, DT_PAD, 0),
            ("full", (1, DT_PAD), F32), ("full", (1, DT_PAD), F32), ("full", (1, D_SSD), F32)]
    scratch = [pltpu.VMEM((SSD_GROUPS, SSD_STATE, 512), F32), pltpu.VMEM((CHUNK, D_SSD), F32), pltpu.VMEM((CHUNK, D_SSD), F32)]
    return _pcall(name, body, nc, ins, outs, scratch=scratch, reverse=True)


def _gate_norm_fwd(name, ycore, proj, norm_w, tile):
    S = ycore.shape[0]
    n = S // tile

    def body(y_ref, z_ref, w_ref, o_ref):
        y2 = y_ref[...] * _silu(z_ref[...].astype(F32))
        wv = w_ref[...]
        for g in range(SSD_GROUPS):
            gs = slice(g * 512, (g + 1) * 512)
            seg = y2[:, gs]
            r = lax.rsqrt(jnp.mean(seg * seg, axis=-1, keepdims=True) + EPS)
            o_ref[:, gs] = (seg * r * wv[:, gs]).astype(BF16)

    return _pcall(name, body, n, [("row", ycore, tile, D_SSD, 0), ("row", proj, tile, *CB_Z), ("full", norm_w)],
                  [("row", S, D_SSD, BF16, tile, D_SSD, 0)])[0]


def _gate_norm_bwd(name, dout, ycore, proj, norm_w, dproj, tile):
    S = ycore.shape[0]
    n = S // tile

    def body(do_ref, y_ref, z_ref, w_ref, buf_ref, dz_ref, dy_ref, dw_ref):
        i = pl.program_id(0)
        yv = y_ref[...]
        zv = z_ref[...].astype(F32)
        sz = _silu(zv)
        y2 = yv * sz
        dov = do_ref[...].astype(F32)
        wv = w_ref[...]
        dws, dy2s = [], []
        for g in range(SSD_GROUPS):
            gs = slice(g * 512, (g + 1) * 512)
            seg = y2[:, gs]
            r = lax.rsqrt(jnp.mean(seg * seg, axis=-1, keepdims=True) + EPS)
            yn = seg * r
            dws.append(jnp.sum(dov[:, gs] * yn, axis=0, keepdims=True))
            dyn = dov[:, gs] * wv[:, gs]
            dy2s.append(r * (dyn - yn * jnp.mean(dyn * yn, axis=-1, keepdims=True)))
        dy2 = jnp.concatenate(dy2s, axis=1)
        dy_ref[...] = dy2 * sz
        dz_ref[...] = (dy2 * yv * _dsilu(zv)).astype(BF16)
        _acc(dw_ref, i, jnp.concatenate(dws, axis=1))

    ins = [("row", dout, tile, D_SSD, 0), ("row", ycore, tile, D_SSD, 0), ("row", proj, tile, *CB_Z), ("full", norm_w),
           ("any", dproj)]
    outs = [("row", S, N_PROJ, BF16, tile, *CB_Z), ("row", S, D_SSD, F32, tile, D_SSD, 0), ("full", (1, D_SSD), F32)]
    return _pcall(name, body, n, ins, outs, aliases={4: 0})


_XA_SCALE = XA_HEAD_DIM ** -0.5
_NT = (((1,), (1,)), ((), ()))
_TN = (((0,), (0,)), ((), ()))


def _xa_probs(qh, kh):
    s = lax.dot_general(qh, kh, _NT, preferred_element_type=F32) * _XA_SCALE
    s = s - jnp.max(s, axis=-1, keepdims=True)
    p = jnp.exp(s)
    return p / jnp.sum(p, axis=-1, keepdims=True)


def _xa_fwd(name, proj, kv, tile):
    S = proj.shape[0]
    n = S // tile
    Dh = XA_HEAD_DIM

    def body(q_ref, kv_ref, o_ref):
        for hd in range(XA_HEADS):
            qh = q_ref[:, hd * Dh:(hd + 1) * Dh]
            kh = kv_ref[:, hd * Dh:(hd + 1) * Dh]
            vh = kv_ref[:, D_MODEL + hd * Dh:D_MODEL + (hd + 1) * Dh]
            p = _xa_probs(qh, kh)
            o_ref[:, hd * Dh:(hd + 1) * Dh] = jnp.dot(p.astype(BF16), vh, preferred_element_type=F32).astype(BF16)

    return _pcall(name, body, n, [("row", proj, tile, *CB_XA_Q), ("full", kv)],
                  [("row", S, D_MODEL, BF16, tile, D_MODEL, 0)])[0]


def _xa_bwd(name, proj, kv, dout, dproj, tile):
    S = proj.shape[0]
    n = S // tile
    Dh = XA_HEAD_DIM

    def body(q_ref, kv_ref, do_ref, buf_ref, dq_ref, dkv_ref):
        i = pl.program_id(0)
        for hd in range(XA_HEADS):
            ks_ = slice(hd * Dh, (hd + 1) * Dh)
            vs_ = slice(D_MODEL + hd * Dh, D_MODEL + (hd + 1) * Dh)
            qh = q_ref[:, ks_]
            kh = kv_ref[:, ks_]
            vh = kv_ref[:, vs_]
            doh = do_ref[:, ks_].astype(BF16)
            p = _xa_probs(qh, kh)
            pb = p.astype(BF16)
            dp = lax.dot_general(doh, vh, _NT, preferred_element_type=F32)
            dv = lax.dot_general(pb, doh, _TN, preferred_element_type=F32)
            ds = (p * (dp - jnp.sum(dp * p, axis=-1, keepdims=True)) * _XA_SCALE).astype(BF16)
            dq_ref[:, ks_] = jnp.dot(ds, kh, preferred_element_type=F32).astype(BF16)
            dk = lax.dot_general(ds, qh, _TN, preferred_element_type=F32)

            @pl.when(i == 0)
            def _(dk=dk, dv=dv, ks_=ks_, vs_=vs_):
                dkv_ref[:, ks_] = dk
                dkv_ref[:, vs_] = dv

            @pl.when(i > 0)
            def _(dk=dk, dv=dv, ks_=ks_, vs_=vs_):
                dkv_ref[:, ks_] += dk
                dkv_ref[:, vs_] += dv

    ins = [("row", proj, tile, *CB_XA_Q), ("full", kv), ("row", dout, tile, D_MODEL, 0), ("any", dproj)]
    outs = [("row", S, N_PROJ, BF16, tile, *CB_XA_Q), ("full", (N_MEM, 2 * D_MODEL), F32)]
    return _pcall(name, body, n, ins, outs, aliases={3: 0})


def _adamw(name, w, g, m, v):
    R, C = w.shape
    tile = _pick(R, [t for t in (256, 128, 64, 32, 16, 8) if t * C <= 128 * 2048])
    n = R // tile
    bc1 = 1.0 - ADAM_B1 ** ADAM_STEP
    bc2 = 1.0 - ADAM_B2 ** ADAM_STEP

    def body(w_ref, g_ref, m_ref, v_ref, d_ref, nm_ref, nv_ref):
        gv = g_ref[...]
        mn = ADAM_B1 * m_ref[...] + (1.0 - ADAM_B1) * gv
        vn = ADAM_B2 * v_ref[...] + (1.0 - ADAM_B2) * (gv * gv)
        nm_ref[...] = mn
        nv_ref[...] = vn
        d_ref[...] = -ADAM_LR * ((mn / bc1) / (jnp.sqrt(vn / bc2) + ADAM_EPS) + ADAM_WD * w_ref[...])

    ins = [("row", a, tile, C, 0) for a in (w, g, m, v)]
    return _pcall(name, body, n, ins, [("row", R, C, F32, tile, C, 0)] * 3)


def _axpy(name, coef, a, b, tile):
    S, C = a.shape
    n = S // tile

    def body(a_ref, b_ref, o_ref):
        o_ref[...] = coef * a_ref[...].astype(F32) + b_ref[...].astype(F32)

    return _pcall(name, body, n, [("row", a, tile, C, 0), ("row", b, tile, C, 0)], [("row", S, C, F32, tile, C, 0)])[0]


def _sum_terms(name, terms, out_dtype):
    R, C = terms[0].shape
    tile = _pick(R, (512, 256, 128, 64, 32, 16, 8))
    n = R // tile
    nt = len(terms)

    def body(*refs):
        vals = [r[...].astype(F32) for r in refs[:nt]]
        while len(vals) > 1:
            vals = [vals[k] + vals[k + 1] for k in range(0, len(vals), 2)]
        refs[nt][...] = vals[0].astype(out_dtype)

    return _pcall(name, body, n, [("row", t, tile, C, 0) for t in terms], [("row", R, C, out_dtype, tile, C, 0)])[0]


def _me():
    return lax.axis_index("x"), lax.axis_index("y"), lax.axis_index("c")


def _other_chips(x, y):
    return [(1 - x, y), (x, 1 - y), (1 - x, 1 - y)]


_ANY = pl.BlockSpec(memory_space=pl.ANY)


def _gather_chips(name, src):
    def body(src_ref, out_ref, send_sems, recv_sems, local_sem):
        x, y, c = _me()
        j = 2 * x + y
        chips = _other_chips(x, y)
        mine = pltpu.make_async_copy(src_ref, out_ref.at[j], local_sem)
        mine.start()

        def copy(k, chip_j, half, to, src=None):
            dst = out_ref.at[chip_j, half]
            return pltpu.make_async_remote_copy(src_ref=dst if src is None else src, dst_ref=dst,
                                                send_sem=send_sems.at[k], recv_sem=recv_sems.at[k],
                                                device_id=to, device_id_type=MESH)

        first = [copy(k, j, c, (cx, cy, c), src=src_ref.at[c]) for k, (cx, cy) in enumerate(chips)]
        for cp in first:
            cp.start()
        passed = [copy(3 + k, 2 * cx + cy, c, (x, y, 1 - c)) for k, (cx, cy) in enumerate(chips)]
        for k, (cx, cy) in enumerate(chips):
            copy(k, 2 * cx + cy, c, (x, y, c)).wait_recv()
            passed[k].start()
        for k, (cx, cy) in enumerate(chips):
            copy(3 + k, 2 * cx + cy, 1 - c, (x, y, c)).wait_recv()
        for cp in first + passed:
            cp.wait_send()
        mine.wait()

    return pl.pallas_call(
        body, name=name, in_specs=[_ANY], out_specs=_ANY,
        out_shape=jax.ShapeDtypeStruct((4,) + src.shape, src.dtype),
        scratch_shapes=[pltpu.SemaphoreType.DMA((6,)), pltpu.SemaphoreType.DMA((6,)), pltpu.SemaphoreType.DMA],
    )(src)


def _swap_sibling(name, src):
    def body(src_ref, out_ref, send_sem, recv_sem):
        x, y, c = _me()
        cp = pltpu.make_async_remote_copy(src_ref=src_ref, dst_ref=out_ref, send_sem=send_sem, recv_sem=recv_sem,
                                          device_id=(x, y, 1 - c), device_id_type=MESH)
        cp.start()
        cp.wait()

    return pl.pallas_call(
        body, name=name, in_specs=[_ANY], out_specs=_ANY, out_shape=jax.ShapeDtypeStruct(src.shape, src.dtype),
        scratch_shapes=[pltpu.SemaphoreType.DMA, pltpu.SemaphoreType.DMA],
    )(src)


def _send_chips(name, src, per_chip):
    shape = src.shape[1:] if per_chip else src.shape

    def body(src_ref, out_ref, send_sems, recv_sems):
        x, y, c = _me()
        cps = []
        for k, (cx, cy) in enumerate(_other_chips(x, y)):
            s = src_ref.at[2 * cx + cy] if per_chip else src_ref
            cps.append(pltpu.make_async_remote_copy(src_ref=s, dst_ref=out_ref.at[k], send_sem=send_sems.at[k],
                                                    recv_sem=recv_sems.at[k], device_id=(cx, cy, c), device_id_type=MESH))
        for cp in cps:
            cp.start()
        for cp in cps:
            cp.wait()

    return pl.pallas_call(
        body, name=name, in_specs=[_ANY], out_specs=_ANY, out_shape=jax.ShapeDtypeStruct((3,) + shape, src.dtype),
        scratch_shapes=[pltpu.SemaphoreType.DMA((3,)), pltpu.SemaphoreType.DMA((3,))],
    )(src)


def _swap_other_halves(name, src):
    n4, _, R, C = src.shape

    def body(src_ref, out_ref, send_sems, recv_sems):
        x, y, c = _me()
        cps = [pltpu.make_async_remote_copy(src_ref=src_ref.at[d, 1 - c], dst_ref=out_ref.at[d], send_sem=send_sems.at[d],
                                            recv_sem=recv_sems.at[d], device_id=(x, y, 1 - c), device_id_type=MESH)
               for d in range(n4)]
        for cp in cps:
            cp.start()
        for cp in cps:
            cp.wait()

    return pl.pallas_call(
        body, name=name, in_specs=[_ANY], out_specs=_ANY, out_shape=jax.ShapeDtypeStruct((n4, R, C), src.dtype),
        scratch_shapes=[pltpu.SemaphoreType.DMA((n4,)), pltpu.SemaphoreType.DMA((n4,))],
    )(src)


LANES = 1024
_BIG = (("w_in", "col"), ("ffn_w_in", "col"), ("mem_w_kv", "col"),
        ("w_br_lru", "row"), ("w_br_ssd", "row"), ("w_br_xa", "row"), ("w_out", "row"), ("ffn_w_down", "row"))
_SMALL_SHARDED = ("b_gate", "lru_conv_w", "ssd_conv_w")
_SMALL = ("b_gate", "lru_conv_w", "lru_conv_b", "lru_w_a", "lru_b_a", "lru_w_i", "lru_b_i", "lru_lambda",
          "ssd_conv_w", "ssd_conv_b", "ssd_dt_bias", "ssd_a_log", "ssd_d", "ssd_norm_w",
          "ln1_g", "ln1_b", "ln2_g", "ln2_b")
_W_NAMES = ("w_in", "b_gate", "lru_conv_w", "lru_conv_b", "lru_w_a", "lru_b_a", "lru_w_i", "lru_b_i", "lru_lambda",
            "ssd_conv_w", "ssd_conv_b", "ssd_dt_bias", "ssd_a_log", "ssd_d", "ssd_norm_w", "mem_w_kv", "w_br_lru",
            "w_br_ssd", "w_br_xa", "w_out", "ln1_g", "ln1_b", "ffn_w_in", "ffn_w_down", "ln2_g", "ln2_b")
_IN_ORDER = ((4096, 7168), (8224, 11296), (2048, 4096), (0, 1024), (1024, 2048), (7200, 8224))
_IN_DT = (7168, 7200)


def _flat_rows(parts, row_multiple):
    flat = jnp.concatenate([p.reshape(-1) for p in parts])
    rows = -(-flat.size // LANES)
    rows = -(-rows // row_multiple) * row_multiple
    return jnp.pad(flat, (0, rows * LANES - flat.size)).reshape(rows, LANES)


def _take_parts(flat, shapes):
    out, off = [], 0
    for shp in shapes:
        size = math.prod(shp)
        out.append(flat[off:off + size].reshape(shp))
        off += size
    return out


def _gather_weights(w):
    parts = [w[nm].astype(BF16) for nm, _ in _BIG]
    parts += [lax.bitcast_convert_type(w[nm], BF16) for nm in _SMALL_SHARDED]
    shapes = [p.shape for p in parts]
    packed = _flat_rows(parts, 2 * HALO)
    rows = packed.shape[0]
    got = _gather_chips("gather_weights", packed.reshape(2, rows // 2, LANES))
    got = got.reshape(4, rows * LANES)
    full = {}
    off = 0
    for (nm, kind), shp in zip(_BIG + tuple((s, "col") for s in _SMALL_SHARDED), shapes):
        size = math.prod(shp)
        g = got[:, off:off + size].reshape((4,) + shp)
        off += size
        if nm in _SMALL_SHARDED:
            g = lax.bitcast_convert_type(g, F32)
            shp = shp[:-1]
        if kind == "col":
            g = jnp.moveaxis(g, 0, 2)
            full[nm] = g.reshape(shp[0], shp[1], 4 * shp[2])
        else:
            g = jnp.moveaxis(g, 0, 1)
            full[nm] = g.reshape(shp[0], 4 * shp[1], shp[2])
    return full


def _big_grad_pack(grads):
    per_chip = []
    for nm, kind in _BIG:
        g = grads[nm]
        L = g.shape[0]
        if kind == "col":
            g = jnp.moveaxis(g.reshape(L, g.shape[1], 4, g.shape[2] // 4), 2, 0)
        else:
            g = jnp.moveaxis(g.reshape(L, 4, g.shape[1] // 4, g.shape[2]), 1, 0)
        per_chip.append(g.reshape(4, -1))
    flat = jnp.concatenate(per_chip, axis=1)
    rows = -(-flat.shape[1] // LANES)
    rows = -(-rows // (2 * HALO)) * (2 * HALO)
    flat = jnp.pad(flat, ((0, 0), (0, rows * LANES - flat.shape[1])))
    return flat.reshape(4, 2, rows // 2, LANES)


def _reduce_big_grads(grads, shard_shapes):
    x, y, c = _me()
    j = 2 * x + y
    pack = _big_grad_pack(grads)
    _, _, rh, _ = pack.shape
    theirs = _swap_other_halves("grads_pair_swap", pack)
    mine = lax.dynamic_index_in_dim(pack, c, axis=1, keepdims=False)
    pair = _sum_terms("grads_pair_sum", [mine.reshape(4 * rh, LANES), theirs.reshape(4 * rh, LANES)], F32)
    sent = _send_chips("grads_chip_send", pair.astype(BF16).reshape(4, rh, LANES), True)
    own = lax.dynamic_index_in_dim(pair.reshape(4, rh, LANES), j, axis=0, keepdims=False)
    half = _sum_terms("grads_chip_sum", [own, sent[2], sent[0], sent[1]], F32)
    other = _swap_sibling("grads_half_swap", half)
    both = jnp.where(c == 0, jnp.concatenate([half, other], axis=0), jnp.concatenate([other, half], axis=0))
    return dict(zip([nm for nm, _ in _BIG], _take_parts(both.reshape(-1), shard_shapes)))


def _reduce_small_grads(parts):
    shapes = [p.shape for p in parts]
    buf = _flat_rows(parts, 8)
    sib = _swap_sibling("small_pair_swap", buf)
    pair = _sum_terms("small_pair_sum", [buf, sib], F32)
    got = _send_chips("small_chip_send", pair, False)
    total = _sum_terms("small_chip_sum", [pair, got[2], got[0], got[1]], F32)
    return _take_parts(total.reshape(-1), shapes)


def _layer_fwd(xin, mem2, W, P, consts):
    S = xin.shape[0]
    T, TX = min(256, S), min(512, S)
    sv = {"xin": xin}
    proj = _mm("proj", xin, W["w_re"], "nn", BF16)
    dt_raw = _mm("proj_dt", xin, W["w_dt"], "nn", F32)
    xc = _conv_fwd("lru_conv", proj, CB_LRU_X, P["lru_conv_w"], P["lru_conv_b"], False, F32, T)
    y_lru, h = _lru_fwd("lru_scan", xc, proj, P["lru_w_a"], P["lru_w_i"], P["lru_b_a"], P["lru_b_i"], P["lru_lambda"], T)
    xact = _conv_fwd("ssd_conv", proj, CB_XBC, P["ssd_conv_w"], P["ssd_conv_b"], True, BF16, T)
    ycore, prev = _ssd_fwd("ssd_scan", xact, dt_raw, P["dt_bias"], P["a_neg"], P["d_exp"], consts)
    y_ssd = _gate_norm_fwd("ssd_norm", ycore, proj, P["ssd_norm_w"], T)
    kv = _mm("mem_kv", mem2, W["w_kv"], "nn", BF16)
    y_xa = _xa_fwd("xattn", proj, kv, TX)
    p_l = _mm("br_lru", y_lru, W["w_l"], "nn", BF16)
    p_s = _mm("br_ssd", y_ssd, W["w_s"], "nn", BF16)
    p_x = _mm("br_xa", y_xa, W["w_x"], "nn", BF16)
    merged = _merge_fwd("merge", proj, P["b_gate"], p_l, p_s, p_x, T)
    mix = _mm("mix_out", merged, W["w_o"], "nn", F32)
    x1, xh1, rs1 = _ln_fwd("ln_fwd", xin, mix, P["ln1_g"], P["ln1_b"], T)
    gu = _mm("ffn_in", x1, W["w_fi"], "nn", BF16)
    hmid = _swiglu_fwd("swiglu", gu, T)
    f = _mm("ffn_down", hmid, W["w_fd"], "nn", F32)
    x2, xh2, rs2 = _ln_fwd("ln_fwd", x1, f, P["ln2_g"], P["ln2_b"], T)
    sv.update(proj=proj, dt_raw=dt_raw, xc=xc, h=h, y_lru=y_lru, xact=xact, ycore=ycore, prev=prev, y_ssd=y_ssd, kv=kv,
              y_xa=y_xa, p_l=p_l, p_s=p_s, p_x=p_x, merged=merged, x1=x1, xh1=xh1, rs1=rs1, gu=gu, hmid=hmid,
              xh2=xh2, rs2=rs2)
    return x2, sv


def _layer_bwd(dys, coefs, sv, mem2, W, P, consts):
    S = sv["xin"].shape[0]
    T, TX = min(256, S), min(512, S)
    proj = sv["proj"]
    g = {}
    dz2, g["ln2_g"], g["ln2_b"] = _ln_bwd("ln_bwd_%d" % len(dys), dys, coefs, sv["xh2"], sv["rs2"], P["ln2_g"], T)
    dhmid = _mm("d_hmid", dz2, W["w_fd"], "nt", BF16)
    g["ffn_w_down"] = _mm("dw_ffn_down", sv["hmid"], dz2, "tn", F32)
    dgu = _swiglu_bwd("swiglu_bwd", sv["gu"], dhmid, T)
    dx1f = _mm("d_x1", dgu, W["w_fi"], "nt", F32)
    g["ffn_w_in"] = _mm("dw_ffn_in", sv["x1"], dgu, "tn", F32)
    dz1, g["ln1_g"], g["ln1_b"] = _ln_bwd("ln_bwd_2", [dz2, dx1f], [ALPHA, 1.0], sv["xh1"], sv["rs1"], P["ln1_g"], T)
    dmerged = _mm("d_merged", dz1, W["w_o"], "nt", BF16)
    g["w_out"] = _mm("dw_out", sv["merged"], dz1, "tn", F32)
    dproj, dpl, dps, dpx, dbg0, dbg1, dbg2 = _merge_bwd("merge_bwd", proj, P["b_gate"], sv["p_l"], sv["p_s"], sv["p_x"],
                                                         dmerged, T)
    g["b_gate"] = jnp.concatenate([dbg0, dbg1, dbg2], axis=0)
    dy_lru = _mm("d_ylru", dpl, W["w_l"], "nt", BF16)
    g["w_br_lru"] = _mm("dw_br_lru", sv["y_lru"], dpl, "tn", F32)
    dy_ssd = _mm("d_yssd", dps, W["w_s"], "nt", BF16)
    g["w_br_ssd"] = _mm("dw_br_ssd", sv["y_ssd"], dps, "tn", F32)
    dy_xa = _mm("d_yxa", dpx, W["w_x"], "nt", BF16)
    g["w_br_xa"] = _mm("dw_br_xa", sv["y_xa"], dpx, "tn", F32)
    dproj, dkv = _xa_bwd("xattn_bwd", proj, sv["kv"], dy_xa, dproj, TX)
    g["mem_w_kv"] = _mm("dw_kv", mem2, dkv, "tn", F32)
    dproj, dycore, g["ssd_norm_w"] = _gate_norm_bwd("ssd_norm_bwd", dy_ssd, sv["ycore"], proj, P["ssd_norm_w"], dproj, T)
    dxact, ddt, d_a, g_dtb, d_dexp = _ssd_bwd("ssd_scan_bwd", sv["xact"], sv["dt_raw"], P["dt_bias"], P["a_neg"], P["d_exp"],
                                              sv["prev"], dycore, consts)
    g["ssd_dt_bias"] = g_dtb[:, :SSD_HEADS]
    g["ssd_a_log"] = d_a[:, :SSD_HEADS] * P["a_neg"][:, :SSD_HEADS]
    g["ssd_d"] = jnp.sum(d_dexp.reshape(SSD_HEADS, SSD_HEAD_DIM), axis=-1)
    dpre = _conv_act_bwd("ssd_conv_act_bwd", proj, CB_XBC, P["ssd_conv_w"], P["ssd_conv_b"], dxact, T)
    dproj, w0, w1, w2, w3, g["ssd_conv_b"] = _conv_bwd("ssd_conv_bwd", proj, CB_XBC, P["ssd_conv_w"], dpre, dproj, T)
    g["ssd_conv_w"] = jnp.concatenate([w0, w1, w2, w3], axis=0)
    dproj, dxc, g["lru_w_a"], g["lru_w_i"], g["lru_b_a"], g["lru_b_i"], g["lru_lambda"] = _lru_bwd(
        "lru_scan_bwd", dy_lru, sv["xc"], proj, sv["h"], P["lru_w_a"], P["lru_w_i"], P["lru_b_a"], P["lru_b_i"],
        P["lru_lambda"], dproj, T)
    dproj, w0, w1, w2, w3, g["lru_conv_b"] = _conv_bwd("lru_conv_bwd", proj, CB_LRU_X, P["lru_conv_w"], dxc, dproj, T)
    g["lru_conv_w"] = jnp.concatenate([w0, w1, w2, w3], axis=0)
    xin = sv["xin"]
    dw_re = _mm("dw_in", xin, dproj, "tn", F32)
    dw_dt = _mm("dw_in_dt", xin, ddt, "tn", F32)
    pieces = {rng: dw_re[:, off:off + rng[1] - rng[0]]
              for rng, off in zip(_IN_ORDER, (0, 3072, 6144, 8192, 9216, 10240))}
    pieces[_IN_DT] = dw_dt[:, :SSD_HEADS]
    g["w_in"] = jnp.concatenate([pieces[k] for k in sorted(pieces)], axis=1)
    dxp = _mm("d_xin", dproj, W["w_re"], "nt", F32)
    dxs = _mm("d_xin_dt", ddt, W["w_dt"], "nt", F32, add=dxp)
    return [dz1, dxs], [ALPHA, 1.0], g


def _step(a):
    x2d, mem2, target = a["x"][0], a["mem"][0], a["loss_target"][0]
    S = x2d.shape[0]
    T = min(256, S)
    xi, yi, ci = _me()
    j = 2 * xi + yi
    w = {nm: a[nm] for nm in _W_NAMES}
    full = _gather_weights(w)
    consts = _ssd_consts()
    row = lambda v: v.reshape(1, -1)
    Ws, Ps = [], []
    for l in range(DEPTH):
        w_in = full["w_in"][l]
        w_re = jnp.concatenate([w_in[:, lo:hi] for lo, hi in _IN_ORDER], axis=1)
        w_dt = jnp.pad(w_in[:, _IN_DT[0]:_IN_DT[1]], ((0, 0), (0, DT_PAD - SSD_HEADS)))
        Ws.append(dict(w_re=w_re, w_dt=w_dt, w_fi=full["ffn_w_in"][l], w_kv=full["mem_w_kv"][l], w_l=full["w_br_lru"][l],
                       w_s=full["w_br_ssd"][l], w_x=full["w_br_xa"][l], w_o=full["w_out"][l], w_fd=full["ffn_w_down"][l]))
        pad_h = lambda v: jnp.pad(v.reshape(1, -1), ((0, 0), (0, DT_PAD - SSD_HEADS)))
        Ps.append(dict(
            b_gate=full["b_gate"][l], lru_conv_w=full["lru_conv_w"][l], ssd_conv_w=full["ssd_conv_w"][l],
            lru_conv_b=row(w["lru_conv_b"][l]), lru_w_a=w["lru_w_a"][l].astype(BF16), lru_w_i=w["lru_w_i"][l].astype(BF16),
            lru_b_a=row(w["lru_b_a"][l]), lru_b_i=row(w["lru_b_i"][l]), lru_lambda=row(w["lru_lambda"][l]),
            ssd_conv_b=row(w["ssd_conv_b"][l]), dt_bias=pad_h(w["ssd_dt_bias"][l]), a_neg=pad_h(-jnp.exp(w["ssd_a_log"][l])),
            d_exp=jnp.broadcast_to(w["ssd_d"][l][:, None], (SSD_HEADS, SSD_HEAD_DIM)).reshape(1, D_SSD),
            ssd_norm_w=row(w["ssd_norm_w"][l]), ln1_g=row(w["ln1_g"][l]), ln1_b=row(w["ln1_b"][l]),
            ln2_g=row(w["ln2_g"][l]), ln2_b=row(w["ln2_b"][l])))

    saved = []
    xcur = x2d
    for l in range(DEPTH):
        xcur, sv = _layer_fwd(xcur, mem2, Ws[l], Ps[l], consts)
        saved.append(sv)
    dy, loss_part = _loss_fwd_bwd("loss", xcur, target, T)
    loss = lax.psum(loss_part[0, 0], ("x", "y", "c"))

    dys, coefs = [dy], [1.0]
    layer_grads = [None] * DEPTH
    for l in reversed(range(DEPTH)):
        dys, coefs, layer_grads[l] = _layer_bwd(dys, coefs, saved[l], mem2, Ws[l], Ps[l], consts)
    grad_x = _axpy("grad_x", coefs[0], dys[0], dys[1], T)[None]

    stacked = {nm: jnp.stack([layer_grads[l][nm] for l in range(DEPTH)]) for nm in layer_grads[0]}
    big = _reduce_big_grads(stacked, [w[nm].shape for nm, _ in _BIG])
    small_parts = [stacked[nm].reshape((DEPTH,) + tuple(sh)) for nm, sh in
                   ((nm, (3, D_MODEL) if nm == "b_gate" else (4, D_MODEL) if nm == "lru_conv_w" else
                     (4, D_XBC) if nm == "ssd_conv_w" else w[nm].shape[1:]) for nm in _SMALL)]
    small = dict(zip(_SMALL, _reduce_small_grads(small_parts)))
    for nm in _SMALL_SHARDED:
        cs = w[nm].shape[2]
        small[nm] = lax.dynamic_slice_in_dim(small[nm], j * cs, cs, axis=2)
    grads = {**big, **small}

    delta, new_m, new_v = {}, {}, {}
    for nm, _ in _BIG:
        shp = w[nm].shape
        two = lambda v: v.reshape(shp[0] * shp[1], shp[2])
        d_, m_, v_ = _adamw("adamw_" + nm, two(w[nm]), two(grads[nm]), two(a["m_" + nm]), two(a["v_" + nm]))
        delta[nm], new_m[nm], new_v[nm] = d_.reshape(shp), m_.reshape(shp), v_.reshape(shp)
    shapes = [w[nm].shape for nm in _SMALL]
    packs = [_flat_rows([src[nm] for nm in _SMALL], 8) for src in
             (w, grads, {nm: a["m_" + nm] for nm in _SMALL}, {nm: a["v_" + nm] for nm in _SMALL})]
    d_, m_, v_ = _adamw("adamw_small", *packs)
    for dst, buf in ((delta, d_), (new_m, m_), (new_v, v_)):
        dst.update(zip(_SMALL, _take_parts(buf.reshape(-1), shapes)))

    outs = [loss, grad_x]
    for group in (grads, delta, new_m, new_v):
        outs += [group[nm] for nm in _W_NAMES]
    return tuple(outs)


def kernel(x, mem, w_in, b_gate, lru_conv_w, lru_conv_b, lru_w_a, lru_b_a, lru_w_i, lru_b_i, lru_lambda, ssd_conv_w, ssd_conv_b, ssd_dt_bias, ssd_a_log, ssd_d, ssd_norm_w, mem_w_kv, w_br_lru, w_br_ssd, w_br_xa, w_out, ln1_g, ln1_b, ffn_w_in, ffn_w_down, ln2_g, ln2_b, loss_target, m_w_in, m_b_gate, m_lru_conv_w, m_lru_conv_b, m_lru_w_a, m_lru_b_a, m_lru_w_i, m_lru_b_i, m_lru_lambda, m_ssd_conv_w, m_ssd_conv_b, m_ssd_dt_bias, m_ssd_a_log, m_ssd_d, m_ssd_norm_w, m_mem_w_kv, m_w_br_lru, m_w_br_ssd, m_w_br_xa, m_w_out, m_ln1_g, m_ln1_b, m_ffn_w_in, m_ffn_w_down, m_ln2_g, m_ln2_b, v_w_in, v_b_gate, v_lru_conv_w, v_lru_conv_b, v_lru_w_a, v_lru_b_a, v_lru_w_i, v_lru_b_i, v_lru_lambda, v_ssd_conv_w, v_ssd_conv_b, v_ssd_dt_bias, v_ssd_a_log, v_ssd_d, v_ssd_norm_w, v_mem_w_kv, v_w_br_lru, v_w_br_ssd, v_w_br_xa, v_w_out, v_ln1_g, v_ln1_b, v_ffn_w_in, v_ffn_w_down, v_ln2_g, v_ln2_b):
    return _step(dict(locals()))
```

```python
import functools
import math

import jax
import jax.numpy as jnp
from jax import lax
from jax.experimental import pallas as pl
from jax.experimental.pallas import tpu as pltpu

F32, BF16 = jnp.float32, jnp.bfloat16
MESH = pl.DeviceIdType.MESH
VMEM_LIMIT_BYTES = 56 * 2**20
HALO = 16

D_MODEL = 1024
DEPTH = 2
CHUNK = 64
N_MEM = 256
LRU_BLOCKS = 8
LRU_BLOCK = 128
LRU_C = 8.0
D_SSD = 2048
SSD_HEADS = 32
SSD_HEAD_DIM = 64
SSD_GROUPS = 4
SSD_STATE = 128
D_BC = SSD_GROUPS * SSD_STATE
D_XBC = D_SSD + 2 * D_BC
XA_HEADS = 4
XA_HEAD_DIM = 256
D_FF = 2816
ALPHA = (2 * DEPTH) ** 0.25
EPS = 1e-5
N_IN = 11296
N_PROJ = 11264
DT_PAD = 128

ADAM_LR, ADAM_B1, ADAM_B2, ADAM_EPS, ADAM_WD, ADAM_STEP = 0.001, 0.9, 0.999, 1e-08, 0.01, 10

CB_XBC = (3072, 0)
CB_XS, CB_BM, CB_CM = (2048, 0), (512, 4), (512, 5)
CB_LOGITS = (3072, 1)
CB_G0, CB_G1, CB_G2 = (1024, 3), (1024, 4), (1024, 5)
CB_Z = (2048, 3)
CB_LRU_X, CB_LRU_GATE, CB_XA_Q = (1024, 8), (1024, 9), (1024, 10)


def _cparams(sem):
    return pltpu.CompilerParams(dimension_semantics=sem, vmem_limit_bytes=VMEM_LIMIT_BYTES)


def _pcall(name, body, n, ins, outs, scratch=(), reverse=False, aliases=None):
    def ridx(i):
        return (n - 1 - i) if reverse else i

    in_specs, args = [], []
    for sp in ins:
        kind, arr = sp[0], sp[1]
        if kind == "row":
            _, _, tile, width, cb = sp
            in_specs.append(pl.BlockSpec((tile, width), lambda i, cb=cb: (ridx(i), cb)))
        elif kind == "prev":
            _, _, tile, width, cb = sp
            t = tile // HALO
            in_specs.append(pl.BlockSpec((HALO, width), lambda i, cb=cb, t=t: (jnp.maximum(ridx(i) * t - 1, 0), cb)))
        elif kind == "next":
            _, _, tile, width, cb = sp
            t = tile // HALO
            last = arr.shape[0] // HALO - 1
            in_specs.append(pl.BlockSpec((HALO, width), lambda i, cb=cb, t=t, last=last: (jnp.minimum((ridx(i) + 1) * t, last), cb)))
        elif kind == "lead":
            nd = arr.ndim
            in_specs.append(pl.BlockSpec((1,) + arr.shape[1:], lambda i, nd=nd: (ridx(i),) + (0,) * (nd - 1)))
        elif kind == "full":
            nd = arr.ndim
            in_specs.append(pl.BlockSpec(arr.shape, lambda i, nd=nd: (0,) * nd))
        elif kind == "any":
            in_specs.append(pl.BlockSpec(memory_space=pl.ANY))
        else:
            raise ValueError(kind)
        args.append(arr)
    out_specs, out_shape = [], []
    for sp in outs:
        kind = sp[0]
        if kind == "row":
            _, rows, cols, dtype, tile, width, cb = sp
            out_shape.append(jax.ShapeDtypeStruct((rows, cols), dtype))
            out_specs.append(pl.BlockSpec((tile, width), lambda i, cb=cb: (ridx(i), cb)))
        elif kind == "lead":
            _, shape, dtype = sp
            nd = len(shape)
            out_shape.append(jax.ShapeDtypeStruct(shape, dtype))
            out_specs.append(pl.BlockSpec((1,) + tuple(shape[1:]), lambda i, nd=nd: (ridx(i),) + (0,) * (nd - 1)))
        elif kind == "full":
            _, shape, dtype = sp
            nd = len(shape)
            out_shape.append(jax.ShapeDtypeStruct(shape, dtype))
            out_specs.append(pl.BlockSpec(tuple(shape), lambda i, nd=nd: (0,) * nd))
        else:
            raise ValueError(kind)
    res = pl.pallas_call(
        body, name=name, grid=(n,), in_specs=in_specs, out_specs=out_specs, out_shape=out_shape,
        scratch_shapes=list(scratch), input_output_aliases=aliases or {},
        compiler_params=_cparams(("arbitrary",)),
    )(*args)
    return res


def _pick(n, cands):
    for c in cands:
        if n % c == 0:
            return c
    return n


def _mm(name, a, b, mode, out_dtype, add=None):
    if mode == "nn":
        (M, K), (K2, N) = a.shape, b.shape
    elif mode == "nt":
        (M, K), (N, K2) = a.shape, b.shape
    else:
        (K, M), (K2, N) = a.shape, b.shape
    assert K == K2, (name, a.shape, b.shape)
    tm = _pick(M, (512, 256, 128))
    tn = _pick(N, (1024, 512, 256, 128))
    if mode == "tn":
        tk = _pick(K, (1024, 512, 256))
    else:
        tk = K if K <= 2816 else _pick(K, (1024, 512, 256, 128))
    nk = K // tk
    has_add = add is not None

    def body(*refs):
        if has_add:
            a_ref, b_ref, add_ref, o_ref, acc_ref = refs
        else:
            a_ref, b_ref, o_ref, acc_ref = refs
        k = pl.program_id(2)
        av = a_ref[...].astype(BF16)
        bv = b_ref[...].astype(BF16)
        if mode == "nn":
            p = jnp.dot(av, bv, preferred_element_type=F32)
        elif mode == "nt":
            p = lax.dot_general(av, bv, (((1,), (1,)), ((), ())), preferred_element_type=F32)
        else:
            p = lax.dot_general(av, bv, (((0,), (0,)), ((), ())), preferred_element_type=F32)

        def fin(v):
            if has_add:
                v = v + add_ref[...].astype(F32)
            o_ref[...] = v.astype(out_dtype)

        if nk == 1:
            fin(p)
        else:
            @pl.when(k == 0)
            def _():
                acc_ref[...] = p

            @pl.when(k > 0)
            def _():
                acc_ref[...] += p

            @pl.when(k == nk - 1)
            def _():
                fin(acc_ref[...])

    if mode == "nn":
        specs = [pl.BlockSpec((tm, tk), lambda i, j, k: (i, k)), pl.BlockSpec((tk, tn), lambda i, j, k: (k, j))]
    elif mode == "nt":
        specs = [pl.BlockSpec((tm, tk), lambda i, j, k: (i, k)), pl.BlockSpec((tn, tk), lambda i, j, k: (j, k))]
    else:
        specs = [pl.BlockSpec((tk, tm), lambda i, j, k: (k, i)), pl.BlockSpec((tk, tn), lambda i, j, k: (k, j))]
    args = [a, b]
    if has_add:
        specs.append(pl.BlockSpec((tm, tn), lambda i, j, k: (i, j)))
        args.append(add)
    acc_shape = (tm, tn) if nk > 1 else (8, 128)
    return pl.pallas_call(
        body, name=name, grid=(M // tm, N // tn, nk), in_specs=specs,
        out_specs=pl.BlockSpec((tm, tn), lambda i, j, k: (i, j)),
        out_shape=jax.ShapeDtypeStruct((M, N), out_dtype),
        scratch_shapes=[pltpu.VMEM(acc_shape, F32)],
        compiler_params=_cparams(("parallel", "parallel", "arbitrary")),
    )(*args)


def _sigmoid(x):
    return 1.0 / (1.0 + jnp.exp(-x))


def _silu(x):
    return x * _sigmoid(x)


def _dsilu(x):
    s = _sigmoid(x)
    return s * (1.0 + x * (1.0 - s))


def _softplus(x):
    return jnp.maximum(x, 0.0) + jnp.log(1.0 + jnp.exp(-jnp.abs(x)))


_GELU_C = math.sqrt(2.0 / math.pi)


def _gelu(x):
    return 0.5 * x * (1.0 + jnp.tanh(_GELU_C * (x + 0.044715 * x * x * x)))


def _dgelu(x):
    t = jnp.tanh(_GELU_C * (x + 0.044715 * x * x * x))
    return 0.5 * (1.0 + t) + 0.5 * x * (1.0 - t * t) * _GELU_C * (1.0 + 3.0 * 0.044715 * x * x)


def _acc(ref, i, val):
    @pl.when(i == 0)
    def _():
        ref[...] = val

    @pl.when(i > 0)
    def _():
        ref[...] += val


def _rows(shape):
    return lax.broadcasted_iota(jnp.int32, shape, 0)


def _shift_down(x, k, halo8, first):
    r = pltpu.roll(x, k, 0)
    h = pltpu.roll(halo8, k, 0)
    h = jnp.where(first, 0.0, h)
    head = jnp.where(_rows(h.shape) < k, h, r[:8])
    return jnp.concatenate([head, r[8:]], axis=0)


def _shift_up(x, k, halo8, last):
    T = x.shape[0]
    r = pltpu.roll(x, T - k, 0)
    h = pltpu.roll(halo8, 8 - k, 0)
    h = jnp.where(last, 0.0, h)
    tail = jnp.where(_rows(h.shape) >= 8 - k, h, r[T - 8:])
    return jnp.concatenate([r[:T - 8], tail], axis=0)


def _ln_fwd(name, a, b, g, beta, tile):
    S, Dm = a.shape
    n = S // tile

    def body(a_ref, b_ref, g_ref, be_ref, y_ref, xh_ref, rs_ref):
        z = ALPHA * a_ref[...] + b_ref[...].astype(F32)
        mu = jnp.mean(z, axis=-1, keepdims=True)
        zc = z - mu
        var = jnp.mean(zc * zc, axis=-1, keepdims=True)
        rstd = lax.rsqrt(var + EPS)
        xh = zc * rstd
        y_ref[...] = xh * g_ref[...] + be_ref[...]
        xh_ref[...] = xh
        rs_ref[...] = rstd

    return _pcall(name, body, n,
                  [("row", a, tile, Dm, 0), ("row", b, tile, Dm, 0), ("full", g), ("full", beta)],
                  [("row", S, Dm, F32, tile, Dm, 0), ("row", S, Dm, F32, tile, Dm, 0), ("row", S, 1, F32, tile, 1, 0)])


def _ln_bwd(name, dys, coefs, xh, rstd, g, tile):
    S, Dm = xh.shape
    n = S // tile
    nd = len(dys)

    def body(*refs):
        dy_refs = refs[:nd]
        xh_ref, rs_ref, g_ref, dz_ref, dg_ref, db_ref = refs[nd:]
        i = pl.program_id(0)
        dy = coefs[0] * dy_refs[0][...].astype(F32)
        for k in range(1, nd):
            dy = dy + coefs[k] * dy_refs[k][...].astype(F32)
        xh_v = xh_ref[...]
        dxh = dy * g_ref[...]
        m1 = jnp.mean(dxh, axis=-1, keepdims=True)
        m2 = jnp.mean(dxh * xh_v, axis=-1, keepdims=True)
        dz_ref[...] = rs_ref[...] * (dxh - m1 - xh_v * m2)
        _acc(dg_ref, i, jnp.sum(dy * xh_v, axis=0, keepdims=True))
        _acc(db_ref, i, jnp.sum(dy, axis=0, keepdims=True))

    ins = [("row", d, tile, Dm, 0) for d in dys]
    ins += [("row", xh, tile, Dm, 0), ("row", rstd, tile, 1, 0), ("full", g)]
    return _pcall(name, body, n, ins,
                  [("row", S, Dm, F32, tile, Dm, 0), ("full", (1, Dm), F32), ("full", (1, Dm), F32)])


def _loss_fwd_bwd(name, y, target, tile):
    S, Dm = y.shape
    n = S // tile

    def body(y_ref, t_ref, dy_ref, l_ref):
        i = pl.program_id(0)
        err = y_ref[...] - t_ref[...]
        dy_ref[...] = err * (1.0 / Dm)
        part = jnp.sum(jnp.sum(err * err, axis=-1, keepdims=True), axis=0, keepdims=True) * (0.5 / Dm)
        _acc(l_ref, i, part)

    return _pcall(name, body, n, [("row", y, tile, Dm, 0), ("row", target, tile, Dm, 0)],
                  [("row", S, Dm, F32, tile, Dm, 0), ("full", (1, 1), F32)])


def _swiglu_fwd(name, gu, tile):
    S = gu.shape[0]
    n = S // tile

    def body(g_ref, u_ref, o_ref):
        o_ref[...] = (_silu(g_ref[...].astype(F32)) * u_ref[...].astype(F32)).astype(BF16)

    return _pcall(name, body, n, [("row", gu, tile, D_FF, 0), ("row", gu, tile, D_FF, 1)],
                  [("row", S, D_FF, BF16, tile, D_FF, 0)])[0]


def _swiglu_bwd(name, gu, dh, tile):
    S = gu.shape[0]
    n = S // tile

    def body(g_ref, u_ref, dh_ref, o_ref):
        gv = g_ref[...].astype(F32)
        uv = u_ref[...].astype(F32)
        dv = dh_ref[...].astype(F32)
        dg = dv * uv * _dsilu(gv)
        du = dv * _silu(gv)
        o_ref[...] = jnp.concatenate([dg, du], axis=1).astype(BF16)

    return _pcall(name, body, n, [("row", gu, tile, D_FF, 0), ("row", gu, tile, D_FF, 1), ("row", dh, tile, D_FF, 0)],
                  [("row", S, 2 * D_FF, BF16, tile, 2 * D_FF, 0)])[0]


def _merge_fwd(name, proj, b_gate, pl_, ps_, px_, tile):
    S = proj.shape[0]
    n = S // tile
    Dm = D_MODEL

    def body(l0, l1, l2, bg, p0, p1, p2, o_ref):
        bgv = bg[...]
        acc = _sigmoid(l0[...].astype(F32) + bgv[0:1]) * p0[...].astype(F32)
        acc = acc + _sigmoid(l1[...].astype(F32) + bgv[1:2]) * p1[...].astype(F32)
        acc = acc + _sigmoid(l2[...].astype(F32) + bgv[2:3]) * p2[...].astype(F32)
        o_ref[...] = acc.astype(BF16)

    ins = [("row", proj, tile, *CB_G0), ("row", proj, tile, *CB_G1), ("row", proj, tile, *CB_G2), ("full", b_gate),
           ("row", pl_, tile, Dm, 0), ("row", ps_, tile, Dm, 0), ("row", px_, tile, Dm, 0)]
    return _pcall(name, body, n, ins, [("row", S, Dm, BF16, tile, Dm, 0)])[0]


def _merge_bwd(name, proj, b_gate, pl_, ps_, px_, dmerged, tile):
    S = proj.shape[0]
    n = S // tile
    Dm = D_MODEL

    def body(l0, l1, l2, bg, p0, p1, p2, dm_ref, dproj_ref, d0, d1, d2, db0, db1, db2):
        i = pl.program_id(0)
        bgv = bg[...]
        dm = dm_ref[...].astype(F32)
        dls = []
        for k, (lr, pr, dr, dbr) in enumerate(((l0, p0, d0, db0), (l1, p1, d1, db1), (l2, p2, d2, db2))):
            gk = _sigmoid(lr[...].astype(F32) + bgv[k:k + 1])
            dr[...] = (dm * gk).astype(BF16)
            dl = dm * pr[...].astype(F32) * gk * (1.0 - gk)
            _acc(dbr, i, jnp.sum(dl, axis=0, keepdims=True))
            dls.append(dl)
        dproj_ref[...] = jnp.concatenate(dls, axis=1).astype(BF16)

    ins = [("row", proj, tile, *CB_G0), ("row", proj, tile, *CB_G1), ("row", proj, tile, *CB_G2), ("full", b_gate),
           ("row", pl_, tile, Dm, 0), ("row", ps_, tile, Dm, 0), ("row", px_, tile, Dm, 0), ("row", dmerged, tile, Dm, 0)]
    outs = [("row", S, N_PROJ, BF16, tile, *CB_LOGITS)] + [("row", S, Dm, BF16, tile, Dm, 0)] * 3 + [("full", (1, Dm), F32)] * 3
    return _pcall(name, body, n, ins, outs)


def _conv_taps(xf, halo8, first, w):
    out = xf * w[3:4]
    for k in (1, 2, 3):
        out = out + _shift_down(xf, k, halo8, first) * w[3 - k:4 - k]
    return out


def _conv_fwd(name, src, cb, w, b, act, out_dtype, tile):
    S = src.shape[0]
    C = cb[0]
    n = S // tile

    def body(x_ref, p_ref, w_ref, b_ref, o_ref):
        i = pl.program_id(0)
        xf = x_ref[...].astype(F32)
        halo8 = p_ref[...].astype(F32)[HALO - 8:]
        pre = _conv_taps(xf, halo8, i == 0, w_ref[...]) + b_ref[...]
        o_ref[...] = (_silu(pre) if act else pre).astype(out_dtype)

    return _pcall(name, body, n, [("row", src, tile, *cb), ("prev", src, tile, *cb), ("full", w), ("full", b)],
                  [("row", S, C, out_dtype, tile, C, 0)])[0]


def _conv_act_bwd(name, src, cb, w, b, dout, tile):
    S = src.shape[0]
    C = cb[0]
    n = S // tile

    def body(x_ref, p_ref, w_ref, b_ref, d_ref, o_ref):
        i = pl.program_id(0)
        xf = x_ref[...].astype(F32)
        halo8 = p_ref[...].astype(F32)[HALO - 8:]
        pre = _conv_taps(xf, halo8, i == 0, w_ref[...]) + b_ref[...]
        o_ref[...] = d_ref[...].astype(F32) * _dsilu(pre)

    return _pcall(name, body, n, [("row", src, tile, *cb), ("prev", src, tile, *cb), ("full", w), ("full", b),
                                  ("row", dout, tile, C, 0)],
                  [("row", S, C, F32, tile, C, 0)])[0]


def _conv_bwd(name, src, cb, w, dpre, dproj, tile):
    S = src.shape[0]
    C = cb[0]
    n = S // tile

    def body(x_ref, p_ref, w_ref, d_ref, nx_ref, buf_ref, dx_ref, dw0, dw1, dw2, dw3, db_ref):
        i = pl.program_id(0)
        first, last = i == 0, i == n - 1
        xf = x_ref[...].astype(F32)
        halo8 = p_ref[...].astype(F32)[HALO - 8:]
        dv = d_ref[...]
        nx8 = nx_ref[...][:8]
        wv = w_ref[...]
        dx = dv * wv[3:4]
        for k in (1, 2, 3):
            dx = dx + _shift_up(dv, k, nx8, last) * wv[3 - k:4 - k]
        dx_ref[...] = dx.astype(BF16)
        _acc(dw3, i, jnp.sum(dv * xf, axis=0, keepdims=True))
        for k, dwr in ((1, dw2), (2, dw1), (3, dw0)):
            _acc(dwr, i, jnp.sum(dv * _shift_down(xf, k, halo8, first), axis=0, keepdims=True))
        _acc(db_ref, i, jnp.sum(dv, axis=0, keepdims=True))

    ins = [("row", src, tile, *cb), ("prev", src, tile, *cb), ("full", w), ("row", dpre, tile, C, 0),
           ("next", dpre, tile, C, 0), ("any", dproj)]
    outs = [("row", S, N_PROJ, BF16, tile, *cb)] + [("full", (1, C), F32)] * 5
    return _pcall(name, body, n, ins, outs, aliases={5: 0})


def _lru_gates(xc, wa_ref, wi_ref, ba, bi, lam):
    xb = xc.astype(BF16)
    pa, pi_ = [], []
    for nb in range(LRU_BLOCKS):
        sl = slice(nb * LRU_BLOCK, (nb + 1) * LRU_BLOCK)
        pa.append(jnp.dot(xb[:, sl], wa_ref[nb], preferred_element_type=F32))
        pi_.append(jnp.dot(xb[:, sl], wi_ref[nb], preferred_element_type=F32))
    r = _sigmoid(jnp.concatenate(pa, axis=1) + ba)
    ig = _sigmoid(jnp.concatenate(pi_, axis=1) + bi)
    sp = _softplus(-lam)
    a = jnp.exp(-LRU_C * r * sp)
    m = jnp.sqrt(1.0 - a * a)
    return xb, r, ig, sp, a, m


def _lru_fwd(name, xc, proj, wa, wi, ba, bi, lam, tile):
    S = xc.shape[0]
    n = S // tile
    C = D_MODEL

    def body(xc_ref, gate_ref, wa_ref, wi_ref, ba_ref, bi_ref, lam_ref, y_ref, h_ref, carry):
        i = pl.program_id(0)

        @pl.when(i == 0)
        def _():
            carry[...] = jnp.zeros_like(carry)

        xcv = xc_ref[...]
        _, r, ig, sp, a, m = _lru_gates(xcv, wa_ref, wi_ref, ba_ref[...], bi_ref[...], lam_ref[...])
        u = m * (ig * xcv)
        rows = _rows(a.shape)
        d = 1
        while d < tile:
            keep = rows >= d
            a_s = jnp.where(keep, pltpu.roll(a, d, 0), 1.0)
            u_s = jnp.where(keep, pltpu.roll(u, d, 0), 0.0)
            u = a * u_s + u
            a = a * a_s
            d *= 2
        h = u + a * carry[0:1, :]
        h_ref[...] = h
        carry[0:1, :] = h_ref[pl.ds(tile - 1, 1), :]
        y_ref[...] = (_gelu(gate_ref[...].astype(F32)) * h).astype(BF16)

    ins = [("row", xc, tile, C, 0), ("row", proj, tile, *CB_LRU_GATE), ("full", wa), ("full", wi),
           ("full", ba), ("full", bi), ("full", lam)]
    return _pcall(name, body, n, ins, [("row", S, C, BF16, tile, C, 0), ("row", S, C, F32, tile, C, 0)],
                  scratch=[pltpu.VMEM((8, C), F32)])


def _lru_bwd(name, dy, xc, proj, h, wa, wi, ba, bi, lam, dproj, tile):
    S = xc.shape[0]
    n = S // tile
    C = D_MODEL

    def body(dy_ref, xc_ref, gate_ref, h_ref, hp_ref, wa_ref, wi_ref, ba_ref, bi_ref, lam_ref, buf_ref,
             dg_ref, dxc_ref, dwa_ref, dwi_ref, dba_ref, dbi_ref, dlam_ref, carry):
        i = pl.program_id(0)
        first_tile = i == n - 1

        @pl.when(i == 0)
        def _():
            carry[...] = jnp.zeros_like(carry)

        xcv = xc_ref[...]
        lamv = lam_ref[...]
        xb, r, ig, sp, a, m = _lru_gates(xcv, wa_ref, wi_ref, ba_ref[...], bi_ref[...], lamv)
        hv = h_ref[...]
        gv = gate_ref[...].astype(F32)
        dyv = dy_ref[...].astype(F32)
        dg_ref[...] = (dyv * hv * _dgelu(gv)).astype(BF16)
        v = dyv * _gelu(gv)
        rows = _rows(a.shape)
        bcoef = jnp.where(rows == tile - 1, 1.0, pltpu.roll(a, tile - 1, 0))
        d = 1
        while d < tile:
            keep = rows < tile - d
            b_s = jnp.where(keep, pltpu.roll(bcoef, tile - d, 0), 1.0)
            v_s = jnp.where(keep, pltpu.roll(v, tile - d, 0), 0.0)
            v = v + bcoef * v_s
            bcoef = bcoef * b_s
            d *= 2
        dH = v + bcoef * carry[0:1, :]
        dxc_ref[...] = dH
        carry[0:1, :] = dxc_ref[pl.ds(0, 1), :] * a[0:1, :]
        halo8 = hp_ref[...][HALO - 8:]
        hprev = _shift_down(hv, 1, halo8, first_tile)
        da = dH * hprev
        ix = ig * xcv
        dm = dH * ix
        di = dH * m * xcv
        dxc = dH * m * ig
        da = da - dm * a / m
        dla = da * a
        dr = dla * (-LRU_C) * sp
        _acc(dlam_ref, i, jnp.sum(dla * (-LRU_C) * r, axis=0, keepdims=True) * (-_sigmoid(-lamv)))
        dpa = dr * r * (1.0 - r)
        dpi = di * ig * (1.0 - ig)
        _acc(dba_ref, i, jnp.sum(dpa, axis=0, keepdims=True))
        _acc(dbi_ref, i, jnp.sum(dpi, axis=0, keepdims=True))
        dpab, dpib = dpa.astype(BF16), dpi.astype(BF16)
        back = []
        for nb in range(LRU_BLOCKS):
            sl = slice(nb * LRU_BLOCK, (nb + 1) * LRU_BLOCK)
            back.append(lax.dot_general(dpab[:, sl], wa_ref[nb], (((1,), (1,)), ((), ())), preferred_element_type=F32)
                        + lax.dot_general(dpib[:, sl], wi_ref[nb], (((1,), (1,)), ((), ())), preferred_element_type=F32))
            ga = lax.dot_general(xb[:, sl], dpab[:, sl], (((0,), (0,)), ((), ())), preferred_element_type=F32)
            gi = lax.dot_general(xb[:, sl], dpib[:, sl], (((0,), (0,)), ((), ())), preferred_element_type=F32)

            @pl.when(i == 0)
            def _(ga=ga, gi=gi, nb=nb):
                dwa_ref[nb] = ga
                dwi_ref[nb] = gi

            @pl.when(i > 0)
            def _(ga=ga, gi=gi, nb=nb):
                dwa_ref[nb] += ga
                dwi_ref[nb] += gi

        dxc_ref[...] = dxc + jnp.concatenate(back, axis=1)

    ins = [("row", dy, tile, C, 0), ("row", xc, tile, C, 0), ("row", proj, tile, *CB_LRU_GATE), ("row", h, tile, C, 0),
           ("prev", h, tile, C, 0), ("full", wa), ("full", wi), ("full", ba), ("full", bi), ("full", lam), ("any", dproj)]
    outs = [("row", S, N_PROJ, BF16, tile, *CB_LRU_GATE), ("row", S, C, F32, tile, C, 0),
            ("full", (LRU_BLOCKS, LRU_BLOCK, LRU_BLOCK), F32), ("full", (LRU_BLOCKS, LRU_BLOCK, LRU_BLOCK), F32),
            ("full", (1, C), F32), ("full", (1, C), F32), ("full", (1, C), F32)]
    return _pcall(name, body, n, ins, outs, scratch=[pltpu.VMEM((8, C), F32)], reverse=True, aliases={10: 0})


def _split3(x):
    h = x.astype(BF16)
    r = x - h.astype(F32)
    m = r.astype(BF16)
    lo = (r - m.astype(F32)).astype(BF16)
    return h, m, lo


def _dot01_r(x, e):
    h, m, lo = _split3(x)
    return (jnp.dot(h, e, preferred_element_type=F32) + jnp.dot(m, e, preferred_element_type=F32)
            + jnp.dot(lo, e, preferred_element_type=F32))


def _dot01_l(e, x):
    h, m, lo = _split3(x)
    return (jnp.dot(e, h, preferred_element_type=F32) + jnp.dot(e, m, preferred_element_type=F32)
            + jnp.dot(e, lo, preferred_element_type=F32))


def _ssd_consts():
    hh = lax.broadcasted_iota(jnp.int32, (DT_PAD, D_SSD), 0)
    cc = lax.broadcasted_iota(jnp.int32, (DT_PAD, D_SSD), 1)
    e = (cc // SSD_HEAD_DIM == hh).astype(BF16)
    li = lax.broadcasted_iota(jnp.int32, (CHUNK, CHUNK), 0)
    si = lax.broadcasted_iota(jnp.int32, (CHUNK, CHUNK), 1)
    ltri = (li >= si).astype(BF16)
    l4 = lax.broadcasted_iota(jnp.int32, (CHUNK, 4 * CHUNK), 0)
    s4 = lax.broadcasted_iota(jnp.int32, (CHUNK, 4 * CHUNK), 1) % CHUNK
    itile = (l4 == s4).astype(F32)
    causal = (l4 >= s4).astype(F32)
    j4 = lax.broadcasted_iota(jnp.int32, (8, 4 * CHUNK), 0)
    c4 = lax.broadcasted_iota(jnp.int32, (8, 4 * CHUNK), 1) // CHUNK
    hmask = (j4 == c4).astype(F32)
    return e, e.T, ltri, ltri.T, itile, causal, hmask


def _ssd_chunk_common(xs_ref, bm_ref, cm_ref, dt_ref, dtb_ref, a_ref, e_ref, ltri_ref):
    xs = xs_ref[...].astype(F32)
    raw = dt_ref[...] + dtb_ref[...]
    dtv = _softplus(raw)
    da = dtv * a_ref[...]
    cs = _dot01_l(ltri_ref[...], da)
    e = e_ref[...]
    dte = _dot01_r(dtv, e)
    ce = _dot01_r(cs, e)
    xdt = xs * dte
    cle = ce[CHUNK - 1:CHUNK, :]
    dend = jnp.exp(cle - ce)
    ecs = jnp.exp(ce)
    return xs, raw, dtv, cs, dte, ce, xdt, cle, dend, ecs


def _quad_terms(ce_q, cb4, itile, causal):
    cr = jnp.sum(ce_q * itile, axis=0, keepdims=True)
    seg = ce_q - cr
    dec = jnp.where(causal > 0.0, jnp.exp(jnp.minimum(seg, 0.0)), 0.0)
    return dec, cb4 * dec


def _block_diag4(xq, hmask):
    return jnp.concatenate([xq * hmask[j:j + 1].astype(xq.dtype) for j in range(4)], axis=0)


def _ssd_fwd(name, xbc, dt_raw, dt_bias, a_neg, d_exp, consts):
    S = xbc.shape[0]
    nc = S // CHUNK
    e, et, ltri, ltri_t, itile, causal, hmask = consts

    def body(xs_ref, bm_ref, cm_ref, dt_ref, dtb_ref, a_ref, dex_ref, e_ref, ltri_ref, it_ref, ca_ref, hm_ref,
             y_ref, prev_ref, hst):
        i = pl.program_id(0)

        @pl.when(i == 0)
        def _():
            hst[...] = jnp.zeros_like(hst)

        xs, raw, dtv, cs, dte, ce, xdt, cle, dend, ecs = _ssd_chunk_common(
            xs_ref, bm_ref, cm_ref, dt_ref, dtb_ref, a_ref, e_ref, ltri_ref)
        xdtb = xdt.astype(BF16)
        xst = (xdt * dend).astype(BF16)
        ecl = jnp.exp(cle)
        itile_v, causal_v, hmask_v = it_ref[...], ca_ref[...], hm_ref[...]
        bm = bm_ref[...]
        cm = cm_ref[...]
        dskip = dex_ref[...] * xs
        for g in range(SSD_GROUPS):
            gs = slice(g * 512, (g + 1) * 512)
            ns = slice(g * SSD_STATE, (g + 1) * SSD_STATE)
            bm_g, cm_g = bm[:, ns], cm[:, ns]
            hprev = hst[g]
            hprev_b = hprev.astype(BF16)
            prev_ref[0, g] = hprev_b
            yoff = jnp.dot(cm_g, hprev_b, preferred_element_type=F32) * ecs[:, gs]
            st = lax.dot_general(bm_g, xst[:, gs], (((0,), (0,)), ((), ())), preferred_element_type=F32)
            hst[g] = hprev * ecl[:, gs] + st
            b4 = jnp.concatenate([bm_g] * 4, axis=0)
            cb4 = lax.dot_general(cm_g, b4, (((1,), (1,)), ((), ())), preferred_element_type=F32)
            for q in range(2):
                cols = slice(g * 512 + q * 256, g * 512 + (q + 1) * 256)
                _, mq = _quad_terms(ce[:, cols], cb4, itile_v, causal_v)
                xbd = _block_diag4(xdtb[:, cols], hmask_v)
                ydiag = jnp.dot(mq.astype(BF16), xbd, preferred_element_type=F32)
                y_ref[:, cols] = ydiag + yoff[:, q * 256:(q + 1) * 256] + dskip[:, cols]

    ins = [("row", xbc, CHUNK, *CB_XS), ("row", xbc, CHUNK, *CB_BM), ("row", xbc, CHUNK, *CB_CM),
           ("row", dt_raw, CHUNK, DT_PAD, 0), ("full", dt_bias), ("full", a_neg), ("full", d_exp),
           ("full", e), ("full", ltri), ("full", itile), ("full", causal), ("full", hmask)]
    outs = [("row", S, D_SSD, F32, CHUNK, D_SSD, 0), ("lead", (nc, SSD_GROUPS, SSD_STATE, 512), BF16)]
    return _pcall(name, body, nc, ins, outs, scratch=[pltpu.VMEM((SSD_GROUPS, SSD_STATE, 512), F32)])


def _ssd_bwd(name, xbc, dt_raw, dt_bias, a_neg, d_exp, prev, dy, consts):
    S = xbc.shape[0]
    nc = S // CHUNK
    e, et, ltri, ltri_t, itile, causal, hmask = consts

    def body(xs_ref, bm_ref, cm_ref, dt_ref, dtb_ref, a_ref, dex_ref, prev_ref, dy_ref,
             e_ref, et_ref, ltri_ref, ltt_ref, it_ref, ca_ref, hm_ref,
             dx_ref, ddt_ref, da_ref, dbias_ref, dd_ref, dh, dce_ref, dxdt_ref):
        i = pl.program_id(0)

        @pl.when(i == 0)
        def _():
            dh[...] = jnp.zeros_like(dh)

        xs, raw, dtv, cs, dte, ce, xdt, cle, dend, ecs = _ssd_chunk_common(
            xs_ref, bm_ref, cm_ref, dt_ref, dtb_ref, a_ref, e_ref, ltri_ref)
        xdtb = xdt.astype(BF16)
        xst = (xdt * dend).astype(BF16)
        ecl = jnp.exp(cle)
        itile_v, causal_v, hmask_v = it_ref[...], ca_ref[...], hm_ref[...]
        bm = bm_ref[...]
        cm = cm_ref[...]
        dyv = dy_ref[...]
        last_row = _rows((CHUNK, 512)) == CHUNK - 1
        for g in range(SSD_GROUPS):
            gs = slice(g * 512, (g + 1) * 512)
            ns = slice(g * SSD_STATE, (g + 1) * SSD_STATE)
            bm_g, cm_g = bm[:, ns], cm[:, ns]
            hprev_b = prev_ref[0, g]
            dhn = dh[g]
            dhn_b = dhn.astype(BF16)
            dy_g = dyv[:, gs]
            ecs_g, dend_g, xdt_g, ecl_g = ecs[:, gs], dend[:, gs], xdt[:, gs], ecl[:, gs]
            z = jnp.dot(cm_g, hprev_b, preferred_element_type=F32)
            dz = dy_g * ecs_g
            dzb = dz.astype(BF16)
            dce_g = dz * z
            dcm_g = lax.dot_general(dzb, hprev_b, (((1,), (1,)), ((), ())), preferred_element_type=F32)
            dprev = lax.dot_general(cm_g, dzb, (((0,), (0,)), ((), ())), preferred_element_type=F32) + dhn * ecl_g
            dcl = jnp.sum(dhn * hprev_b.astype(F32), axis=0, keepdims=True) * ecl_g
            gmat = jnp.dot(bm_g, dhn_b, preferred_element_type=F32)
            dbm_g = lax.dot_general(xst[:, gs], dhn_b, (((1,), (1,)), ((), ())), preferred_element_type=F32)
            dxdt_g = gmat * dend_g
            t = gmat * xdt_g * dend_g
            dce_g = dce_g - t
            dcl = dcl + jnp.sum(t, axis=0, keepdims=True)
            dce_g = dce_g + jnp.where(last_row, dcl, 0.0)
            dh[g] = dprev
            b4 = jnp.concatenate([bm_g] * 4, axis=0)
            cb4 = lax.dot_general(cm_g, b4, (((1,), (1,)), ((), ())), preferred_element_type=F32)
            for q in range(2):
                qs = slice(q * 256, (q + 1) * 256)
                cols = slice(g * 512 + q * 256, g * 512 + (q + 1) * 256)
                dec, mq = _quad_terms(ce[:, cols], cb4, itile_v, causal_v)
                mqb = mq.astype(BF16)
                xbd = _block_diag4(xdtb[:, cols], hmask_v)
                dyq = dy_g[:, qs].astype(BF16)
                dm = lax.dot_general(dyq, xbd, (((1,), (1,)), ((), ())), preferred_element_type=F32)
                rmat = lax.dot_general(mqb, dyq, (((0,), (0,)), ((), ())), preferred_element_type=F32)
                dxq = rmat[0:64] * hmask_v[0:1]
                for j in range(1, 4):
                    dxq = dxq + rmat[64 * j:64 * (j + 1)] * hmask_v[j:j + 1]
                tq = dm * dec
                tqb = tq.astype(BF16)
                dcm_g = dcm_g + jnp.dot(tqb, b4, preferred_element_type=F32)
                rb = lax.dot_general(tqb, cm_g, (((0,), (0,)), ((), ())), preferred_element_type=F32)
                dbm_g = dbm_g + rb[0:64] + rb[64:128] + rb[128:192] + rb[192:256]
                dseg = tq * cb4
                colsum = jnp.sum(dseg, axis=0, keepdims=True)
                dce_ref[:, cols] = dce_g[:, qs] + dseg - itile_v * colsum
                dxdt_ref[:, cols] = dxdt_g[:, qs] + dxq
            dx_ref[:, D_SSD + g * SSD_STATE:D_SSD + (g + 1) * SSD_STATE] = dbm_g
            dx_ref[:, D_SSD + D_BC + g * SSD_STATE:D_SSD + D_BC + (g + 1) * SSD_STATE] = dcm_g
        dxdt = dxdt_ref[...]
        dexv = dex_ref[...]
        dx_ref[:, 0:D_SSD] = dxdt * dte + dyv * dexv
        _acc(dd_ref, i, jnp.sum(dyv * xs, axis=0, keepdims=True))
        etv = et_ref[...]
        dcs = _dot01_r(dce_ref[...], etv)
        dda = _dot01_l(ltt_ref[...], dcs)
        av = a_ref[...]
        ddtv = dda * av + _dot01_r(dxdt * xs, etv)
        _acc(da_ref, i, jnp.sum(dda * dtv, axis=0, keepdims=True))
        draw = ddtv * _sigmoid(raw)
        ddt_ref[...] = draw.astype(BF16)
        _acc(dbias_ref, i, jnp.sum(draw, axis=0, keepdims=True))

    ins = [("row", xbc, CHUNK, *CB_XS), ("row", xbc, CHUNK, *CB_BM), ("row", xbc, CHUNK, *CB_CM),
           ("row", dt_raw, CHUNK, DT_PAD, 0), ("full", dt_bias), ("full", a_neg), ("full", d_exp),
           ("lead", prev), ("row", dy, CHUNK, D_SSD, 0),
           ("full", e), ("full", et), ("full", ltri), ("full", ltri_t), ("full", itile), ("full", causal), ("full", hmask)]
    outs = [("row", S, D_XBC, F32, CHUNK, D_XBC, 0), ("row", S, DT_PAD, BF16, CHUNK, DT_PAD, 0),
            ("full", (1, DT_PAD), F32), ("full", (1, DT_PAD), F32), ("full", (1, D_SSD), F32)]
    scratch = [pltpu.VMEM((SSD_GROUPS, SSD_STATE, 512), F32), pltpu.VMEM((CHUNK, D_SSD), F32), pltpu.VMEM((CHUNK, D_SSD), F32)]
    return _pcall(name, body, nc, ins, outs, scratch=scratch, reverse=True)


def _gate_norm_fwd(name, ycore, proj, norm_w, tile):
    S = ycore.shape[0]
    n = S // tile

    def body(y_ref, z_ref, w_ref, o_ref):
        y2 = y_ref[...] * _silu(z_ref[...].astype(F32))
        wv = w_ref[...]
        for g in range(SSD_GROUPS):
            gs = slice(g * 512, (g + 1) * 512)
            seg = y2[:, gs]
            r = lax.rsqrt(jnp.mean(seg * seg, axis=-1, keepdims=True) + EPS)
            o_ref[:, gs] = (seg * r * wv[:, gs]).astype(BF16)

    return _pcall(name, body, n, [("row", ycore, tile, D_SSD, 0), ("row", proj, tile, *CB_Z), ("full", norm_w)],
                  [("row", S, D_SSD, BF16, tile, D_SSD, 0)])[0]


def _gate_norm_bwd(name, dout, ycore, proj, norm_w, dproj, tile):
    S = ycore.shape[0]
    n = S // tile

    def body(do_ref, y_ref, z_ref, w_ref, buf_ref, dz_ref, dy_ref, dw_ref):
        i = pl.program_id(0)
        yv = y_ref[...]
        zv = z_ref[...].astype(F32)
        sz = _silu(zv)
        y2 = yv * sz
        dov = do_ref[...].astype(F32)
        wv = w_ref[...]
        dws, dy2s = [], []
        for g in range(SSD_GROUPS):
            gs = slice(g * 512, (g + 1) * 512)
            seg = y2[:, gs]
            r = lax.rsqrt(jnp.mean(seg * seg, axis=-1, keepdims=True) + EPS)
            yn = seg * r
            dws.append(jnp.sum(dov[:, gs] * yn, axis=0, keepdims=True))
            dyn = dov[:, gs] * wv[:, gs]
            dy2s.append(r * (dyn - yn * jnp.mean(dyn * yn, axis=-1, keepdims=True)))
        dy2 = jnp.concatenate(dy2s, axis=1)
        dy_ref[...] = dy2 * sz
        dz_ref[...] = (dy2 * yv * _dsilu(zv)).astype(BF16)
        _acc(dw_ref, i, jnp.concatenate(dws, axis=1))

    ins = [("row", dout, tile, D_SSD, 0), ("row", ycore, tile, D_SSD, 0), ("row", proj, tile, *CB_Z), ("full", norm_w),
           ("any", dproj)]
    outs = [("row", S, N_PROJ, BF16, tile, *CB_Z), ("row", S, D_SSD, F32, tile, D_SSD, 0), ("full", (1, D_SSD), F32)]
    return _pcall(name, body, n, ins, outs, aliases={4: 0})


_XA_SCALE = XA_HEAD_DIM ** -0.5
_NT = (((1,), (1,)), ((), ()))
_TN = (((0,), (0,)), ((), ()))


def _xa_probs(qh, kh):
    s = lax.dot_general(qh, kh, _NT, preferred_element_type=F32) * _XA_SCALE
    s = s - jnp.max(s, axis=-1, keepdims=True)
    p = jnp.exp(s)
    return p / jnp.sum(p, axis=-1, keepdims=True)


def _xa_fwd(name, proj, kv, tile):
    S = proj.shape[0]
    n = S // tile
    Dh = XA_HEAD_DIM

    def body(q_ref, kv_ref, o_ref):
        for hd in range(XA_HEADS):
            qh = q_ref[:, hd * Dh:(hd + 1) * Dh]
            kh = kv_ref[:, hd * Dh:(hd + 1) * Dh]
            vh = kv_ref[:, D_MODEL + hd * Dh:D_MODEL + (hd + 1) * Dh]
            p = _xa_probs(qh, kh)
            o_ref[:, hd * Dh:(hd + 1) * Dh] = jnp.dot(p.astype(BF16), vh, preferred_element_type=F32).astype(BF16)

    return _pcall(name, body, n, [("row", proj, tile, *CB_XA_Q), ("full", kv)],
                  [("row", S, D_MODEL, BF16, tile, D_MODEL, 0)])[0]


def _xa_bwd(name, proj, kv, dout, dproj, tile):
    S = proj.shape[0]
    n = S // tile
    Dh = XA_HEAD_DIM

    def body(q_ref, kv_ref, do_ref, buf_ref, dq_ref, dkv_ref):
        i = pl.program_id(0)
        for hd in range(XA_HEADS):
            ks_ = slice(hd * Dh, (hd + 1) * Dh)
            vs_ = slice(D_MODEL + hd * Dh, D_MODEL + (hd + 1) * Dh)
            qh = q_ref[:, ks_]
            kh = kv_ref[:, ks_]
            vh = kv_ref[:, vs_]
            doh = do_ref[:, ks_].astype(BF16)
            p = _xa_probs(qh, kh)
            pb = p.astype(BF16)
            dp = lax.dot_general(doh, vh, _NT, preferred_element_type=F32)
            dv = lax.dot_general(pb, doh, _TN, preferred_element_type=F32)
            ds = (p * (dp - jnp.sum(dp * p, axis=-1, keepdims=True)) * _XA_SCALE).astype(BF16)
            dq_ref[:, ks_] = jnp.dot(ds, kh, preferred_element_type=F32).astype(BF16)
            dk = lax.dot_general(ds, qh, _TN, preferred_element_type=F32)

            @pl.when(i == 0)
            def _(dk=dk, dv=dv, ks_=ks_, vs_=vs_):
                dkv_ref[:, ks_] = dk
                dkv_ref[:, vs_] = dv

            @pl.when(i > 0)
            def _(dk=dk, dv=dv, ks_=ks_, vs_=vs_):
                dkv_ref[:, ks_] += dk
                dkv_ref[:, vs_] += dv

    ins = [("row", proj, tile, *CB_XA_Q), ("full", kv), ("row", dout, tile, D_MODEL, 0), ("any", dproj)]
    outs = [("row", S, N_PROJ, BF16, tile, *CB_XA_Q), ("full", (N_MEM, 2 * D_MODEL), F32)]
    return _pcall(name, body, n, ins, outs, aliases={3: 0})


def _adamw(name, w, g, m, v):
    R, C = w.shape
    tile = _pick(R, [t for t in (256, 128, 64, 32, 16, 8) if t * C <= 128 * 2048])
    n = R // tile
    bc1 = 1.0 - ADAM_B1 ** ADAM_STEP
    bc2 = 1.0 - ADAM_B2 ** ADAM_STEP

    def body(w_ref, g_ref, m_ref, v_ref, d_ref, nm_ref, nv_ref):
        gv = g_ref[...]
        mn = ADAM_B1 * m_ref[...] + (1.0 - ADAM_B1) * gv
        vn = ADAM_B2 * v_ref[...] + (1.0 - ADAM_B2) * (gv * gv)
        nm_ref[...] = mn
        nv_ref[...] = vn
        d_ref[...] = -ADAM_LR * ((mn / bc1) / (jnp.sqrt(vn / bc2) + ADAM_EPS) + ADAM_WD * w_ref[...])

    ins = [("row", a, tile, C, 0) for a in (w, g, m, v)]
    return _pcall(name, body, n, ins, [("row", R, C, F32, tile, C, 0)] * 3)


def _axpy(name, coef, a, b, tile):
    S, C = a.shape
    n = S // tile

    def body(a_ref, b_ref, o_ref):
        o_ref[...] = coef * a_ref[...].astype(F32) + b_ref[...].astype(F32)

    return _pcall(name, body, n, [("row", a, tile, C, 0), ("row", b, tile, C, 0)], [("row", S, C, F32, tile, C, 0)])[0]


def _sum_terms(name, terms, out_dtype):
    R, C = terms[0].shape
    tile = _pick(R, (512, 256, 128, 64, 32, 16, 8))
    n = R // tile
    nt = len(terms)

    def body(*refs):
        vals = [r[...].astype(F32) for r in refs[:nt]]
        while len(vals) > 1:
            vals = [vals[k] + vals[k + 1] for k in range(0, len(vals), 2)]
        refs[nt][...] = vals[0].astype(out_dtype)

    return _pcall(name, body, n, [("row", t, tile, C, 0) for t in terms], [("row", R, C, out_dtype, tile, C, 0)])[0]


def _me():
    return lax.axis_index("x"), lax.axis_index("y"), lax.axis_index("c")


def _other_chips(x, y):
    return [(1 - x, y), (x, 1 - y), (1 - x, 1 - y)]


_ANY = pl.BlockSpec(memory_space=pl.ANY)


def _swap_sibling(name, src):
    def body(src_ref, out_ref, send_sem, recv_sem):
        x, y, c = _me()
        cp = pltpu.make_async_remote_copy(src_ref=src_ref, dst_ref=out_ref, send_sem=send_sem, recv_sem=recv_sem,
                                          device_id=(x, y, 1 - c), device_id_type=MESH)
        cp.start()
        cp.wait()

    return pl.pallas_call(
        body, name=name, in_specs=[_ANY], out_specs=_ANY, out_shape=jax.ShapeDtypeStruct(src.shape, src.dtype),
        scratch_shapes=[pltpu.SemaphoreType.DMA, pltpu.SemaphoreType.DMA],
    )(src)


def _send_chips(name, src, per_chip):
    shape = src.shape[1:] if per_chip else src.shape

    def body(src_ref, out_ref, send_sems, recv_sems):
        x, y, c = _me()
        cps = []
        for k, (cx, cy) in enumerate(_other_chips(x, y)):
            s = src_ref.at[2 * cx + cy] if per_chip else src_ref
            cps.append(pltpu.make_async_remote_copy(src_ref=s, dst_ref=out_ref.at[k], send_sem=send_sems.at[k],
                                                    recv_sem=recv_sems.at[k], device_id=(cx, cy, c), device_id_type=MESH))
        for cp in cps:
            cp.start()
        for cp in cps:
            cp.wait()

    return pl.pallas_call(
        body, name=name, in_specs=[_ANY], out_specs=_ANY, out_shape=jax.ShapeDtypeStruct((3,) + shape, src.dtype),
        scratch_shapes=[pltpu.SemaphoreType.DMA((3,)), pltpu.SemaphoreType.DMA((3,))],
    )(src)


W_IN_SHARD = N_IN // 4
W_IN_INNER = W_IN_SHARD - 8


def _win_rows(chip_x, chip_y):
    start = (2 * chip_x + chip_y) * W_IN_SHARD + 8 * chip_y
    return pl.ds(pl.multiple_of(start, 2 * 8), W_IN_INNER)


def _gather_multi(name, srcs, kinds):
    n = len(srcs)

    def body(*refs):
        src_refs, out_refs = refs[:n], refs[n:2 * n]
        send_sems, recv_sems, local_sems = refs[2 * n:]
        x, y, c = _me()
        chips = _other_chips(x, y)

        def dst(i, part, cx, cy):
            if kinds[i] == "blk":
                return out_refs[i].at[part, 2 * cx + cy]
            return out_refs[i].at[part, _win_rows(cx, cy)]

        def copy(i, k, part, cx, cy, to, src=None):
            d = dst(i, part, cx, cy)
            return pltpu.make_async_remote_copy(src_ref=d if src is None else src, dst_ref=d,
                                                send_sem=send_sems.at[i, k], recv_sem=recv_sems.at[i, k],
                                                device_id=to, device_id_type=MESH)

        mine = [pltpu.make_async_copy(src_refs[i].at[p], dst(i, p, x, y), local_sems.at[i, p])
                for i in range(n) for p in range(2)]
        for cp in mine:
            cp.start()
        first = [copy(i, k, c, x, y, (cx, cy, c), src=src_refs[i].at[c]) for k, (cx, cy) in enumerate(chips) for i in range(n)]
        for cp in first:
            cp.start()
        passed = []
        for k, (cx, cy) in enumerate(chips):
            for i in range(n):
                copy(i, k, c, cx, cy, (x, y, c)).wait_recv()
                fw = copy(i, 3 + k, c, cx, cy, (x, y, 1 - c))
                fw.start()
                passed.append(fw)
        for k, (cx, cy) in enumerate(chips):
            for i in range(n):
                copy(i, 3 + k, 1 - c, cx, cy, (x, y, c)).wait_recv()
        for cp in first + passed:
            cp.wait_send()
        for cp in mine:
            cp.wait()

    out_shape = [jax.ShapeDtypeStruct((2, 4) + s.shape[1:] if kd == "blk" else (2, N_IN, D_MODEL), s.dtype)
                 for s, kd in zip(srcs, kinds)]
    return pl.pallas_call(
        body, name=name, in_specs=[_ANY] * n, out_specs=[_ANY] * n, out_shape=out_shape,
        scratch_shapes=[pltpu.SemaphoreType.DMA((n, 6)), pltpu.SemaphoreType.DMA((n, 6)), pltpu.SemaphoreType.DMA((n, 2))],
    )(*srcs)


def _swap_parts_multi(name, srcs):
    n = len(srcs)

    def body(*refs):
        src_refs, out_refs = refs[:n], refs[n:2 * n]
        send_sems, recv_sems = refs[2 * n:]
        x, y, c = _me()
        cps = [pltpu.make_async_remote_copy(src_ref=src_refs[i].at[1 - c], dst_ref=out_refs[i], send_sem=send_sems.at[i],
                                            recv_sem=recv_sems.at[i], device_id=(x, y, 1 - c), device_id_type=MESH)
               for i in range(n)]
        for cp in cps:
            cp.start()
        for cp in cps:
            cp.wait()

    return pl.pallas_call(
        body, name=name, in_specs=[_ANY] * n, out_specs=[_ANY] * n,
        out_shape=[jax.ShapeDtypeStruct(s.shape[1:], s.dtype) for s in srcs],
        scratch_shapes=[pltpu.SemaphoreType.DMA((n,)), pltpu.SemaphoreType.DMA((n,))],
    )(*srcs)


def _send_chips_multi(name, srcs):
    n = len(srcs)

    def body(*refs):
        src_refs, out_refs = refs[:n], refs[n:2 * n]
        send_sems, recv_sems = refs[2 * n:]
        x, y, c = _me()
        cps = [pltpu.make_async_remote_copy(src_ref=src_refs[i].at[2 * cx + cy], dst_ref=out_refs[i].at[k],
                                            send_sem=send_sems.at[i, k], recv_sem=recv_sems.at[i, k],
                                            device_id=(cx, cy, c), device_id_type=MESH)
               for k, (cx, cy) in enumerate(_other_chips(x, y)) for i in range(n)]
        for cp in cps:
            cp.start()
        for cp in cps:
            cp.wait()

    return pl.pallas_call(
        body, name=name, in_specs=[_ANY] * n, out_specs=[_ANY] * n,
        out_shape=[jax.ShapeDtypeStruct((3,) + s.shape[1:], s.dtype) for s in srcs],
        scratch_shapes=[pltpu.SemaphoreType.DMA((n, 3)), pltpu.SemaphoreType.DMA((n, 3))],
    )(*srcs)


def _join_parts_multi(name, srcs):
    n = len(srcs)

    def body(*refs):
        src_refs, out_refs = refs[:n], refs[n:2 * n]
        send_sems, recv_sems, local_sems = refs[2 * n:]
        x, y, c = _me()
        mine = [pltpu.make_async_copy(src_refs[i], out_refs[i].at[c], local_sems.at[i]) for i in range(n)]
        cps = [pltpu.make_async_remote_copy(src_ref=src_refs[i], dst_ref=out_refs[i].at[c], send_sem=send_sems.at[i],
                                            recv_sem=recv_sems.at[i], device_id=(x, y, 1 - c), device_id_type=MESH)
               for i in range(n)]
        for cp in mine + cps:
            cp.start()
        for i in range(n):
            cps[i].wait_send()
            pltpu.make_async_remote_copy(src_ref=src_refs[i], dst_ref=out_refs[i].at[1 - c], send_sem=send_sems.at[i],
                                         recv_sem=recv_sems.at[i], device_id=(x, y, 1 - c), device_id_type=MESH).wait_recv()
        for cp in mine:
            cp.wait()

    return pl.pallas_call(
        body, name=name, in_specs=[_ANY] * n, out_specs=[_ANY] * n,
        out_shape=[jax.ShapeDtypeStruct((2,) + s.shape, s.dtype) for s in srcs],
        scratch_shapes=[pltpu.SemaphoreType.DMA((n,)), pltpu.SemaphoreType.DMA((n,)), pltpu.SemaphoreType.DMA((n,))],
    )(*srcs)


def _col_tiles(R, C):
    tr = _pick(R, (512, 256, 128))
    if tr != R:
        return tr, C
    if R * C <= 512 * 1024:
        return R, C
    return R, _pick(C, (256, 128))


def _pair_sum(name, g, theirs, c):
    _, n4, R, C = g.shape
    tr, tc = _col_tiles(R, C)

    def body(c_ref, g_ref, t_ref, o_ref, ob_ref):
        s = g_ref[0, 0] + t_ref[0]
        o_ref[0] = s
        ob_ref[0] = s.astype(BF16)

    spec3 = pl.BlockSpec((1, tr, tc), lambda d, i, k, c_ref: (d, i, k))
    return pl.pallas_call(
        body, name=name,
        grid_spec=pltpu.PrefetchScalarGridSpec(
            num_scalar_prefetch=1, grid=(n4, R // tr, C // tc),
            in_specs=[pl.BlockSpec((1, 1, tr, tc), lambda d, i, k, c_ref: (c_ref[0], d, i, k)), spec3],
            out_specs=[spec3, spec3]),
        out_shape=[jax.ShapeDtypeStruct((n4, R, C), F32), jax.ShapeDtypeStruct((n4, R, C), BF16)],
        compiler_params=_cparams(("arbitrary", "arbitrary", "arbitrary")),
    )(c.reshape(1).astype(jnp.int32), g, theirs)


def _chip_sum(name, pair, got, j):
    _, R, C = pair.shape
    tr, tc = _col_tiles(R, C)

    def body(j_ref, p_ref, g0, g1, g2, o_ref):
        o_ref[...] = (p_ref[0] + g2[0].astype(F32)) + (g0[0].astype(F32) + g1[0].astype(F32))

    def gspec(k):
        return pl.BlockSpec((1, tr, tc), lambda i, q, j_ref, k=k: (k, i, q))

    return pl.pallas_call(
        body, name=name,
        grid_spec=pltpu.PrefetchScalarGridSpec(
            num_scalar_prefetch=1, grid=(R // tr, C // tc),
            in_specs=[pl.BlockSpec((1, tr, tc), lambda i, q, j_ref: (j_ref[0], i, q)), gspec(0), gspec(1), gspec(2)],
            out_specs=pl.BlockSpec((tr, tc), lambda i, q, j_ref: (i, q))),
        out_shape=jax.ShapeDtypeStruct((R, C), F32),
        compiler_params=_cparams(("arbitrary", "arbitrary")),
    )(j.reshape(1).astype(jnp.int32), pair, got, got, got)


LANES = 1024
_BIG = (("w_in", "col"), ("ffn_w_in", "col"), ("mem_w_kv", "col"),
        ("w_br_lru", "row"), ("w_br_ssd", "row"), ("w_br_xa", "row"), ("w_out", "row"), ("ffn_w_down", "row"))
_SMALL_SHARDED = ("b_gate", "lru_conv_w", "ssd_conv_w")
_SMALL = ("b_gate", "lru_conv_w", "lru_conv_b", "lru_w_a", "lru_b_a", "lru_w_i", "lru_b_i", "lru_lambda",
          "ssd_conv_w", "ssd_conv_b", "ssd_dt_bias", "ssd_a_log", "ssd_d", "ssd_norm_w",
          "ln1_g", "ln1_b", "ln2_g", "ln2_b")
_W_NAMES = ("w_in", "b_gate", "lru_conv_w", "lru_conv_b", "lru_w_a", "lru_b_a", "lru_w_i", "lru_b_i", "lru_lambda",
            "ssd_conv_w", "ssd_conv_b", "ssd_dt_bias", "ssd_a_log", "ssd_d", "ssd_norm_w", "mem_w_kv", "w_br_lru",
            "w_br_ssd", "w_br_xa", "w_out", "ln1_g", "ln1_b", "ffn_w_in", "ffn_w_down", "ln2_g", "ln2_b")
_IN_ORDER = ((4096, 7168), (8224, 11296), (2048, 4096), (0, 1024), (1024, 2048), (7200, 8224))
_IN_DT = (7168, 7200)


def _flat_rows(parts, row_multiple):
    flat = jnp.concatenate([p.reshape(-1) for p in parts])
    rows = -(-flat.size // LANES)
    rows = -(-rows // row_multiple) * row_multiple
    return jnp.pad(flat, (0, rows * LANES - flat.size)).reshape(rows, LANES)


def _take_parts(flat, shapes):
    out, off = [], 0
    for shp in shapes:
        size = math.prod(shp)
        out.append(flat[off:off + size].reshape(shp))
        off += size
    return out


def _gather_weights(w):
    _, y, _ = _me()
    wt = jnp.swapaxes(w["w_in"], 1, 2)
    inner = lax.dynamic_slice_in_dim(wt, 8 * y, W_IN_INNER, axis=1).astype(BF16)
    edge = lax.dynamic_slice_in_dim(wt, (1 - y) * W_IN_INNER, 8, axis=1)
    names = [nm for nm, _ in _BIG[1:]] + list(_SMALL_SHARDED)
    srcs = [inner, edge] + [w[nm].astype(BF16) for nm, _ in _BIG[1:]] + [w[nm] for nm in _SMALL_SHARDED]
    got = _gather_multi("gather_weights", srcs, ["rows"] + ["blk"] * (len(srcs) - 1))
    wt_all, edges = got[0], got[1]
    for a, b in ((0, 1), (2, 3)):
        tile = jnp.concatenate([edges[:, a], edges[:, b]], axis=1).astype(BF16)
        wt_all = lax.dynamic_update_slice_in_dim(wt_all, tile, b * W_IN_SHARD - 8, axis=1)
    full = {"w_in_t": wt_all}
    kinds = dict(_BIG)
    for nm, g in zip(names, got[2:]):
        L, _, r, cdim = g.shape
        if kinds.get(nm, "col") == "col":
            full[nm] = jnp.moveaxis(g, 1, 2).reshape(L, r, 4 * cdim)
        else:
            full[nm] = g.reshape(L, 4 * r, cdim)
    return full


def _reduce_big_grads(grads):
    x, y, c = _me()
    j = 2 * x + y
    gs = []
    for nm, kind in _BIG:
        g = jnp.stack(grads[nm])
        L, r, cdim = g.shape
        if nm == "w_in" or kind == "row":
            g = g.reshape(L, 4, r // 4, cdim)
        else:
            g = jnp.moveaxis(g.reshape(L, r, 4, cdim // 4), 2, 1)
        gs.append(g)
    theirs = _swap_parts_multi("grads_pair_swap", gs)
    pairs = [_pair_sum("grads_pair_sum", g, t, c) for g, t in zip(gs, theirs)]
    got = _send_chips_multi("grads_chip_send", [pb for _, pb in pairs])
    halves = [_chip_sum("grads_chip_sum", p, gt, j) for (p, _), gt in zip(pairs, got)]
    both = _join_parts_multi("grads_join", halves)
    out = dict(zip([nm for nm, _ in _BIG], both))
    out["w_in"] = jnp.swapaxes(out["w_in"], 1, 2)
    return out


def _reduce_small_grads(parts):
    shapes = [p.shape for p in parts]
    buf = _flat_rows(parts, 8)
    sib = _swap_sibling("small_pair_swap", buf)
    pair = _sum_terms("small_pair_sum", [buf, sib], F32)
    got = _send_chips("small_chip_send", pair, False)
    total = _sum_terms("small_chip_sum", [pair, got[2], got[0], got[1]], F32)
    return _take_parts(total.reshape(-1), shapes)


def _layer_fwd(xin, mem2, W, P, consts):
    S = xin.shape[0]
    T, TX = min(256, S), min(512, S)
    sv = {"xin": xin}
    proj = _mm("proj", xin, W["w_re_t"], "nt", BF16)
    dt_raw = _mm("proj_dt", xin, W["w_dt_t"], "nt", F32)
    xc = _conv_fwd("lru_conv", proj, CB_LRU_X, P["lru_conv_w"], P["lru_conv_b"], False, F32, T)
    y_lru, h = _lru_fwd("lru_scan", xc, proj, P["lru_w_a"], P["lru_w_i"], P["lru_b_a"], P["lru_b_i"], P["lru_lambda"], T)
    xact = _conv_fwd("ssd_conv", proj, CB_XBC, P["ssd_conv_w"], P["ssd_conv_b"], True, BF16, T)
    ycore, prev = _ssd_fwd("ssd_scan", xact, dt_raw, P["dt_bias"], P["a_neg"], P["d_exp"], consts)
    y_ssd = _gate_norm_fwd("ssd_norm", ycore, proj, P["ssd_norm_w"], T)
    kv = _mm("mem_kv", mem2, W["w_kv"], "nn", BF16)
    y_xa = _xa_fwd("xattn", proj, kv, TX)
    p_l = _mm("br_lru", y_lru, W["w_l"], "nn", BF16)
    p_s = _mm("br_ssd", y_ssd, W["w_s"], "nn", BF16)
    p_x = _mm("br_xa", y_xa, W["w_x"], "nn", BF16)
    merged = _merge_fwd("merge", proj, P["b_gate"], p_l, p_s, p_x, T)
    mix = _mm("mix_out", merged, W["w_o"], "nn", F32)
    x1, xh1, rs1 = _ln_fwd("ln_fwd", xin, mix, P["ln1_g"], P["ln1_b"], T)
    gu = _mm("ffn_in", x1, W["w_fi"], "nn", BF16)
    hmid = _swiglu_fwd("swiglu", gu, T)
    f = _mm("ffn_down", hmid, W["w_fd"], "nn", F32)
    x2, xh2, rs2 = _ln_fwd("ln_fwd", x1, f, P["ln2_g"], P["ln2_b"], T)
    sv.update(proj=proj, dt_raw=dt_raw, xc=xc, h=h, y_lru=y_lru, xact=xact, ycore=ycore, prev=prev, y_ssd=y_ssd, kv=kv,
              y_xa=y_xa, p_l=p_l, p_s=p_s, p_x=p_x, merged=merged, x1=x1, xh1=xh1, rs1=rs1, gu=gu, hmid=hmid,
              xh2=xh2, rs2=rs2)
    return x2, sv


def _layer_bwd(dys, coefs, sv, mem2, W, P, consts):
    S = sv["xin"].shape[0]
    T, TX = min(256, S), min(512, S)
    proj = sv["proj"]
    g = {}
    dz2, g["ln2_g"], g["ln2_b"] = _ln_bwd("ln_bwd_%d" % len(dys), dys, coefs, sv["xh2"], sv["rs2"], P["ln2_g"], T)
    dhmid = _mm("d_hmid", dz2, W["w_fd"], "nt", BF16)
    g["ffn_w_down"] = _mm("dw_ffn_down", sv["hmid"], dz2, "tn", F32)
    dgu = _swiglu_bwd("swiglu_bwd", sv["gu"], dhmid, T)
    dx1f = _mm("d_x1", dgu, W["w_fi"], "nt", F32)
    g["ffn_w_in"] = _mm("dw_ffn_in", sv["x1"], dgu, "tn", F32)
    dz1, g["ln1_g"], g["ln1_b"] = _ln_bwd("ln_bwd_2", [dz2, dx1f], [ALPHA, 1.0], sv["xh1"], sv["rs1"], P["ln1_g"], T)
    dmerged = _mm("d_merged", dz1, W["w_o"], "nt", BF16)
    g["w_out"] = _mm("dw_out", sv["merged"], dz1, "tn", F32)
    dproj, dpl, dps, dpx, dbg0, dbg1, dbg2 = _merge_bwd("merge_bwd", proj, P["b_gate"], sv["p_l"], sv["p_s"], sv["p_x"],
                                                         dmerged, T)
    g["b_gate"] = jnp.concatenate([dbg0, dbg1, dbg2], axis=0)
    dy_lru = _mm("d_ylru", dpl, W["w_l"], "nt", BF16)
    g["w_br_lru"] = _mm("dw_br_lru", sv["y_lru"], dpl, "tn", F32)
    dy_ssd = _mm("d_yssd", dps, W["w_s"], "nt", BF16)
    g["w_br_ssd"] = _mm("dw_br_ssd", sv["y_ssd"], dps, "tn", F32)
    dy_xa = _mm("d_yxa", dpx, W["w_x"], "nt", BF16)
    g["w_br_xa"] = _mm("dw_br_xa", sv["y_xa"], dpx, "tn", F32)
    dproj, dkv = _xa_bwd("xattn_bwd", proj, sv["kv"], dy_xa, dproj, TX)
    g["mem_w_kv"] = _mm("dw_kv", mem2, dkv, "tn", F32)
    dproj, dycore, g["ssd_norm_w"] = _gate_norm_bwd("ssd_norm_bwd", dy_ssd, sv["ycore"], proj, P["ssd_norm_w"], dproj, T)
    dxact, ddt, d_a, g_dtb, d_dexp = _ssd_bwd("ssd_scan_bwd", sv["xact"], sv["dt_raw"], P["dt_bias"], P["a_neg"], P["d_exp"],
                                              sv["prev"], dycore, consts)
    g["ssd_dt_bias"] = g_dtb[:, :SSD_HEADS]
    g["ssd_a_log"] = d_a[:, :SSD_HEADS] * P["a_neg"][:, :SSD_HEADS]
    g["ssd_d"] = jnp.sum(d_dexp.reshape(SSD_HEADS, SSD_HEAD_DIM), axis=-1)
    dpre = _conv_act_bwd("ssd_conv_act_bwd", proj, CB_XBC, P["ssd_conv_w"], P["ssd_conv_b"], dxact, T)
    dproj, w0, w1, w2, w3, g["ssd_conv_b"] = _conv_bwd("ssd_conv_bwd", proj, CB_XBC, P["ssd_conv_w"], dpre, dproj, T)
    g["ssd_conv_w"] = jnp.concatenate([w0, w1, w2, w3], axis=0)
    dproj, dxc, g["lru_w_a"], g["lru_w_i"], g["lru_b_a"], g["lru_b_i"], g["lru_lambda"] = _lru_bwd(
        "lru_scan_bwd", dy_lru, sv["xc"], proj, sv["h"], P["lru_w_a"], P["lru_w_i"], P["lru_b_a"], P["lru_b_i"],
        P["lru_lambda"], dproj, T)
    dproj, w0, w1, w2, w3, g["lru_conv_b"] = _conv_bwd("lru_conv_bwd", proj, CB_LRU_X, P["lru_conv_w"], dxc, dproj, T)
    g["lru_conv_w"] = jnp.concatenate([w0, w1, w2, w3], axis=0)
    xin = sv["xin"]
    dw_re_t = _mm("dw_in", dproj, xin, "tn", F32)
    dw_dt_t = _mm("dw_in_dt", ddt, xin, "tn", F32)
    pieces = {rng: dw_re_t[off:off + rng[1] - rng[0]]
              for rng, off in zip(_IN_ORDER, (0, 3072, 6144, 8192, 9216, 10240))}
    pieces[_IN_DT] = dw_dt_t[:SSD_HEADS]
    g["w_in"] = jnp.concatenate([pieces[k] for k in sorted(pieces)], axis=0)
    dxp = _mm("d_xin", dproj, W["w_re_t"], "nn", F32)
    dxs = _mm("d_xin_dt", ddt, W["w_dt_t"], "nn", F32, add=dxp)
    return [dz1, dxs], [ALPHA, 1.0], g


def _step(a):
    x2d, mem2, target = a["x"][0], a["mem"][0], a["loss_target"][0]
    S = x2d.shape[0]
    T = min(256, S)
    xi, yi, ci = _me()
    j = 2 * xi + yi
    w = {nm: a[nm] for nm in _W_NAMES}
    full = _gather_weights(w)
    consts = _ssd_consts()
    row = lambda v: v.reshape(1, -1)
    Ws, Ps = [], []
    for l in range(DEPTH):
        w_in_t = full["w_in_t"][l]
        w_re_t = jnp.concatenate([w_in_t[lo:hi] for lo, hi in _IN_ORDER], axis=0)
        w_dt_t = jnp.pad(w_in_t[_IN_DT[0]:_IN_DT[1]], ((0, DT_PAD - SSD_HEADS), (0, 0)))
        Ws.append(dict(w_re_t=w_re_t, w_dt_t=w_dt_t, w_fi=full["ffn_w_in"][l], w_kv=full["mem_w_kv"][l], w_l=full["w_br_lru"][l],
                       w_s=full["w_br_ssd"][l], w_x=full["w_br_xa"][l], w_o=full["w_out"][l], w_fd=full["ffn_w_down"][l]))
        pad_h = lambda v: jnp.pad(v.reshape(1, -1), ((0, 0), (0, DT_PAD - SSD_HEADS)))
        Ps.append(dict(
            b_gate=full["b_gate"][l], lru_conv_w=full["lru_conv_w"][l], ssd_conv_w=full["ssd_conv_w"][l],
            lru_conv_b=row(w["lru_conv_b"][l]), lru_w_a=w["lru_w_a"][l].astype(BF16), lru_w_i=w["lru_w_i"][l].astype(BF16),
            lru_b_a=row(w["lru_b_a"][l]), lru_b_i=row(w["lru_b_i"][l]), lru_lambda=row(w["lru_lambda"][l]),
            ssd_conv_b=row(w["ssd_conv_b"][l]), dt_bias=pad_h(w["ssd_dt_bias"][l]), a_neg=pad_h(-jnp.exp(w["ssd_a_log"][l])),
            d_exp=jnp.broadcast_to(w["ssd_d"][l][:, None], (SSD_HEADS, SSD_HEAD_DIM)).reshape(1, D_SSD),
            ssd_norm_w=row(w["ssd_norm_w"][l]), ln1_g=row(w["ln1_g"][l]), ln1_b=row(w["ln1_b"][l]),
            ln2_g=row(w["ln2_g"][l]), ln2_b=row(w["ln2_b"][l])))

    saved = []
    xcur = x2d
    for l in range(DEPTH):
        xcur, sv = _layer_fwd(xcur, mem2, Ws[l], Ps[l], consts)
        saved.append(sv)
    dy, loss_part = _loss_fwd_bwd("loss", xcur, target, T)
    loss = lax.psum(loss_part[0, 0], ("x", "y", "c"))

    dys, coefs = [dy], [1.0]
    layer_grads = [None] * DEPTH
    for l in reversed(range(DEPTH)):
        dys, coefs, layer_grads[l] = _layer_bwd(dys, coefs, saved[l], mem2, Ws[l], Ps[l], consts)
    grad_x = _axpy("grad_x", coefs[0], dys[0], dys[1], T)[None]

    big = _reduce_big_grads({nm: [layer_grads[l][nm] for l in range(DEPTH)] for nm, _ in _BIG})
    stacked = {nm: jnp.stack([layer_grads[l][nm] for l in range(DEPTH)]) for nm in _SMALL}
    small_parts = [stacked[nm].reshape((DEPTH,) + tuple(sh)) for nm, sh in
                   ((nm, (3, D_MODEL) if nm == "b_gate" else (4, D_MODEL) if nm == "lru_conv_w" else
                     (4, D_XBC) if nm == "ssd_conv_w" else w[nm].shape[1:]) for nm in _SMALL)]
    small = dict(zip(_SMALL, _reduce_small_grads(small_parts)))
    for nm in _SMALL_SHARDED:
        cs = w[nm].shape[2]
        small[nm] = lax.dynamic_slice_in_dim(small[nm], j * cs, cs, axis=2)
    grads = {**big, **small}

    delta, new_m, new_v = {}, {}, {}
    for nm, _ in _BIG:
        shp = w[nm].shape
        two = lambda v: v.reshape(shp[0] * shp[1], shp[2])
        d_, m_, v_ = _adamw("adamw_" + nm, two(w[nm]), two(grads[nm]), two(a["m_" + nm]), two(a["v_" + nm]))
        delta[nm], new_m[nm], new_v[nm] = d_.reshape(shp), m_.reshape(shp), v_.reshape(shp)
    shapes = [w[nm].shape for nm in _SMALL]
    packs = [_flat_rows([src[nm] for nm in _SMALL], 8) for src in
             (w, grads, {nm: a["m_" + nm] for nm in _SMALL}, {nm: a["v_" + nm] for nm in _SMALL})]
    d_, m_, v_ = _adamw("adamw_small", *packs)
    for dst, buf in ((delta, d_), (new_m, m_), (new_v, v_)):
        dst.update(zip(_SMALL, _take_parts(buf.reshape(-1), shapes)))

    outs = [loss, grad_x]
    for group in (grads, delta, new_m, new_v):
        outs += [group[nm] for nm in _W_NAMES]
    return tuple(outs)


def kernel(x, mem, w_in, b_gate, lru_conv_w, lru_conv_b, lru_w_a, lru_b_a, lru_w_i, lru_b_i, lru_lambda, ssd_conv_w, ssd_conv_b, ssd_dt_bias, ssd_a_log, ssd_d, ssd_norm_w, mem_w_kv, w_br_lru, w_br_ssd, w_br_xa, w_out, ln1_g, ln1_b, ffn_w_in, ffn_w_down, ln2_g, ln2_b, loss_target, m_w_in, m_b_gate, m_lru_conv_w, m_lru_conv_b, m_lru_w_a, m_lru_b_a, m_lru_w_i, m_lru_b_i, m_lru_lambda, m_ssd_conv_w, m_ssd_conv_b, m_ssd_dt_bias, m_ssd_a_log, m_ssd_d, m_ssd_norm_w, m_mem_w_kv, m_w_br_lru, m_w_br_ssd, m_w_br_xa, m_w_out, m_ln1_g, m_ln1_b, m_ffn_w_in, m_ffn_w_down, m_ln2_g, m_ln2_b, v_w_in, v_b_gate, v_lru_conv_w, v_lru_conv_b, v_lru_w_a, v_lru_b_a, v_lru_w_i, v_lru_b_i, v_lru_lambda, v_ssd_conv_w, v_ssd_conv_b, v_ssd_dt_bias, v_ssd_a_log, v_ssd_d, v_ssd_norm_w, v_mem_w_kv, v_w_br_lru, v_w_br_ssd, v_w_br_xa, v_w_out, v_ln1_g, v_ln1_b, v_ffn_w_in, v_ffn_w_down, v_ln2_g, v_ln2_b):
    return _step(dict(locals()))
```

```python
import functools
import math

import jax
import jax.numpy as jnp
from jax import lax
from jax.experimental import pallas as pl
from jax.experimental.pallas import tpu as pltpu

F32, BF16 = jnp.float32, jnp.bfloat16
MESH = pl.DeviceIdType.MESH
VMEM_LIMIT_BYTES = 56 * 2**20
HALO = 16

D_MODEL = 1024
DEPTH = 2
CHUNK = 64
N_MEM = 256
LRU_BLOCKS = 8
LRU_BLOCK = 128
LRU_C = 8.0
D_SSD = 2048
SSD_HEADS = 32
SSD_HEAD_DIM = 64
SSD_GROUPS = 4
SSD_STATE = 128
D_BC = SSD_GROUPS * SSD_STATE
D_XBC = D_SSD + 2 * D_BC
XA_HEADS = 4
XA_HEAD_DIM = 256
D_FF = 2816
ALPHA = (2 * DEPTH) ** 0.25
EPS = 1e-5
N_IN = 11296
N_PROJ = 11264
DT_PAD = 128

ADAM_LR, ADAM_B1, ADAM_B2, ADAM_EPS, ADAM_WD, ADAM_STEP = 0.001, 0.9, 0.999, 1e-08, 0.01, 10

CB_XBC = (3072, 0)
CB_XS, CB_BM, CB_CM = (2048, 0), (512, 4), (512, 5)
CB_LOGITS = (3072, 1)
CB_G0, CB_G1, CB_G2 = (1024, 3), (1024, 4), (1024, 5)
CB_Z = (2048, 3)
CB_LRU_X, CB_LRU_GATE, CB_XA_Q = (1024, 8), (1024, 9), (1024, 10)


def _cparams(sem):
    return pltpu.CompilerParams(dimension_semantics=sem, vmem_limit_bytes=VMEM_LIMIT_BYTES)


def _pcall(name, body, n, ins, outs, scratch=(), reverse=False, aliases=None):
    def ridx(i):
        return (n - 1 - i) if reverse else i

    in_specs, args = [], []
    for sp in ins:
        kind, arr = sp[0], sp[1]
        if kind == "row":
            _, _, tile, width, cb = sp
            in_specs.append(pl.BlockSpec((tile, width), lambda i, cb=cb: (ridx(i), cb)))
        elif kind == "prev":
            _, _, tile, width, cb = sp
            t = tile // HALO
            in_specs.append(pl.BlockSpec((HALO, width), lambda i, cb=cb, t=t: (jnp.maximum(ridx(i) * t - 1, 0), cb)))
        elif kind == "next":
            _, _, tile, width, cb = sp
            t = tile // HALO
            last = arr.shape[0] // HALO - 1
            in_specs.append(pl.BlockSpec((HALO, width), lambda i, cb=cb, t=t, last=last: (jnp.minimum((ridx(i) + 1) * t, last), cb)))
        elif kind == "lead":
            nd = arr.ndim
            in_specs.append(pl.BlockSpec((1,) + arr.shape[1:], lambda i, nd=nd: (ridx(i),) + (0,) * (nd - 1)))
        elif kind == "full":
            nd = arr.ndim
            in_specs.append(pl.BlockSpec(arr.shape, lambda i, nd=nd: (0,) * nd))
        elif kind == "any":
            in_specs.append(pl.BlockSpec(memory_space=pl.ANY))
        else:
            raise ValueError(kind)
        args.append(arr)
    out_specs, out_shape = [], []
    for sp in outs:
        kind = sp[0]
        if kind == "row":
            _, rows, cols, dtype, tile, width, cb = sp
            out_shape.append(jax.ShapeDtypeStruct((rows, cols), dtype))
            out_specs.append(pl.BlockSpec((tile, width), lambda i, cb=cb: (ridx(i), cb)))
        elif kind == "lead":
            _, shape, dtype = sp
            nd = len(shape)
            out_shape.append(jax.ShapeDtypeStruct(shape, dtype))
            out_specs.append(pl.BlockSpec((1,) + tuple(shape[1:]), lambda i, nd=nd: (ridx(i),) + (0,) * (nd - 1)))
        elif kind == "full":
            _, shape, dtype = sp
            nd = len(shape)
            out_shape.append(jax.ShapeDtypeStruct(shape, dtype))
            out_specs.append(pl.BlockSpec(tuple(shape), lambda i, nd=nd: (0,) * nd))
        else:
            raise ValueError(kind)
    res = pl.pallas_call(
        body, name=name, grid=(n,), in_specs=in_specs, out_specs=out_specs, out_shape=out_shape,
        scratch_shapes=list(scratch), input_output_aliases=aliases or {},
        compiler_params=_cparams(("arbitrary",)),
    )(*args)
    return res


def _pick(n, cands):
    for c in cands:
        if n % c == 0:
            return c
    return n


_MM_TILES = (1024, 1408, 512, 256, 128)


def _mm(name, a, b, mode, out_dtype, add=None):
    if mode == "nn":
        (M, K), (K2, N) = a.shape, b.shape
    elif mode == "nt":
        (M, K), (N, K2) = a.shape, b.shape
    else:
        (K, M), (K2, N) = a.shape, b.shape
    assert K == K2, (name, a.shape, b.shape)
    tm = _pick(M, _MM_TILES)
    tn = _pick(N, _MM_TILES)
    if mode == "tn":
        tk = _pick(K, (1024, 512, 256))
    else:
        tk = K if K <= 2816 else _pick(K, _MM_TILES)
    nk = K // tk
    has_add = add is not None

    def body(*refs):
        if has_add:
            a_ref, b_ref, add_ref, o_ref, acc_ref = refs
        else:
            a_ref, b_ref, o_ref, acc_ref = refs
        k = pl.program_id(2)
        av = a_ref[...].astype(BF16)
        bv = b_ref[...].astype(BF16)
        if mode == "nn":
            p = jnp.dot(av, bv, preferred_element_type=F32)
        elif mode == "nt":
            p = lax.dot_general(av, bv, (((1,), (1,)), ((), ())), preferred_element_type=F32)
        else:
            p = lax.dot_general(av, bv, (((0,), (0,)), ((), ())), preferred_element_type=F32)

        def fin(v):
            if has_add:
                v = v + add_ref[...].astype(F32)
            o_ref[...] = v.astype(out_dtype)

        if nk == 1:
            fin(p)
        else:
            @pl.when(k == 0)
            def _():
                acc_ref[...] = p

            @pl.when(k > 0)
            def _():
                acc_ref[...] += p

            @pl.when(k == nk - 1)
            def _():
                fin(acc_ref[...])

    if mode == "nn":
        specs = [pl.BlockSpec((tm, tk), lambda i, j, k: (i, k)), pl.BlockSpec((tk, tn), lambda i, j, k: (k, j))]
    elif mode == "nt":
        specs = [pl.BlockSpec((tm, tk), lambda i, j, k: (i, k)), pl.BlockSpec((tn, tk), lambda i, j, k: (j, k))]
    else:
        specs = [pl.BlockSpec((tk, tm), lambda i, j, k: (k, i)), pl.BlockSpec((tk, tn), lambda i, j, k: (k, j))]
    args = [a, b]
    if has_add:
        specs.append(pl.BlockSpec((tm, tn), lambda i, j, k: (i, j)))
        args.append(add)
    acc_shape = (tm, tn) if nk > 1 else (8, 128)
    return pl.pallas_call(
        body, name=name, grid=(M // tm, N // tn, nk), in_specs=specs,
        out_specs=pl.BlockSpec((tm, tn), lambda i, j, k: (i, j)),
        out_shape=jax.ShapeDtypeStruct((M, N), out_dtype),
        scratch_shapes=[pltpu.VMEM(acc_shape, F32)],
        compiler_params=_cparams(("parallel", "parallel", "arbitrary")),
    )(*args)


def _sigmoid(x):
    return 1.0 / (1.0 + jnp.exp(-x))


def _silu(x):
    return x * _sigmoid(x)


def _dsilu(x):
    s = _sigmoid(x)
    return s * (1.0 + x * (1.0 - s))


def _softplus(x):
    return jnp.maximum(x, 0.0) + jnp.log(1.0 + jnp.exp(-jnp.abs(x)))


_GELU_C = math.sqrt(2.0 / math.pi)


def _gelu(x):
    return 0.5 * x * (1.0 + jnp.tanh(_GELU_C * (x + 0.044715 * x * x * x)))


def _dgelu(x):
    t = jnp.tanh(_GELU_C * (x + 0.044715 * x * x * x))
    return 0.5 * (1.0 + t) + 0.5 * x * (1.0 - t * t) * _GELU_C * (1.0 + 3.0 * 0.044715 * x * x)


def _acc(ref, i, val):
    @pl.when(i == 0)
    def _():
        ref[...] = val

    @pl.when(i > 0)
    def _():
        ref[...] += val


def _rows(shape):
    return lax.broadcasted_iota(jnp.int32, shape, 0)


def _shift_down(x, k, halo8, first):
    r = pltpu.roll(x, k, 0)
    h = pltpu.roll(halo8, k, 0)
    h = jnp.where(first, 0.0, h)
    head = jnp.where(_rows(h.shape) < k, h, r[:8])
    return jnp.concatenate([head, r[8:]], axis=0)


def _shift_up(x, k, halo8, last):
    T = x.shape[0]
    r = pltpu.roll(x, T - k, 0)
    h = pltpu.roll(halo8, 8 - k, 0)
    h = jnp.where(last, 0.0, h)
    tail = jnp.where(_rows(h.shape) >= 8 - k, h, r[T - 8:])
    return jnp.concatenate([r[:T - 8], tail], axis=0)


def _ln_fwd(name, a, b, g, beta, tile):
    S, Dm = a.shape
    n = S // tile

    def body(a_ref, b_ref, g_ref, be_ref, y_ref, xh_ref, rs_ref):
        z = ALPHA * a_ref[...] + b_ref[...].astype(F32)
        mu = jnp.mean(z, axis=-1, keepdims=True)
        zc = z - mu
        var = jnp.mean(zc * zc, axis=-1, keepdims=True)
        rstd = lax.rsqrt(var + EPS)
        xh = zc * rstd
        y_ref[...] = xh * g_ref[...] + be_ref[...]
        xh_ref[...] = xh
        rs_ref[...] = rstd

    return _pcall(name, body, n,
                  [("row", a, tile, Dm, 0), ("row", b, tile, Dm, 0), ("full", g), ("full", beta)],
                  [("row", S, Dm, F32, tile, Dm, 0), ("row", S, Dm, F32, tile, Dm, 0), ("row", S, 1, F32, tile, 1, 0)])


def _ln_bwd(name, dys, coefs, xh, rstd, g, tile):
    S, Dm = xh.shape
    n = S // tile
    nd = len(dys)

    def body(*refs):
        dy_refs = refs[:nd]
        xh_ref, rs_ref, g_ref, dz_ref, dg_ref, db_ref = refs[nd:]
        i = pl.program_id(0)
        dy = coefs[0] * dy_refs[0][...].astype(F32)
        for k in range(1, nd):
            dy = dy + coefs[k] * dy_refs[k][...].astype(F32)
        xh_v = xh_ref[...]
        dxh = dy * g_ref[...]
        m1 = jnp.mean(dxh, axis=-1, keepdims=True)
        m2 = jnp.mean(dxh * xh_v, axis=-1, keepdims=True)
        dz_ref[...] = rs_ref[...] * (dxh - m1 - xh_v * m2)
        _acc(dg_ref, i, jnp.sum(dy * xh_v, axis=0, keepdims=True))
        _acc(db_ref, i, jnp.sum(dy, axis=0, keepdims=True))

    ins = [("row", d, tile, Dm, 0) for d in dys]
    ins += [("row", xh, tile, Dm, 0), ("row", rstd, tile, 1, 0), ("full", g)]
    return _pcall(name, body, n, ins,
                  [("row", S, Dm, F32, tile, Dm, 0), ("full", (1, Dm), F32), ("full", (1, Dm), F32)])


def _loss_fwd_bwd(name, y, target, tile):
    S, Dm = y.shape
    n = S // tile

    def body(y_ref, t_ref, dy_ref, l_ref):
        i = pl.program_id(0)
        err = y_ref[...] - t_ref[...]
        dy_ref[...] = err * (1.0 / Dm)
        part = jnp.sum(jnp.sum(err * err, axis=-1, keepdims=True), axis=0, keepdims=True) * (0.5 / Dm)
        _acc(l_ref, i, part)

    return _pcall(name, body, n, [("row", y, tile, Dm, 0), ("row", target, tile, Dm, 0)],
                  [("row", S, Dm, F32, tile, Dm, 0), ("full", (1, 1), F32)])


def _swiglu_fwd(name, gu, tile):
    S = gu.shape[0]
    n = S // tile

    def body(g_ref, u_ref, o_ref):
        o_ref[...] = (_silu(g_ref[...].astype(F32)) * u_ref[...].astype(F32)).astype(BF16)

    return _pcall(name, body, n, [("row", gu, tile, D_FF, 0), ("row", gu, tile, D_FF, 1)],
                  [("row", S, D_FF, BF16, tile, D_FF, 0)])[0]


def _swiglu_bwd(name, gu, dh, tile):
    S = gu.shape[0]
    n = S // tile

    def body(g_ref, u_ref, dh_ref, o_ref):
        gv = g_ref[...].astype(F32)
        uv = u_ref[...].astype(F32)
        dv = dh_ref[...].astype(F32)
        dg = dv * uv * _dsilu(gv)
        du = dv * _silu(gv)
        o_ref[...] = jnp.concatenate([dg, du], axis=1).astype(BF16)

    return _pcall(name, body, n, [("row", gu, tile, D_FF, 0), ("row", gu, tile, D_FF, 1), ("row", dh, tile, D_FF, 0)],
                  [("row", S, 2 * D_FF, BF16, tile, 2 * D_FF, 0)])[0]


def _merge_fwd(name, proj, b_gate, pl_, ps_, px_, tile):
    S = proj.shape[0]
    n = S // tile
    Dm = D_MODEL

    def body(l0, l1, l2, bg, p0, p1, p2, o_ref):
        bgv = bg[...]
        acc = _sigmoid(l0[...].astype(F32) + bgv[0:1]) * p0[...].astype(F32)
        acc = acc + _sigmoid(l1[...].astype(F32) + bgv[1:2]) * p1[...].astype(F32)
        acc = acc + _sigmoid(l2[...].astype(F32) + bgv[2:3]) * p2[...].astype(F32)
        o_ref[...] = acc.astype(BF16)

    ins = [("row", proj, tile, *CB_G0), ("row", proj, tile, *CB_G1), ("row", proj, tile, *CB_G2), ("full", b_gate),
           ("row", pl_, tile, Dm, 0), ("row", ps_, tile, Dm, 0), ("row", px_, tile, Dm, 0)]
    return _pcall(name, body, n, ins, [("row", S, Dm, BF16, tile, Dm, 0)])[0]


def _merge_bwd(name, proj, b_gate, pl_, ps_, px_, dmerged, tile):
    S = proj.shape[0]
    n = S // tile
    Dm = D_MODEL

    def body(l0, l1, l2, bg, p0, p1, p2, dm_ref, dproj_ref, d0, d1, d2, db0, db1, db2):
        i = pl.program_id(0)
        bgv = bg[...]
        dm = dm_ref[...].astype(F32)
        dls = []
        for k, (lr, pr, dr, dbr) in enumerate(((l0, p0, d0, db0), (l1, p1, d1, db1), (l2, p2, d2, db2))):
            gk = _sigmoid(lr[...].astype(F32) + bgv[k:k + 1])
            dr[...] = (dm * gk).astype(BF16)
            dl = dm * pr[...].astype(F32) * gk * (1.0 - gk)
            _acc(dbr, i, jnp.sum(dl, axis=0, keepdims=True))
            dls.append(dl)
        dproj_ref[...] = jnp.concatenate(dls, axis=1).astype(BF16)

    ins = [("row", proj, tile, *CB_G0), ("row", proj, tile, *CB_G1), ("row", proj, tile, *CB_G2), ("full", b_gate),
           ("row", pl_, tile, Dm, 0), ("row", ps_, tile, Dm, 0), ("row", px_, tile, Dm, 0), ("row", dmerged, tile, Dm, 0)]
    outs = [("row", S, N_PROJ, BF16, tile, *CB_LOGITS)] + [("row", S, Dm, BF16, tile, Dm, 0)] * 3 + [("full", (1, Dm), F32)] * 3
    return _pcall(name, body, n, ins, outs)


def _conv_taps(xf, halo8, first, w):
    out = xf * w[3:4]
    for k in (1, 2, 3):
        out = out + _shift_down(xf, k, halo8, first) * w[3 - k:4 - k]
    return out


def _conv_fwd(name, src, cb, w, b, act, out_dtype, tile):
    S = src.shape[0]
    C = cb[0]
    n = S // tile

    def body(x_ref, p_ref, w_ref, b_ref, o_ref):
        i = pl.program_id(0)
        xf = x_ref[...].astype(F32)
        halo8 = p_ref[...].astype(F32)[HALO - 8:]
        pre = _conv_taps(xf, halo8, i == 0, w_ref[...]) + b_ref[...]
        o_ref[...] = (_silu(pre) if act else pre).astype(out_dtype)

    return _pcall(name, body, n, [("row", src, tile, *cb), ("prev", src, tile, *cb), ("full", w), ("full", b)],
                  [("row", S, C, out_dtype, tile, C, 0)])[0]


def _conv_act_bwd(name, src, cb, w, b, dout, tile):
    S = src.shape[0]
    C = cb[0]
    n = S // tile

    def body(x_ref, p_ref, w_ref, b_ref, d_ref, o_ref):
        i = pl.program_id(0)
        xf = x_ref[...].astype(F32)
        halo8 = p_ref[...].astype(F32)[HALO - 8:]
        pre = _conv_taps(xf, halo8, i == 0, w_ref[...]) + b_ref[...]
        o_ref[...] = d_ref[...].astype(F32) * _dsilu(pre)

    return _pcall(name, body, n, [("row", src, tile, *cb), ("prev", src, tile, *cb), ("full", w), ("full", b),
                                  ("row", dout, tile, C, 0)],
                  [("row", S, C, F32, tile, C, 0)])[0]


def _conv_bwd(name, src, cb, w, dpre, dproj, tile):
    S = src.shape[0]
    C = cb[0]
    n = S // tile

    def body(x_ref, p_ref, w_ref, d_ref, nx_ref, buf_ref, dx_ref, dw0, dw1, dw2, dw3, db_ref):
        i = pl.program_id(0)
        first, last = i == 0, i == n - 1
        xf = x_ref[...].astype(F32)
        halo8 = p_ref[...].astype(F32)[HALO - 8:]
        dv = d_ref[...]
        nx8 = nx_ref[...][:8]
        wv = w_ref[...]
        dx = dv * wv[3:4]
        for k in (1, 2, 3):
            dx = dx + _shift_up(dv, k, nx8, last) * wv[3 - k:4 - k]
        dx_ref[...] = dx.astype(BF16)
        _acc(dw3, i, jnp.sum(dv * xf, axis=0, keepdims=True))
        for k, dwr in ((1, dw2), (2, dw1), (3, dw0)):
            _acc(dwr, i, jnp.sum(dv * _shift_down(xf, k, halo8, first), axis=0, keepdims=True))
        _acc(db_ref, i, jnp.sum(dv, axis=0, keepdims=True))

    ins = [("row", src, tile, *cb), ("prev", src, tile, *cb), ("full", w), ("row", dpre, tile, C, 0),
           ("next", dpre, tile, C, 0), ("any", dproj)]
    outs = [("row", S, N_PROJ, BF16, tile, *cb)] + [("full", (1, C), F32)] * 5
    return _pcall(name, body, n, ins, outs, aliases={5: 0})


def _lru_gates(xc, wa_ref, wi_ref, ba, bi, lam):
    xb = xc.astype(BF16)
    pa, pi_ = [], []
    for nb in range(LRU_BLOCKS):
        sl = slice(nb * LRU_BLOCK, (nb + 1) * LRU_BLOCK)
        pa.append(jnp.dot(xb[:, sl], wa_ref[nb], preferred_element_type=F32))
        pi_.append(jnp.dot(xb[:, sl], wi_ref[nb], preferred_element_type=F32))
    r = _sigmoid(jnp.concatenate(pa, axis=1) + ba)
    ig = _sigmoid(jnp.concatenate(pi_, axis=1) + bi)
    sp = _softplus(-lam)
    a = jnp.exp(-LRU_C * r * sp)
    m = jnp.sqrt(1.0 - a * a)
    return xb, r, ig, sp, a, m


def _lru_fwd(name, xc, proj, wa, wi, ba, bi, lam, tile):
    S = xc.shape[0]
    n = S // tile
    C = D_MODEL

    def body(xc_ref, gate_ref, wa_ref, wi_ref, ba_ref, bi_ref, lam_ref, y_ref, h_ref, carry):
        i = pl.program_id(0)

        @pl.when(i == 0)
        def _():
            carry[...] = jnp.zeros_like(carry)

        xcv = xc_ref[...]
        _, r, ig, sp, a, m = _lru_gates(xcv, wa_ref, wi_ref, ba_ref[...], bi_ref[...], lam_ref[...])
        u = m * (ig * xcv)
        rows = _rows(a.shape)
        d = 1
        while d < tile:
            keep = rows >= d
            a_s = jnp.where(keep, pltpu.roll(a, d, 0), 1.0)
            u_s = jnp.where(keep, pltpu.roll(u, d, 0), 0.0)
            u = a * u_s + u
            a = a * a_s
            d *= 2
        h = u + a * carry[0:1, :]
        h_ref[...] = h
        carry[0:1, :] = h_ref[pl.ds(tile - 1, 1), :]
        y_ref[...] = (_gelu(gate_ref[...].astype(F32)) * h).astype(BF16)

    ins = [("row", xc, tile, C, 0), ("row", proj, tile, *CB_LRU_GATE), ("full", wa), ("full", wi),
           ("full", ba), ("full", bi), ("full", lam)]
    return _pcall(name, body, n, ins, [("row", S, C, BF16, tile, C, 0), ("row", S, C, F32, tile, C, 0)],
                  scratch=[pltpu.VMEM((8, C), F32)])


def _lru_bwd(name, dy, xc, proj, h, wa, wi, ba, bi, lam, dproj, tile):
    S = xc.shape[0]
    n = S // tile
    C = D_MODEL

    def body(dy_ref, xc_ref, gate_ref, h_ref, hp_ref, wa_ref, wi_ref, ba_ref, bi_ref, lam_ref, buf_ref,
             dg_ref, dxc_ref, dwa_ref, dwi_ref, dba_ref, dbi_ref, dlam_ref, carry):
        i = pl.program_id(0)
        first_tile = i == n - 1

        @pl.when(i == 0)
        def _():
            carry[...] = jnp.zeros_like(carry)

        xcv = xc_ref[...]
        lamv = lam_ref[...]
        xb, r, ig, sp, a, m = _lru_gates(xcv, wa_ref, wi_ref, ba_ref[...], bi_ref[...], lamv)
        hv = h_ref[...]
        gv = gate_ref[...].astype(F32)
        dyv = dy_ref[...].astype(F32)
        dg_ref[...] = (dyv * hv * _dgelu(gv)).astype(BF16)
        v = dyv * _gelu(gv)
        rows = _rows(a.shape)
        bcoef = jnp.where(rows == tile - 1, 1.0, pltpu.roll(a, tile - 1, 0))
        d = 1
        while d < tile:
            keep = rows < tile - d
            b_s = jnp.where(keep, pltpu.roll(bcoef, tile - d, 0), 1.0)
            v_s = jnp.where(keep, pltpu.roll(v, tile - d, 0), 0.0)
            v = v + bcoef * v_s
            bcoef = bcoef * b_s
            d *= 2
        dH = v + bcoef * carry[0:1, :]
        dxc_ref[...] = dH
        carry[0:1, :] = dxc_ref[pl.ds(0, 1), :] * a[0:1, :]
        halo8 = hp_ref[...][HALO - 8:]
        hprev = _shift_down(hv, 1, halo8, first_tile)
        da = dH * hprev
        ix = ig * xcv
        dm = dH * ix
        di = dH * m * xcv
        dxc = dH * m * ig
        da = da - dm * a / m
        dla = da * a
        dr = dla * (-LRU_C) * sp
        _acc(dlam_ref, i, jnp.sum(dla * (-LRU_C) * r, axis=0, keepdims=True) * (-_sigmoid(-lamv)))
        dpa = dr * r * (1.0 - r)
        dpi = di * ig * (1.0 - ig)
        _acc(dba_ref, i, jnp.sum(dpa, axis=0, keepdims=True))
        _acc(dbi_ref, i, jnp.sum(dpi, axis=0, keepdims=True))
        dpab, dpib = dpa.astype(BF16), dpi.astype(BF16)
        back = []
        for nb in range(LRU_BLOCKS):
            sl = slice(nb * LRU_BLOCK, (nb + 1) * LRU_BLOCK)
            back.append(lax.dot_general(dpab[:, sl], wa_ref[nb], (((1,), (1,)), ((), ())), preferred_element_type=F32)
                        + lax.dot_general(dpib[:, sl], wi_ref[nb], (((1,), (1,)), ((), ())), preferred_element_type=F32))
            ga = lax.dot_general(xb[:, sl], dpab[:, sl], (((0,), (0,)), ((), ())), preferred_element_type=F32)
            gi = lax.dot_general(xb[:, sl], dpib[:, sl], (((0,), (0,)), ((), ())), preferred_element_type=F32)

            @pl.when(i == 0)
            def _(ga=ga, gi=gi, nb=nb):
                dwa_ref[nb] = ga
                dwi_ref[nb] = gi

            @pl.when(i > 0)
            def _(ga=ga, gi=gi, nb=nb):
                dwa_ref[nb] += ga
                dwi_ref[nb] += gi

        dxc_ref[...] = dxc + jnp.concatenate(back, axis=1)

    ins = [("row", dy, tile, C, 0), ("row", xc, tile, C, 0), ("row", proj, tile, *CB_LRU_GATE), ("row", h, tile, C, 0),
           ("prev", h, tile, C, 0), ("full", wa), ("full", wi), ("full", ba), ("full", bi), ("full", lam), ("any", dproj)]
    outs = [("row", S, N_PROJ, BF16, tile, *CB_LRU_GATE), ("row", S, C, F32, tile, C, 0),
            ("full", (LRU_BLOCKS, LRU_BLOCK, LRU_BLOCK), F32), ("full", (LRU_BLOCKS, LRU_BLOCK, LRU_BLOCK), F32),
            ("full", (1, C), F32), ("full", (1, C), F32), ("full", (1, C), F32)]
    return _pcall(name, body, n, ins, outs, scratch=[pltpu.VMEM((8, C), F32)], reverse=True, aliases={10: 0})


def _split3(x):
    h = x.astype(BF16)
    r = x - h.astype(F32)
    m = r.astype(BF16)
    lo = (r - m.astype(F32)).astype(BF16)
    return h, m, lo


def _dot01_r(x, e):
    h, m, lo = _split3(x)
    return (jnp.dot(h, e, preferred_element_type=F32) + jnp.dot(m, e, preferred_element_type=F32)
            + jnp.dot(lo, e, preferred_element_type=F32))


def _dot01_l(e, x):
    h, m, lo = _split3(x)
    return (jnp.dot(e, h, preferred_element_type=F32) + jnp.dot(e, m, preferred_element_type=F32)
            + jnp.dot(e, lo, preferred_element_type=F32))


def _ssd_consts():
    hh = lax.broadcasted_iota(jnp.int32, (DT_PAD, D_SSD), 0)
    cc = lax.broadcasted_iota(jnp.int32, (DT_PAD, D_SSD), 1)
    e = (cc // SSD_HEAD_DIM == hh).astype(BF16)
    li = lax.broadcasted_iota(jnp.int32, (CHUNK, CHUNK), 0)
    si = lax.broadcasted_iota(jnp.int32, (CHUNK, CHUNK), 1)
    ltri = (li >= si).astype(BF16)
    l4 = lax.broadcasted_iota(jnp.int32, (CHUNK, 4 * CHUNK), 0)
    s4 = lax.broadcasted_iota(jnp.int32, (CHUNK, 4 * CHUNK), 1) % CHUNK
    itile = (l4 == s4).astype(F32)
    causal = (l4 >= s4).astype(F32)
    j4 = lax.broadcasted_iota(jnp.int32, (8, 4 * CHUNK), 0)
    c4 = lax.broadcasted_iota(jnp.int32, (8, 4 * CHUNK), 1) // CHUNK
    hmask = (j4 == c4).astype(F32)
    return e, e.T, ltri, ltri.T, itile, causal, hmask


def _ssd_chunk_common(xs_ref, bm_ref, cm_ref, dt_ref, dtb_ref, a_ref, e_ref, ltri_ref):
    xs = xs_ref[...].astype(F32)
    raw = dt_ref[...] + dtb_ref[...]
    dtv = _softplus(raw)
    da = dtv * a_ref[...]
    cs = _dot01_l(ltri_ref[...], da)
    e = e_ref[...]
    dte = _dot01_r(dtv, e)
    ce = _dot01_r(cs, e)
    xdt = xs * dte
    cle = ce[CHUNK - 1:CHUNK, :]
    dend = jnp.exp(cle - ce)
    ecs = jnp.exp(ce)
    return xs, raw, dtv, cs, dte, ce, xdt, cle, dend, ecs


def _quad_terms(ce_q, cb4, itile, causal):
    cr = jnp.sum(ce_q * itile, axis=0, keepdims=True)
    seg = ce_q - cr
    dec = jnp.where(causal > 0.0, jnp.exp(jnp.minimum(seg, 0.0)), 0.0)
    return dec, cb4 * dec


def _block_diag4(xq, hmask):
    return jnp.concatenate([xq * hmask[j:j + 1].astype(xq.dtype) for j in range(4)], axis=0)


def _ssd_fwd(name, xbc, dt_raw, dt_bias, a_neg, d_exp, consts):
    S = xbc.shape[0]
    nc = S // CHUNK
    e, et, ltri, ltri_t, itile, causal, hmask = consts

    def body(xs_ref, bm_ref, cm_ref, dt_ref, dtb_ref, a_ref, dex_ref, e_ref, ltri_ref, it_ref, ca_ref, hm_ref,
             y_ref, prev_ref, hst):
        i = pl.program_id(0)

        @pl.when(i == 0)
        def _():
            hst[...] = jnp.zeros_like(hst)

        xs, raw, dtv, cs, dte, ce, xdt, cle, dend, ecs = _ssd_chunk_common(
            xs_ref, bm_ref, cm_ref, dt_ref, dtb_ref, a_ref, e_ref, ltri_ref)
        xdtb = xdt.astype(BF16)
        xst = (xdt * dend).astype(BF16)
        ecl = jnp.exp(cle)
        itile_v, causal_v, hmask_v = it_ref[...], ca_ref[...], hm_ref[...]
        bm = bm_ref[...]
        cm = cm_ref[...]
        dskip = dex_ref[...] * xs
        for g in range(SSD_GROUPS):
            gs = slice(g * 512, (g + 1) * 512)
            ns = slice(g * SSD_STATE, (g + 1) * SSD_STATE)
            bm_g, cm_g = bm[:, ns], cm[:, ns]
            hprev = hst[g]
            hprev_b = hprev.astype(BF16)
            prev_ref[0, g] = hprev_b
            yoff = jnp.dot(cm_g, hprev_b, preferred_element_type=F32) * ecs[:, gs]
            st = lax.dot_general(bm_g, xst[:, gs], (((0,), (0,)), ((), ())), preferred_element_type=F32)
            hst[g] = hprev * ecl[:, gs] + st
            b4 = jnp.concatenate([bm_g] * 4, axis=0)
            cb4 = lax.dot_general(cm_g, b4, (((1,), (1,)), ((), ())), preferred_element_type=F32)
            for q in range(2):
                cols = slice(g * 512 + q * 256, g * 512 + (q + 1) * 256)
                _, mq = _quad_terms(ce[:, cols], cb4, itile_v, causal_v)
                xbd = _block_diag4(xdtb[:, cols], hmask_v)
                ydiag = jnp.dot(mq.astype(BF16), xbd, preferred_element_type=F32)
                y_ref[:, cols] = ydiag + yoff[:, q * 256:(q + 1) * 256] + dskip[:, cols]

    ins = [("row", xbc, CHUNK, *CB_XS), ("row", xbc, CHUNK, *CB_BM), ("row", xbc, CHUNK, *CB_CM),
           ("row", dt_raw, CHUNK, DT_PAD, 0), ("full", dt_bias), ("full", a_neg), ("full", d_exp),
           ("full", e), ("full", ltri), ("full", itile), ("full", causal), ("full", hmask)]
    outs = [("row", S, D_SSD, F32, CHUNK, D_SSD, 0), ("lead", (nc, SSD_GROUPS, SSD_STATE, 512), BF16)]
    return _pcall(name, body, nc, ins, outs, scratch=[pltpu.VMEM((SSD_GROUPS, SSD_STATE, 512), F32)])


def _ssd_bwd(name, xbc, dt_raw, dt_bias, a_neg, d_exp, prev, dy, consts):
    S = xbc.shape[0]
    nc = S // CHUNK
    e, et, ltri, ltri_t, itile, causal, hmask = consts

    def body(xs_ref, bm_ref, cm_ref, dt_ref, dtb_ref, a_ref, dex_ref, prev_ref, dy_ref,
             e_ref, et_ref, ltri_ref, ltt_ref, it_ref, ca_ref, hm_ref,
             dx_ref, ddt_ref, da_ref, dbias_ref, dd_ref, dh, dce_ref, dxdt_ref):
        i = pl.program_id(0)

        @pl.when(i == 0)
        def _():
            dh[...] = jnp.zeros_like(dh)

        xs, raw, dtv, cs, dte, ce, xdt, cle, dend, ecs = _ssd_chunk_common(
            xs_ref, bm_ref, cm_ref, dt_ref, dtb_ref, a_ref, e_ref, ltri_ref)
        xdtb = xdt.astype(BF16)
        xst = (xdt * dend).astype(BF16)
        ecl = jnp.exp(cle)
        itile_v, causal_v, hmask_v = it_ref[...], ca_ref[...], hm_ref[...]
        bm = bm_ref[...]
        cm = cm_ref[...]
        dyv = dy_ref[...]
        last_row = _rows((CHUNK, 512)) == CHUNK - 1
        for g in range(SSD_GROUPS):
            gs = slice(g * 512, (g + 1) * 512)
            ns = slice(g * SSD_STATE, (g + 1) * SSD_STATE)
            bm_g, cm_g = bm[:, ns], cm[:, ns]
            hprev_b = prev_ref[0, g]
            dhn = dh[g]
            dhn_b = dhn.astype(BF16)
            dy_g = dyv[:, gs]
            ecs_g, dend_g, xdt_g, ecl_g = ecs[:, gs], dend[:, gs], xdt[:, gs], ecl[:, gs]
            z = jnp.dot(cm_g, hprev_b, preferred_element_type=F32)
            dz = dy_g * ecs_g
            dzb = dz.astype(BF16)
            dce_g = dz * z
            dcm_g = lax.dot_general(dzb, hprev_b, (((1,), (1,)), ((), ())), preferred_element_type=F32)
            dprev = lax.dot_general(cm_g, dzb, (((0,), (0,)), ((), ())), preferred_element_type=F32) + dhn * ecl_g
            dcl = jnp.sum(dhn * hprev_b.astype(F32), axis=0, keepdims=True) * ecl_g
            gmat = jnp.dot(bm_g, dhn_b, preferred_element_type=F32)
            dbm_g = lax.dot_general(xst[:, gs], dhn_b, (((1,), (1,)), ((), ())), preferred_element_type=F32)
            dxdt_g = gmat * dend_g
            t = gmat * xdt_g * dend_g
            dce_g = dce_g - t
            dcl = dcl + jnp.sum(t, axis=0, keepdims=True)
            dce_g = dce_g + jnp.where(last_row, dcl, 0.0)
            dh[g] = dprev
            b4 = jnp.concatenate([bm_g] * 4, axis=0)
            cb4 = lax.dot_general(cm_g, b4, (((1,), (1,)), ((), ())), preferred_element_type=F32)
            for q in range(2):
                qs = slice(q * 256, (q + 1) * 256)
                cols = slice(g * 512 + q * 256, g * 512 + (q + 1) * 256)
                dec, mq = _quad_terms(ce[:, cols], cb4, itile_v, causal_v)
                mqb = mq.astype(BF16)
                xbd = _block_diag4(xdtb[:, cols], hmask_v)
                dyq = dy_g[:, qs].astype(BF16)
                dm = lax.dot_general(dyq, xbd, (((1,), (1,)), ((), ())), preferred_element_type=F32)
                rmat = lax.dot_general(mqb, dyq, (((0,), (0,)), ((), ())), preferred_element_type=F32)
                dxq = rmat[0:64] * hmask_v[0:1]
                for j in range(1, 4):
                    dxq = dxq + rmat[64 * j:64 * (j + 1)] * hmask_v[j:j + 1]
                tq = dm * dec
                tqb = tq.astype(BF16)
                dcm_g = dcm_g + jnp.dot(tqb, b4, preferred_element_type=F32)
                rb = lax.dot_general(tqb, cm_g, (((0,), (0,)), ((), ())), preferred_element_type=F32)
                dbm_g = dbm_g + rb[0:64] + rb[64:128] + rb[128:192] + rb[192:256]
                dseg = tq * cb4
                colsum = jnp.sum(dseg, axis=0, keepdims=True)
                dce_ref[:, cols] = dce_g[:, qs] + dseg - itile_v * colsum
                dxdt_ref[:, cols] = dxdt_g[:, qs] + dxq
            dx_ref[:, D_SSD + g * SSD_STATE:D_SSD + (g + 1) * SSD_STATE] = dbm_g
            dx_ref[:, D_SSD + D_BC + g * SSD_STATE:D_SSD + D_BC + (g + 1) * SSD_STATE] = dcm_g
        dxdt = dxdt_ref[...]
        dexv = dex_ref[...]
        dx_ref[:, 0:D_SSD] = dxdt * dte + dyv * dexv
        _acc(dd_ref, i, jnp.sum(dyv * xs, axis=0, keepdims=True))
        etv = et_ref[...]
        dcs = _dot01_r(dce_ref[...], etv)
        dda = _dot01_l(ltt_ref[...], dcs)
        av = a_ref[...]
        ddtv = dda * av + _dot01_r(dxdt * xs, etv)
        _acc(da_ref, i, jnp.sum(dda * dtv, axis=0, keepdims=True))
        draw = ddtv * _sigmoid(raw)
        ddt_ref[...] = draw.astype(BF16)
        _acc(dbias_ref, i, jnp.sum(draw, axis=0, keepdims=True))

    ins = [("row", xbc, CHUNK, *CB_XS), ("row", xbc, CHUNK, *CB_BM), ("row", xbc, CHUNK, *CB_CM),
           ("row", dt_raw, CHUNK, DT_PAD, 0), ("full", dt_bias), ("full", a_neg), ("full", d_exp),
           ("lead", prev), ("row", dy, CHUNK, D_SSD, 0),
           ("full", e), ("full", et), ("full", ltri), ("full", ltri_t), ("full", itile), ("full", causal), ("full", hmask)]
    outs = [("row", S, D_XBC, F32, CHUNK, D_XBC, 0), ("row", S, DT_PAD, BF16, CHUNK, DT_PAD, 0),
            ("full", (1, DT_PAD), F32), ("full", (1, DT_PAD), F32), ("full", (1, D_SSD), F32)]
    scratch = [pltpu.VMEM((SSD_GROUPS, SSD_STATE, 512), F32), pltpu.VMEM((CHUNK, D_SSD), F32), pltpu.VMEM((CHUNK, D_SSD), F32)]
    return _pcall(name, body, nc, ins, outs, scratch=scratch, reverse=True)


def _gate_norm_fwd(name, ycore, proj, norm_w, tile):
    S = ycore.shape[0]
    n = S // tile

    def body(y_ref, z_ref, w_ref, o_ref):
        y2 = y_ref[...] * _silu(z_ref[...].astype(F32))
        wv = w_ref[...]
        for g in range(SSD_GROUPS):
            gs = slice(g * 512, (g + 1) * 512)
            seg = y2[:, gs]
            r = lax.rsqrt(jnp.mean(seg * seg, axis=-1, keepdims=True) + EPS)
            o_ref[:, gs] = (seg * r * wv[:, gs]).astype(BF16)

    return _pcall(name, body, n, [("row", ycore, tile, D_SSD, 0), ("row", proj, tile, *CB_Z), ("full", norm_w)],
                  [("row", S, D_SSD, BF16, tile, D_SSD, 0)])[0]


def _gate_norm_bwd(name, dout, ycore, proj, norm_w, dproj, tile):
    S = ycore.shape[0]
    n = S // tile

    def body(do_ref, y_ref, z_ref, w_ref, buf_ref, dz_ref, dy_ref, dw_ref):
        i = pl.program_id(0)
        yv = y_ref[...]
        zv = z_ref[...].astype(F32)
        sz = _silu(zv)
        y2 = yv * sz
        dov = do_ref[...].astype(F32)
        wv = w_ref[...]
        dws, dy2s = [], []
        for g in range(SSD_GROUPS):
            gs = slice(g * 512, (g + 1) * 512)
            seg = y2[:, gs]
            r = lax.rsqrt(jnp.mean(seg * seg, axis=-1, keepdims=True) + EPS)
            yn = seg * r
            dws.append(jnp.sum(dov[:, gs] * yn, axis=0, keepdims=True))
            dyn = dov[:, gs] * wv[:, gs]
            dy2s.append(r * (dyn - yn * jnp.mean(dyn * yn, axis=-1, keepdims=True)))
        dy2 = jnp.concatenate(dy2s, axis=1)
        dy_ref[...] = dy2 * sz
        dz_ref[...] = (dy2 * yv * _dsilu(zv)).astype(BF16)
        _acc(dw_ref, i, jnp.concatenate(dws, axis=1))

    ins = [("row", dout, tile, D_SSD, 0), ("row", ycore, tile, D_SSD, 0), ("row", proj, tile, *CB_Z), ("full", norm_w),
           ("any", dproj)]
    outs = [("row", S, N_PROJ, BF16, tile, *CB_Z), ("row", S, D_SSD, F32, tile, D_SSD, 0), ("full", (1, D_SSD), F32)]
    return _pcall(name, body, n, ins, outs, aliases={4: 0})


_XA_SCALE = XA_HEAD_DIM ** -0.5
_NT = (((1,), (1,)), ((), ()))
_TN = (((0,), (0,)), ((), ()))


def _xa_probs(qh, kh):
    s = lax.dot_general(qh, kh, _NT, preferred_element_type=F32) * _XA_SCALE
    s = s - jnp.max(s, axis=-1, keepdims=True)
    p = jnp.exp(s)
    return p / jnp.sum(p, axis=-1, keepdims=True)


def _xa_fwd(name, proj, kv, tile):
    S = proj.shape[0]
    n = S // tile
    Dh = XA_HEAD_DIM

    def body(q_ref, kv_ref, o_ref):
        for hd in range(XA_HEADS):
            qh = q_ref[:, hd * Dh:(hd + 1) * Dh]
            kh = kv_ref[:, hd * Dh:(hd + 1) * Dh]
            vh = kv_ref[:, D_MODEL + hd * Dh:D_MODEL + (hd + 1) * Dh]
            p = _xa_probs(qh, kh)
            o_ref[:, hd * Dh:(hd + 1) * Dh] = jnp.dot(p.astype(BF16), vh, preferred_element_type=F32).astype(BF16)

    return _pcall(name, body, n, [("row", proj, tile, *CB_XA_Q), ("full", kv)],
                  [("row", S, D_MODEL, BF16, tile, D_MODEL, 0)])[0]


def _xa_bwd(name, proj, kv, dout, dproj, tile):
    S = proj.shape[0]
    n = S // tile
    Dh = XA_HEAD_DIM

    def body(q_ref, kv_ref, do_ref, buf_ref, dq_ref, dkv_ref):
        i = pl.program_id(0)
        for hd in range(XA_HEADS):
            ks_ = slice(hd * Dh, (hd + 1) * Dh)
            vs_ = slice(D_MODEL + hd * Dh, D_MODEL + (hd + 1) * Dh)
            qh = q_ref[:, ks_]
            kh = kv_ref[:, ks_]
            vh = kv_ref[:, vs_]
            doh = do_ref[:, ks_].astype(BF16)
            p = _xa_probs(qh, kh)
            pb = p.astype(BF16)
            dp = lax.dot_general(doh, vh, _NT, preferred_element_type=F32)
            dv = lax.dot_general(pb, doh, _TN, preferred_element_type=F32)
            ds = (p * (dp - jnp.sum(dp * p, axis=-1, keepdims=True)) * _XA_SCALE).astype(BF16)
            dq_ref[:, ks_] = jnp.dot(ds, kh, preferred_element_type=F32).astype(BF16)
            dk = lax.dot_general(ds, qh, _TN, preferred_element_type=F32)

            @pl.when(i == 0)
            def _(dk=dk, dv=dv, ks_=ks_, vs_=vs_):
                dkv_ref[:, ks_] = dk
                dkv_ref[:, vs_] = dv

            @pl.when(i > 0)
            def _(dk=dk, dv=dv, ks_=ks_, vs_=vs_):
                dkv_ref[:, ks_] += dk
                dkv_ref[:, vs_] += dv

    ins = [("row", proj, tile, *CB_XA_Q), ("full", kv), ("row", dout, tile, D_MODEL, 0), ("any", dproj)]
    outs = [("row", S, N_PROJ, BF16, tile, *CB_XA_Q), ("full", (N_MEM, 2 * D_MODEL), F32)]
    return _pcall(name, body, n, ins, outs, aliases={3: 0})


def _adamw(name, w, g, m, v):
    R, C = w.shape
    tile = _pick(R, [t for t in (256, 128, 64, 32, 16, 8) if t * C <= 128 * 2048])
    n = R // tile
    bc1 = 1.0 - ADAM_B1 ** ADAM_STEP
    bc2 = 1.0 - ADAM_B2 ** ADAM_STEP

    def body(w_ref, g_ref, m_ref, v_ref, d_ref, nm_ref, nv_ref):
        gv = g_ref[...]
        mn = ADAM_B1 * m_ref[...] + (1.0 - ADAM_B1) * gv
        vn = ADAM_B2 * v_ref[...] + (1.0 - ADAM_B2) * (gv * gv)
        nm_ref[...] = mn
        nv_ref[...] = vn
        d_ref[...] = -ADAM_LR * ((mn / bc1) / (jnp.sqrt(vn / bc2) + ADAM_EPS) + ADAM_WD * w_ref[...])

    ins = [("row", a, tile, C, 0) for a in (w, g, m, v)]
    return _pcall(name, body, n, ins, [("row", R, C, F32, tile, C, 0)] * 3)


def _axpy(name, coef, a, b, tile):
    S, C = a.shape
    n = S // tile

    def body(a_ref, b_ref, o_ref):
        o_ref[...] = coef * a_ref[...].astype(F32) + b_ref[...].astype(F32)

    return _pcall(name, body, n, [("row", a, tile, C, 0), ("row", b, tile, C, 0)], [("row", S, C, F32, tile, C, 0)])[0]


def _sum_terms(name, terms, out_dtype):
    R, C = terms[0].shape
    tile = _pick(R, (512, 256, 128, 64, 32, 16, 8))
    n = R // tile
    nt = len(terms)

    def body(*refs):
        vals = [r[...].astype(F32) for r in refs[:nt]]
        while len(vals) > 1:
            vals = [vals[k] + vals[k + 1] for k in range(0, len(vals), 2)]
        refs[nt][...] = vals[0].astype(out_dtype)

    return _pcall(name, body, n, [("row", t, tile, C, 0) for t in terms], [("row", R, C, out_dtype, tile, C, 0)])[0]


def _me():
    return lax.axis_index("x"), lax.axis_index("y"), lax.axis_index("c")


def _other_chips(x, y):
    return [(1 - x, y), (x, 1 - y), (1 - x, 1 - y)]


_ANY = pl.BlockSpec(memory_space=pl.ANY)


def _swap_sibling(name, src):
    def body(src_ref, out_ref, send_sem, recv_sem):
        x, y, c = _me()
        cp = pltpu.make_async_remote_copy(src_ref=src_ref, dst_ref=out_ref, send_sem=send_sem, recv_sem=recv_sem,
                                          device_id=(x, y, 1 - c), device_id_type=MESH)
        cp.start()
        cp.wait()

    return pl.pallas_call(
        body, name=name, in_specs=[_ANY], out_specs=_ANY, out_shape=jax.ShapeDtypeStruct(src.shape, src.dtype),
        scratch_shapes=[pltpu.SemaphoreType.DMA, pltpu.SemaphoreType.DMA],
    )(src)


def _send_chips(name, src, per_chip):
    shape = src.shape[1:] if per_chip else src.shape

    def body(src_ref, out_ref, send_sems, recv_sems):
        x, y, c = _me()
        cps = []
        for k, (cx, cy) in enumerate(_other_chips(x, y)):
            s = src_ref.at[2 * cx + cy] if per_chip else src_ref
            cps.append(pltpu.make_async_remote_copy(src_ref=s, dst_ref=out_ref.at[k], send_sem=send_sems.at[k],
                                                    recv_sem=recv_sems.at[k], device_id=(cx, cy, c), device_id_type=MESH))
        for cp in cps:
            cp.start()
        for cp in cps:
            cp.wait()

    return pl.pallas_call(
        body, name=name, in_specs=[_ANY], out_specs=_ANY, out_shape=jax.ShapeDtypeStruct((3,) + shape, src.dtype),
        scratch_shapes=[pltpu.SemaphoreType.DMA((3,)), pltpu.SemaphoreType.DMA((3,))],
    )(src)


W_IN_SHARD = N_IN // 4
W_IN_INNER = W_IN_SHARD - 8


def _win_rows(chip_x, chip_y):
    start = (2 * chip_x + chip_y) * W_IN_SHARD + 8 * chip_y
    return pl.ds(pl.multiple_of(start, 2 * 8), W_IN_INNER)


def _gather_multi(name, srcs, kinds):
    n = len(srcs)
    xo, yo, _ = _me()
    bufs = []
    for s, kd in zip(srcs, kinds):
        if kd == "blk":
            buf = lax.empty((2, 4) + s.shape[1:], s.dtype)
            bufs.append(lax.dynamic_update_slice_in_dim(buf, s[:, None], 2 * xo + yo, axis=1))
        else:
            buf = lax.empty((2, N_IN, D_MODEL), s.dtype)
            bufs.append(lax.dynamic_update_slice_in_dim(buf, s, (2 * xo + yo) * W_IN_SHARD + 8 * yo, axis=1))

    def body(*refs):
        src_refs, out_refs = refs[:n], refs[2 * n:3 * n]
        send_sems, recv_sems = refs[3 * n:]
        x, y, c = _me()
        chips = _other_chips(x, y)

        def dst(i, part, cx, cy):
            if kinds[i] == "blk":
                return out_refs[i].at[part, 2 * cx + cy]
            return out_refs[i].at[part, _win_rows(cx, cy)]

        def copy(i, k, part, cx, cy, to, src=None):
            d = dst(i, part, cx, cy)
            return pltpu.make_async_remote_copy(src_ref=d if src is None else src, dst_ref=d,
                                                send_sem=send_sems.at[i, k], recv_sem=recv_sems.at[i, k],
                                                device_id=to, device_id_type=MESH)

        first = [copy(i, k, c, x, y, (cx, cy, c), src=src_refs[i].at[c]) for k, (cx, cy) in enumerate(chips) for i in range(n)]
        for cp in first:
            cp.start()
        passed = []
        for k, (cx, cy) in enumerate(chips):
            for i in range(n):
                copy(i, k, c, cx, cy, (x, y, c)).wait_recv()
                fw = copy(i, 3 + k, c, cx, cy, (x, y, 1 - c))
                fw.start()
                passed.append(fw)
        for k, (cx, cy) in enumerate(chips):
            for i in range(n):
                copy(i, 3 + k, 1 - c, cx, cy, (x, y, c)).wait_recv()
        for cp in first + passed:
            cp.wait_send()

    return pl.pallas_call(
        body, name=name, in_specs=[_ANY] * (2 * n), out_specs=[_ANY] * n,
        out_shape=[jax.ShapeDtypeStruct(b.shape, b.dtype) for b in bufs],
        input_output_aliases={n + i: i for i in range(n)},
        scratch_shapes=[pltpu.SemaphoreType.DMA((n, 6)), pltpu.SemaphoreType.DMA((n, 6))],
    )(*srcs, *bufs)


def _swap_parts_multi(name, srcs):
    n = len(srcs)

    def body(*refs):
        src_refs, out_refs = refs[:n], refs[n:2 * n]
        send_sems, recv_sems = refs[2 * n:]
        x, y, c = _me()
        cps = [pltpu.make_async_remote_copy(src_ref=src_refs[i].at[1 - c], dst_ref=out_refs[i], send_sem=send_sems.at[i],
                                            recv_sem=recv_sems.at[i], device_id=(x, y, 1 - c), device_id_type=MESH)
               for i in range(n)]
        for cp in cps:
            cp.start()
        for cp in cps:
            cp.wait()

    return pl.pallas_call(
        body, name=name, in_specs=[_ANY] * n, out_specs=[_ANY] * n,
        out_shape=[jax.ShapeDtypeStruct(s.shape[1:], s.dtype) for s in srcs],
        scratch_shapes=[pltpu.SemaphoreType.DMA((n,)), pltpu.SemaphoreType.DMA((n,))],
    )(*srcs)


def _send_chips_multi(name, srcs):
    n = len(srcs)

    def body(*refs):
        src_refs, out_refs = refs[:n], refs[n:2 * n]
        send_sems, recv_sems = refs[2 * n:]
        x, y, c = _me()
        cps = [pltpu.make_async_remote_copy(src_ref=src_refs[i].at[2 * cx + cy], dst_ref=out_refs[i].at[k],
                                            send_sem=send_sems.at[i, k], recv_sem=recv_sems.at[i, k],
                                            device_id=(cx, cy, c), device_id_type=MESH)
               for k, (cx, cy) in enumerate(_other_chips(x, y)) for i in range(n)]
        for cp in cps:
            cp.start()
        for cp in cps:
            cp.wait()

    return pl.pallas_call(
        body, name=name, in_specs=[_ANY] * n, out_specs=[_ANY] * n,
        out_shape=[jax.ShapeDtypeStruct((3,) + s.shape[1:], s.dtype) for s in srcs],
        scratch_shapes=[pltpu.SemaphoreType.DMA((n, 3)), pltpu.SemaphoreType.DMA((n, 3))],
    )(*srcs)


def _join_parts_multi(name, bufs):
    n = len(bufs)

    def body(*refs):
        out_refs = refs[n:2 * n]
        send_sems, recv_sems = refs[2 * n:]
        x, y, c = _me()

        def copy(i, part):
            return pltpu.make_async_remote_copy(src_ref=out_refs[i].at[part], dst_ref=out_refs[i].at[part],
                                                send_sem=send_sems.at[i], recv_sem=recv_sems.at[i],
                                                device_id=(x, y, 1 - c), device_id_type=MESH)

        for i in range(n):
            copy(i, c).start()
        for i in range(n):
            copy(i, c).wait_send()
            copy(i, 1 - c).wait_recv()

    return pl.pallas_call(
        body, name=name, in_specs=[_ANY] * n, out_specs=[_ANY] * n,
        out_shape=[jax.ShapeDtypeStruct(b.shape, b.dtype) for b in bufs],
        input_output_aliases={i: i for i in range(n)},
        scratch_shapes=[pltpu.SemaphoreType.DMA((n,)), pltpu.SemaphoreType.DMA((n,))],
    )(*bufs)


def _col_tiles(R, C):
    tr = _pick(R, (512, 256, 128))
    if tr != R:
        return tr, C
    if R * C <= 512 * 1024:
        return R, C
    return R, _pick(C, (256, 128))


def _pair_sum(name, g, theirs, c):
    _, n4, R, C = g.shape
    tr, tc = _col_tiles(R, C)

    def body(c_ref, g_ref, t_ref, o_ref, ob_ref):
        s = g_ref[0, 0] + t_ref[0]
        o_ref[0] = s
        ob_ref[0] = s.astype(BF16)

    spec3 = pl.BlockSpec((1, tr, tc), lambda d, i, k, c_ref: (d, i, k))
    return pl.pallas_call(
        body, name=name,
        grid_spec=pltpu.PrefetchScalarGridSpec(
            num_scalar_prefetch=1, grid=(n4, R // tr, C // tc),
            in_specs=[pl.BlockSpec((1, 1, tr, tc), lambda d, i, k, c_ref: (c_ref[0], d, i, k)), spec3],
            out_specs=[spec3, spec3]),
        out_shape=[jax.ShapeDtypeStruct((n4, R, C), F32), jax.ShapeDtypeStruct((n4, R, C), BF16)],
        compiler_params=_cparams(("arbitrary", "arbitrary", "arbitrary")),
    )(c.reshape(1).astype(jnp.int32), g, theirs)


def _chip_sum(name, pair, got, j, c):
    _, R, C = pair.shape
    tr, tc = _col_tiles(R, C)

    def body(jc_ref, p_ref, g0, g1, g2, o_ref):
        o_ref[0] = (p_ref[0] + g2[0].astype(F32)) + (g0[0].astype(F32) + g1[0].astype(F32))

    def gspec(k):
        return pl.BlockSpec((1, tr, tc), lambda i, q, jc_ref, k=k: (k, i, q))

    return pl.pallas_call(
        body, name=name,
        grid_spec=pltpu.PrefetchScalarGridSpec(
            num_scalar_prefetch=1, grid=(R // tr, C // tc),
            in_specs=[pl.BlockSpec((1, tr, tc), lambda i, q, jc_ref: (jc_ref[0], i, q)), gspec(0), gspec(1), gspec(2)],
            out_specs=pl.BlockSpec((1, tr, tc), lambda i, q, jc_ref: (jc_ref[1], i, q))),
        out_shape=jax.ShapeDtypeStruct((2, R, C), F32),
        compiler_params=_cparams(("arbitrary", "arbitrary")),
    )(jnp.stack([j, c]).astype(jnp.int32), pair, got, got, got)


LANES = 1024
_BIG = (("w_in", "col"), ("ffn_w_in", "col"), ("mem_w_kv", "col"),
        ("w_br_lru", "row"), ("w_br_ssd", "row"), ("w_br_xa", "row"), ("w_out", "row"), ("ffn_w_down", "row"))
_SMALL_SHARDED = ("b_gate", "lru_conv_w", "ssd_conv_w")
_SMALL = ("b_gate", "lru_conv_w", "lru_conv_b", "lru_w_a", "lru_b_a", "lru_w_i", "lru_b_i", "lru_lambda",
          "ssd_conv_w", "ssd_conv_b", "ssd_dt_bias", "ssd_a_log", "ssd_d", "ssd_norm_w",
          "ln1_g", "ln1_b", "ln2_g", "ln2_b")
_W_NAMES = ("w_in", "b_gate", "lru_conv_w", "lru_conv_b", "lru_w_a", "lru_b_a", "lru_w_i", "lru_b_i", "lru_lambda",
            "ssd_conv_w", "ssd_conv_b", "ssd_dt_bias", "ssd_a_log", "ssd_d", "ssd_norm_w", "mem_w_kv", "w_br_lru",
            "w_br_ssd", "w_br_xa", "w_out", "ln1_g", "ln1_b", "ffn_w_in", "ffn_w_down", "ln2_g", "ln2_b")
_IN_ORDER = ((4096, 7168), (8224, 11296), (2048, 4096), (0, 1024), (1024, 2048), (7200, 8224))
_IN_DT = (7168, 7200)


def _flat_rows(parts, row_multiple):
    flat = jnp.concatenate([p.reshape(-1) for p in parts])
    rows = -(-flat.size // LANES)
    rows = -(-rows // row_multiple) * row_multiple
    return jnp.pad(flat, (0, rows * LANES - flat.size)).reshape(rows, LANES)


def _take_parts(flat, shapes):
    out, off = [], 0
    for shp in shapes:
        size = math.prod(shp)
        out.append(flat[off:off + size].reshape(shp))
        off += size
    return out


def _gather_weights(w):
    _, y, _ = _me()
    wt = jnp.swapaxes(w["w_in"], 1, 2)
    inner = lax.dynamic_slice_in_dim(wt, 8 * y, W_IN_INNER, axis=1).astype(BF16)
    edge = lax.dynamic_slice_in_dim(wt, (1 - y) * W_IN_INNER, 8, axis=1)
    names = [nm for nm, _ in _BIG[1:]] + list(_SMALL_SHARDED)
    srcs = [inner, edge] + [w[nm].astype(BF16) for nm, _ in _BIG[1:]] + [w[nm] for nm in _SMALL_SHARDED]
    got = _gather_multi("gather_weights", srcs, ["rows"] + ["blk"] * (len(srcs) - 1))
    wt_all, edges = got[0], got[1]
    for a, b in ((0, 1), (2, 3)):
        tile = jnp.concatenate([edges[:, a], edges[:, b]], axis=1).astype(BF16)
        wt_all = lax.dynamic_update_slice_in_dim(wt_all, tile, b * W_IN_SHARD - 8, axis=1)
    full = {"w_in_t": wt_all}
    kinds = dict(_BIG)
    for nm, g in zip(names, got[2:]):
        L, _, r, cdim = g.shape
        if kinds.get(nm, "col") == "col":
            full[nm] = jnp.moveaxis(g, 1, 2).reshape(L, r, 4 * cdim)
        else:
            full[nm] = g.reshape(L, 4 * r, cdim)
    return full


def _reduce_big_grads(grads):
    x, y, c = _me()
    j = 2 * x + y
    gs = []
    for nm, kind in _BIG:
        g = jnp.stack(grads[nm])
        L, r, cdim = g.shape
        if nm == "w_in" or kind == "row":
            g = g.reshape(L, 4, r // 4, cdim)
        else:
            g = jnp.moveaxis(g.reshape(L, r, 4, cdim // 4), 2, 1)
        gs.append(g)
    theirs = _swap_parts_multi("grads_pair_swap", gs)
    pairs = [_pair_sum("grads_pair_sum", g, t, c) for g, t in zip(gs, theirs)]
    got = _send_chips_multi("grads_chip_send", [pb for _, pb in pairs])
    halves = [_chip_sum("grads_chip_sum", p, gt, j, c) for (p, _), gt in zip(pairs, got)]
    both = _join_parts_multi("grads_join", halves)
    out = dict(zip([nm for nm, _ in _BIG], both))
    out["w_in"] = jnp.swapaxes(out["w_in"], 1, 2)
    return out


def _reduce_small_grads(parts):
    shapes = [p.shape for p in parts]
    buf = _flat_rows(parts, 8)
    sib = _swap_sibling("small_pair_swap", buf)
    pair = _sum_terms("small_pair_sum", [buf, sib], F32)
    got = _send_chips("small_chip_send", pair, False)
    total = _sum_terms("small_chip_sum", [pair, got[2], got[0], got[1]], F32)
    return _take_parts(total.reshape(-1), shapes)


def _layer_fwd(xin, mem2, W, P, consts):
    S = xin.shape[0]
    T, TX = min(256, S), min(512, S)
    sv = {"xin": xin}
    proj = _mm("proj", xin, W["w_re_t"], "nt", BF16)
    dt_raw = _mm("proj_dt", xin, W["w_dt_t"], "nt", F32)
    xc = _conv_fwd("lru_conv", proj, CB_LRU_X, P["lru_conv_w"], P["lru_conv_b"], False, F32, T)
    y_lru, h = _lru_fwd("lru_scan", xc, proj, P["lru_w_a"], P["lru_w_i"], P["lru_b_a"], P["lru_b_i"], P["lru_lambda"], T)
    xact = _conv_fwd("ssd_conv", proj, CB_XBC, P["ssd_conv_w"], P["ssd_conv_b"], True, BF16, T)
    ycore, prev = _ssd_fwd("ssd_scan", xact, dt_raw, P["dt_bias"], P["a_neg"], P["d_exp"], consts)
    y_ssd = _gate_norm_fwd("ssd_norm", ycore, proj, P["ssd_norm_w"], T)
    kv = _mm("mem_kv", mem2, W["w_kv"], "nn", BF16)
    y_xa = _xa_fwd("xattn", proj, kv, TX)
    p_l = _mm("br_lru", y_lru, W["w_l"], "nn", BF16)
    p_s = _mm("br_ssd", y_ssd, W["w_s"], "nn", BF16)
    p_x = _mm("br_xa", y_xa, W["w_x"], "nn", BF16)
    merged = _merge_fwd("merge", proj, P["b_gate"], p_l, p_s, p_x, T)
    mix = _mm("mix_out", merged, W["w_o"], "nn", F32)
    x1, xh1, rs1 = _ln_fwd("ln_fwd", xin, mix, P["ln1_g"], P["ln1_b"], T)
    gu = _mm("ffn_in", x1, W["w_fi"], "nn", BF16)
    hmid = _swiglu_fwd("swiglu", gu, T)
    f = _mm("ffn_down", hmid, W["w_fd"], "nn", F32)
    x2, xh2, rs2 = _ln_fwd("ln_fwd", x1, f, P["ln2_g"], P["ln2_b"], T)
    sv.update(proj=proj, dt_raw=dt_raw, xc=xc, h=h, y_lru=y_lru, xact=xact, ycore=ycore, prev=prev, y_ssd=y_ssd, kv=kv,
              y_xa=y_xa, p_l=p_l, p_s=p_s, p_x=p_x, merged=merged, x1=x1, xh1=xh1, rs1=rs1, gu=gu, hmid=hmid,
              xh2=xh2, rs2=rs2)
    return x2, sv


def _layer_bwd(dys, coefs, sv, mem2, W, P, consts):
    S = sv["xin"].shape[0]
    T, TX = min(256, S), min(512, S)
    proj = sv["proj"]
    g = {}
    dz2, g["ln2_g"], g["ln2_b"] = _ln_bwd("ln_bwd_%d" % len(dys), dys, coefs, sv["xh2"], sv["rs2"], P["ln2_g"], T)
    dhmid = _mm("d_hmid", dz2, W["w_fd"], "nt", BF16)
    g["ffn_w_down"] = _mm("dw_ffn_down", sv["hmid"], dz2, "tn", F32)
    dgu = _swiglu_bwd("swiglu_bwd", sv["gu"], dhmid, T)
    dx1f = _mm("d_x1", dgu, W["w_fi"], "nt", F32)
    g["ffn_w_in"] = _mm("dw_ffn_in", sv["x1"], dgu, "tn", F32)
    dz1, g["ln1_g"], g["ln1_b"] = _ln_bwd("ln_bwd_2", [dz2, dx1f], [ALPHA, 1.0], sv["xh1"], sv["rs1"], P["ln1_g"], T)
    dmerged = _mm("d_merged", dz1, W["w_o"], "nt", BF16)
    g["w_out"] = _mm("dw_out", sv["merged"], dz1, "tn", F32)
    dproj, dpl, dps, dpx, dbg0, dbg1, dbg2 = _merge_bwd("merge_bwd", proj, P["b_gate"], sv["p_l"], sv["p_s"], sv["p_x"],
                                                         dmerged, T)
    g["b_gate"] = jnp.concatenate([dbg0, dbg1, dbg2], axis=0)
    dy_lru = _mm("d_ylru", dpl, W["w_l"], "nt", BF16)
    g["w_br_lru"] = _mm("dw_br_lru", sv["y_lru"], dpl, "tn", F32)
    dy_ssd = _mm("d_yssd", dps, W["w_s"], "nt", BF16)
    g["w_br_ssd"] = _mm("dw_br_ssd", sv["y_ssd"], dps, "tn", F32)
    dy_xa = _mm("d_yxa", dpx, W["w_x"], "nt", BF16)
    g["w_br_xa"] = _mm("dw_br_xa", sv["y_xa"], dpx, "tn", F32)
    dproj, dkv = _xa_bwd("xattn_bwd", proj, sv["kv"], dy_xa, dproj, TX)
    g["mem_w_kv"] = _mm("dw_kv", mem2, dkv, "tn", F32)
    dproj, dycore, g["ssd_norm_w"] = _gate_norm_bwd("ssd_norm_bwd", dy_ssd, sv["ycore"], proj, P["ssd_norm_w"], dproj, T)
    dxact, ddt, d_a, g_dtb, d_dexp = _ssd_bwd("ssd_scan_bwd", sv["xact"], sv["dt_raw"], P["dt_bias"], P["a_neg"], P["d_exp"],
                                              sv["prev"], dycore, consts)
    g["ssd_dt_bias"] = g_dtb[:, :SSD_HEADS]
    g["ssd_a_log"] = d_a[:, :SSD_HEADS] * P["a_neg"][:, :SSD_HEADS]
    g["ssd_d"] = jnp.sum(d_dexp.reshape(SSD_HEADS, SSD_HEAD_DIM), axis=-1)
    dpre = _conv_act_bwd("ssd_conv_act_bwd", proj, CB_XBC, P["ssd_conv_w"], P["ssd_conv_b"], dxact, T)
    dproj, w0, w1, w2, w3, g["ssd_conv_b"] = _conv_bwd("ssd_conv_bwd", proj, CB_XBC, P["ssd_conv_w"], dpre, dproj, T)
    g["ssd_conv_w"] = jnp.concatenate([w0, w1, w2, w3], axis=0)
    dproj, dxc, g["lru_w_a"], g["lru_w_i"], g["lru_b_a"], g["lru_b_i"], g["lru_lambda"] = _lru_bwd(
        "lru_scan_bwd", dy_lru, sv["xc"], proj, sv["h"], P["lru_w_a"], P["lru_w_i"], P["lru_b_a"], P["lru_b_i"],
        P["lru_lambda"], dproj, T)
    dproj, w0, w1, w2, w3, g["lru_conv_b"] = _conv_bwd("lru_conv_bwd", proj, CB_LRU_X, P["lru_conv_w"], dxc, dproj, T)
    g["lru_conv_w"] = jnp.concatenate([w0, w1, w2, w3], axis=0)
    xin = sv["xin"]
    dw_re_t = _mm("dw_in", dproj, xin, "tn", F32)
    dw_dt_t = _mm("dw_in_dt", ddt, xin, "tn", F32)
    pieces = {rng: dw_re_t[off:off + rng[1] - rng[0]]
              for rng, off in zip(_IN_ORDER, (0, 3072, 6144, 8192, 9216, 10240))}
    pieces[_IN_DT] = dw_dt_t[:SSD_HEADS]
    g["w_in"] = jnp.concatenate([pieces[k] for k in sorted(pieces)], axis=0)
    dxp = _mm("d_xin", dproj, W["w_re_t"], "nn", F32)
    dxs = _mm("d_xin_dt", ddt, W["w_dt_t"], "nn", F32, add=dxp)
    return [dz1, dxs], [ALPHA, 1.0], g


def _step(a):
    x2d, mem2, target = a["x"][0], a["mem"][0], a["loss_target"][0]
    S = x2d.shape[0]
    T = min(256, S)
    xi, yi, ci = _me()
    j = 2 * xi + yi
    w = {nm: a[nm] for nm in _W_NAMES}
    full = _gather_weights(w)
    consts = _ssd_consts()
    row = lambda v: v.reshape(1, -1)
    Ws, Ps = [], []
    for l in range(DEPTH):
        w_in_t = full["w_in_t"][l]
        w_re_t = jnp.concatenate([w_in_t[lo:hi] for lo, hi in _IN_ORDER], axis=0)
        w_dt_t = jnp.pad(w_in_t[_IN_DT[0]:_IN_DT[1]], ((0, DT_PAD - SSD_HEADS), (0, 0)))
        Ws.append(dict(w_re_t=w_re_t, w_dt_t=w_dt_t, w_fi=full["ffn_w_in"][l], w_kv=full["mem_w_kv"][l], w_l=full["w_br_lru"][l],
                       w_s=full["w_br_ssd"][l], w_x=full["w_br_xa"][l], w_o=full["w_out"][l], w_fd=full["ffn_w_down"][l]))
        pad_h = lambda v: jnp.pad(v.reshape(1, -1), ((0, 0), (0, DT_PAD - SSD_HEADS)))
        Ps.append(dict(
            b_gate=full["b_gate"][l], lru_conv_w=full["lru_conv_w"][l], ssd_conv_w=full["ssd_conv_w"][l],
            lru_conv_b=row(w["lru_conv_b"][l]), lru_w_a=w["lru_w_a"][l].astype(BF16), lru_w_i=w["lru_w_i"][l].astype(BF16),
            lru_b_a=row(w["lru_b_a"][l]), lru_b_i=row(w["lru_b_i"][l]), lru_lambda=row(w["lru_lambda"][l]),
            ssd_conv_b=row(w["ssd_conv_b"][l]), dt_bias=pad_h(w["ssd_dt_bias"][l]), a_neg=pad_h(-jnp.exp(w["ssd_a_log"][l])),
            d_exp=jnp.broadcast_to(w["ssd_d"][l][:, None], (SSD_HEADS, SSD_HEAD_DIM)).reshape(1, D_SSD),
            ssd_norm_w=row(w["ssd_norm_w"][l]), ln1_g=row(w["ln1_g"][l]), ln1_b=row(w["ln1_b"][l]),
            ln2_g=row(w["ln2_g"][l]), ln2_b=row(w["ln2_b"][l])))

    saved = []
    xcur = x2d
    for l in range(DEPTH):
        xcur, sv = _layer_fwd(xcur, mem2, Ws[l], Ps[l], consts)
        saved.append(sv)
    dy, loss_part = _loss_fwd_bwd("loss", xcur, target, T)
    loss = lax.psum(loss_part[0, 0], ("x", "y", "c"))

    dys, coefs = [dy], [1.0]
    layer_grads = [None] * DEPTH
    for l in reversed(range(DEPTH)):
        dys, coefs, layer_grads[l] = _layer_bwd(dys, coefs, saved[l], mem2, Ws[l], Ps[l], consts)
    grad_x = _axpy("grad_x", coefs[0], dys[0], dys[1], T)[None]

    big = _reduce_big_grads({nm: [layer_grads[l][nm] for l in range(DEPTH)] for nm, _ in _BIG})
    stacked = {nm: jnp.stack([layer_grads[l][nm] for l in range(DEPTH)]) for nm in _SMALL}
    small_parts = [stacked[nm].reshape((DEPTH,) + tuple(sh)) for nm, sh in
                   ((nm, (3, D_MODEL) if nm == "b_gate" else (4, D_MODEL) if nm == "lru_conv_w" else
                     (4, D_XBC) if nm == "ssd_conv_w" else w[nm].shape[1:]) for nm in _SMALL)]
    small = dict(zip(_SMALL, _reduce_small_grads(small_parts)))
    for nm in _SMALL_SHARDED:
        cs = w[nm].shape[2]
        small[nm] = lax.dynamic_slice_in_dim(small[nm], j * cs, cs, axis=2)
    grads = {**big, **small}

    delta, new_m, new_v = {}, {}, {}
    for nm, _ in _BIG:
        shp = w[nm].shape
        two = lambda v: v.reshape(shp[0] * shp[1], shp[2])
        d_, m_, v_ = _adamw("adamw_" + nm, two(w[nm]), two(grads[nm]), two(a["m_" + nm]), two(a["v_" + nm]))
        delta[nm], new_m[nm], new_v[nm] = d_.reshape(shp), m_.reshape(shp), v_.reshape(shp)
    shapes = [w[nm].shape for nm in _SMALL]
    packs = [_flat_rows([src[nm] for nm in _SMALL], 8) for src in
             (w, grads, {nm: a["m_" + nm] for nm in _SMALL}, {nm: a["v_" + nm] for nm in _SMALL})]
    d_, m_, v_ = _adamw("adamw_small", *packs)
    for dst, buf in ((delta, d_), (new_m, m_), (new_v, v_)):
        dst.update(zip(_SMALL, _take_parts(buf.reshape(-1), shapes)))

    outs = [loss, grad_x]
    for group in (grads, delta, new_m, new_v):
        outs += [group[nm] for nm in _W_NAMES]
    return tuple(outs)


def kernel(x, mem, w_in, b_gate, lru_conv_w, lru_conv_b, lru_w_a, lru_b_a, lru_w_i, lru_b_i, lru_lambda, ssd_conv_w, ssd_conv_b, ssd_dt_bias, ssd_a_log, ssd_d, ssd_norm_w, mem_w_kv, w_br_lru, w_br_ssd, w_br_xa, w_out, ln1_g, ln1_b, ffn_w_in, ffn_w_down, ln2_g, ln2_b, loss_target, m_w_in, m_b_gate, m_lru_conv_w, m_lru_conv_b, m_lru_w_a, m_lru_b_a, m_lru_w_i, m_lru_b_i, m_lru_lambda, m_ssd_conv_w, m_ssd_conv_b, m_ssd_dt_bias, m_ssd_a_log, m_ssd_d, m_ssd_norm_w, m_mem_w_kv, m_w_br_lru, m_w_br_ssd, m_w_br_xa, m_w_out, m_ln1_g, m_ln1_b, m_ffn_w_in, m_ffn_w_down, m_ln2_g, m_ln2_b, v_w_in, v_b_gate, v_lru_conv_w, v_lru_conv_b, v_lru_w_a, v_lru_b_a, v_lru_w_i, v_lru_b_i, v_lru_lambda, v_ssd_conv_w, v_ssd_conv_b, v_ssd_dt_bias, v_ssd_a_log, v_ssd_d, v_ssd_norm_w, v_mem_w_kv, v_w_br_lru, v_w_br_ssd, v_w_br_xa, v_w_out, v_ln1_g, v_ln1_b, v_ffn_w_in, v_ffn_w_down, v_ln2_g, v_ln2_b):
    return _step(dict(locals()))
```

```python
import functools
import math

import jax
import jax.numpy as jnp
from jax import lax
from jax.experimental import pallas as pl
from jax.experimental.pallas import tpu as pltpu

F32, BF16 = jnp.float32, jnp.bfloat16
MESH = pl.DeviceIdType.MESH
VMEM_LIMIT_BYTES = 56 * 2**20
HALO = 16

D_MODEL = 1024
DEPTH = 2
CHUNK = 64
N_MEM = 256
LRU_BLOCKS = 8
LRU_BLOCK = 128
LRU_C = 8.0
D_SSD = 2048
SSD_HEADS = 32
SSD_HEAD_DIM = 64
SSD_GROUPS = 4
SSD_STATE = 128
D_BC = SSD_GROUPS * SSD_STATE
D_XBC = D_SSD + 2 * D_BC
XA_HEADS = 4
XA_HEAD_DIM = 256
D_FF = 2816
ALPHA = (2 * DEPTH) ** 0.25
EPS = 1e-5
N_IN = 11296
N_PROJ = 11264
DT_PAD = 128

ADAM_LR, ADAM_B1, ADAM_B2, ADAM_EPS, ADAM_WD, ADAM_STEP = 0.001, 0.9, 0.999, 1e-08, 0.01, 10

CB_XBC = (3072, 0)
CB_XS, CB_BM, CB_CM = (2048, 0), (512, 4), (512, 5)
CB_LOGITS = (3072, 1)
CB_G0, CB_G1, CB_G2 = (1024, 3), (1024, 4), (1024, 5)
CB_Z = (2048, 3)
CB_LRU_X, CB_LRU_GATE, CB_XA_Q = (1024, 8), (1024, 9), (1024, 10)


def _cparams(sem):
    return pltpu.CompilerParams(dimension_semantics=sem, vmem_limit_bytes=VMEM_LIMIT_BYTES)


def _pcall(name, body, n, ins, outs, scratch=(), reverse=False, aliases=None):
    def ridx(i):
        return (n - 1 - i) if reverse else i

    in_specs, args = [], []
    for sp in ins:
        kind, arr = sp[0], sp[1]
        if kind == "row":
            _, _, tile, width, cb = sp
            in_specs.append(pl.BlockSpec((tile, width), lambda i, cb=cb: (ridx(i), cb)))
        elif kind == "prev":
            _, _, tile, width, cb = sp
            t = tile // HALO
            in_specs.append(pl.BlockSpec((HALO, width), lambda i, cb=cb, t=t: (jnp.maximum(ridx(i) * t - 1, 0), cb)))
        elif kind == "next":
            _, _, tile, width, cb = sp
            t = tile // HALO
            last = arr.shape[0] // HALO - 1
            in_specs.append(pl.BlockSpec((HALO, width), lambda i, cb=cb, t=t, last=last: (jnp.minimum((ridx(i) + 1) * t, last), cb)))
        elif kind == "lead":
            nd = arr.ndim
            in_specs.append(pl.BlockSpec((sp[2],) + arr.shape[1:], lambda i, nd=nd: (ridx(i),) + (0,) * (nd - 1)))
        elif kind == "full":
            nd = arr.ndim
            in_specs.append(pl.BlockSpec(arr.shape, lambda i, nd=nd: (0,) * nd))
        elif kind == "any":
            in_specs.append(pl.BlockSpec(memory_space=pl.ANY))
        else:
            raise ValueError(kind)
        args.append(arr)
    out_specs, out_shape = [], []
    for sp in outs:
        kind = sp[0]
        if kind == "row":
            _, rows, cols, dtype, tile, width, cb = sp
            out_shape.append(jax.ShapeDtypeStruct((rows, cols), dtype))
            out_specs.append(pl.BlockSpec((tile, width), lambda i, cb=cb: (ridx(i), cb)))
        elif kind == "lead":
            _, shape, dtype, lead = sp
            nd = len(shape)
            out_shape.append(jax.ShapeDtypeStruct(shape, dtype))
            out_specs.append(pl.BlockSpec((lead,) + tuple(shape[1:]), lambda i, nd=nd: (ridx(i),) + (0,) * (nd - 1)))
        elif kind == "full":
            _, shape, dtype = sp
            nd = len(shape)
            out_shape.append(jax.ShapeDtypeStruct(shape, dtype))
            out_specs.append(pl.BlockSpec(tuple(shape), lambda i, nd=nd: (0,) * nd))
        else:
            raise ValueError(kind)
    res = pl.pallas_call(
        body, name=name, grid=(n,), in_specs=in_specs, out_specs=out_specs, out_shape=out_shape,
        scratch_shapes=list(scratch), input_output_aliases=aliases or {},
        compiler_params=_cparams(("arbitrary",)),
    )(*args)
    return res


def _pick(n, cands):
    for c in cands:
        if n % c == 0:
            return c
    return n


_MM_TILES = (1024, 1408, 512, 256, 128)


def _mm(name, a, b, mode, out_dtype, add=None):
    if mode == "nn":
        (M, K), (K2, N) = a.shape, b.shape
    elif mode == "nt":
        (M, K), (N, K2) = a.shape, b.shape
    else:
        (K, M), (K2, N) = a.shape, b.shape
    assert K == K2, (name, a.shape, b.shape)
    tm = _pick(M, _MM_TILES)
    tn = _pick(N, _MM_TILES)
    if mode == "tn":
        tk = _pick(K, (1024, 512, 256))
    else:
        tk = K if K <= 2816 else _pick(K, _MM_TILES)
    nk = K // tk
    has_add = add is not None

    def body(*refs):
        if has_add:
            a_ref, b_ref, add_ref, o_ref, acc_ref = refs
        else:
            a_ref, b_ref, o_ref, acc_ref = refs
        k = pl.program_id(2)
        av = a_ref[...].astype(BF16)
        bv = b_ref[...].astype(BF16)
        if mode == "nn":
            p = jnp.dot(av, bv, preferred_element_type=F32)
        elif mode == "nt":
            p = lax.dot_general(av, bv, (((1,), (1,)), ((), ())), preferred_element_type=F32)
        else:
            p = lax.dot_general(av, bv, (((0,), (0,)), ((), ())), preferred_element_type=F32)

        def fin(v):
            if has_add:
                v = v + add_ref[...].astype(F32)
            o_ref[...] = v.astype(out_dtype)

        if nk == 1:
            fin(p)
        else:
            @pl.when(k == 0)
            def _():
                acc_ref[...] = p

            @pl.when(k > 0)
            def _():
                acc_ref[...] += p

            @pl.when(k == nk - 1)
            def _():
                fin(acc_ref[...])

    if mode == "nn":
        specs = [pl.BlockSpec((tm, tk), lambda i, j, k: (i, k)), pl.BlockSpec((tk, tn), lambda i, j, k: (k, j))]
    elif mode == "nt":
        specs = [pl.BlockSpec((tm, tk), lambda i, j, k: (i, k)), pl.BlockSpec((tn, tk), lambda i, j, k: (j, k))]
    else:
        specs = [pl.BlockSpec((tk, tm), lambda i, j, k: (k, i)), pl.BlockSpec((tk, tn), lambda i, j, k: (k, j))]
    args = [a, b]
    if has_add:
        specs.append(pl.BlockSpec((tm, tn), lambda i, j, k: (i, j)))
        args.append(add)
    acc_shape = (tm, tn) if nk > 1 else (8, 128)
    return pl.pallas_call(
        body, name=name, grid=(M // tm, N // tn, nk), in_specs=specs,
        out_specs=pl.BlockSpec((tm, tn), lambda i, j, k: (i, j)),
        out_shape=jax.ShapeDtypeStruct((M, N), out_dtype),
        scratch_shapes=[pltpu.VMEM(acc_shape, F32)],
        compiler_params=_cparams(("parallel", "parallel", "arbitrary")),
    )(*args)


def _sigmoid(x):
    return 1.0 / (1.0 + jnp.exp(-x))


def _silu(x):
    return x * _sigmoid(x)


def _dsilu(x):
    s = _sigmoid(x)
    return s * (1.0 + x * (1.0 - s))


def _softplus(x):
    return jnp.maximum(x, 0.0) + jnp.log(1.0 + jnp.exp(-jnp.abs(x)))


_GELU_C = math.sqrt(2.0 / math.pi)


def _gelu(x):
    return 0.5 * x * (1.0 + jnp.tanh(_GELU_C * (x + 0.044715 * x * x * x)))


def _dgelu(x):
    t = jnp.tanh(_GELU_C * (x + 0.044715 * x * x * x))
    return 0.5 * (1.0 + t) + 0.5 * x * (1.0 - t * t) * _GELU_C * (1.0 + 3.0 * 0.044715 * x * x)


def _acc(ref, i, val):
    @pl.when(i == 0)
    def _():
        ref[...] = val

    @pl.when(i > 0)
    def _():
        ref[...] += val


def _rows(shape):
    return lax.broadcasted_iota(jnp.int32, shape, 0)


def _shift_down(x, k, halo8, first):
    r = pltpu.roll(x, k, 0)
    h = pltpu.roll(halo8, k, 0)
    h = jnp.where(first, 0.0, h)
    head = jnp.where(_rows(h.shape) < k, h, r[:8])
    if x.shape[0] == 8:
        return head
    return jnp.concatenate([head, r[8:]], axis=0)


def _shift_up(x, k, halo8, last):
    T = x.shape[0]
    r = pltpu.roll(x, T - k, 0)
    h = pltpu.roll(halo8, 8 - k, 0)
    h = jnp.where(last, 0.0, h)
    tail = jnp.where(_rows(h.shape) >= 8 - k, h, r[T - 8:])
    return jnp.concatenate([r[:T - 8], tail], axis=0)


def _ln_fwd(name, a, b, g, beta, tile):
    S, Dm = a.shape
    n = S // tile

    def body(a_ref, b_ref, g_ref, be_ref, y_ref, xh_ref, rs_ref):
        z = ALPHA * a_ref[...] + b_ref[...].astype(F32)
        mu = jnp.mean(z, axis=-1, keepdims=True)
        zc = z - mu
        var = jnp.mean(zc * zc, axis=-1, keepdims=True)
        rstd = lax.rsqrt(var + EPS)
        xh = zc * rstd
        y_ref[...] = xh * g_ref[...] + be_ref[...]
        xh_ref[...] = xh
        rs_ref[...] = rstd

    return _pcall(name, body, n,
                  [("row", a, tile, Dm, 0), ("row", b, tile, Dm, 0), ("full", g), ("full", beta)],
                  [("row", S, Dm, F32, tile, Dm, 0), ("row", S, Dm, F32, tile, Dm, 0), ("row", S, 1, F32, tile, 1, 0)])


def _ln_bwd(name, dys, coefs, xh, rstd, g, tile):
    S, Dm = xh.shape
    n = S // tile
    nd = len(dys)

    def body(*refs):
        dy_refs = refs[:nd]
        xh_ref, rs_ref, g_ref, dz_ref, dg_ref, db_ref = refs[nd:]
        i = pl.program_id(0)
        dy = coefs[0] * dy_refs[0][...].astype(F32)
        for k in range(1, nd):
            dy = dy + coefs[k] * dy_refs[k][...].astype(F32)
        xh_v = xh_ref[...]
        dxh = dy * g_ref[...]
        m1 = jnp.mean(dxh, axis=-1, keepdims=True)
        m2 = jnp.mean(dxh * xh_v, axis=-1, keepdims=True)
        dz_ref[...] = rs_ref[...] * (dxh - m1 - xh_v * m2)
        _acc(dg_ref, i, jnp.sum(dy * xh_v, axis=0, keepdims=True))
        _acc(db_ref, i, jnp.sum(dy, axis=0, keepdims=True))

    ins = [("row", d, tile, Dm, 0) for d in dys]
    ins += [("row", xh, tile, Dm, 0), ("row", rstd, tile, 1, 0), ("full", g)]
    return _pcall(name, body, n, ins,
                  [("row", S, Dm, F32, tile, Dm, 0), ("full", (1, Dm), F32), ("full", (1, Dm), F32)])


def _loss_fwd_bwd(name, y, target, tile):
    S, Dm = y.shape
    n = S // tile

    def body(y_ref, t_ref, dy_ref, l_ref):
        i = pl.program_id(0)
        err = y_ref[...] - t_ref[...]
        dy_ref[...] = err * (1.0 / Dm)
        part = jnp.sum(jnp.sum(err * err, axis=-1, keepdims=True), axis=0, keepdims=True) * (0.5 / Dm)
        _acc(l_ref, i, part)

    return _pcall(name, body, n, [("row", y, tile, Dm, 0), ("row", target, tile, Dm, 0)],
                  [("row", S, Dm, F32, tile, Dm, 0), ("full", (1, 1), F32)])


def _swiglu_fwd(name, gu, tile):
    S = gu.shape[0]
    n = S // tile

    def body(g_ref, u_ref, o_ref):
        o_ref[...] = (_silu(g_ref[...].astype(F32)) * u_ref[...].astype(F32)).astype(BF16)

    return _pcall(name, body, n, [("row", gu, tile, D_FF, 0), ("row", gu, tile, D_FF, 1)],
                  [("row", S, D_FF, BF16, tile, D_FF, 0)])[0]


def _swiglu_bwd(name, gu, dh, tile):
    S = gu.shape[0]
    n = S // tile

    def body(g_ref, u_ref, dh_ref, o_ref):
        gv = g_ref[...].astype(F32)
        uv = u_ref[...].astype(F32)
        dv = dh_ref[...].astype(F32)
        dg = dv * uv * _dsilu(gv)
        du = dv * _silu(gv)
        o_ref[...] = jnp.concatenate([dg, du], axis=1).astype(BF16)

    return _pcall(name, body, n, [("row", gu, tile, D_FF, 0), ("row", gu, tile, D_FF, 1), ("row", dh, tile, D_FF, 0)],
                  [("row", S, 2 * D_FF, BF16, tile, 2 * D_FF, 0)])[0]


def _merge_fwd(name, proj, b_gate, pl_, ps_, px_, tile):
    S = proj.shape[0]
    n = S // tile
    Dm = D_MODEL

    def body(l0, l1, l2, bg, p0, p1, p2, o_ref):
        bgv = bg[...]
        acc = _sigmoid(l0[...].astype(F32) + bgv[0:1]) * p0[...].astype(F32)
        acc = acc + _sigmoid(l1[...].astype(F32) + bgv[1:2]) * p1[...].astype(F32)
        acc = acc + _sigmoid(l2[...].astype(F32) + bgv[2:3]) * p2[...].astype(F32)
        o_ref[...] = acc.astype(BF16)

    ins = [("row", proj, tile, *CB_G0), ("row", proj, tile, *CB_G1), ("row", proj, tile, *CB_G2), ("full", b_gate),
           ("row", pl_, tile, Dm, 0), ("row", ps_, tile, Dm, 0), ("row", px_, tile, Dm, 0)]
    return _pcall(name, body, n, ins, [("row", S, Dm, BF16, tile, Dm, 0)])[0]


def _merge_bwd(name, proj, b_gate, pl_, ps_, px_, dmerged, tile):
    S = proj.shape[0]
    n = S // tile
    Dm = D_MODEL

    def body(l0, l1, l2, bg, p0, p1, p2, dm_ref, dproj_ref, d0, d1, d2, db0, db1, db2):
        i = pl.program_id(0)
        bgv = bg[...]
        dm = dm_ref[...].astype(F32)
        dls = []
        for k, (lr, pr, dr, dbr) in enumerate(((l0, p0, d0, db0), (l1, p1, d1, db1), (l2, p2, d2, db2))):
            gk = _sigmoid(lr[...].astype(F32) + bgv[k:k + 1])
            dr[...] = (dm * gk).astype(BF16)
            dl = dm * pr[...].astype(F32) * gk * (1.0 - gk)
            _acc(dbr, i, jnp.sum(dl, axis=0, keepdims=True))
            dls.append(dl)
        dproj_ref[...] = jnp.concatenate(dls, axis=1).astype(BF16)

    ins = [("row", proj, tile, *CB_G0), ("row", proj, tile, *CB_G1), ("row", proj, tile, *CB_G2), ("full", b_gate),
           ("row", pl_, tile, Dm, 0), ("row", ps_, tile, Dm, 0), ("row", px_, tile, Dm, 0), ("row", dmerged, tile, Dm, 0)]
    outs = [("row", S, N_PROJ, BF16, tile, *CB_LOGITS)] + [("row", S, Dm, BF16, tile, Dm, 0)] * 3 + [("full", (1, Dm), F32)] * 3
    return _pcall(name, body, n, ins, outs)


def _conv_taps(xf, halo8, first, w):
    out = xf * w[3:4]
    for k in (1, 2, 3):
        out = out + _shift_down(xf, k, halo8, first) * w[3 - k:4 - k]
    return out


def _conv_fwd(name, src, cb, w, b, act, out_dtype, tile):
    S = src.shape[0]
    C = cb[0]
    n = S // tile

    def body(x_ref, p_ref, w_ref, b_ref, o_ref):
        i = pl.program_id(0)
        xf = x_ref[...].astype(F32)
        halo8 = p_ref[...].astype(F32)[HALO - 8:]
        pre = _conv_taps(xf, halo8, i == 0, w_ref[...]) + b_ref[...]
        o_ref[...] = (_silu(pre) if act else pre).astype(out_dtype)

    return _pcall(name, body, n, [("row", src, tile, *cb), ("prev", src, tile, *cb), ("full", w), ("full", b)],
                  [("row", S, C, out_dtype, tile, C, 0)])[0]


def _conv_bwd(name, src, cb, w, b, act, dout, dproj, tile):
    S = src.shape[0]
    C = cb[0]
    n = S // tile

    def body(x_ref, p_ref, xn_ref, w_ref, b_ref, d_ref, dn_ref, buf_ref, dx_ref, dw0, dw1, dw2, dw3, db_ref):
        i = pl.program_id(0)
        first, last = i == 0, i == n - 1
        xf = x_ref[...].astype(F32)
        halo8 = p_ref[...].astype(F32)[HALO - 8:]
        wv = w_ref[...]
        dv = d_ref[...].astype(F32)
        nx8 = dn_ref[...].astype(F32)[:8]
        if act:
            bv = b_ref[...]
            dv = dv * _dsilu(_conv_taps(xf, halo8, first, wv) + bv)
            xn8 = xn_ref[...].astype(F32)[:8]
            nx8 = nx8 * _dsilu(_conv_taps(xn8, xf[tile - 8:], False, wv) + bv)
        dx = dv * wv[3:4]
        for k in (1, 2, 3):
            dx = dx + _shift_up(dv, k, nx8, last) * wv[3 - k:4 - k]
        dx_ref[...] = dx.astype(BF16)
        _acc(dw3, i, jnp.sum(dv * xf, axis=0, keepdims=True))
        for k, dwr in ((1, dw2), (2, dw1), (3, dw0)):
            _acc(dwr, i, jnp.sum(dv * _shift_down(xf, k, halo8, first), axis=0, keepdims=True))
        _acc(db_ref, i, jnp.sum(dv, axis=0, keepdims=True))

    ins = [("row", src, tile, *cb), ("prev", src, tile, *cb), ("next", src, tile, *cb), ("full", w), ("full", b),
           ("row", dout, tile, C, 0), ("next", dout, tile, C, 0), ("any", dproj)]
    outs = [("row", S, N_PROJ, BF16, tile, *cb)] + [("full", (1, C), F32)] * 5
    return _pcall(name, body, n, ins, outs, aliases={7: 0})


def _lru_gates(xc, wa_ref, wi_ref, ba, bi, lam):
    xb = xc.astype(BF16)
    pa, pi_ = [], []
    for nb in range(LRU_BLOCKS):
        sl = slice(nb * LRU_BLOCK, (nb + 1) * LRU_BLOCK)
        pa.append(jnp.dot(xb[:, sl], wa_ref[nb], preferred_element_type=F32))
        pi_.append(jnp.dot(xb[:, sl], wi_ref[nb], preferred_element_type=F32))
    r = _sigmoid(jnp.concatenate(pa, axis=1) + ba)
    ig = _sigmoid(jnp.concatenate(pi_, axis=1) + bi)
    sp = _softplus(-lam)
    a = jnp.exp(-LRU_C * r * sp)
    m = jnp.sqrt(1.0 - a * a)
    return xb, r, ig, sp, a, m


def _lru_fwd(name, xc, proj, wa, wi, ba, bi, lam, tile):
    S = xc.shape[0]
    n = S // tile
    C = D_MODEL

    def body(xc_ref, gate_ref, wa_ref, wi_ref, ba_ref, bi_ref, lam_ref, y_ref, h_ref, carry):
        i = pl.program_id(0)

        @pl.when(i == 0)
        def _():
            carry[...] = jnp.zeros_like(carry)

        xcv = xc_ref[...]
        _, r, ig, sp, a, m = _lru_gates(xcv, wa_ref, wi_ref, ba_ref[...], bi_ref[...], lam_ref[...])
        u = m * (ig * xcv)
        rows = _rows(a.shape)
        d = 1
        while d < tile:
            keep = rows >= d
            a_s = jnp.where(keep, pltpu.roll(a, d, 0), 1.0)
            u_s = jnp.where(keep, pltpu.roll(u, d, 0), 0.0)
            u = a * u_s + u
            a = a * a_s
            d *= 2
        h = u + a * carry[0:1, :]
        h_ref[...] = h
        carry[0:1, :] = h_ref[pl.ds(tile - 1, 1), :]
        y_ref[...] = (_gelu(gate_ref[...].astype(F32)) * h).astype(BF16)

    ins = [("row", xc, tile, C, 0), ("row", proj, tile, *CB_LRU_GATE), ("full", wa), ("full", wi),
           ("full", ba), ("full", bi), ("full", lam)]
    return _pcall(name, body, n, ins, [("row", S, C, BF16, tile, C, 0), ("row", S, C, F32, tile, C, 0)],
                  scratch=[pltpu.VMEM((8, C), F32)])


def _lru_bwd(name, dy, xc, proj, h, wa, wi, ba, bi, lam, dproj, tile):
    S = xc.shape[0]
    n = S // tile
    C = D_MODEL

    def body(dy_ref, xc_ref, gate_ref, h_ref, hp_ref, wa_ref, wi_ref, ba_ref, bi_ref, lam_ref, buf_ref,
             dg_ref, dxc_ref, dwa_ref, dwi_ref, dba_ref, dbi_ref, dlam_ref, carry):
        i = pl.program_id(0)
        first_tile = i == n - 1

        @pl.when(i == 0)
        def _():
            carry[...] = jnp.zeros_like(carry)

        xcv = xc_ref[...]
        lamv = lam_ref[...]
        xb, r, ig, sp, a, m = _lru_gates(xcv, wa_ref, wi_ref, ba_ref[...], bi_ref[...], lamv)
        hv = h_ref[...]
        gv = gate_ref[...].astype(F32)
        dyv = dy_ref[...].astype(F32)
        dg_ref[...] = (dyv * hv * _dgelu(gv)).astype(BF16)
        v = dyv * _gelu(gv)
        rows = _rows(a.shape)
        bcoef = jnp.where(rows == tile - 1, 1.0, pltpu.roll(a, tile - 1, 0))
        d = 1
        while d < tile:
            keep = rows < tile - d
            b_s = jnp.where(keep, pltpu.roll(bcoef, tile - d, 0), 1.0)
            v_s = jnp.where(keep, pltpu.roll(v, tile - d, 0), 0.0)
            v = v + bcoef * v_s
            bcoef = bcoef * b_s
            d *= 2
        dH = v + bcoef * carry[0:1, :]
        dxc_ref[...] = dH
        carry[0:1, :] = dxc_ref[pl.ds(0, 1), :] * a[0:1, :]
        halo8 = hp_ref[...][HALO - 8:]
        hprev = _shift_down(hv, 1, halo8, first_tile)
        da = dH * hprev
        ix = ig * xcv
        dm = dH * ix
        di = dH * m * xcv
        dxc = dH * m * ig
        da = da - dm * a / m
        dla = da * a
        dr = dla * (-LRU_C) * sp
        _acc(dlam_ref, i, jnp.sum(dla * (-LRU_C) * r, axis=0, keepdims=True) * (-_sigmoid(-lamv)))
        dpa = dr * r * (1.0 - r)
        dpi = di * ig * (1.0 - ig)
        _acc(dba_ref, i, jnp.sum(dpa, axis=0, keepdims=True))
        _acc(dbi_ref, i, jnp.sum(dpi, axis=0, keepdims=True))
        dpab, dpib = dpa.astype(BF16), dpi.astype(BF16)
        back = []
        for nb in range(LRU_BLOCKS):
            sl = slice(nb * LRU_BLOCK, (nb + 1) * LRU_BLOCK)
            back.append(lax.dot_general(dpab[:, sl], wa_ref[nb], (((1,), (1,)), ((), ())), preferred_element_type=F32)
                        + lax.dot_general(dpib[:, sl], wi_ref[nb], (((1,), (1,)), ((), ())), preferred_element_type=F32))
            ga = lax.dot_general(xb[:, sl], dpab[:, sl], (((0,), (0,)), ((), ())), preferred_element_type=F32)
            gi = lax.dot_general(xb[:, sl], dpib[:, sl], (((0,), (0,)), ((), ())), preferred_element_type=F32)

            @pl.when(i == 0)
            def _(ga=ga, gi=gi, nb=nb):
                dwa_ref[nb] = ga
                dwi_ref[nb] = gi

            @pl.when(i > 0)
            def _(ga=ga, gi=gi, nb=nb):
                dwa_ref[nb] += ga
                dwi_ref[nb] += gi

        dxc_ref[...] = dxc + jnp.concatenate(back, axis=1)

    ins = [("row", dy, tile, C, 0), ("row", xc, tile, C, 0), ("row", proj, tile, *CB_LRU_GATE), ("row", h, tile, C, 0),
           ("prev", h, tile, C, 0), ("full", wa), ("full", wi), ("full", ba), ("full", bi), ("full", lam), ("any", dproj)]
    outs = [("row", S, N_PROJ, BF16, tile, *CB_LRU_GATE), ("row", S, C, F32, tile, C, 0),
            ("full", (LRU_BLOCKS, LRU_BLOCK, LRU_BLOCK), F32), ("full", (LRU_BLOCKS, LRU_BLOCK, LRU_BLOCK), F32),
            ("full", (1, C), F32), ("full", (1, C), F32), ("full", (1, C), F32)]
    return _pcall(name, body, n, ins, outs, scratch=[pltpu.VMEM((8, C), F32)], reverse=True, aliases={10: 0})


SSD_STEP = 4

def _split3(x):
    h = x.astype(BF16)
    r = x - h.astype(F32)
    m = r.astype(BF16)
    lo = (r - m.astype(F32)).astype(BF16)
    return h, m, lo


def _dot01_r(x, e):
    h = x.astype(BF16)
    m = (x - h.astype(F32)).astype(BF16)
    return jnp.dot(h, e, preferred_element_type=F32) + jnp.dot(m, e, preferred_element_type=F32)


def _dot01_l(e, x):
    h, m, lo = _split3(x)
    return (jnp.dot(e, h, preferred_element_type=F32) + jnp.dot(e, m, preferred_element_type=F32)
            + jnp.dot(e, lo, preferred_element_type=F32))


def _ssd_consts():
    hh = lax.broadcasted_iota(jnp.int32, (DT_PAD, D_SSD), 0)
    cc = lax.broadcasted_iota(jnp.int32, (DT_PAD, D_SSD), 1)
    e = (cc // SSD_HEAD_DIM == hh).astype(BF16)
    rows = SSD_STEP * CHUNK
    li = lax.broadcasted_iota(jnp.int32, (rows, rows), 0)
    si = lax.broadcasted_iota(jnp.int32, (rows, rows), 1)
    ltri = ((li >= si) & (li // CHUNK == si // CHUNK)).astype(BF16)
    l4 = lax.broadcasted_iota(jnp.int32, (CHUNK, 4 * CHUNK), 0)
    s4 = lax.broadcasted_iota(jnp.int32, (CHUNK, 4 * CHUNK), 1) % CHUNK
    itile = (l4 == s4).astype(F32)
    causal = (l4 >= s4).astype(F32)
    j4 = lax.broadcasted_iota(jnp.int32, (8, 4 * CHUNK), 0)
    c4 = lax.broadcasted_iota(jnp.int32, (8, 4 * CHUNK), 1) // CHUNK
    hmask = (j4 == c4).astype(F32)
    return e, e.T, ltri, ltri.T, itile, causal, hmask


def _ssd_chunk_common(xs_ref, bm_ref, cm_ref, dt_ref, dtb_ref, a_ref, e_ref, ltri_ref):
    xs = xs_ref[...].astype(F32)
    raw = dt_ref[...] + dtb_ref[...]
    dtv = _softplus(raw)
    da = dtv * a_ref[...]
    cs = _dot01_l(ltri_ref[...], da)
    e = e_ref[...]
    dte = _dot01_r(dtv, e)
    ce = _dot01_r(cs, e)
    xdt = xs * dte
    ecs = jnp.exp(ce)
    return xs, raw, dtv, cs, dte, ce, xdt, ecs


def _quad_terms(ce_q, cb4, itile, causal):
    cr = jnp.sum(ce_q * itile, axis=0, keepdims=True)
    seg = ce_q - cr
    dec = jnp.where(causal > 0.0, jnp.exp(jnp.minimum(seg, 0.0)), 0.0)
    return dec, cb4 * dec


def _block_diag4(xq, hmask):
    return jnp.concatenate([xq * hmask[j:j + 1].astype(xq.dtype) for j in range(4)], axis=0)


def _ssd_fwd(name, xbc, dt_raw, dt_bias, a_neg, d_exp, consts):
    S = xbc.shape[0]
    nc = S // CHUNK
    e, et, ltri, ltri_t, itile, causal, hmask = consts

    def body(xs_ref, bm_ref, cm_ref, dt_ref, dtb_ref, a_ref, dex_ref, e_ref, ltri_ref, it_ref, ca_ref, hm_ref,
             y_ref, prev_ref, hst):
        i = pl.program_id(0)

        @pl.when(i == 0)
        def _():
            hst[...] = jnp.zeros_like(hst)

        xs, raw, dtv, cs, dte, ce_all, xdt_all, ecs_all = _ssd_chunk_common(
            xs_ref, bm_ref, cm_ref, dt_ref, dtb_ref, a_ref, e_ref, ltri_ref)
        itile_v, causal_v, hmask_v = it_ref[...], ca_ref[...], hm_ref[...]
        dskip_all = dex_ref[...] * xs
        for k in range(SSD_STEP):
            rs = slice(k * CHUNK, (k + 1) * CHUNK)
            ce, xdt, ecs, dskip = ce_all[rs], xdt_all[rs], ecs_all[rs], dskip_all[rs]
            cle = ce[CHUNK - 1:CHUNK, :]
            xdtb = xdt.astype(BF16)
            xst = (xdt * jnp.exp(cle - ce)).astype(BF16)
            ecl = jnp.exp(cle)
            bm = bm_ref[rs, :]
            cm = cm_ref[rs, :]
            for g in range(SSD_GROUPS):
                gs = slice(g * 512, (g + 1) * 512)
                ns = slice(g * SSD_STATE, (g + 1) * SSD_STATE)
                bm_g, cm_g = bm[:, ns], cm[:, ns]
                hprev = hst[g]
                hprev_b = hprev.astype(BF16)
                prev_ref[k, g] = hprev_b
                yoff = jnp.dot(cm_g, hprev_b, preferred_element_type=F32) * ecs[:, gs]
                st = lax.dot_general(bm_g, xst[:, gs], (((0,), (0,)), ((), ())), preferred_element_type=F32)
                hst[g] = hprev * ecl[:, gs] + st
                b4 = jnp.concatenate([bm_g] * 4, axis=0)
                cb4 = lax.dot_general(cm_g, b4, (((1,), (1,)), ((), ())), preferred_element_type=F32)
                for q in range(2):
                    cols = slice(g * 512 + q * 256, g * 512 + (q + 1) * 256)
                    _, mq = _quad_terms(ce[:, cols], cb4, itile_v, causal_v)
                    xbd = _block_diag4(xdtb[:, cols], hmask_v)
                    ydiag = jnp.dot(mq.astype(BF16), xbd, preferred_element_type=F32)
                    y_ref[rs, cols] = ydiag + yoff[:, q * 256:(q + 1) * 256] + dskip[:, cols]

    T = SSD_STEP * CHUNK
    ins = [("row", xbc, T, *CB_XS), ("row", xbc, T, *CB_BM), ("row", xbc, T, *CB_CM),
           ("row", dt_raw, T, DT_PAD, 0), ("full", dt_bias), ("full", a_neg), ("full", d_exp),
           ("full", e), ("full", ltri), ("full", itile), ("full", causal), ("full", hmask)]
    outs = [("row", S, D_SSD, F32, T, D_SSD, 0), ("lead", (nc, SSD_GROUPS, SSD_STATE, 512), BF16, SSD_STEP)]
    return _pcall(name, body, S // T, ins, outs, scratch=[pltpu.VMEM((SSD_GROUPS, SSD_STATE, 512), F32)])


def _ssd_bwd(name, xbc, dt_raw, dt_bias, a_neg, d_exp, prev, dy, consts):
    S = xbc.shape[0]
    nc = S // CHUNK
    e, et, ltri, ltri_t, itile, causal, hmask = consts

    def body(xs_ref, bm_ref, cm_ref, dt_ref, dtb_ref, a_ref, dex_ref, prev_ref, dy_ref,
             e_ref, et_ref, ltri_ref, ltt_ref, it_ref, ca_ref, hm_ref,
             dx_ref, ddt_ref, da_ref, dbias_ref, dd_ref, dh, dce_ref, dxdt_ref):
        i = pl.program_id(0)

        @pl.when(i == 0)
        def _():
            dh[...] = jnp.zeros_like(dh)

        xs, raw, dtv, cs, dte, ce_all, xdt_all, ecs_all = _ssd_chunk_common(
            xs_ref, bm_ref, cm_ref, dt_ref, dtb_ref, a_ref, e_ref, ltri_ref)
        itile_v, causal_v, hmask_v = it_ref[...], ca_ref[...], hm_ref[...]
        dyv = dy_ref[...]
        last_row = _rows((CHUNK, 512)) == CHUNK - 1
        for k in reversed(range(SSD_STEP)):
            rs = slice(k * CHUNK, (k + 1) * CHUNK)
            ce, xdt, ecs = ce_all[rs], xdt_all[rs], ecs_all[rs]
            cle = ce[CHUNK - 1:CHUNK, :]
            dend = jnp.exp(cle - ce)
            xdtb = xdt.astype(BF16)
            xst = (xdt * dend).astype(BF16)
            ecl = jnp.exp(cle)
            bm = bm_ref[rs, :]
            cm = cm_ref[rs, :]
            for g in range(SSD_GROUPS):
                gs = slice(g * 512, (g + 1) * 512)
                ns = slice(g * SSD_STATE, (g + 1) * SSD_STATE)
                bm_g, cm_g = bm[:, ns], cm[:, ns]
                hprev_b = prev_ref[k, g]
                dhn = dh[g]
                dhn_b = dhn.astype(BF16)
                dy_g = dyv[rs, gs]
                ecs_g, dend_g, xdt_g, ecl_g = ecs[:, gs], dend[:, gs], xdt[:, gs], ecl[:, gs]
                z = jnp.dot(cm_g, hprev_b, preferred_element_type=F32)
                dz = dy_g * ecs_g
                dzb = dz.astype(BF16)
                dce_g = dz * z
                dcm_g = lax.dot_general(dzb, hprev_b, (((1,), (1,)), ((), ())), preferred_element_type=F32)
                dprev = lax.dot_general(cm_g, dzb, (((0,), (0,)), ((), ())), preferred_element_type=F32) + dhn * ecl_g
                dcl = jnp.sum(dhn * hprev_b.astype(F32), axis=0, keepdims=True) * ecl_g
                gmat = jnp.dot(bm_g, dhn_b, preferred_element_type=F32)
                dbm_g = lax.dot_general(xst[:, gs], dhn_b, (((1,), (1,)), ((), ())), preferred_element_type=F32)
                dxdt_g = gmat * dend_g
                t = gmat * xdt_g * dend_g
                dce_g = dce_g - t
                dcl = dcl + jnp.sum(t, axis=0, keepdims=True)
                dce_g = dce_g + jnp.where(last_row, dcl, 0.0)
                dh[g] = dprev
                b4 = jnp.concatenate([bm_g] * 4, axis=0)
                cb4 = lax.dot_general(cm_g, b4, (((1,), (1,)), ((), ())), preferred_element_type=F32)
                for q in range(2):
                    qs = slice(q * 256, (q + 1) * 256)
                    cols = slice(g * 512 + q * 256, g * 512 + (q + 1) * 256)
                    dec, mq = _quad_terms(ce[:, cols], cb4, itile_v, causal_v)
                    mqb = mq.astype(BF16)
                    xbd = _block_diag4(xdtb[:, cols], hmask_v)
                    dyq = dy_g[:, qs].astype(BF16)
                    dm = lax.dot_general(dyq, xbd, (((1,), (1,)), ((), ())), preferred_element_type=F32)
                    rmat = lax.dot_general(mqb, dyq, (((0,), (0,)), ((), ())), preferred_element_type=F32)
                    dxq = rmat[0:64] * hmask_v[0:1]
                    for j in range(1, 4):
                        dxq = dxq + rmat[64 * j:64 * (j + 1)] * hmask_v[j:j + 1]
                    tq = dm * dec
                    tqb = tq.astype(BF16)
                    dcm_g = dcm_g + jnp.dot(tqb, b4, preferred_element_type=F32)
                    rb = lax.dot_general(tqb, cm_g, (((0,), (0,)), ((), ())), preferred_element_type=F32)
                    dbm_g = dbm_g + rb[0:64] + rb[64:128] + rb[128:192] + rb[192:256]
                    dseg = tq * cb4
                    colsum = jnp.sum(dseg, axis=0, keepdims=True)
                    dce_ref[rs, cols] = dce_g[:, qs] + dseg - itile_v * colsum
                    dxdt_ref[rs, cols] = dxdt_g[:, qs] + dxq
                dx_ref[rs, D_SSD + g * SSD_STATE:D_SSD + (g + 1) * SSD_STATE] = dbm_g
                dx_ref[rs, D_SSD + D_BC + g * SSD_STATE:D_SSD + D_BC + (g + 1) * SSD_STATE] = dcm_g
        dxdt = dxdt_ref[...]
        dexv = dex_ref[...]
        dx_ref[:, 0:D_SSD] = dxdt * dte + dyv * dexv
        _acc(dd_ref, i, jnp.sum(dyv * xs, axis=0, keepdims=True))
        etv = et_ref[...]
        dcs = _dot01_r(dce_ref[...], etv)
        dda = _dot01_l(ltt_ref[...], dcs)
        av = a_ref[...]
        ddtv = dda * av + _dot01_r(dxdt * xs, etv)
        _acc(da_ref, i, jnp.sum(dda * dtv, axis=0, keepdims=True))
        draw = ddtv * _sigmoid(raw)
        ddt_ref[...] = draw.astype(BF16)
        _acc(dbias_ref, i, jnp.sum(draw, axis=0, keepdims=True))

    T = SSD_STEP * CHUNK
    ins = [("row", xbc, T, *CB_XS), ("row", xbc, T, *CB_BM), ("row", xbc, T, *CB_CM),
           ("row", dt_raw, T, DT_PAD, 0), ("full", dt_bias), ("full", a_neg), ("full", d_exp),
           ("lead", prev, SSD_STEP), ("row", dy, T, D_SSD, 0),
           ("full", e), ("full", et), ("full", ltri), ("full", ltri_t), ("full", itile), ("full", causal), ("full", hmask)]
    outs = [("row", S, D_XBC, F32, T, D_XBC, 0), ("row", S, DT_PAD, BF16, T, DT_PAD, 0),
            ("full", (1, DT_PAD), F32), ("full", (1, DT_PAD), F32), ("full", (1, D_SSD), F32)]
    scratch = [pltpu.VMEM((SSD_GROUPS, SSD_STATE, 512), F32), pltpu.VMEM((T, D_SSD), F32), pltpu.VMEM((T, D_SSD), F32)]
    return _pcall(name, body, S // T, ins, outs, scratch=scratch, reverse=True)


def _gate_norm_fwd(name, ycore, proj, norm_w, tile):
    S = ycore.shape[0]
    n = S // tile

    def body(y_ref, z_ref, w_ref, o_ref):
        y2 = y_ref[...] * _silu(z_ref[...].astype(F32))
        wv = w_ref[...]
        for g in range(SSD_GROUPS):
            gs = slice(g * 512, (g + 1) * 512)
            seg = y2[:, gs]
            r = lax.rsqrt(jnp.mean(seg * seg, axis=-1, keepdims=True) + EPS)
            o_ref[:, gs] = (seg * r * wv[:, gs]).astype(BF16)

    return _pcall(name, body, n, [("row", ycore, tile, D_SSD, 0), ("row", proj, tile, *CB_Z), ("full", norm_w)],
                  [("row", S, D_SSD, BF16, tile, D_SSD, 0)])[0]


def _gate_norm_bwd(name, dout, ycore, proj, norm_w, dproj, tile):
    S = ycore.shape[0]
    n = S // tile

    def body(do_ref, y_ref, z_ref, w_ref, buf_ref, dz_ref, dy_ref, dw_ref):
        i = pl.program_id(0)
        yv = y_ref[...]
        zv = z_ref[...].astype(F32)
        sz = _silu(zv)
        y2 = yv * sz
        dov = do_ref[...].astype(F32)
        wv = w_ref[...]
        dws, dy2s = [], []
        for g in range(SSD_GROUPS):
            gs = slice(g * 512, (g + 1) * 512)
            seg = y2[:, gs]
            r = lax.rsqrt(jnp.mean(seg * seg, axis=-1, keepdims=True) + EPS)
            yn = seg * r
            dws.append(jnp.sum(dov[:, gs] * yn, axis=0, keepdims=True))
            dyn = dov[:, gs] * wv[:, gs]
            dy2s.append(r * (dyn - yn * jnp.mean(dyn * yn, axis=-1, keepdims=True)))
        dy2 = jnp.concatenate(dy2s, axis=1)
        dy_ref[...] = dy2 * sz
        dz_ref[...] = (dy2 * yv * _dsilu(zv)).astype(BF16)
        _acc(dw_ref, i, jnp.concatenate(dws, axis=1))

    ins = [("row", dout, tile, D_SSD, 0), ("row", ycore, tile, D_SSD, 0), ("row", proj, tile, *CB_Z), ("full", norm_w),
           ("any", dproj)]
    outs = [("row", S, N_PROJ, BF16, tile, *CB_Z), ("row", S, D_SSD, F32, tile, D_SSD, 0), ("full", (1, D_SSD), F32)]
    return _pcall(name, body, n, ins, outs, aliases={4: 0})


_XA_SCALE = XA_HEAD_DIM ** -0.5
_NT = (((1,), (1,)), ((), ()))
_TN = (((0,), (0,)), ((), ()))


def _xa_probs(qh, kh):
    s = lax.dot_general(qh, kh, _NT, preferred_element_type=F32) * _XA_SCALE
    s = s - jnp.max(s, axis=-1, keepdims=True)
    p = jnp.exp(s)
    return p / jnp.sum(p, axis=-1, keepdims=True)


def _xa_fwd(name, proj, kv, tile):
    S = proj.shape[0]
    n = S // tile
    Dh = XA_HEAD_DIM

    def body(q_ref, kv_ref, o_ref):
        for hd in range(XA_HEADS):
            qh = q_ref[:, hd * Dh:(hd + 1) * Dh]
            kh = kv_ref[:, hd * Dh:(hd + 1) * Dh]
            vh = kv_ref[:, D_MODEL + hd * Dh:D_MODEL + (hd + 1) * Dh]
            p = _xa_probs(qh, kh)
            o_ref[:, hd * Dh:(hd + 1) * Dh] = jnp.dot(p.astype(BF16), vh, preferred_element_type=F32).astype(BF16)

    return _pcall(name, body, n, [("row", proj, tile, *CB_XA_Q), ("full", kv)],
                  [("row", S, D_MODEL, BF16, tile, D_MODEL, 0)])[0]


def _xa_bwd(name, proj, kv, dout, dproj, tile):
    S = proj.shape[0]
    n = S // tile
    Dh = XA_HEAD_DIM

    def body(q_ref, kv_ref, do_ref, buf_ref, dq_ref, dkv_ref):
        i = pl.program_id(0)
        for hd in range(XA_HEADS):
            ks_ = slice(hd * Dh, (hd + 1) * Dh)
            vs_ = slice(D_MODEL + hd * Dh, D_MODEL + (hd + 1) * Dh)
            qh = q_ref[:, ks_]
            kh = kv_ref[:, ks_]
            vh = kv_ref[:, vs_]
            doh = do_ref[:, ks_].astype(BF16)
            p = _xa_probs(qh, kh)
            pb = p.astype(BF16)
            dp = lax.dot_general(doh, vh, _NT, preferred_element_type=F32)
            dv = lax.dot_general(pb, doh, _TN, preferred_element_type=F32)
            ds = (p * (dp - jnp.sum(dp * p, axis=-1, keepdims=True)) * _XA_SCALE).astype(BF16)
            dq_ref[:, ks_] = jnp.dot(ds, kh, preferred_element_type=F32).astype(BF16)
            dk = lax.dot_general(ds, qh, _TN, preferred_element_type=F32)

            @pl.when(i == 0)
            def _(dk=dk, dv=dv, ks_=ks_, vs_=vs_):
                dkv_ref[:, ks_] = dk
                dkv_ref[:, vs_] = dv

            @pl.when(i > 0)
            def _(dk=dk, dv=dv, ks_=ks_, vs_=vs_):
                dkv_ref[:, ks_] += dk
                dkv_ref[:, vs_] += dv

    ins = [("row", proj, tile, *CB_XA_Q), ("full", kv), ("row", dout, tile, D_MODEL, 0), ("any", dproj)]
    outs = [("row", S, N_PROJ, BF16, tile, *CB_XA_Q), ("full", (N_MEM, 2 * D_MODEL), F32)]
    return _pcall(name, body, n, ins, outs, aliases={3: 0})


def _adamw(name, w, g, m, v):
    L, R, C = w.shape
    tile = _pick(R, [t for t in (256, 128, 64, 32, 16, 8) if t * C <= 128 * 2048])
    bc1 = 1.0 - ADAM_B1 ** ADAM_STEP
    bc2 = 1.0 - ADAM_B2 ** ADAM_STEP

    def body(w_ref, g_ref, m_ref, v_ref, d_ref, nm_ref, nv_ref):
        gv = g_ref[...]
        mn = ADAM_B1 * m_ref[...] + (1.0 - ADAM_B1) * gv
        vn = ADAM_B2 * v_ref[...] + (1.0 - ADAM_B2) * (gv * gv)
        nm_ref[...] = mn
        nv_ref[...] = vn
        d_ref[...] = -ADAM_LR * ((mn / bc1) / (jnp.sqrt(vn / bc2) + ADAM_EPS) + ADAM_WD * w_ref[...])

    spec = pl.BlockSpec((1, tile, C), lambda l, i: (l, i, 0))
    return pl.pallas_call(
        body, name=name, grid=(L, R // tile), in_specs=[spec] * 4, out_specs=[spec] * 3,
        out_shape=[jax.ShapeDtypeStruct((L, R, C), F32)] * 3,
        compiler_params=_cparams(("arbitrary", "arbitrary")),
    )(w, g, m, v)


def _axpy(name, coef, a, b, tile):
    S, C = a.shape
    n = S // tile

    def body(a_ref, b_ref, o_ref):
        o_ref[...] = coef * a_ref[...].astype(F32) + b_ref[...].astype(F32)

    return _pcall(name, body, n, [("row", a, tile, C, 0), ("row", b, tile, C, 0)], [("row", S, C, F32, tile, C, 0)])[0]


def _sum_terms(name, terms, out_dtype):
    R, C = terms[0].shape
    tile = _pick(R, (512, 256, 128, 64, 32, 16, 8))
    n = R // tile
    nt = len(terms)

    def body(*refs):
        vals = [r[...].astype(F32) for r in refs[:nt]]
        while len(vals) > 1:
            vals = [vals[k] + vals[k + 1] for k in range(0, len(vals), 2)]
        refs[nt][...] = vals[0].astype(out_dtype)

    return _pcall(name, body, n, [("row", t, tile, C, 0) for t in terms], [("row", R, C, out_dtype, tile, C, 0)])[0]


def _me():
    return lax.axis_index("x"), lax.axis_index("y"), lax.axis_index("c")


def _other_chips(x, y):
    return [(1 - x, y), (x, 1 - y), (1 - x, 1 - y)]


_ANY = pl.BlockSpec(memory_space=pl.ANY)


def _swap_sibling(name, src):
    def body(src_ref, out_ref, send_sem, recv_sem):
        x, y, c = _me()
        cp = pltpu.make_async_remote_copy(src_ref=src_ref, dst_ref=out_ref, send_sem=send_sem, recv_sem=recv_sem,
                                          device_id=(x, y, 1 - c), device_id_type=MESH)
        cp.start()
        cp.wait()

    return pl.pallas_call(
        body, name=name, in_specs=[_ANY], out_specs=_ANY, out_shape=jax.ShapeDtypeStruct(src.shape, src.dtype),
        scratch_shapes=[pltpu.SemaphoreType.DMA, pltpu.SemaphoreType.DMA],
    )(src)


def _send_chips(name, src, per_chip):
    shape = src.shape[1:] if per_chip else src.shape

    def body(src_ref, out_ref, send_sems, recv_sems):
        x, y, c = _me()
        cps = []
        for k, (cx, cy) in enumerate(_other_chips(x, y)):
            s = src_ref.at[2 * cx + cy] if per_chip else src_ref
            cps.append(pltpu.make_async_remote_copy(src_ref=s, dst_ref=out_ref.at[k], send_sem=send_sems.at[k],
                                                    recv_sem=recv_sems.at[k], device_id=(cx, cy, c), device_id_type=MESH))
        for cp in cps:
            cp.start()
        for cp in cps:
            cp.wait()

    return pl.pallas_call(
        body, name=name, in_specs=[_ANY], out_specs=_ANY, out_shape=jax.ShapeDtypeStruct((3,) + shape, src.dtype),
        scratch_shapes=[pltpu.SemaphoreType.DMA((3,)), pltpu.SemaphoreType.DMA((3,))],
    )(src)


W_IN_SHARD = N_IN // 4
W_IN_INNER = W_IN_SHARD - 8


def _win_rows(chip_x, chip_y):
    start = (2 * chip_x + chip_y) * W_IN_SHARD + 8 * chip_y
    return pl.ds(pl.multiple_of(start, 2 * 8), W_IN_INNER)


def _gather_multi(name, srcs, kinds):
    n = len(srcs)
    xo, yo, _ = _me()
    bufs = []
    for s, kd in zip(srcs, kinds):
        if kd == "blk":
            buf = lax.empty((2, 4) + s.shape[1:], s.dtype)
            bufs.append(lax.dynamic_update_slice_in_dim(buf, s[:, None], 2 * xo + yo, axis=1))
        else:
            buf = lax.empty((2, N_IN, D_MODEL), s.dtype)
            bufs.append(lax.dynamic_update_slice_in_dim(buf, s, (2 * xo + yo) * W_IN_SHARD + 8 * yo, axis=1))

    def body(*refs):
        src_refs, out_refs = refs[:n], refs[2 * n:3 * n]
        send_sems, recv_sems = refs[3 * n:]
        x, y, c = _me()
        chips = _other_chips(x, y)

        def dst(i, part, cx, cy):
            if kinds[i] == "blk":
                return out_refs[i].at[part, 2 * cx + cy]
            return out_refs[i].at[part, _win_rows(cx, cy)]

        def copy(i, k, part, cx, cy, to, src=None):
            d = dst(i, part, cx, cy)
            return pltpu.make_async_remote_copy(src_ref=d if src is None else src, dst_ref=d,
                                                send_sem=send_sems.at[i, k], recv_sem=recv_sems.at[i, k],
                                                device_id=to, device_id_type=MESH)

        first = [copy(i, k, c, x, y, (cx, cy, c), src=src_refs[i].at[c]) for k, (cx, cy) in enumerate(chips) for i in range(n)]
        for cp in first:
            cp.start()
        passed = []
        for k, (cx, cy) in enumerate(chips):
            for i in range(n):
                copy(i, k, c, cx, cy, (x, y, c)).wait_recv()
                fw = copy(i, 3 + k, c, cx, cy, (x, y, 1 - c))
                fw.start()
                passed.append(fw)
        for k, (cx, cy) in enumerate(chips):
            for i in range(n):
                copy(i, 3 + k, 1 - c, cx, cy, (x, y, c)).wait_recv()
        for cp in first + passed:
            cp.wait_send()

    return pl.pallas_call(
        body, name=name, in_specs=[_ANY] * (2 * n), out_specs=[_ANY] * n,
        out_shape=[jax.ShapeDtypeStruct(b.shape, b.dtype) for b in bufs],
        input_output_aliases={n + i: i for i in range(n)},
        scratch_shapes=[pltpu.SemaphoreType.DMA((n, 6)), pltpu.SemaphoreType.DMA((n, 6))],
    )(*srcs, *bufs)


def _swap_parts_multi(name, srcs):
    n = len(srcs)

    def body(*refs):
        src_refs, out_refs = refs[:n], refs[n:2 * n]
        send_sems, recv_sems = refs[2 * n:]
        x, y, c = _me()
        cps = [pltpu.make_async_remote_copy(src_ref=src_refs[i].at[1 - c], dst_ref=out_refs[i], send_sem=send_sems.at[i],
                                            recv_sem=recv_sems.at[i], device_id=(x, y, 1 - c), device_id_type=MESH)
               for i in range(n)]
        for cp in cps:
            cp.start()
        for cp in cps:
            cp.wait()

    return pl.pallas_call(
        body, name=name, in_specs=[_ANY] * n, out_specs=[_ANY] * n,
        out_shape=[jax.ShapeDtypeStruct(s.shape[1:], s.dtype) for s in srcs],
        scratch_shapes=[pltpu.SemaphoreType.DMA((n,)), pltpu.SemaphoreType.DMA((n,))],
    )(*srcs)


def _send_chips_multi(name, srcs):
    n = len(srcs)

    def body(*refs):
        src_refs, out_refs = refs[:n], refs[n:2 * n]
        send_sems, recv_sems = refs[2 * n:]
        x, y, c = _me()
        cps = [pltpu.make_async_remote_copy(src_ref=src_refs[i].at[2 * cx + cy], dst_ref=out_refs[i].at[k],
                                            send_sem=send_sems.at[i, k], recv_sem=recv_sems.at[i, k],
                                            device_id=(cx, cy, c), device_id_type=MESH)
               for k, (cx, cy) in enumerate(_other_chips(x, y)) for i in range(n)]
        for cp in cps:
            cp.start()
        for cp in cps:
            cp.wait()

    return pl.pallas_call(
        body, name=name, in_specs=[_ANY] * n, out_specs=[_ANY] * n,
        out_shape=[jax.ShapeDtypeStruct((3,) + s.shape[1:], s.dtype) for s in srcs],
        scratch_shapes=[pltpu.SemaphoreType.DMA((n, 3)), pltpu.SemaphoreType.DMA((n, 3))],
    )(*srcs)


def _join_parts_multi(name, bufs):
    n = len(bufs)

    def body(*refs):
        out_refs = refs[n:2 * n]
        send_sems, recv_sems = refs[2 * n:]
        x, y, c = _me()

        def copy(i, part):
            return pltpu.make_async_remote_copy(src_ref=out_refs[i].at[part], dst_ref=out_refs[i].at[part],
                                                send_sem=send_sems.at[i], recv_sem=recv_sems.at[i],
                                                device_id=(x, y, 1 - c), device_id_type=MESH)

        for i in range(n):
            copy(i, c).start()
        for i in range(n):
            copy(i, c).wait_send()
            copy(i, 1 - c).wait_recv()

    return pl.pallas_call(
        body, name=name, in_specs=[_ANY] * n, out_specs=[_ANY] * n,
        out_shape=[jax.ShapeDtypeStruct(b.shape, b.dtype) for b in bufs],
        input_output_aliases={i: i for i in range(n)},
        scratch_shapes=[pltpu.SemaphoreType.DMA((n,)), pltpu.SemaphoreType.DMA((n,))],
    )(*bufs)


def _col_tiles(R, C):
    tr = _pick(R, (512, 256, 128))
    if tr != R:
        return tr, C
    if R * C <= 512 * 1024:
        return R, C
    return R, _pick(C, (256, 128))


def _pair_sum(name, g, theirs, c):
    _, n4, R, C = g.shape
    tr, tc = _col_tiles(R, C)

    def body(c_ref, g_ref, t_ref, o_ref, ob_ref):
        s = g_ref[0, 0] + t_ref[0]
        o_ref[0] = s
        ob_ref[0] = s.astype(BF16)

    spec3 = pl.BlockSpec((1, tr, tc), lambda d, i, k, c_ref: (d, i, k))
    return pl.pallas_call(
        body, name=name,
        grid_spec=pltpu.PrefetchScalarGridSpec(
            num_scalar_prefetch=1, grid=(n4, R // tr, C // tc),
            in_specs=[pl.BlockSpec((1, 1, tr, tc), lambda d, i, k, c_ref: (c_ref[0], d, i, k)), spec3],
            out_specs=[spec3, spec3]),
        out_shape=[jax.ShapeDtypeStruct((n4, R, C), F32), jax.ShapeDtypeStruct((n4, R, C), BF16)],
        compiler_params=_cparams(("arbitrary", "arbitrary", "arbitrary")),
    )(c.reshape(1).astype(jnp.int32), g, theirs)


def _chip_sum(name, pair, got, j, c):
    _, R, C = pair.shape
    tr, tc = _col_tiles(R, C)

    def body(jc_ref, p_ref, g0, g1, g2, o_ref):
        o_ref[0] = (p_ref[0] + g2[0].astype(F32)) + (g0[0].astype(F32) + g1[0].astype(F32))

    def gspec(k):
        return pl.BlockSpec((1, tr, tc), lambda i, q, jc_ref, k=k: (k, i, q))

    return pl.pallas_call(
        body, name=name,
        grid_spec=pltpu.PrefetchScalarGridSpec(
            num_scalar_prefetch=1, grid=(R // tr, C // tc),
            in_specs=[pl.BlockSpec((1, tr, tc), lambda i, q, jc_ref: (jc_ref[0], i, q)), gspec(0), gspec(1), gspec(2)],
            out_specs=pl.BlockSpec((1, tr, tc), lambda i, q, jc_ref: (jc_ref[1], i, q))),
        out_shape=jax.ShapeDtypeStruct((2, R, C), F32),
        compiler_params=_cparams(("arbitrary", "arbitrary")),
    )(jnp.stack([j, c]).astype(jnp.int32), pair, got, got, got)


LANES = 1024
_BIG = (("w_in", "col"), ("ffn_w_in", "col"), ("mem_w_kv", "col"),
        ("w_br_lru", "row"), ("w_br_ssd", "row"), ("w_br_xa", "row"), ("w_out", "row"), ("ffn_w_down", "row"))
_SMALL_SHARDED = ("b_gate", "lru_conv_w", "ssd_conv_w")
_SMALL = ("b_gate", "lru_conv_w", "lru_conv_b", "lru_w_a", "lru_b_a", "lru_w_i", "lru_b_i", "lru_lambda",
          "ssd_conv_w", "ssd_conv_b", "ssd_dt_bias", "ssd_a_log", "ssd_d", "ssd_norm_w",
          "ln1_g", "ln1_b", "ln2_g", "ln2_b")
_W_NAMES = ("w_in", "b_gate", "lru_conv_w", "lru_conv_b", "lru_w_a", "lru_b_a", "lru_w_i", "lru_b_i", "lru_lambda",
            "ssd_conv_w", "ssd_conv_b", "ssd_dt_bias", "ssd_a_log", "ssd_d", "ssd_norm_w", "mem_w_kv", "w_br_lru",
            "w_br_ssd", "w_br_xa", "w_out", "ln1_g", "ln1_b", "ffn_w_in", "ffn_w_down", "ln2_g", "ln2_b")
_IN_ORDER = ((4096, 7168), (8224, 11296), (2048, 4096), (0, 1024), (1024, 2048), (7200, 8224))
_IN_DT = (7168, 7200)


def _flat_rows(parts, row_multiple):
    flat = jnp.concatenate([p.reshape(-1) for p in parts])
    rows = -(-flat.size // LANES)
    rows = -(-rows // row_multiple) * row_multiple
    return jnp.pad(flat, (0, rows * LANES - flat.size)).reshape(rows, LANES)


def _take_parts(flat, shapes):
    out, off = [], 0
    for shp in shapes:
        size = math.prod(shp)
        out.append(flat[off:off + size].reshape(shp))
        off += size
    return out


def _gather_weights(w):
    _, y, _ = _me()
    wt = jnp.swapaxes(w["w_in"], 1, 2)
    inner = lax.dynamic_slice_in_dim(wt, 8 * y, W_IN_INNER, axis=1).astype(BF16)
    edge = lax.dynamic_slice_in_dim(wt, (1 - y) * W_IN_INNER, 8, axis=1)
    names = [nm for nm, _ in _BIG[1:]] + list(_SMALL_SHARDED)
    srcs = [inner, edge] + [w[nm].astype(BF16) for nm, _ in _BIG[1:]] + [w[nm] for nm in _SMALL_SHARDED]
    got = _gather_multi("gather_weights", srcs, ["rows"] + ["blk"] * (len(srcs) - 1))
    wt_all, edges = got[0], got[1]
    for a, b in ((0, 1), (2, 3)):
        tile = jnp.concatenate([edges[:, a], edges[:, b]], axis=1).astype(BF16)
        wt_all = lax.dynamic_update_slice_in_dim(wt_all, tile, b * W_IN_SHARD - 8, axis=1)
    full = {"w_in_t": wt_all}
    kinds = dict(_BIG)
    for nm, g in zip(names, got[2:]):
        L, _, r, cdim = g.shape
        if kinds.get(nm, "col") == "col":
            full[nm] = jnp.moveaxis(g, 1, 2).reshape(L, r, 4 * cdim)
        else:
            full[nm] = g.reshape(L, 4 * r, cdim)
    return full


def _reduce_big_grads(grads):
    x, y, c = _me()
    j = 2 * x + y
    gs = []
    for nm, kind in _BIG:
        g = jnp.stack(grads[nm])
        L, r, cdim = g.shape
        if nm == "w_in" or kind == "row":
            g = g.reshape(L, 4, r // 4, cdim)
        else:
            g = jnp.moveaxis(g.reshape(L, r, 4, cdim // 4), 2, 1)
        gs.append(g)
    theirs = _swap_parts_multi("grads_pair_swap", gs)
    pairs = [_pair_sum("grads_pair_sum", g, t, c) for g, t in zip(gs, theirs)]
    got = _send_chips_multi("grads_chip_send", [pb for _, pb in pairs])
    halves = [_chip_sum("grads_chip_sum", p, gt, j, c) for (p, _), gt in zip(pairs, got)]
    both = _join_parts_multi("grads_join", halves)
    out = dict(zip([nm for nm, _ in _BIG], both))
    out["w_in"] = jnp.swapaxes(out["w_in"], 1, 2)
    return out


def _reduce_small_grads(parts):
    shapes = [p.shape for p in parts]
    buf = _flat_rows(parts, 8)
    sib = _swap_sibling("small_pair_swap", buf)
    pair = _sum_terms("small_pair_sum", [buf, sib], F32)
    got = _send_chips("small_chip_send", pair, False)
    total = _sum_terms("small_chip_sum", [pair, got[2], got[0], got[1]], F32)
    return _take_parts(total.reshape(-1), shapes)


def _layer_fwd(xin, mem2, W, P, consts):
    S = xin.shape[0]
    T, TX = min(256, S), min(512, S)
    sv = {"xin": xin}
    proj = _mm("proj", xin, W["w_re_t"], "nt", BF16)
    dt_raw = _mm("proj_dt", xin, W["w_dt_t"], "nt", F32)
    xc = _conv_fwd("lru_conv", proj, CB_LRU_X, P["lru_conv_w"], P["lru_conv_b"], False, F32, T)
    y_lru, h = _lru_fwd("lru_scan", xc, proj, P["lru_w_a"], P["lru_w_i"], P["lru_b_a"], P["lru_b_i"], P["lru_lambda"], T)
    xact = _conv_fwd("ssd_conv", proj, CB_XBC, P["ssd_conv_w"], P["ssd_conv_b"], True, BF16, T)
    ycore, prev = _ssd_fwd("ssd_scan", xact, dt_raw, P["dt_bias"], P["a_neg"], P["d_exp"], consts)
    y_ssd = _gate_norm_fwd("ssd_norm", ycore, proj, P["ssd_norm_w"], T)
    kv = _mm("mem_kv", mem2, W["w_kv"], "nn", BF16)
    y_xa = _xa_fwd("xattn", proj, kv, TX)
    p_l = _mm("br_lru", y_lru, W["w_l"], "nn", BF16)
    p_s = _mm("br_ssd", y_ssd, W["w_s"], "nn", BF16)
    p_x = _mm("br_xa", y_xa, W["w_x"], "nn", BF16)
    merged = _merge_fwd("merge", proj, P["b_gate"], p_l, p_s, p_x, T)
    mix = _mm("mix_out", merged, W["w_o"], "nn", F32)
    x1, xh1, rs1 = _ln_fwd("ln_fwd", xin, mix, P["ln1_g"], P["ln1_b"], T)
    gu = _mm("ffn_in", x1, W["w_fi"], "nn", BF16)
    hmid = _swiglu_fwd("swiglu", gu, T)
    f = _mm("ffn_down", hmid, W["w_fd"], "nn", F32)
    x2, xh2, rs2 = _ln_fwd("ln_fwd", x1, f, P["ln2_g"], P["ln2_b"], T)
    sv.update(proj=proj, dt_raw=dt_raw, xc=xc, h=h, y_lru=y_lru, xact=xact, ycore=ycore, prev=prev, y_ssd=y_ssd, kv=kv,
              y_xa=y_xa, p_l=p_l, p_s=p_s, p_x=p_x, merged=merged, x1=x1, xh1=xh1, rs1=rs1, gu=gu, hmid=hmid,
              xh2=xh2, rs2=rs2)
    return x2, sv


def _layer_bwd(dys, coefs, sv, mem2, W, P, consts):
    S = sv["xin"].shape[0]
    T, TX = min(256, S), min(512, S)
    proj = sv["proj"]
    g = {}
    dz2, g["ln2_g"], g["ln2_b"] = _ln_bwd("ln_bwd_%d" % len(dys), dys, coefs, sv["xh2"], sv["rs2"], P["ln2_g"], T)
    dhmid = _mm("d_hmid", dz2, W["w_fd"], "nt", BF16)
    g["ffn_w_down"] = _mm("dw_ffn_down", sv["hmid"], dz2, "tn", F32)
    dgu = _swiglu_bwd("swiglu_bwd", sv["gu"], dhmid, T)
    dx1f = _mm("d_x1", dgu, W["w_fi"], "nt", F32)
    g["ffn_w_in"] = _mm("dw_ffn_in", sv["x1"], dgu, "tn", F32)
    dz1, g["ln1_g"], g["ln1_b"] = _ln_bwd("ln_bwd_2", [dz2, dx1f], [ALPHA, 1.0], sv["xh1"], sv["rs1"], P["ln1_g"], T)
    dmerged = _mm("d_merged", dz1, W["w_o"], "nt", BF16)
    g["w_out"] = _mm("dw_out", sv["merged"], dz1, "tn", F32)
    dproj, dpl, dps, dpx, dbg0, dbg1, dbg2 = _merge_bwd("merge_bwd", proj, P["b_gate"], sv["p_l"], sv["p_s"], sv["p_x"],
                                                         dmerged, T)
    g["b_gate"] = jnp.concatenate([dbg0, dbg1, dbg2], axis=0)
    dy_lru = _mm("d_ylru", dpl, W["w_l"], "nt", BF16)
    g["w_br_lru"] = _mm("dw_br_lru", sv["y_lru"], dpl, "tn", F32)
    dy_ssd = _mm("d_yssd", dps, W["w_s"], "nt", BF16)
    g["w_br_ssd"] = _mm("dw_br_ssd", sv["y_ssd"], dps, "tn", F32)
    dy_xa = _mm("d_yxa", dpx, W["w_x"], "nt", BF16)
    g["w_br_xa"] = _mm("dw_br_xa", sv["y_xa"], dpx, "tn", F32)
    dproj, dkv = _xa_bwd("xattn_bwd", proj, sv["kv"], dy_xa, dproj, TX)
    g["mem_w_kv"] = _mm("dw_kv", mem2, dkv, "tn", F32)
    dproj, dycore, g["ssd_norm_w"] = _gate_norm_bwd("ssd_norm_bwd", dy_ssd, sv["ycore"], proj, P["ssd_norm_w"], dproj, T)
    dxact, ddt, d_a, g_dtb, d_dexp = _ssd_bwd("ssd_scan_bwd", sv["xact"], sv["dt_raw"], P["dt_bias"], P["a_neg"], P["d_exp"],
                                              sv["prev"], dycore, consts)
    g["ssd_dt_bias"] = g_dtb[:, :SSD_HEADS]
    g["ssd_a_log"] = d_a[:, :SSD_HEADS] * P["a_neg"][:, :SSD_HEADS]
    g["ssd_d"] = jnp.sum(d_dexp.reshape(SSD_HEADS, SSD_HEAD_DIM), axis=-1)
    dproj, w0, w1, w2, w3, g["ssd_conv_b"] = _conv_bwd("ssd_conv_bwd", proj, CB_XBC, P["ssd_conv_w"], P["ssd_conv_b"], True,
                                                       dxact, dproj, T)
    g["ssd_conv_w"] = jnp.concatenate([w0, w1, w2, w3], axis=0)
    dproj, dxc, g["lru_w_a"], g["lru_w_i"], g["lru_b_a"], g["lru_b_i"], g["lru_lambda"] = _lru_bwd(
        "lru_scan_bwd", dy_lru, sv["xc"], proj, sv["h"], P["lru_w_a"], P["lru_w_i"], P["lru_b_a"], P["lru_b_i"],
        P["lru_lambda"], dproj, T)
    dproj, w0, w1, w2, w3, g["lru_conv_b"] = _conv_bwd("lru_conv_bwd", proj, CB_LRU_X, P["lru_conv_w"], P["lru_conv_b"], False,
                                                       dxc, dproj, T)
    g["lru_conv_w"] = jnp.concatenate([w0, w1, w2, w3], axis=0)
    xin = sv["xin"]
    dw_re_t = _mm("dw_in", dproj, xin, "tn", F32)
    dw_dt_t = _mm("dw_in_dt", ddt, xin, "tn", F32)
    pieces = {rng: dw_re_t[off:off + rng[1] - rng[0]]
              for rng, off in zip(_IN_ORDER, (0, 3072, 6144, 8192, 9216, 10240))}
    pieces[_IN_DT] = dw_dt_t[:SSD_HEADS]
    g["w_in"] = jnp.concatenate([pieces[k] for k in sorted(pieces)], axis=0)
    dxp = _mm("d_xin", dproj, W["w_re_t"], "nn", F32)
    dxs = _mm("d_xin_dt", ddt, W["w_dt_t"], "nn", F32, add=dxp)
    return [dz1, dxs], [ALPHA, 1.0], g


def _step(a):
    x2d, mem2, target = a["x"][0], a["mem"][0], a["loss_target"][0]
    S = x2d.shape[0]
    T = min(256, S)
    xi, yi, ci = _me()
    j = 2 * xi + yi
    w = {nm: a[nm] for nm in _W_NAMES}
    full = _gather_weights(w)
    consts = _ssd_consts()
    row = lambda v: v.reshape(1, -1)
    Ws, Ps = [], []
    for l in range(DEPTH):
        w_in_t = full["w_in_t"][l]
        w_re_t = jnp.concatenate([w_in_t[lo:hi] for lo, hi in _IN_ORDER], axis=0)
        w_dt_t = jnp.pad(w_in_t[_IN_DT[0]:_IN_DT[1]], ((0, DT_PAD - SSD_HEADS), (0, 0)))
        Ws.append(dict(w_re_t=w_re_t, w_dt_t=w_dt_t, w_fi=full["ffn_w_in"][l], w_kv=full["mem_w_kv"][l], w_l=full["w_br_lru"][l],
                       w_s=full["w_br_ssd"][l], w_x=full["w_br_xa"][l], w_o=full["w_out"][l], w_fd=full["ffn_w_down"][l]))
        pad_h = lambda v: jnp.pad(v.reshape(1, -1), ((0, 0), (0, DT_PAD - SSD_HEADS)))
        Ps.append(dict(
            b_gate=full["b_gate"][l], lru_conv_w=full["lru_conv_w"][l], ssd_conv_w=full["ssd_conv_w"][l],
            lru_conv_b=row(w["lru_conv_b"][l]), lru_w_a=w["lru_w_a"][l].astype(BF16), lru_w_i=w["lru_w_i"][l].astype(BF16),
            lru_b_a=row(w["lru_b_a"][l]), lru_b_i=row(w["lru_b_i"][l]), lru_lambda=row(w["lru_lambda"][l]),
            ssd_conv_b=row(w["ssd_conv_b"][l]), dt_bias=pad_h(w["ssd_dt_bias"][l]), a_neg=pad_h(-jnp.exp(w["ssd_a_log"][l])),
            d_exp=jnp.broadcast_to(w["ssd_d"][l][:, None], (SSD_HEADS, SSD_HEAD_DIM)).reshape(1, D_SSD),
            ssd_norm_w=row(w["ssd_norm_w"][l]), ln1_g=row(w["ln1_g"][l]), ln1_b=row(w["ln1_b"][l]),
            ln2_g=row(w["ln2_g"][l]), ln2_b=row(w["ln2_b"][l])))

    saved = []
    xcur = x2d
    for l in range(DEPTH):
        xcur, sv = _layer_fwd(xcur, mem2, Ws[l], Ps[l], consts)
        saved.append(sv)
    dy, loss_part = _loss_fwd_bwd("loss", xcur, target, T)
    loss = lax.psum(loss_part[0, 0], ("x", "y", "c"))

    dys, coefs = [dy], [1.0]
    layer_grads = [None] * DEPTH
    for l in reversed(range(DEPTH)):
        dys, coefs, layer_grads[l] = _layer_bwd(dys, coefs, saved[l], mem2, Ws[l], Ps[l], consts)
    grad_x = _axpy("grad_x", coefs[0], dys[0], dys[1], T)[None]

    big = _reduce_big_grads({nm: [layer_grads[l][nm] for l in range(DEPTH)] for nm, _ in _BIG})
    stacked = {nm: jnp.stack([layer_grads[l][nm] for l in range(DEPTH)]) for nm in _SMALL}
    small_parts = [stacked[nm].reshape((DEPTH,) + tuple(sh)) for nm, sh in
                   ((nm, (3, D_MODEL) if nm == "b_gate" else (4, D_MODEL) if nm == "lru_conv_w" else
                     (4, D_XBC) if nm == "ssd_conv_w" else w[nm].shape[1:]) for nm in _SMALL)]
    small = dict(zip(_SMALL, _reduce_small_grads(small_parts)))
    for nm in _SMALL_SHARDED:
        cs = w[nm].shape[2]
        small[nm] = lax.dynamic_slice_in_dim(small[nm], j * cs, cs, axis=2)
    grads = {**big, **small}

    delta, new_m, new_v = {}, {}, {}
    for nm, _ in _BIG:
        delta[nm], new_m[nm], new_v[nm] = _adamw("adamw_" + nm, w[nm], grads[nm], a["m_" + nm], a["v_" + nm])
    shapes = [w[nm].shape for nm in _SMALL]
    packs = [_flat_rows([src[nm] for nm in _SMALL], 8)[None] for src in
             (w, grads, {nm: a["m_" + nm] for nm in _SMALL}, {nm: a["v_" + nm] for nm in _SMALL})]
    d_, m_, v_ = _adamw("adamw_small", *packs)
    for dst, buf in ((delta, d_), (new_m, m_), (new_v, v_)):
        dst.update(zip(_SMALL, _take_parts(buf.reshape(-1), shapes)))

    outs = [loss, grad_x]
    for group in (grads, delta, new_m, new_v):
        outs += [group[nm] for nm in _W_NAMES]
    return tuple(outs)


def kernel(x, mem, w_in, b_gate, lru_conv_w, lru_conv_b, lru_w_a, lru_b_a, lru_w_i, lru_b_i, lru_lambda, ssd_conv_w, ssd_conv_b, ssd_dt_bias, ssd_a_log, ssd_d, ssd_norm_w, mem_w_kv, w_br_lru, w_br_ssd, w_br_xa, w_out, ln1_g, ln1_b, ffn_w_in, ffn_w_down, ln2_g, ln2_b, loss_target, m_w_in, m_b_gate, m_lru_conv_w, m_lru_conv_b, m_lru_w_a, m_lru_b_a, m_lru_w_i, m_lru_b_i, m_lru_lambda, m_ssd_conv_w, m_ssd_conv_b, m_ssd_dt_bias, m_ssd_a_log, m_ssd_d, m_ssd_norm_w, m_mem_w_kv, m_w_br_lru, m_w_br_ssd, m_w_br_xa, m_w_out, m_ln1_g, m_ln1_b, m_ffn_w_in, m_ffn_w_down, m_ln2_g, m_ln2_b, v_w_in, v_b_gate, v_lru_conv_w, v_lru_conv_b, v_lru_w_a, v_lru_b_a, v_lru_w_i, v_lru_b_i, v_lru_lambda, v_ssd_conv_w, v_ssd_conv_b, v_ssd_dt_bias, v_ssd_a_log, v_ssd_d, v_ssd_norm_w, v_mem_w_kv, v_w_br_lru, v_w_br_ssd, v_w_br_xa, v_w_out, v_ln1_g, v_ln1_b, v_ffn_w_in, v_ffn_w_down, v_ln2_g, v_ln2_b):
    return _step(dict(locals()))
```

```python
import functools
import math

import jax
import jax.numpy as jnp
from jax import lax
from jax.experimental import pallas as pl
from jax.experimental.pallas import tpu as pltpu

F32, BF16 = jnp.float32, jnp.bfloat16
MESH = pl.DeviceIdType.MESH
VMEM_LIMIT_BYTES = 56 * 2**20
HALO = 16

D_MODEL = 1024
DEPTH = 2
CHUNK = 64
N_MEM = 256
LRU_BLOCKS = 8
LRU_BLOCK = 128
LRU_C = 8.0
D_SSD = 2048
SSD_HEADS = 32
SSD_HEAD_DIM = 64
SSD_GROUPS = 4
SSD_STATE = 128
D_BC = SSD_GROUPS * SSD_STATE
D_XBC = D_SSD + 2 * D_BC
XA_HEADS = 4
XA_HEAD_DIM = 256
D_FF = 2816
ALPHA = (2 * DEPTH) ** 0.25
EPS = 1e-5
N_IN = 11296
N_PROJ = 11264
DT_PAD = 128

ADAM_LR, ADAM_B1, ADAM_B2, ADAM_EPS, ADAM_WD, ADAM_STEP = 0.001, 0.9, 0.999, 1e-08, 0.01, 10

CB_XBC = (3072, 0)
CB_XS, CB_BM, CB_CM = (2048, 0), (512, 4), (512, 5)
CB_LOGITS = (3072, 1)
CB_G0, CB_G1, CB_G2 = (1024, 3), (1024, 4), (1024, 5)
CB_Z = (2048, 3)
CB_LRU_X, CB_LRU_GATE, CB_XA_Q = (1024, 8), (1024, 9), (1024, 10)


def _cparams(sem):
    return pltpu.CompilerParams(dimension_semantics=sem, vmem_limit_bytes=VMEM_LIMIT_BYTES)


class _Rider:
    def __init__(self, ins, alias, outs, sems, start, finish):
        self.ins, self.alias, self.outs, self.sems = list(ins), list(alias), list(outs), list(sems)
        self.start, self.finish = start, finish

    def operands(self):
        return self.ins + self.alias

    def out_shapes(self):
        return [jax.ShapeDtypeStruct(a.shape, a.dtype) for a in self.alias] + self.outs

    def aliases(self, n_in, n_out):
        return {n_in + len(self.ins) + q: n_out + q for q in range(len(self.alias))}

    def split(self, refs, n_in, n_out, n_scratch):
        ni, na, no = len(self.ins), len(self.alias), len(self.outs)
        main = list(refs[:n_in])
        r_in = refs[n_in:n_in + ni]
        p = n_in + ni + na
        main += refs[p:p + n_out]
        r_alias = refs[p + n_out:p + n_out + na]
        r_out = refs[p + n_out + na:p + n_out + na + no]
        p = p + n_out + na + no
        main += refs[p:p + n_scratch]
        sems = refs[p + n_scratch:]
        return main, (r_in, r_alias, r_out, sems)


def _run_rider(name, rider):
    def body(*refs):
        _, parts = rider.split(refs, 0, 0, 0)
        rider.start(*parts)
        rider.finish(*parts)

    n_ops = len(rider.operands())
    return pl.pallas_call(
        body, name=name, in_specs=[_ANY] * n_ops, out_specs=[_ANY] * len(rider.out_shapes()),
        out_shape=rider.out_shapes(), input_output_aliases=rider.aliases(0, 0), scratch_shapes=rider.sems,
    )(*rider.operands())


def _pcall(name, body, n, ins, outs, scratch=(), reverse=False, aliases=None, rider=None):
    def ridx(i):
        return (n - 1 - i) if reverse else i

    in_specs, args = [], []
    for sp in ins:
        kind, arr = sp[0], sp[1]
        if kind == "row":
            _, _, tile, width, cb = sp
            in_specs.append(pl.BlockSpec((tile, width), lambda i, cb=cb: (ridx(i), cb)))
        elif kind == "prev":
            _, _, tile, width, cb = sp
            t = tile // HALO
            in_specs.append(pl.BlockSpec((HALO, width), lambda i, cb=cb, t=t: (jnp.maximum(ridx(i) * t - 1, 0), cb)))
        elif kind == "next":
            _, _, tile, width, cb = sp
            t = tile // HALO
            last = arr.shape[0] // HALO - 1
            in_specs.append(pl.BlockSpec((HALO, width), lambda i, cb=cb, t=t, last=last: (jnp.minimum((ridx(i) + 1) * t, last), cb)))
        elif kind == "lead":
            nd = arr.ndim
            in_specs.append(pl.BlockSpec((sp[2],) + arr.shape[1:], lambda i, nd=nd: (ridx(i),) + (0,) * (nd - 1)))
        elif kind == "full":
            nd = arr.ndim
            in_specs.append(pl.BlockSpec(arr.shape, lambda i, nd=nd: (0,) * nd))
        elif kind == "any":
            in_specs.append(pl.BlockSpec(memory_space=pl.ANY))
        else:
            raise ValueError(kind)
        args.append(arr)
    out_specs, out_shape = [], []
    for sp in outs:
        kind = sp[0]
        if kind == "row":
            _, rows, cols, dtype, tile, width, cb = sp
            out_shape.append(jax.ShapeDtypeStruct((rows, cols), dtype))
            out_specs.append(pl.BlockSpec((tile, width), lambda i, cb=cb: (ridx(i), cb)))
        elif kind == "lead":
            _, shape, dtype, lead = sp
            nd = len(shape)
            out_shape.append(jax.ShapeDtypeStruct(shape, dtype))
            out_specs.append(pl.BlockSpec((lead,) + tuple(shape[1:]), lambda i, nd=nd: (ridx(i),) + (0,) * (nd - 1)))
        elif kind == "full":
            _, shape, dtype = sp
            nd = len(shape)
            out_shape.append(jax.ShapeDtypeStruct(shape, dtype))
            out_specs.append(pl.BlockSpec(tuple(shape), lambda i, nd=nd: (0,) * nd))
        else:
            raise ValueError(kind)
    aliases = dict(aliases or {})
    scratch = list(scratch)
    kernel_body = body
    if rider is not None:
        n_in, n_out, n_scratch = len(args), len(out_shape), len(scratch)

        def kernel_body(*refs):
            main, parts = rider.split(refs, n_in, n_out, n_scratch)
            i = pl.program_id(0)

            @pl.when(i == 0)
            def _():
                rider.start(*parts)

            body(*main)

            @pl.when(i == n - 1)
            def _():
                rider.finish(*parts)

        aliases.update(rider.aliases(n_in, n_out))
        args += rider.operands()
        in_specs += [_ANY] * len(rider.operands())
        out_shape += rider.out_shapes()
        out_specs += [_ANY] * len(rider.out_shapes())
        scratch += rider.sems
    res = pl.pallas_call(
        kernel_body, name=name, grid=(n,), in_specs=in_specs, out_specs=out_specs, out_shape=out_shape,
        scratch_shapes=scratch, input_output_aliases=aliases,
        compiler_params=_cparams(("arbitrary",)),
    )(*args)
    return res


def _pick(n, cands):
    for c in cands:
        if n % c == 0:
            return c
    return n


_MM_TILES = (1024, 1408, 512, 256, 128)


def _mm(name, a, b, mode, out_dtype, add=None, rider=None):
    if mode == "nn":
        (M, K), (K2, N) = a.shape, b.shape
    elif mode == "nt":
        (M, K), (N, K2) = a.shape, b.shape
    else:
        (K, M), (K2, N) = a.shape, b.shape
    assert K == K2, (name, a.shape, b.shape)
    tm = _pick(M, _MM_TILES)
    tn = _pick(N, _MM_TILES)
    if mode == "tn":
        tk = _pick(K, (1024, 512, 256))
    else:
        tk = K if K <= 2816 else _pick(K, _MM_TILES)
    nk = K // tk
    has_add = add is not None

    def body(*refs):
        if has_add:
            a_ref, b_ref, add_ref, o_ref, acc_ref = refs
        else:
            a_ref, b_ref, o_ref, acc_ref = refs
        k = pl.program_id(2)
        av = a_ref[...].astype(BF16)
        bv = b_ref[...].astype(BF16)
        if mode == "nn":
            p = jnp.dot(av, bv, preferred_element_type=F32)
        elif mode == "nt":
            p = lax.dot_general(av, bv, (((1,), (1,)), ((), ())), preferred_element_type=F32)
        else:
            p = lax.dot_general(av, bv, (((0,), (0,)), ((), ())), preferred_element_type=F32)

        def fin(v):
            if has_add:
                v = v + add_ref[...].astype(F32)
            o_ref[...] = v.astype(out_dtype)

        if nk == 1:
            fin(p)
        else:
            @pl.when(k == 0)
            def _():
                acc_ref[...] = p

            @pl.when(k > 0)
            def _():
                acc_ref[...] += p

            @pl.when(k == nk - 1)
            def _():
                fin(acc_ref[...])

    if mode == "nn":
        specs = [pl.BlockSpec((tm, tk), lambda i, j, k: (i, k)), pl.BlockSpec((tk, tn), lambda i, j, k: (k, j))]
    elif mode == "nt":
        specs = [pl.BlockSpec((tm, tk), lambda i, j, k: (i, k)), pl.BlockSpec((tn, tk), lambda i, j, k: (j, k))]
    else:
        specs = [pl.BlockSpec((tk, tm), lambda i, j, k: (k, i)), pl.BlockSpec((tk, tn), lambda i, j, k: (k, j))]
    args = [a, b]
    if has_add:
        specs.append(pl.BlockSpec((tm, tn), lambda i, j, k: (i, j)))
        args.append(add)
    acc_shape = (tm, tn) if nk > 1 else (8, 128)
    grid = (M // tm, N // tn, nk)
    out_spec = pl.BlockSpec((tm, tn), lambda i, j, k: (i, j))
    out_shape = jax.ShapeDtypeStruct((M, N), out_dtype)
    if rider is None:
        return pl.pallas_call(
            body, name=name, grid=grid, in_specs=specs, out_specs=out_spec, out_shape=out_shape,
            scratch_shapes=[pltpu.VMEM(acc_shape, F32)],
            compiler_params=_cparams(("parallel", "parallel", "arbitrary")),
        )(*args)
    n_in = len(args)

    def kernel_body(*refs):
        main, parts = rider.split(refs, n_in, 1, 1)
        i, j, k = pl.program_id(0), pl.program_id(1), pl.program_id(2)

        @pl.when((i == 0) & (j == 0) & (k == 0))
        def _():
            rider.start(*parts)

        body(*main)

        @pl.when((i == grid[0] - 1) & (j == grid[1] - 1) & (k == grid[2] - 1))
        def _():
            rider.finish(*parts)

    return pl.pallas_call(
        kernel_body, name=name, grid=grid, in_specs=specs + [_ANY] * len(rider.operands()),
        out_specs=[out_spec] + [_ANY] * len(rider.out_shapes()), out_shape=[out_shape] + rider.out_shapes(),
        input_output_aliases=rider.aliases(n_in, 1), scratch_shapes=[pltpu.VMEM(acc_shape, F32)] + rider.sems,
        compiler_params=_cparams(("arbitrary", "arbitrary", "arbitrary")),
    )(*args, *rider.operands())


def _sigmoid(x):
    return 1.0 / (1.0 + jnp.exp(-x))


def _silu(x):
    return x * _sigmoid(x)


def _dsilu(x):
    s = _sigmoid(x)
    return s * (1.0 + x * (1.0 - s))


def _softplus(x):
    return jnp.maximum(x, 0.0) + jnp.log(1.0 + jnp.exp(-jnp.abs(x)))


_GELU_C = math.sqrt(2.0 / math.pi)


def _gelu(x):
    return 0.5 * x * (1.0 + jnp.tanh(_GELU_C * (x + 0.044715 * x * x * x)))


def _dgelu(x):
    t = jnp.tanh(_GELU_C * (x + 0.044715 * x * x * x))
    return 0.5 * (1.0 + t) + 0.5 * x * (1.0 - t * t) * _GELU_C * (1.0 + 3.0 * 0.044715 * x * x)


def _acc(ref, i, val):
    @pl.when(i == 0)
    def _():
        ref[...] = val

    @pl.when(i > 0)
    def _():
        ref[...] += val


def _rows(shape):
    return lax.broadcasted_iota(jnp.int32, shape, 0)


def _shift_down(x, k, halo8, first):
    r = pltpu.roll(x, k, 0)
    h = pltpu.roll(halo8, k, 0)
    h = jnp.where(first, 0.0, h)
    head = jnp.where(_rows(h.shape) < k, h, r[:8])
    if x.shape[0] == 8:
        return head
    return jnp.concatenate([head, r[8:]], axis=0)


def _shift_up(x, k, halo8, last):
    T = x.shape[0]
    r = pltpu.roll(x, T - k, 0)
    h = pltpu.roll(halo8, 8 - k, 0)
    h = jnp.where(last, 0.0, h)
    tail = jnp.where(_rows(h.shape) >= 8 - k, h, r[T - 8:])
    return jnp.concatenate([r[:T - 8], tail], axis=0)


def _ln_fwd(name, a, b, g, beta, tile):
    S, Dm = a.shape
    n = S // tile

    def body(a_ref, b_ref, g_ref, be_ref, y_ref, xh_ref, rs_ref):
        z = ALPHA * a_ref[...] + b_ref[...].astype(F32)
        mu = jnp.mean(z, axis=-1, keepdims=True)
        zc = z - mu
        var = jnp.mean(zc * zc, axis=-1, keepdims=True)
        rstd = lax.rsqrt(var + EPS)
        xh = zc * rstd
        y_ref[...] = xh * g_ref[...] + be_ref[...]
        xh_ref[...] = xh
        rs_ref[...] = rstd

    return _pcall(name, body, n,
                  [("row", a, tile, Dm, 0), ("row", b, tile, Dm, 0), ("full", g), ("full", beta)],
                  [("row", S, Dm, F32, tile, Dm, 0), ("row", S, Dm, F32, tile, Dm, 0), ("row", S, 1, F32, tile, 1, 0)])


def _ln_bwd(name, dys, coefs, xh, rstd, g, tile):
    S, Dm = xh.shape
    n = S // tile
    nd = len(dys)

    def body(*refs):
        dy_refs = refs[:nd]
        xh_ref, rs_ref, g_ref, dz_ref, dg_ref, db_ref = refs[nd:]
        i = pl.program_id(0)
        dy = coefs[0] * dy_refs[0][...].astype(F32)
        for k in range(1, nd):
            dy = dy + coefs[k] * dy_refs[k][...].astype(F32)
        xh_v = xh_ref[...]
        dxh = dy * g_ref[...]
        m1 = jnp.mean(dxh, axis=-1, keepdims=True)
        m2 = jnp.mean(dxh * xh_v, axis=-1, keepdims=True)
        dz_ref[...] = rs_ref[...] * (dxh - m1 - xh_v * m2)
        _acc(dg_ref, i, jnp.sum(dy * xh_v, axis=0, keepdims=True))
        _acc(db_ref, i, jnp.sum(dy, axis=0, keepdims=True))

    ins = [("row", d, tile, Dm, 0) for d in dys]
    ins += [("row", xh, tile, Dm, 0), ("row", rstd, tile, 1, 0), ("full", g)]
    return _pcall(name, body, n, ins,
                  [("row", S, Dm, F32, tile, Dm, 0), ("full", (1, Dm), F32), ("full", (1, Dm), F32)])


def _loss_fwd_bwd(name, y, target, tile):
    S, Dm = y.shape
    n = S // tile

    def body(y_ref, t_ref, dy_ref, l_ref):
        i = pl.program_id(0)
        err = y_ref[...] - t_ref[...]
        dy_ref[...] = err * (1.0 / Dm)
        part = jnp.sum(jnp.sum(err * err, axis=-1, keepdims=True), axis=0, keepdims=True) * (0.5 / Dm)
        _acc(l_ref, i, part)

    return _pcall(name, body, n, [("row", y, tile, Dm, 0), ("row", target, tile, Dm, 0)],
                  [("row", S, Dm, F32, tile, Dm, 0), ("full", (1, 1), F32)])


def _swiglu_fwd(name, gu, tile):
    S = gu.shape[0]
    n = S // tile

    def body(g_ref, u_ref, o_ref):
        o_ref[...] = (_silu(g_ref[...].astype(F32)) * u_ref[...].astype(F32)).astype(BF16)

    return _pcall(name, body, n, [("row", gu, tile, D_FF, 0), ("row", gu, tile, D_FF, 1)],
                  [("row", S, D_FF, BF16, tile, D_FF, 0)])[0]


def _swiglu_bwd(name, gu, dh, tile):
    S = gu.shape[0]
    n = S // tile

    def body(g_ref, u_ref, dh_ref, o_ref):
        gv = g_ref[...].astype(F32)
        uv = u_ref[...].astype(F32)
        dv = dh_ref[...].astype(F32)
        dg = dv * uv * _dsilu(gv)
        du = dv * _silu(gv)
        o_ref[...] = jnp.concatenate([dg, du], axis=1).astype(BF16)

    return _pcall(name, body, n, [("row", gu, tile, D_FF, 0), ("row", gu, tile, D_FF, 1), ("row", dh, tile, D_FF, 0)],
                  [("row", S, 2 * D_FF, BF16, tile, 2 * D_FF, 0)])[0]


def _merge_fwd(name, proj, b_gate, pl_, ps_, px_, tile):
    S = proj.shape[0]
    n = S // tile
    Dm = D_MODEL

    def body(l0, l1, l2, bg, p0, p1, p2, o_ref):
        bgv = bg[...]
        acc = _sigmoid(l0[...].astype(F32) + bgv[0:1]) * p0[...].astype(F32)
        acc = acc + _sigmoid(l1[...].astype(F32) + bgv[1:2]) * p1[...].astype(F32)
        acc = acc + _sigmoid(l2[...].astype(F32) + bgv[2:3]) * p2[...].astype(F32)
        o_ref[...] = acc.astype(BF16)

    ins = [("row", proj, tile, *CB_G0), ("row", proj, tile, *CB_G1), ("row", proj, tile, *CB_G2), ("full", b_gate),
           ("row", pl_, tile, Dm, 0), ("row", ps_, tile, Dm, 0), ("row", px_, tile, Dm, 0)]
    return _pcall(name, body, n, ins, [("row", S, Dm, BF16, tile, Dm, 0)])[0]


def _merge_bwd(name, proj, b_gate, pl_, ps_, px_, dmerged, tile):
    S = proj.shape[0]
    n = S // tile
    Dm = D_MODEL

    def body(l0, l1, l2, bg, p0, p1, p2, dm_ref, dproj_ref, d0, d1, d2, db0, db1, db2):
        i = pl.program_id(0)
        bgv = bg[...]
        dm = dm_ref[...].astype(F32)
        dls = []
        for k, (lr, pr, dr, dbr) in enumerate(((l0, p0, d0, db0), (l1, p1, d1, db1), (l2, p2, d2, db2))):
            gk = _sigmoid(lr[...].astype(F32) + bgv[k:k + 1])
            dr[...] = (dm * gk).astype(BF16)
            dl = dm * pr[...].astype(F32) * gk * (1.0 - gk)
            _acc(dbr, i, jnp.sum(dl, axis=0, keepdims=True))
            dls.append(dl)
        dproj_ref[...] = jnp.concatenate(dls, axis=1).astype(BF16)

    ins = [("row", proj, tile, *CB_G0), ("row", proj, tile, *CB_G1), ("row", proj, tile, *CB_G2), ("full", b_gate),
           ("row", pl_, tile, Dm, 0), ("row", ps_, tile, Dm, 0), ("row", px_, tile, Dm, 0), ("row", dmerged, tile, Dm, 0)]
    outs = [("row", S, N_PROJ, BF16, tile, *CB_LOGITS)] + [("row", S, Dm, BF16, tile, Dm, 0)] * 3 + [("full", (1, Dm), F32)] * 3
    return _pcall(name, body, n, ins, outs)


def _conv_taps(xf, halo8, first, w):
    out = xf * w[3:4]
    for k in (1, 2, 3):
        out = out + _shift_down(xf, k, halo8, first) * w[3 - k:4 - k]
    return out


def _conv_fwd(name, src, cb, w, b, act, out_dtype, tile):
    S = src.shape[0]
    C = cb[0]
    n = S // tile

    def body(x_ref, p_ref, w_ref, b_ref, o_ref):
        i = pl.program_id(0)
        xf = x_ref[...].astype(F32)
        halo8 = p_ref[...].astype(F32)[HALO - 8:]
        pre = _conv_taps(xf, halo8, i == 0, w_ref[...]) + b_ref[...]
        o_ref[...] = (_silu(pre) if act else pre).astype(out_dtype)

    return _pcall(name, body, n, [("row", src, tile, *cb), ("prev", src, tile, *cb), ("full", w), ("full", b)],
                  [("row", S, C, out_dtype, tile, C, 0)])[0]


def _conv_bwd(name, src, cb, w, b, act, dout, dproj, tile):
    S = src.shape[0]
    C = cb[0]
    n = S // tile

    def body(x_ref, p_ref, xn_ref, w_ref, b_ref, d_ref, dn_ref, buf_ref, dx_ref, dw0, dw1, dw2, dw3, db_ref):
        i = pl.program_id(0)
        first, last = i == 0, i == n - 1
        xf = x_ref[...].astype(F32)
        halo8 = p_ref[...].astype(F32)[HALO - 8:]
        wv = w_ref[...]
        dv = d_ref[...].astype(F32)
        nx8 = dn_ref[...].astype(F32)[:8]
        if act:
            bv = b_ref[...]
            dv = dv * _dsilu(_conv_taps(xf, halo8, first, wv) + bv)
            xn8 = xn_ref[...].astype(F32)[:8]
            nx8 = nx8 * _dsilu(_conv_taps(xn8, xf[tile - 8:], False, wv) + bv)
        dx = dv * wv[3:4]
        for k in (1, 2, 3):
            dx = dx + _shift_up(dv, k, nx8, last) * wv[3 - k:4 - k]
        dx_ref[...] = dx.astype(BF16)
        _acc(dw3, i, jnp.sum(dv * xf, axis=0, keepdims=True))
        for k, dwr in ((1, dw2), (2, dw1), (3, dw0)):
            _acc(dwr, i, jnp.sum(dv * _shift_down(xf, k, halo8, first), axis=0, keepdims=True))
        _acc(db_ref, i, jnp.sum(dv, axis=0, keepdims=True))

    ins = [("row", src, tile, *cb), ("prev", src, tile, *cb), ("next", src, tile, *cb), ("full", w), ("full", b),
           ("row", dout, tile, C, 0), ("next", dout, tile, C, 0), ("any", dproj)]
    outs = [("row", S, N_PROJ, BF16, tile, *cb)] + [("full", (1, C), F32)] * 5
    return _pcall(name, body, n, ins, outs, aliases={7: 0})


def _lru_gates(xc, wa_ref, wi_ref, ba, bi, lam):
    xb = xc.astype(BF16)
    pa, pi_ = [], []
    for nb in range(LRU_BLOCKS):
        sl = slice(nb * LRU_BLOCK, (nb + 1) * LRU_BLOCK)
        pa.append(jnp.dot(xb[:, sl], wa_ref[nb], preferred_element_type=F32))
        pi_.append(jnp.dot(xb[:, sl], wi_ref[nb], preferred_element_type=F32))
    r = _sigmoid(jnp.concatenate(pa, axis=1) + ba)
    ig = _sigmoid(jnp.concatenate(pi_, axis=1) + bi)
    sp = _softplus(-lam)
    a = jnp.exp(-LRU_C * r * sp)
    m = jnp.sqrt(1.0 - a * a)
    return xb, r, ig, sp, a, m


def _lru_fwd(name, xc, proj, wa, wi, ba, bi, lam, tile):
    S = xc.shape[0]
    n = S // tile
    C = D_MODEL

    def body(xc_ref, gate_ref, wa_ref, wi_ref, ba_ref, bi_ref, lam_ref, y_ref, h_ref, carry):
        i = pl.program_id(0)

        @pl.when(i == 0)
        def _():
            carry[...] = jnp.zeros_like(carry)

        xcv = xc_ref[...]
        _, r, ig, sp, a, m = _lru_gates(xcv, wa_ref, wi_ref, ba_ref[...], bi_ref[...], lam_ref[...])
        u = m * (ig * xcv)
        rows = _rows(a.shape)
        d = 1
        while d < tile:
            keep = rows >= d
            a_s = jnp.where(keep, pltpu.roll(a, d, 0), 1.0)
            u_s = jnp.where(keep, pltpu.roll(u, d, 0), 0.0)
            u = a * u_s + u
            a = a * a_s
            d *= 2
        h = u + a * carry[0:1, :]
        h_ref[...] = h
        carry[0:1, :] = h_ref[pl.ds(tile - 1, 1), :]
        y_ref[...] = (_gelu(gate_ref[...].astype(F32)) * h).astype(BF16)

    ins = [("row", xc, tile, C, 0), ("row", proj, tile, *CB_LRU_GATE), ("full", wa), ("full", wi),
           ("full", ba), ("full", bi), ("full", lam)]
    return _pcall(name, body, n, ins, [("row", S, C, BF16, tile, C, 0), ("row", S, C, F32, tile, C, 0)],
                  scratch=[pltpu.VMEM((8, C), F32)])


def _lru_bwd(name, dy, xc, proj, h, wa, wi, ba, bi, lam, dproj, tile, rider=None):
    S = xc.shape[0]
    n = S // tile
    C = D_MODEL

    def body(dy_ref, xc_ref, gate_ref, h_ref, hp_ref, wa_ref, wi_ref, ba_ref, bi_ref, lam_ref, buf_ref,
             dg_ref, dxc_ref, dwa_ref, dwi_ref, dba_ref, dbi_ref, dlam_ref, carry):
        i = pl.program_id(0)
        first_tile = i == n - 1

        @pl.when(i == 0)
        def _():
            carry[...] = jnp.zeros_like(carry)

        xcv = xc_ref[...]
        lamv = lam_ref[...]
        xb, r, ig, sp, a, m = _lru_gates(xcv, wa_ref, wi_ref, ba_ref[...], bi_ref[...], lamv)
        hv = h_ref[...]
        gv = gate_ref[...].astype(F32)
        dyv = dy_ref[...].astype(F32)
        dg_ref[...] = (dyv * hv * _dgelu(gv)).astype(BF16)
        v = dyv * _gelu(gv)
        rows = _rows(a.shape)
        bcoef = jnp.where(rows == tile - 1, 1.0, pltpu.roll(a, tile - 1, 0))
        d = 1
        while d < tile:
            keep = rows < tile - d
            b_s = jnp.where(keep, pltpu.roll(bcoef, tile - d, 0), 1.0)
            v_s = jnp.where(keep, pltpu.roll(v, tile - d, 0), 0.0)
            v = v + bcoef * v_s
            bcoef = bcoef * b_s
            d *= 2
        dH = v + bcoef * carry[0:1, :]
        dxc_ref[...] = dH
        carry[0:1, :] = dxc_ref[pl.ds(0, 1), :] * a[0:1, :]
        halo8 = hp_ref[...][HALO - 8:]
        hprev = _shift_down(hv, 1, halo8, first_tile)
        da = dH * hprev
        ix = ig * xcv
        dm = dH * ix
        di = dH * m * xcv
        dxc = dH * m * ig
        da = da - dm * a / m
        dla = da * a
        dr = dla * (-LRU_C) * sp
        _acc(dlam_ref, i, jnp.sum(dla * (-LRU_C) * r, axis=0, keepdims=True) * (-_sigmoid(-lamv)))
        dpa = dr * r * (1.0 - r)
        dpi = di * ig * (1.0 - ig)
        _acc(dba_ref, i, jnp.sum(dpa, axis=0, keepdims=True))
        _acc(dbi_ref, i, jnp.sum(dpi, axis=0, keepdims=True))
        dpab, dpib = dpa.astype(BF16), dpi.astype(BF16)
        back = []
        for nb in range(LRU_BLOCKS):
            sl = slice(nb * LRU_BLOCK, (nb + 1) * LRU_BLOCK)
            back.append(lax.dot_general(dpab[:, sl], wa_ref[nb], (((1,), (1,)), ((), ())), preferred_element_type=F32)
                        + lax.dot_general(dpib[:, sl], wi_ref[nb], (((1,), (1,)), ((), ())), preferred_element_type=F32))
            ga = lax.dot_general(xb[:, sl], dpab[:, sl], (((0,), (0,)), ((), ())), preferred_element_type=F32)
            gi = lax.dot_general(xb[:, sl], dpib[:, sl], (((0,), (0,)), ((), ())), preferred_element_type=F32)

            @pl.when(i == 0)
            def _(ga=ga, gi=gi, nb=nb):
                dwa_ref[nb] = ga
                dwi_ref[nb] = gi

            @pl.when(i > 0)
            def _(ga=ga, gi=gi, nb=nb):
                dwa_ref[nb] += ga
                dwi_ref[nb] += gi

        dxc_ref[...] = dxc + jnp.concatenate(back, axis=1)

    ins = [("row", dy, tile, C, 0), ("row", xc, tile, C, 0), ("row", proj, tile, *CB_LRU_GATE), ("row", h, tile, C, 0),
           ("prev", h, tile, C, 0), ("full", wa), ("full", wi), ("full", ba), ("full", bi), ("full", lam), ("any", dproj)]
    outs = [("row", S, N_PROJ, BF16, tile, *CB_LRU_GATE), ("row", S, C, F32, tile, C, 0),
            ("full", (LRU_BLOCKS, LRU_BLOCK, LRU_BLOCK), F32), ("full", (LRU_BLOCKS, LRU_BLOCK, LRU_BLOCK), F32),
            ("full", (1, C), F32), ("full", (1, C), F32), ("full", (1, C), F32)]
    return _pcall(name, body, n, ins, outs, scratch=[pltpu.VMEM((8, C), F32)], reverse=True, aliases={10: 0}, rider=rider)


SSD_STEP = 4

def _split3(x):
    h = x.astype(BF16)
    r = x - h.astype(F32)
    m = r.astype(BF16)
    lo = (r - m.astype(F32)).astype(BF16)
    return h, m, lo


def _dot01_r(x, e):
    h = x.astype(BF16)
    m = (x - h.astype(F32)).astype(BF16)
    return jnp.dot(h, e, preferred_element_type=F32) + jnp.dot(m, e, preferred_element_type=F32)


def _dot01_l(e, x):
    h, m, lo = _split3(x)
    return (jnp.dot(e, h, preferred_element_type=F32) + jnp.dot(e, m, preferred_element_type=F32)
            + jnp.dot(e, lo, preferred_element_type=F32))


def _ssd_consts():
    hh = lax.broadcasted_iota(jnp.int32, (DT_PAD, D_SSD), 0)
    cc = lax.broadcasted_iota(jnp.int32, (DT_PAD, D_SSD), 1)
    e = (cc // SSD_HEAD_DIM == hh).astype(BF16)
    rows = SSD_STEP * CHUNK
    li = lax.broadcasted_iota(jnp.int32, (rows, rows), 0)
    si = lax.broadcasted_iota(jnp.int32, (rows, rows), 1)
    ltri = ((li >= si) & (li // CHUNK == si // CHUNK)).astype(BF16)
    l4 = lax.broadcasted_iota(jnp.int32, (CHUNK, 4 * CHUNK), 0)
    s4 = lax.broadcasted_iota(jnp.int32, (CHUNK, 4 * CHUNK), 1) % CHUNK
    itile = (l4 == s4).astype(F32)
    causal = (l4 >= s4).astype(F32)
    j4 = lax.broadcasted_iota(jnp.int32, (8, 4 * CHUNK), 0)
    c4 = lax.broadcasted_iota(jnp.int32, (8, 4 * CHUNK), 1) // CHUNK
    hmask = (j4 == c4).astype(F32)
    return e, e.T, ltri, ltri.T, itile, causal, hmask


def _ssd_chunk_common(xs_ref, bm_ref, cm_ref, dt_ref, dtb_ref, a_ref, e_ref, ltri_ref):
    xs = xs_ref[...].astype(F32)
    raw = dt_ref[...] + dtb_ref[...]
    dtv = _softplus(raw)
    da = dtv * a_ref[...]
    cs = _dot01_l(ltri_ref[...], da)
    e = e_ref[...]
    dte = _dot01_r(dtv, e)
    ce = _dot01_r(cs, e)
    xdt = xs * dte
    ecs = jnp.exp(ce)
    return xs, raw, dtv, cs, dte, ce, xdt, ecs


def _quad_terms(ce_q, cb4, itile, causal):
    cr = jnp.sum(ce_q * itile, axis=0, keepdims=True)
    seg = ce_q - cr
    dec = jnp.where(causal > 0.0, jnp.exp(jnp.minimum(seg, 0.0)), 0.0)
    return dec, cb4 * dec


def _block_diag4(xq, hmask):
    return jnp.concatenate([xq * hmask[j:j + 1].astype(xq.dtype) for j in range(4)], axis=0)


def _ssd_fwd(name, xbc, dt_raw, dt_bias, a_neg, d_exp, consts, rider=None):
    S = xbc.shape[0]
    nc = S // CHUNK
    e, et, ltri, ltri_t, itile, causal, hmask = consts

    def body(xs_ref, bm_ref, cm_ref, dt_ref, dtb_ref, a_ref, dex_ref, e_ref, ltri_ref, it_ref, ca_ref, hm_ref,
             y_ref, prev_ref, hst):
        i = pl.program_id(0)

        @pl.when(i == 0)
        def _():
            hst[...] = jnp.zeros_like(hst)

        xs, raw, dtv, cs, dte, ce_all, xdt_all, ecs_all = _ssd_chunk_common(
            xs_ref, bm_ref, cm_ref, dt_ref, dtb_ref, a_ref, e_ref, ltri_ref)
        itile_v, causal_v, hmask_v = it_ref[...], ca_ref[...], hm_ref[...]
        dskip_all = dex_ref[...] * xs
        for k in range(SSD_STEP):
            rs = slice(k * CHUNK, (k + 1) * CHUNK)
            ce, xdt, ecs, dskip = ce_all[rs], xdt_all[rs], ecs_all[rs], dskip_all[rs]
            cle = ce[CHUNK - 1:CHUNK, :]
            xdtb = xdt.astype(BF16)
            xst = (xdt * jnp.exp(cle - ce)).astype(BF16)
            ecl = jnp.exp(cle)
            bm = bm_ref[rs, :]
            cm = cm_ref[rs, :]
            for g in range(SSD_GROUPS):
                gs = slice(g * 512, (g + 1) * 512)
                ns = slice(g * SSD_STATE, (g + 1) * SSD_STATE)
                bm_g, cm_g = bm[:, ns], cm[:, ns]
                hprev = hst[g]
                hprev_b = hprev.astype(BF16)
                prev_ref[k, g] = hprev_b
                yoff = jnp.dot(cm_g, hprev_b, preferred_element_type=F32) * ecs[:, gs]
                st = lax.dot_general(bm_g, xst[:, gs], (((0,), (0,)), ((), ())), preferred_element_type=F32)
                hst[g] = hprev * ecl[:, gs] + st
                b4 = jnp.concatenate([bm_g] * 4, axis=0)
                cb4 = lax.dot_general(cm_g, b4, (((1,), (1,)), ((), ())), preferred_element_type=F32)
                for q in range(2):
                    cols = slice(g * 512 + q * 256, g * 512 + (q + 1) * 256)
                    _, mq = _quad_terms(ce[:, cols], cb4, itile_v, causal_v)
                    xbd = _block_diag4(xdtb[:, cols], hmask_v)
                    ydiag = jnp.dot(mq.astype(BF16), xbd, preferred_element_type=F32)
                    y_ref[rs, cols] = ydiag + yoff[:, q * 256:(q + 1) * 256] + dskip[:, cols]

    T = SSD_STEP * CHUNK
    ins = [("row", xbc, T, *CB_XS), ("row", xbc, T, *CB_BM), ("row", xbc, T, *CB_CM),
           ("row", dt_raw, T, DT_PAD, 0), ("full", dt_bias), ("full", a_neg), ("full", d_exp),
           ("full", e), ("full", ltri), ("full", itile), ("full", causal), ("full", hmask)]
    outs = [("row", S, D_SSD, F32, T, D_SSD, 0), ("lead", (nc, SSD_GROUPS, SSD_STATE, 512), BF16, SSD_STEP)]
    return _pcall(name, body, S // T, ins, outs, scratch=[pltpu.VMEM((SSD_GROUPS, SSD_STATE, 512), F32)], rider=rider)


def _ssd_bwd(name, xbc, dt_raw, dt_bias, a_neg, d_exp, prev, dy, consts, rider=None):
    S = xbc.shape[0]
    nc = S // CHUNK
    e, et, ltri, ltri_t, itile, causal, hmask = consts

    def body(xs_ref, bm_ref, cm_ref, dt_ref, dtb_ref, a_ref, dex_ref, prev_ref, dy_ref,
             e_ref, et_ref, ltri_ref, ltt_ref, it_ref, ca_ref, hm_ref,
             dx_ref, ddt_ref, da_ref, dbias_ref, dd_ref, dh, dce_ref, dxdt_ref):
        i = pl.program_id(0)

        @pl.when(i == 0)
        def _():
            dh[...] = jnp.zeros_like(dh)

        xs, raw, dtv, cs, dte, ce_all, xdt_all, ecs_all = _ssd_chunk_common(
            xs_ref, bm_ref, cm_ref, dt_ref, dtb_ref, a_ref, e_ref, ltri_ref)
        itile_v, causal_v, hmask_v = it_ref[...], ca_ref[...], hm_ref[...]
        dyv = dy_ref[...]
        last_row = _rows((CHUNK, 512)) == CHUNK - 1
        for k in reversed(range(SSD_STEP)):
            rs = slice(k * CHUNK, (k + 1) * CHUNK)
            ce, xdt, ecs = ce_all[rs], xdt_all[rs], ecs_all[rs]
            cle = ce[CHUNK - 1:CHUNK, :]
            dend = jnp.exp(cle - ce)
            xdtb = xdt.astype(BF16)
            xst = (xdt * dend).astype(BF16)
            ecl = jnp.exp(cle)
            bm = bm_ref[rs, :]
            cm = cm_ref[rs, :]
            for g in range(SSD_GROUPS):
                gs = slice(g * 512, (g + 1) * 512)
                ns = slice(g * SSD_STATE, (g + 1) * SSD_STATE)
                bm_g, cm_g = bm[:, ns], cm[:, ns]
                hprev_b = prev_ref[k, g]
                dhn = dh[g]
                dhn_b = dhn.astype(BF16)
                dy_g = dyv[rs, gs]
                ecs_g, dend_g, xdt_g, ecl_g = ecs[:, gs], dend[:, gs], xdt[:, gs], ecl[:, gs]
                z = jnp.dot(cm_g, hprev_b, preferred_element_type=F32)
                dz = dy_g * ecs_g
                dzb = dz.astype(BF16)
                dce_g = dz * z
                dcm_g = lax.dot_general(dzb, hprev_b, (((1,), (1,)), ((), ())), preferred_element_type=F32)
                dprev = lax.dot_general(cm_g, dzb, (((0,), (0,)), ((), ())), preferred_element_type=F32) + dhn * ecl_g
                dcl = jnp.sum(dhn * hprev_b.astype(F32), axis=0, keepdims=True) * ecl_g
                gmat = jnp.dot(bm_g, dhn_b, preferred_element_type=F32)
                dbm_g = lax.dot_general(xst[:, gs], dhn_b, (((1,), (1,)), ((), ())), preferred_element_type=F32)
                dxdt_g = gmat * dend_g
                t = gmat * xdt_g * dend_g
                dce_g = dce_g - t
                dcl = dcl + jnp.sum(t, axis=0, keepdims=True)
                dce_g = dce_g + jnp.where(last_row, dcl, 0.0)
                dh[g] = dprev
                b4 = jnp.concatenate([bm_g] * 4, axis=0)
                cb4 = lax.dot_general(cm_g, b4, (((1,), (1,)), ((), ())), preferred_element_type=F32)
                for q in range(2):
                    qs = slice(q * 256, (q + 1) * 256)
                    cols = slice(g * 512 + q * 256, g * 512 + (q + 1) * 256)
                    dec, mq = _quad_terms(ce[:, cols], cb4, itile_v, causal_v)
                    mqb = mq.astype(BF16)
                    xbd = _block_diag4(xdtb[:, cols], hmask_v)
                    dyq = dy_g[:, qs].astype(BF16)
                    dm = lax.dot_general(dyq, xbd, (((1,), (1,)), ((), ())), preferred_element_type=F32)
                    rmat = lax.dot_general(mqb, dyq, (((0,), (0,)), ((), ())), preferred_element_type=F32)
                    dxq = rmat[0:64] * hmask_v[0:1]
                    for j in range(1, 4):
                        dxq = dxq + rmat[64 * j:64 * (j + 1)] * hmask_v[j:j + 1]
                    tq = dm * dec
                    tqb = tq.astype(BF16)
                    dcm_g = dcm_g + jnp.dot(tqb, b4, preferred_element_type=F32)
                    rb = lax.dot_general(tqb, cm_g, (((0,), (0,)), ((), ())), preferred_element_type=F32)
                    dbm_g = dbm_g + rb[0:64] + rb[64:128] + rb[128:192] + rb[192:256]
                    dseg = tq * cb4
                    colsum = jnp.sum(dseg, axis=0, keepdims=True)
                    dce_ref[rs, cols] = dce_g[:, qs] + dseg - itile_v * colsum
                    dxdt_ref[rs, cols] = dxdt_g[:, qs] + dxq
                dx_ref[rs, D_SSD + g * SSD_STATE:D_SSD + (g + 1) * SSD_STATE] = dbm_g
                dx_ref[rs, D_SSD + D_BC + g * SSD_STATE:D_SSD + D_BC + (g + 1) * SSD_STATE] = dcm_g
        dxdt = dxdt_ref[...]
        dexv = dex_ref[...]
        dx_ref[:, 0:D_SSD] = dxdt * dte + dyv * dexv
        _acc(dd_ref, i, jnp.sum(dyv * xs, axis=0, keepdims=True))
        etv = et_ref[...]
        dcs = _dot01_r(dce_ref[...], etv)
        dda = _dot01_l(ltt_ref[...], dcs)
        av = a_ref[...]
        ddtv = dda * av + _dot01_r(dxdt * xs, etv)
        _acc(da_ref, i, jnp.sum(dda * dtv, axis=0, keepdims=True))
        draw = ddtv * _sigmoid(raw)
        ddt_ref[...] = draw.astype(BF16)
        _acc(dbias_ref, i, jnp.sum(draw, axis=0, keepdims=True))

    T = SSD_STEP * CHUNK
    ins = [("row", xbc, T, *CB_XS), ("row", xbc, T, *CB_BM), ("row", xbc, T, *CB_CM),
           ("row", dt_raw, T, DT_PAD, 0), ("full", dt_bias), ("full", a_neg), ("full", d_exp),
           ("lead", prev, SSD_STEP), ("row", dy, T, D_SSD, 0),
           ("full", e), ("full", et), ("full", ltri), ("full", ltri_t), ("full", itile), ("full", causal), ("full", hmask)]
    outs = [("row", S, D_XBC, F32, T, D_XBC, 0), ("row", S, DT_PAD, BF16, T, DT_PAD, 0),
            ("full", (1, DT_PAD), F32), ("full", (1, DT_PAD), F32), ("full", (1, D_SSD), F32)]
    scratch = [pltpu.VMEM((SSD_GROUPS, SSD_STATE, 512), F32), pltpu.VMEM((T, D_SSD), F32), pltpu.VMEM((T, D_SSD), F32)]
    return _pcall(name, body, S // T, ins, outs, scratch=scratch, reverse=True, rider=rider)


def _gate_norm_fwd(name, ycore, proj, norm_w, tile):
    S = ycore.shape[0]
    n = S // tile

    def body(y_ref, z_ref, w_ref, o_ref):
        y2 = y_ref[...] * _silu(z_ref[...].astype(F32))
        wv = w_ref[...]
        for g in range(SSD_GROUPS):
            gs = slice(g * 512, (g + 1) * 512)
            seg = y2[:, gs]
            r = lax.rsqrt(jnp.mean(seg * seg, axis=-1, keepdims=True) + EPS)
            o_ref[:, gs] = (seg * r * wv[:, gs]).astype(BF16)

    return _pcall(name, body, n, [("row", ycore, tile, D_SSD, 0), ("row", proj, tile, *CB_Z), ("full", norm_w)],
                  [("row", S, D_SSD, BF16, tile, D_SSD, 0)])[0]


def _gate_norm_bwd(name, dout, ycore, proj, norm_w, dproj, tile):
    S = ycore.shape[0]
    n = S // tile

    def body(do_ref, y_ref, z_ref, w_ref, buf_ref, dz_ref, dy_ref, dw_ref):
        i = pl.program_id(0)
        yv = y_ref[...]
        zv = z_ref[...].astype(F32)
        sz = _silu(zv)
        y2 = yv * sz
        dov = do_ref[...].astype(F32)
        wv = w_ref[...]
        dws, dy2s = [], []
        for g in range(SSD_GROUPS):
            gs = slice(g * 512, (g + 1) * 512)
            seg = y2[:, gs]
            r = lax.rsqrt(jnp.mean(seg * seg, axis=-1, keepdims=True) + EPS)
            yn = seg * r
            dws.append(jnp.sum(dov[:, gs] * yn, axis=0, keepdims=True))
            dyn = dov[:, gs] * wv[:, gs]
            dy2s.append(r * (dyn - yn * jnp.mean(dyn * yn, axis=-1, keepdims=True)))
        dy2 = jnp.concatenate(dy2s, axis=1)
        dy_ref[...] = dy2 * sz
        dz_ref[...] = (dy2 * yv * _dsilu(zv)).astype(BF16)
        _acc(dw_ref, i, jnp.concatenate(dws, axis=1))

    ins = [("row", dout, tile, D_SSD, 0), ("row", ycore, tile, D_SSD, 0), ("row", proj, tile, *CB_Z), ("full", norm_w),
           ("any", dproj)]
    outs = [("row", S, N_PROJ, BF16, tile, *CB_Z), ("row", S, D_SSD, F32, tile, D_SSD, 0), ("full", (1, D_SSD), F32)]
    return _pcall(name, body, n, ins, outs, aliases={4: 0})


_XA_SCALE = XA_HEAD_DIM ** -0.5
_NT = (((1,), (1,)), ((), ()))
_TN = (((0,), (0,)), ((), ()))


def _xa_probs(qh, kh):
    s = lax.dot_general(qh, kh, _NT, preferred_element_type=F32) * _XA_SCALE
    s = s - jnp.max(s, axis=-1, keepdims=True)
    p = jnp.exp(s)
    return p / jnp.sum(p, axis=-1, keepdims=True)


def _xa_fwd(name, proj, kv, tile):
    S = proj.shape[0]
    n = S // tile
    Dh = XA_HEAD_DIM

    def body(q_ref, kv_ref, o_ref):
        for hd in range(XA_HEADS):
            qh = q_ref[:, hd * Dh:(hd + 1) * Dh]
            kh = kv_ref[:, hd * Dh:(hd + 1) * Dh]
            vh = kv_ref[:, D_MODEL + hd * Dh:D_MODEL + (hd + 1) * Dh]
            p = _xa_probs(qh, kh)
            o_ref[:, hd * Dh:(hd + 1) * Dh] = jnp.dot(p.astype(BF16), vh, preferred_element_type=F32).astype(BF16)

    return _pcall(name, body, n, [("row", proj, tile, *CB_XA_Q), ("full", kv)],
                  [("row", S, D_MODEL, BF16, tile, D_MODEL, 0)])[0]


def _xa_bwd(name, proj, kv, dout, dproj, tile):
    S = proj.shape[0]
    n = S // tile
    Dh = XA_HEAD_DIM

    def body(q_ref, kv_ref, do_ref, buf_ref, dq_ref, dkv_ref):
        i = pl.program_id(0)
        for hd in range(XA_HEADS):
            ks_ = slice(hd * Dh, (hd + 1) * Dh)
            vs_ = slice(D_MODEL + hd * Dh, D_MODEL + (hd + 1) * Dh)
            qh = q_ref[:, ks_]
            kh = kv_ref[:, ks_]
            vh = kv_ref[:, vs_]
            doh = do_ref[:, ks_].astype(BF16)
            p = _xa_probs(qh, kh)
            pb = p.astype(BF16)
            dp = lax.dot_general(doh, vh, _NT, preferred_element_type=F32)
            dv = lax.dot_general(pb, doh, _TN, preferred_element_type=F32)
            ds = (p * (dp - jnp.sum(dp * p, axis=-1, keepdims=True)) * _XA_SCALE).astype(BF16)
            dq_ref[:, ks_] = jnp.dot(ds, kh, preferred_element_type=F32).astype(BF16)
            dk = lax.dot_general(ds, qh, _TN, preferred_element_type=F32)

            @pl.when(i == 0)
            def _(dk=dk, dv=dv, ks_=ks_, vs_=vs_):
                dkv_ref[:, ks_] = dk
                dkv_ref[:, vs_] = dv

            @pl.when(i > 0)
            def _(dk=dk, dv=dv, ks_=ks_, vs_=vs_):
                dkv_ref[:, ks_] += dk
                dkv_ref[:, vs_] += dv

    ins = [("row", proj, tile, *CB_XA_Q), ("full", kv), ("row", dout, tile, D_MODEL, 0), ("any", dproj)]
    outs = [("row", S, N_PROJ, BF16, tile, *CB_XA_Q), ("full", (N_MEM, 2 * D_MODEL), F32)]
    return _pcall(name, body, n, ins, outs, aliases={3: 0})


def _adamw(name, w, g, m, v):
    L, R, C = w.shape
    tile = _pick(R, [t for t in (256, 128, 64, 32, 16, 8) if t * C <= 128 * 2048])
    bc1 = 1.0 - ADAM_B1 ** ADAM_STEP
    bc2 = 1.0 - ADAM_B2 ** ADAM_STEP

    def body(w_ref, g_ref, m_ref, v_ref, d_ref, nm_ref, nv_ref):
        gv = g_ref[...]
        mn = ADAM_B1 * m_ref[...] + (1.0 - ADAM_B1) * gv
        vn = ADAM_B2 * v_ref[...] + (1.0 - ADAM_B2) * (gv * gv)
        nm_ref[...] = mn
        nv_ref[...] = vn
        d_ref[...] = -ADAM_LR * ((mn / bc1) / (jnp.sqrt(vn / bc2) + ADAM_EPS) + ADAM_WD * w_ref[...])

    spec = pl.BlockSpec((1, tile, C), lambda l, i: (l, i, 0))
    return pl.pallas_call(
        body, name=name, grid=(L, R // tile), in_specs=[spec] * 4, out_specs=[spec] * 3,
        out_shape=[jax.ShapeDtypeStruct((L, R, C), F32)] * 3,
        compiler_params=_cparams(("arbitrary", "arbitrary")),
    )(w, g, m, v)


def _axpy(name, coef, a, b, tile):
    S, C = a.shape
    n = S // tile

    def body(a_ref, b_ref, o_ref):
        o_ref[...] = coef * a_ref[...].astype(F32) + b_ref[...].astype(F32)

    return _pcall(name, body, n, [("row", a, tile, C, 0), ("row", b, tile, C, 0)], [("row", S, C, F32, tile, C, 0)])[0]


def _sum_terms(name, terms, out_dtype):
    R, C = terms[0].shape
    tile = _pick(R, (512, 256, 128, 64, 32, 16, 8))
    n = R // tile
    nt = len(terms)

    def body(*refs):
        vals = [r[...].astype(F32) for r in refs[:nt]]
        while len(vals) > 1:
            vals = [vals[k] + vals[k + 1] for k in range(0, len(vals), 2)]
        refs[nt][...] = vals[0].astype(out_dtype)

    return _pcall(name, body, n, [("row", t, tile, C, 0) for t in terms], [("row", R, C, out_dtype, tile, C, 0)])[0]


def _me():
    return lax.axis_index("x"), lax.axis_index("y"), lax.axis_index("c")


def _other_chips(x, y):
    return [(1 - x, y), (x, 1 - y), (1 - x, 1 - y)]


_ANY = pl.BlockSpec(memory_space=pl.ANY)


def _swap_sibling(name, src):
    def body(src_ref, out_ref, send_sem, recv_sem):
        x, y, c = _me()
        cp = pltpu.make_async_remote_copy(src_ref=src_ref, dst_ref=out_ref, send_sem=send_sem, recv_sem=recv_sem,
                                          device_id=(x, y, 1 - c), device_id_type=MESH)
        cp.start()
        cp.wait()

    return pl.pallas_call(
        body, name=name, in_specs=[_ANY], out_specs=_ANY, out_shape=jax.ShapeDtypeStruct(src.shape, src.dtype),
        scratch_shapes=[pltpu.SemaphoreType.DMA, pltpu.SemaphoreType.DMA],
    )(src)


def _send_chips(name, src, per_chip):
    shape = src.shape[1:] if per_chip else src.shape

    def body(src_ref, out_ref, send_sems, recv_sems):
        x, y, c = _me()
        cps = []
        for k, (cx, cy) in enumerate(_other_chips(x, y)):
            s = src_ref.at[2 * cx + cy] if per_chip else src_ref
            cps.append(pltpu.make_async_remote_copy(src_ref=s, dst_ref=out_ref.at[k], send_sem=send_sems.at[k],
                                                    recv_sem=recv_sems.at[k], device_id=(cx, cy, c), device_id_type=MESH))
        for cp in cps:
            cp.start()
        for cp in cps:
            cp.wait()

    return pl.pallas_call(
        body, name=name, in_specs=[_ANY], out_specs=_ANY, out_shape=jax.ShapeDtypeStruct((3,) + shape, src.dtype),
        scratch_shapes=[pltpu.SemaphoreType.DMA((3,)), pltpu.SemaphoreType.DMA((3,))],
    )(src)


W_IN_SHARD = N_IN // 4
W_IN_INNER = W_IN_SHARD - 8


def _win_rows(chip_x, chip_y):
    start = (2 * chip_x + chip_y) * W_IN_SHARD + 8 * chip_y
    return pl.ds(pl.multiple_of(start, 2 * 8), W_IN_INNER)


def _dma_sems(*shape):
    return [pltpu.SemaphoreType.DMA(shape), pltpu.SemaphoreType.DMA(shape)]


def _gather_rider(srcs, kinds, bufs, part):
    n = len(srcs)

    def copy(refs, i, k, cx, cy, to, own=False):
        src_refs, out_refs, _, (send_sems, recv_sems) = refs
        d = out_refs[i].at[2 * cx + cy] if kinds[i] == "blk" else out_refs[i].at[_win_rows(cx, cy)]
        return pltpu.make_async_remote_copy(src_ref=src_refs[i].at[part] if own else d, dst_ref=d,
                                            send_sem=send_sems.at[i, k], recv_sem=recv_sems.at[i, k],
                                            device_id=to, device_id_type=MESH)

    def start(*refs):
        x, y, c = _me()

        @pl.when(c == part)
        def _():
            for k, (cx, cy) in enumerate(_other_chips(x, y)):
                for i in range(n):
                    copy(refs, i, k, x, y, (cx, cy, part), own=True).start()

    def finish(*refs):
        x, y, c = _me()
        chips = _other_chips(x, y)

        @pl.when(c == part)
        def _():
            for k, (cx, cy) in enumerate(chips):
                for i in range(n):
                    copy(refs, i, k, cx, cy, (x, y, part)).wait_recv()
                    copy(refs, i, 3 + k, cx, cy, (x, y, 1 - part)).start()
            for k, (cx, cy) in enumerate(chips):
                for i in range(n):
                    copy(refs, i, k, x, y, (cx, cy, part), own=True).wait_send()
                    copy(refs, i, 3 + k, cx, cy, (x, y, 1 - part)).wait_send()

        @pl.when(c != part)
        def _():
            for k, (cx, cy) in enumerate(chips):
                for i in range(n):
                    copy(refs, i, 3 + k, cx, cy, (x, y, 1 - part)).wait_recv()

    return _Rider(srcs, bufs, [], _dma_sems(n, 6), start, finish)


def _pair_swap_rider(gs, layer):
    n = len(gs)

    def copy(refs, i):
        in_refs, _, out_refs, (send_sems, recv_sems) = refs
        x, y, _c = _me()
        return pltpu.make_async_remote_copy(src_ref=in_refs[i], dst_ref=out_refs[i], send_sem=send_sems.at[i],
                                            recv_sem=recv_sems.at[i], device_id=(x, y, layer), device_id_type=MESH)

    def start(*refs):
        @pl.when(_me()[2] != layer)
        def _():
            for i in range(n):
                copy(refs, i).start()

    def finish(*refs):
        @pl.when(_me()[2] != layer)
        def _():
            for i in range(n):
                copy(refs, i).wait_send()

        @pl.when(_me()[2] == layer)
        def _():
            for i in range(n):
                copy(refs, i).wait_recv()

    return _Rider(gs, [], [jax.ShapeDtypeStruct(g.shape, g.dtype) for g in gs], _dma_sems(n), start, finish)


def _chip_send_rider(pairs, layer):
    n = len(pairs)

    def copies(refs):
        in_refs, _, out_refs, (send_sems, recv_sems) = refs
        x, y, _c = _me()
        return [pltpu.make_async_remote_copy(src_ref=in_refs[i].at[2 * cx + cy], dst_ref=out_refs[i].at[k],
                                             send_sem=send_sems.at[i, k], recv_sem=recv_sems.at[i, k],
                                             device_id=(cx, cy, layer), device_id_type=MESH)
                for k, (cx, cy) in enumerate(_other_chips(x, y)) for i in range(n)]

    def start(*refs):
        @pl.when(_me()[2] == layer)
        def _():
            for cp in copies(refs):
                cp.start()

    def finish(*refs):
        @pl.when(_me()[2] == layer)
        def _():
            for cp in copies(refs):
                cp.wait()

    outs = [jax.ShapeDtypeStruct((3,) + p.shape[1:], p.dtype) for p in pairs]
    return _Rider(pairs, [], outs, _dma_sems(n, 3), start, finish)


def _join_parts_multi(name, bufs):
    n = len(bufs)

    def body(*refs):
        out_refs = refs[n:2 * n]
        send_sems, recv_sems = refs[2 * n:]
        x, y, c = _me()

        def copy(i, part):
            return pltpu.make_async_remote_copy(src_ref=out_refs[i].at[part], dst_ref=out_refs[i].at[part],
                                                send_sem=send_sems.at[i], recv_sem=recv_sems.at[i],
                                                device_id=(x, y, 1 - c), device_id_type=MESH)

        for i in range(n):
            copy(i, c).start()
        for i in range(n):
            copy(i, c).wait_send()
            copy(i, 1 - c).wait_recv()

    return pl.pallas_call(
        body, name=name, in_specs=[_ANY] * n, out_specs=[_ANY] * n,
        out_shape=[jax.ShapeDtypeStruct(b.shape, b.dtype) for b in bufs],
        input_output_aliases={i: i for i in range(n)},
        scratch_shapes=[pltpu.SemaphoreType.DMA((n,)), pltpu.SemaphoreType.DMA((n,))],
    )(*bufs)


def _col_tiles(R, C):
    tr = _pick(R, (512, 256, 128))
    if tr != R:
        return tr, C
    if R * C <= 512 * 1024:
        return R, C
    return R, _pick(C, (256, 128))


def _pair_sum(name, g, theirs):
    n4, R, C = g.shape
    tr, tc = _col_tiles(R, C)

    def body(g_ref, t_ref, o_ref, ob_ref):
        s = g_ref[...] + t_ref[...]
        o_ref[...] = s
        ob_ref[...] = s.astype(BF16)

    spec = pl.BlockSpec((1, tr, tc), lambda d, i, k: (d, i, k))
    return pl.pallas_call(
        body, name=name, grid=(n4, R // tr, C // tc), in_specs=[spec, spec], out_specs=[spec, spec],
        out_shape=[jax.ShapeDtypeStruct((n4, R, C), F32), jax.ShapeDtypeStruct((n4, R, C), BF16)],
        compiler_params=_cparams(("arbitrary", "arbitrary", "arbitrary")),
    )(g, theirs)


def _chip_sum(name, pair, got, j, layer, buf=None):
    _, R, C = pair.shape
    tr, tc = _col_tiles(R, C)

    def body(j_ref, p_ref, g0, g1, g2, *rest):
        rest[-1][0] = (p_ref[0] + g2[0].astype(F32)) + (g0[0].astype(F32) + g1[0].astype(F32))

    def gspec(k):
        return pl.BlockSpec((1, tr, tc), lambda i, q, j_ref, k=k: (k, i, q))

    in_specs = [pl.BlockSpec((1, tr, tc), lambda i, q, j_ref: (j_ref[0], i, q)), gspec(0), gspec(1), gspec(2)]
    args = [j.reshape(1).astype(jnp.int32), pair, got, got, got]
    aliases = {}
    if buf is not None:
        in_specs.append(_ANY)
        args.append(buf)
        aliases = {5: 0}
    return pl.pallas_call(
        body, name=name,
        grid_spec=pltpu.PrefetchScalarGridSpec(
            num_scalar_prefetch=1, grid=(R // tr, C // tc), in_specs=in_specs,
            out_specs=pl.BlockSpec((1, tr, tc), lambda i, q, j_ref: (layer, i, q))),
        out_shape=jax.ShapeDtypeStruct((2, R, C), F32), input_output_aliases=aliases,
        compiler_params=_cparams(("arbitrary", "arbitrary")),
    )(*args)


LANES = 1024
_BIG = (("w_in", "col"), ("ffn_w_in", "col"), ("mem_w_kv", "col"),
        ("w_br_lru", "row"), ("w_br_ssd", "row"), ("w_br_xa", "row"), ("w_out", "row"), ("ffn_w_down", "row"))
_SMALL_SHARDED = ("b_gate", "lru_conv_w", "ssd_conv_w")
_SMALL = ("b_gate", "lru_conv_w", "lru_conv_b", "lru_w_a", "lru_b_a", "lru_w_i", "lru_b_i", "lru_lambda",
          "ssd_conv_w", "ssd_conv_b", "ssd_dt_bias", "ssd_a_log", "ssd_d", "ssd_norm_w",
          "ln1_g", "ln1_b", "ln2_g", "ln2_b")
_W_NAMES = ("w_in", "b_gate", "lru_conv_w", "lru_conv_b", "lru_w_a", "lru_b_a", "lru_w_i", "lru_b_i", "lru_lambda",
            "ssd_conv_w", "ssd_conv_b", "ssd_dt_bias", "ssd_a_log", "ssd_d", "ssd_norm_w", "mem_w_kv", "w_br_lru",
            "w_br_ssd", "w_br_xa", "w_out", "ln1_g", "ln1_b", "ffn_w_in", "ffn_w_down", "ln2_g", "ln2_b")
_IN_ORDER = ((4096, 7168), (8224, 11296), (2048, 4096), (0, 1024), (1024, 2048), (7200, 8224))
_IN_DT = (7168, 7200)


def _flat_rows(parts, row_multiple):
    flat = jnp.concatenate([p.reshape(-1) for p in parts])
    rows = -(-flat.size // LANES)
    rows = -(-rows // row_multiple) * row_multiple
    return jnp.pad(flat, (0, rows * LANES - flat.size)).reshape(rows, LANES)


def _take_parts(flat, shapes):
    out, off = [], 0
    for shp in shapes:
        size = math.prod(shp)
        out.append(flat[off:off + size].reshape(shp))
        off += size
    return out


_GATHER_NAMES = ("w_in", "w_in_edge") + tuple(nm for nm, _ in _BIG[1:]) + _SMALL_SHARDED
_GATHER_KINDS = ("rows",) + ("blk",) * (len(_GATHER_NAMES) - 1)


def _gather_sources(w):
    x, y, _ = _me()
    j = 2 * x + y
    wt = jnp.swapaxes(w["w_in"], 1, 2)
    inner = lax.dynamic_slice_in_dim(wt, 8 * y, W_IN_INNER, axis=1).astype(BF16)
    edge = lax.dynamic_slice_in_dim(wt, (1 - y) * W_IN_INNER, 8, axis=1)
    srcs = [inner, edge] + [w[nm].astype(BF16) for nm, _ in _BIG[1:]] + [w[nm] for nm in _SMALL_SHARDED]
    bufs = []
    for l in range(DEPTH):
        row = []
        for s, kd in zip(srcs, _GATHER_KINDS):
            if kd == "blk":
                row.append(lax.dynamic_update_slice_in_dim(lax.empty((4,) + s.shape[1:], s.dtype), s[l][None], j, axis=0))
            else:
                row.append(lax.dynamic_update_slice_in_dim(lax.empty((N_IN, D_MODEL), s.dtype), s[l],
                                                           j * W_IN_SHARD + 8 * y, axis=0))
        bufs.append(row)
    return srcs, bufs


def _layer_weights(got):
    g = dict(zip(_GATHER_NAMES, got))
    wt_all, edges = g["w_in"], g["w_in_edge"]
    for a, b in ((0, 1), (2, 3)):
        tile = jnp.concatenate([edges[a], edges[b]], axis=0).astype(BF16)
        wt_all = lax.dynamic_update_slice_in_dim(wt_all, tile, b * W_IN_SHARD - 8, axis=0)
    full = {"w_re_t": jnp.concatenate([wt_all[lo:hi] for lo, hi in _IN_ORDER], axis=0),
            "w_dt_t": jnp.pad(wt_all[_IN_DT[0]:_IN_DT[1]], ((0, DT_PAD - SSD_HEADS), (0, 0)))}
    kinds = dict(_BIG)
    for nm in _GATHER_NAMES[2:]:
        _, r, cdim = g[nm].shape
        if kinds.get(nm, "col") == "col":
            full[nm] = jnp.moveaxis(g[nm], 0, 1).reshape(r, 4 * cdim)
        else:
            full[nm] = g[nm].reshape(4 * r, cdim)
    return full


def _per_chip(g, nm):
    r, cdim = g.shape
    if nm == "w_in" or dict(_BIG)[nm] == "row":
        return g.reshape(4, r // 4, cdim)
    return jnp.moveaxis(g.reshape(r, 4, cdim // 4), 1, 0)


def _reduce_small_grads(parts):
    shapes = [p.shape for p in parts]
    buf = _flat_rows(parts, 8)
    sib = _swap_sibling("small_pair_swap", buf)
    pair = _sum_terms("small_pair_sum", [buf, sib], F32)
    got = _send_chips("small_chip_send", pair, False)
    total = _sum_terms("small_chip_sum", [pair, got[2], got[0], got[1]], F32)
    return _take_parts(total.reshape(-1), shapes)


def _layer_fwd(xin, mem2, W, P, consts, riders=None):
    S = xin.shape[0]
    T, TX = min(256, S), min(512, S)
    sv = {"xin": xin}
    riders = riders or {}
    rout = {}
    proj = _mm("proj", xin, W["w_re_t"], "nt", BF16, rider=riders.get("proj"))
    if "proj" in riders:
        proj, rout["proj"] = proj[0], proj[1:]
    dt_raw = _mm("proj_dt", xin, W["w_dt_t"], "nt", F32)
    xc = _conv_fwd("lru_conv", proj, CB_LRU_X, P["lru_conv_w"], P["lru_conv_b"], False, F32, T)
    y_lru, h = _lru_fwd("lru_scan", xc, proj, P["lru_w_a"], P["lru_w_i"], P["lru_b_a"], P["lru_b_i"], P["lru_lambda"], T)
    xact = _conv_fwd("ssd_conv", proj, CB_XBC, P["ssd_conv_w"], P["ssd_conv_b"], True, BF16, T)
    res = _ssd_fwd("ssd_scan", xact, dt_raw, P["dt_bias"], P["a_neg"], P["d_exp"], consts, rider=riders.get("ssd_scan"))
    ycore, prev = res[0], res[1]
    if "ssd_scan" in riders:
        rout["ssd_scan"] = res[2:]
    y_ssd = _gate_norm_fwd("ssd_norm", ycore, proj, P["ssd_norm_w"], T)
    kv = _mm("mem_kv", mem2, W["w_kv"], "nn", BF16)
    y_xa = _xa_fwd("xattn", proj, kv, TX)
    p_l = _mm("br_lru", y_lru, W["w_l"], "nn", BF16)
    p_s = _mm("br_ssd", y_ssd, W["w_s"], "nn", BF16)
    p_x = _mm("br_xa", y_xa, W["w_x"], "nn", BF16)
    merged = _merge_fwd("merge", proj, P["b_gate"], p_l, p_s, p_x, T)
    mix = _mm("mix_out", merged, W["w_o"], "nn", F32)
    x1, xh1, rs1 = _ln_fwd("ln_fwd", xin, mix, P["ln1_g"], P["ln1_b"], T)
    gu = _mm("ffn_in", x1, W["w_fi"], "nn", BF16)
    hmid = _swiglu_fwd("swiglu", gu, T)
    f = _mm("ffn_down", hmid, W["w_fd"], "nn", F32)
    x2, xh2, rs2 = _ln_fwd("ln_fwd", x1, f, P["ln2_g"], P["ln2_b"], T)
    sv.update(proj=proj, dt_raw=dt_raw, xc=xc, h=h, y_lru=y_lru, xact=xact, ycore=ycore, prev=prev, y_ssd=y_ssd, kv=kv,
              y_xa=y_xa, p_l=p_l, p_s=p_s, p_x=p_x, merged=merged, x1=x1, xh1=xh1, rs1=rs1, gu=gu, hmid=hmid,
              xh2=xh2, rs2=rs2)
    return x2, sv, rout


def _layer_bwd(dys, coefs, sv, mem2, W, P, consts, riders=None):
    S = sv["xin"].shape[0]
    T, TX = min(256, S), min(512, S)
    proj = sv["proj"]
    g = {}
    rout = {}

    def ride(host):
        return riders[host](g, rout) if riders and host in riders else None

    dz2, g["ln2_g"], g["ln2_b"] = _ln_bwd("ln_bwd_%d" % len(dys), dys, coefs, sv["xh2"], sv["rs2"], P["ln2_g"], T)
    dhmid = _mm("d_hmid", dz2, W["w_fd"], "nt", BF16)
    g["ffn_w_down"] = _mm("dw_ffn_down", sv["hmid"], dz2, "tn", F32)
    dgu = _swiglu_bwd("swiglu_bwd", sv["gu"], dhmid, T)
    dx1f = _mm("d_x1", dgu, W["w_fi"], "nt", F32)
    g["ffn_w_in"] = _mm("dw_ffn_in", sv["x1"], dgu, "tn", F32)
    dz1, g["ln1_g"], g["ln1_b"] = _ln_bwd("ln_bwd_2", [dz2, dx1f], [ALPHA, 1.0], sv["xh1"], sv["rs1"], P["ln1_g"], T)
    dmerged = _mm("d_merged", dz1, W["w_o"], "nt", BF16)
    g["w_out"] = _mm("dw_out", sv["merged"], dz1, "tn", F32)
    dproj, dpl, dps, dpx, dbg0, dbg1, dbg2 = _merge_bwd("merge_bwd", proj, P["b_gate"], sv["p_l"], sv["p_s"], sv["p_x"],
                                                         dmerged, T)
    g["b_gate"] = jnp.concatenate([dbg0, dbg1, dbg2], axis=0)
    dy_lru = _mm("d_ylru", dpl, W["w_l"], "nt", BF16)
    g["w_br_lru"] = _mm("dw_br_lru", sv["y_lru"], dpl, "tn", F32)
    dy_ssd = _mm("d_yssd", dps, W["w_s"], "nt", BF16)
    g["w_br_ssd"] = _mm("dw_br_ssd", sv["y_ssd"], dps, "tn", F32)
    dy_xa = _mm("d_yxa", dpx, W["w_x"], "nt", BF16)
    g["w_br_xa"] = _mm("dw_br_xa", sv["y_xa"], dpx, "tn", F32)
    dproj, dkv = _xa_bwd("xattn_bwd", proj, sv["kv"], dy_xa, dproj, TX)
    g["mem_w_kv"] = _mm("dw_kv", mem2, dkv, "tn", F32)
    dproj, dycore, g["ssd_norm_w"] = _gate_norm_bwd("ssd_norm_bwd", dy_ssd, sv["ycore"], proj, P["ssd_norm_w"], dproj, T)
    res = _ssd_bwd("ssd_scan_bwd", sv["xact"], sv["dt_raw"], P["dt_bias"], P["a_neg"], P["d_exp"],
                   sv["prev"], dycore, consts, rider=ride("ssd_scan_bwd"))
    dxact, ddt, d_a, g_dtb, d_dexp = res[:5]
    rout["ssd_scan_bwd"] = res[5:]
    g["ssd_dt_bias"] = g_dtb[:, :SSD_HEADS]
    g["ssd_a_log"] = d_a[:, :SSD_HEADS] * P["a_neg"][:, :SSD_HEADS]
    g["ssd_d"] = jnp.sum(d_dexp.reshape(SSD_HEADS, SSD_HEAD_DIM), axis=-1)
    dproj, w0, w1, w2, w3, g["ssd_conv_b"] = _conv_bwd("ssd_conv_bwd", proj, CB_XBC, P["ssd_conv_w"], P["ssd_conv_b"], True,
                                                       dxact, dproj, T)
    g["ssd_conv_w"] = jnp.concatenate([w0, w1, w2, w3], axis=0)
    res = _lru_bwd("lru_scan_bwd", dy_lru, sv["xc"], proj, sv["h"], P["lru_w_a"], P["lru_w_i"], P["lru_b_a"], P["lru_b_i"],
                   P["lru_lambda"], dproj, T, rider=ride("lru_scan_bwd"))
    dproj, dxc, g["lru_w_a"], g["lru_w_i"], g["lru_b_a"], g["lru_b_i"], g["lru_lambda"] = res[:7]
    rout["lru_scan_bwd"] = res[7:]
    dproj, w0, w1, w2, w3, g["lru_conv_b"] = _conv_bwd("lru_conv_bwd", proj, CB_LRU_X, P["lru_conv_w"], P["lru_conv_b"], False,
                                                       dxc, dproj, T)
    g["lru_conv_w"] = jnp.concatenate([w0, w1, w2, w3], axis=0)
    xin = sv["xin"]
    rd = ride("dw_in")
    dw_re_t = _mm("dw_in", dproj, xin, "tn", F32, rider=rd)
    if rd is not None:
        dw_re_t, rout["dw_in"] = dw_re_t[0], dw_re_t[1:]
    dw_dt_t = _mm("dw_in_dt", ddt, xin, "tn", F32)
    pieces = {rng: dw_re_t[off:off + rng[1] - rng[0]]
              for rng, off in zip(_IN_ORDER, (0, 3072, 6144, 8192, 9216, 10240))}
    pieces[_IN_DT] = dw_dt_t[:SSD_HEADS]
    g["w_in"] = jnp.concatenate([pieces[k] for k in sorted(pieces)], axis=0)
    rd = ride("d_xin")
    dxp = _mm("d_xin", dproj, W["w_re_t"], "nn", F32, rider=rd)
    if rd is not None:
        dxp, rout["d_xin"] = dxp[0], dxp[1:]
    dxs = _mm("d_xin_dt", ddt, W["w_dt_t"], "nn", F32, add=dxp)
    return [dz1, dxs], [ALPHA, 1.0], g, rout


def _step(a):
    x2d, mem2, target = a["x"][0], a["mem"][0], a["loss_target"][0]
    S = x2d.shape[0]
    T = min(256, S)
    xi, yi, ci = _me()
    j = 2 * xi + yi
    w = {nm: a[nm] for nm in _W_NAMES}
    consts = _ssd_consts()
    row = lambda v: v.reshape(1, -1)
    pad_h = lambda v: jnp.pad(v.reshape(1, -1), ((0, 0), (0, DT_PAD - SSD_HEADS)))

    def layer_params(l, full):
        W = dict(w_re_t=full["w_re_t"], w_dt_t=full["w_dt_t"], w_fi=full["ffn_w_in"], w_kv=full["mem_w_kv"],
                 w_l=full["w_br_lru"], w_s=full["w_br_ssd"], w_x=full["w_br_xa"], w_o=full["w_out"], w_fd=full["ffn_w_down"])
        P = dict(
            b_gate=full["b_gate"], lru_conv_w=full["lru_conv_w"], ssd_conv_w=full["ssd_conv_w"],
            lru_conv_b=row(w["lru_conv_b"][l]), lru_w_a=w["lru_w_a"][l].astype(BF16), lru_w_i=w["lru_w_i"][l].astype(BF16),
            lru_b_a=row(w["lru_b_a"][l]), lru_b_i=row(w["lru_b_i"][l]), lru_lambda=row(w["lru_lambda"][l]),
            ssd_conv_b=row(w["ssd_conv_b"][l]), dt_bias=pad_h(w["ssd_dt_bias"][l]), a_neg=pad_h(-jnp.exp(w["ssd_a_log"][l])),
            d_exp=jnp.broadcast_to(w["ssd_d"][l][:, None], (SSD_HEADS, SSD_HEAD_DIM)).reshape(1, D_SSD),
            ssd_norm_w=row(w["ssd_norm_w"][l]), ln1_g=row(w["ln1_g"][l]), ln1_b=row(w["ln1_b"][l]),
            ln2_g=row(w["ln2_g"][l]), ln2_b=row(w["ln2_b"][l]))
        return W, P

    srcs, bufs = _gather_sources(w)
    kinds = list(_GATHER_KINDS)
    got0 = _run_rider("gather_weights", _gather_rider(srcs, kinds, bufs[0], 0))
    W0, P0 = layer_params(0, _layer_weights(got0))
    x1, sv0, rout = _layer_fwd(x2d, mem2, W0, P0, consts, riders={
        "proj": _gather_rider(srcs[:2], kinds[:2], bufs[1][:2], 1),
        "ssd_scan": _gather_rider(srcs[2:], kinds[2:], bufs[1][2:], 1)})
    W1, P1 = layer_params(1, _layer_weights(list(rout["proj"]) + list(rout["ssd_scan"])))
    xcur, sv1, _ = _layer_fwd(x1, mem2, W1, P1, consts)
    dy, loss_part = _loss_fwd_bwd("loss", xcur, target, T)
    loss = lax.psum(loss_part[0, 0], ("x", "y", "c"))

    big_names = [nm for nm, _ in _BIG]
    rest_names = big_names[1:]
    st = {}

    def swap_rider(key, names, layer):
        def build(g, rout_):
            st[key] = [_per_chip(g[nm], nm) for nm in names]
            return _pair_swap_rider(st[key], layer)
        return build

    def pair_sums(key, theirs):
        return [_pair_sum("grads_pair_sum", g_, t_) for g_, t_ in zip(st[key], theirs)]

    dys, coefs, g1, rout1 = _layer_bwd([dy], [1.0], sv1, mem2, W1, P1, consts,
                                       riders={"d_xin": swap_rider("g1", big_names, 1)})
    pairs1 = pair_sums("g1", rout1["d_xin"])

    def send_rest0(g, rout_):
        st["pairs0"] = pair_sums("g0", rout_["lru_scan_bwd"])
        return _chip_send_rider([pb for _, pb in st["pairs0"]], 0)

    dys, coefs, g0, rout0 = _layer_bwd(dys, coefs, sv0, mem2, W0, P0, consts, riders={
        "ssd_scan_bwd": lambda g, r: _chip_send_rider([pb for _, pb in pairs1], 1),
        "lru_scan_bwd": swap_rider("g0", rest_names, 0),
        "dw_in": send_rest0,
        "d_xin": swap_rider("g0_in", ["w_in"], 0)})
    grad_x = _axpy("grad_x", coefs[0], dys[0], dys[1], T)[None]
    pairs0_in = pair_sums("g0_in", rout0["d_xin"])
    got0_in = _run_rider("grads_chip_send", _chip_send_rider([pb for _, pb in pairs0_in], 0))
    pairs0 = pairs0_in + st["pairs0"]
    gots0 = list(got0_in) + list(rout0["dw_in"])
    halves = []
    for (p1, _), gt1, (p0, _), gt0 in zip(pairs1, rout0["ssd_scan_bwd"], pairs0, gots0):
        buf = _chip_sum("grads_chip_sum", p1, gt1, j, 1)
        halves.append(_chip_sum("grads_chip_sum", p0, gt0, j, 0, buf))
    big = dict(zip(big_names, _join_parts_multi("grads_join", halves)))
    big["w_in"] = jnp.swapaxes(big["w_in"], 1, 2)
    layer_grads = [g0, g1]
    stacked = {nm: jnp.stack([layer_grads[l][nm] for l in range(DEPTH)]) for nm in _SMALL}
    small_parts = [stacked[nm].reshape((DEPTH,) + tuple(sh)) for nm, sh in
                   ((nm, (3, D_MODEL) if nm == "b_gate" else (4, D_MODEL) if nm == "lru_conv_w" else
                     (4, D_XBC) if nm == "ssd_conv_w" else w[nm].shape[1:]) for nm in _SMALL)]
    small = dict(zip(_SMALL, _reduce_small_grads(small_parts)))
    for nm in _SMALL_SHARDED:
        cs = w[nm].shape[2]
        small[nm] = lax.dynamic_slice_in_dim(small[nm], j * cs, cs, axis=2)
    grads = {**big, **small}

    delta, new_m, new_v = {}, {}, {}
    for nm, _ in _BIG:
        delta[nm], new_m[nm], new_v[nm] = _adamw("adamw_" + nm, w[nm], grads[nm], a["m_" + nm], a["v_" + nm])
    shapes = [w[nm].shape for nm in _SMALL]
    packs = [_flat_rows([src[nm] for nm in _SMALL], 8)[None] for src in
             (w, grads, {nm: a["m_" + nm] for nm in _SMALL}, {nm: a["v_" + nm] for nm in _SMALL})]
    d_, m_, v_ = _adamw("adamw_small", *packs)
    for dst, buf in ((delta, d_), (new_m, m_), (new_v, v_)):
        dst.update(zip(_SMALL, _take_parts(buf.reshape(-1), shapes)))

    outs = [loss, grad_x]
    for group in (grads, delta, new_m, new_v):
        outs += [group[nm] for nm in _W_NAMES]
    return tuple(outs)


def kernel(x, mem, w_in, b_gate, lru_conv_w, lru_conv_b, lru_w_a, lru_b_a, lru_w_i, lru_b_i, lru_lambda, ssd_conv_w, ssd_conv_b, ssd_dt_bias, ssd_a_log, ssd_d, ssd_norm_w, mem_w_kv, w_br_lru, w_br_ssd, w_br_xa, w_out, ln1_g, ln1_b, ffn_w_in, ffn_w_down, ln2_g, ln2_b, loss_target, m_w_in, m_b_gate, m_lru_conv_w, m_lru_conv_b, m_lru_w_a, m_lru_b_a, m_lru_w_i, m_lru_b_i, m_lru_lambda, m_ssd_conv_w, m_ssd_conv_b, m_ssd_dt_bias, m_ssd_a_log, m_ssd_d, m_ssd_norm_w, m_mem_w_kv, m_w_br_lru, m_w_br_ssd, m_w_br_xa, m_w_out, m_ln1_g, m_ln1_b, m_ffn_w_in, m_ffn_w_down, m_ln2_g, m_ln2_b, v_w_in, v_b_gate, v_lru_conv_w, v_lru_conv_b, v_lru_w_a, v_lru_b_a, v_lru_w_i, v_lru_b_i, v_lru_lambda, v_ssd_conv_w, v_ssd_conv_b, v_ssd_dt_bias, v_ssd_a_log, v_ssd_d, v_ssd_norm_w, v_mem_w_kv, v_w_br_lru, v_w_br_ssd, v_w_br_xa, v_w_out, v_ln1_g, v_ln1_b, v_ffn_w_in, v_ffn_w_down, v_ln2_g, v_ln2_b):
    return _step(dict(locals()))
```

```python
import functools
import math

import jax
import jax.numpy as jnp
from jax import lax
from jax.experimental import pallas as pl
from jax.experimental.pallas import tpu as pltpu

F32, BF16 = jnp.float32, jnp.bfloat16
MESH = pl.DeviceIdType.MESH
VMEM_LIMIT_BYTES = 56 * 2**20
HALO = 16

D_MODEL = 1024
DEPTH = 2
CHUNK = 64
N_MEM = 256
LRU_BLOCKS = 8
LRU_BLOCK = 128
LRU_C = 8.0
D_SSD = 2048
SSD_HEADS = 32
SSD_HEAD_DIM = 64
SSD_GROUPS = 4
SSD_STATE = 128
D_BC = SSD_GROUPS * SSD_STATE
D_XBC = D_SSD + 2 * D_BC
XA_HEADS = 4
XA_HEAD_DIM = 256
D_FF = 2816
ALPHA = (2 * DEPTH) ** 0.25
EPS = 1e-5
N_IN = 11296
N_PROJ = 11264
DT_PAD = 128

ADAM_LR, ADAM_B1, ADAM_B2, ADAM_EPS, ADAM_WD, ADAM_STEP = 0.001, 0.9, 0.999, 1e-08, 0.01, 10

CB_XBC = (3072, 0)
CB_XS, CB_BM, CB_CM = (2048, 0), (512, 4), (512, 5)
CB_LOGITS = (3072, 1)
CB_G0, CB_G1, CB_G2 = (1024, 3), (1024, 4), (1024, 5)
CB_Z = (2048, 3)
CB_LRU_X, CB_LRU_GATE, CB_XA_Q = (1024, 8), (1024, 9), (1024, 10)


def _cparams(sem):
    return pltpu.CompilerParams(dimension_semantics=sem, vmem_limit_bytes=VMEM_LIMIT_BYTES)


MID_STEP_PCT = 65


class _Rider:
    def __init__(self, ins, alias, outs, sems, start, finish, mid=None):
        self.ins, self.alias, self.outs, self.sems = list(ins), list(alias), list(outs), list(sems)
        self.start, self.finish = start, finish
        self.mid = mid

    def operands(self):
        return self.ins + self.alias

    def out_shapes(self):
        return [jax.ShapeDtypeStruct(a.shape, a.dtype) for a in self.alias] + self.outs

    def aliases(self, n_in, n_out):
        return {n_in + len(self.ins) + q: n_out + q for q in range(len(self.alias))}

    def split(self, refs, n_in, n_out, n_scratch):
        ni, na, no = len(self.ins), len(self.alias), len(self.outs)
        main = list(refs[:n_in])
        r_in = refs[n_in:n_in + ni]
        p = n_in + ni + na
        main += refs[p:p + n_out]
        r_alias = refs[p + n_out:p + n_out + na]
        r_out = refs[p + n_out + na:p + n_out + na + no]
        p = p + n_out + na + no
        main += refs[p:p + n_scratch]
        sems = refs[p + n_scratch:]
        return main, (r_in, r_alias, r_out, sems)


def _run_rider(name, rider):
    def body(*refs):
        _, parts = rider.split(refs, 0, 0, 0)
        rider.start(*parts)
        if rider.mid is not None:
            rider.mid(*parts)
        rider.finish(*parts)

    n_ops = len(rider.operands())
    return pl.pallas_call(
        body, name=name, in_specs=[_ANY] * n_ops, out_specs=[_ANY] * len(rider.out_shapes()),
        out_shape=rider.out_shapes(), input_output_aliases=rider.aliases(0, 0), scratch_shapes=rider.sems,
    )(*rider.operands())


def _pcall(name, body, n, ins, outs, scratch=(), reverse=False, aliases=None, rider=None):
    def ridx(i):
        return (n - 1 - i) if reverse else i

    in_specs, args = [], []
    for sp in ins:
        kind, arr = sp[0], sp[1]
        if kind == "row":
            _, _, tile, width, cb = sp
            in_specs.append(pl.BlockSpec((tile, width), lambda i, cb=cb: (ridx(i), cb)))
        elif kind == "prev":
            _, _, tile, width, cb = sp
            t = tile // HALO
            in_specs.append(pl.BlockSpec((HALO, width), lambda i, cb=cb, t=t: (jnp.maximum(ridx(i) * t - 1, 0), cb)))
        elif kind == "next":
            _, _, tile, width, cb = sp
            t = tile // HALO
            last = arr.shape[0] // HALO - 1
            in_specs.append(pl.BlockSpec((HALO, width), lambda i, cb=cb, t=t, last=last: (jnp.minimum((ridx(i) + 1) * t, last), cb)))
        elif kind == "lead":
            nd = arr.ndim
            in_specs.append(pl.BlockSpec((sp[2],) + arr.shape[1:], lambda i, nd=nd: (ridx(i),) + (0,) * (nd - 1)))
        elif kind == "full":
            nd = arr.ndim
            in_specs.append(pl.BlockSpec(arr.shape, lambda i, nd=nd: (0,) * nd))
        elif kind == "any":
            in_specs.append(pl.BlockSpec(memory_space=pl.ANY))
        else:
            raise ValueError(kind)
        args.append(arr)
    out_specs, out_shape = [], []
    for sp in outs:
        kind = sp[0]
        if kind == "row":
            _, rows, cols, dtype, tile, width, cb = sp
            out_shape.append(jax.ShapeDtypeStruct((rows, cols), dtype))
            out_specs.append(pl.BlockSpec((tile, width), lambda i, cb=cb: (ridx(i), cb)))
        elif kind == "lead":
            _, shape, dtype, lead = sp
            nd = len(shape)
            out_shape.append(jax.ShapeDtypeStruct(shape, dtype))
            out_specs.append(pl.BlockSpec((lead,) + tuple(shape[1:]), lambda i, nd=nd: (ridx(i),) + (0,) * (nd - 1)))
        elif kind == "full":
            _, shape, dtype = sp
            nd = len(shape)
            out_shape.append(jax.ShapeDtypeStruct(shape, dtype))
            out_specs.append(pl.BlockSpec(tuple(shape), lambda i, nd=nd: (0,) * nd))
        else:
            raise ValueError(kind)
    aliases = dict(aliases or {})
    scratch = list(scratch)
    kernel_body = body
    if rider is not None:
        n_in, n_out, n_scratch = len(args), len(out_shape), len(scratch)

        def kernel_body(*refs):
            main, parts = rider.split(refs, n_in, n_out, n_scratch)
            i = pl.program_id(0)

            @pl.when(i == 0)
            def _():
                rider.start(*parts)

            if rider.mid is not None:
                @pl.when(i == (n * MID_STEP_PCT) // 100)
                def _():
                    rider.mid(*parts)

            body(*main)

            @pl.when(i == n - 1)
            def _():
                rider.finish(*parts)

        aliases.update(rider.aliases(n_in, n_out))
        args += rider.operands()
        in_specs += [_ANY] * len(rider.operands())
        out_shape += rider.out_shapes()
        out_specs += [_ANY] * len(rider.out_shapes())
        scratch += rider.sems
    res = pl.pallas_call(
        kernel_body, name=name, grid=(n,), in_specs=in_specs, out_specs=out_specs, out_shape=out_shape,
        scratch_shapes=scratch, input_output_aliases=aliases,
        compiler_params=_cparams(("arbitrary",)),
    )(*args)
    return res


def _pick(n, cands):
    for c in cands:
        if n % c == 0:
            return c
    return n


_MM_TILES = (1024, 1408, 512, 256, 128)


def _mm(name, a, b, mode, out_dtype, add=None, rider=None):
    if mode == "nn":
        (M, K), (K2, N) = a.shape, b.shape
    elif mode == "nt":
        (M, K), (N, K2) = a.shape, b.shape
    else:
        (K, M), (K2, N) = a.shape, b.shape
    assert K == K2, (name, a.shape, b.shape)
    tm = _pick(M, _MM_TILES)
    tn = _pick(N, _MM_TILES)
    if mode == "tn":
        tk = _pick(K, (1024, 512, 256))
    else:
        tk = K if K <= 2816 else _pick(K, _MM_TILES)
    nk = K // tk
    has_add = add is not None

    def body(*refs):
        if has_add:
            a_ref, b_ref, add_ref, o_ref, acc_ref = refs
        else:
            a_ref, b_ref, o_ref, acc_ref = refs
        k = pl.program_id(2)
        av = a_ref[...].astype(BF16)
        bv = b_ref[...].astype(BF16)
        if mode == "nn":
            p = jnp.dot(av, bv, preferred_element_type=F32)
        elif mode == "nt":
            p = lax.dot_general(av, bv, (((1,), (1,)), ((), ())), preferred_element_type=F32)
        else:
            p = lax.dot_general(av, bv, (((0,), (0,)), ((), ())), preferred_element_type=F32)

        def fin(v):
            if has_add:
                v = v + add_ref[...].astype(F32)
            o_ref[...] = v.astype(out_dtype)

        if nk == 1:
            fin(p)
        else:
            @pl.when(k == 0)
            def _():
                acc_ref[...] = p

            @pl.when(k > 0)
            def _():
                acc_ref[...] += p

            @pl.when(k == nk - 1)
            def _():
                fin(acc_ref[...])

    if mode == "nn":
        specs = [pl.BlockSpec((tm, tk), lambda i, j, k: (i, k)), pl.BlockSpec((tk, tn), lambda i, j, k: (k, j))]
    elif mode == "nt":
        specs = [pl.BlockSpec((tm, tk), lambda i, j, k: (i, k)), pl.BlockSpec((tn, tk), lambda i, j, k: (j, k))]
    else:
        specs = [pl.BlockSpec((tk, tm), lambda i, j, k: (k, i)), pl.BlockSpec((tk, tn), lambda i, j, k: (k, j))]
    args = [a, b]
    if has_add:
        specs.append(pl.BlockSpec((tm, tn), lambda i, j, k: (i, j)))
        args.append(add)
    acc_shape = (tm, tn) if nk > 1 else (8, 128)
    grid = (M // tm, N // tn, nk)
    out_spec = pl.BlockSpec((tm, tn), lambda i, j, k: (i, j))
    out_shape = jax.ShapeDtypeStruct((M, N), out_dtype)
    if rider is None:
        return pl.pallas_call(
            body, name=name, grid=grid, in_specs=specs, out_specs=out_spec, out_shape=out_shape,
            scratch_shapes=[pltpu.VMEM(acc_shape, F32)],
            compiler_params=_cparams(("parallel", "parallel", "arbitrary")),
        )(*args)
    n_in = len(args)

    def kernel_body(*refs):
        main, parts = rider.split(refs, n_in, 1, 1)
        i, j, k = pl.program_id(0), pl.program_id(1), pl.program_id(2)

        @pl.when((i == 0) & (j == 0) & (k == 0))
        def _():
            rider.start(*parts)

        if rider.mid is not None:
            flat = (grid[0] * grid[1] * grid[2] * MID_STEP_PCT) // 100
            mi, mj, mk = flat // (grid[1] * grid[2]), (flat // grid[2]) % grid[1], flat % grid[2]

            @pl.when((i == mi) & (j == mj) & (k == mk))
            def _():
                rider.mid(*parts)

        body(*main)

        @pl.when((i == grid[0] - 1) & (j == grid[1] - 1) & (k == grid[2] - 1))
        def _():
            rider.finish(*parts)

    return pl.pallas_call(
        kernel_body, name=name, grid=grid, in_specs=specs + [_ANY] * len(rider.operands()),
        out_specs=[out_spec] + [_ANY] * len(rider.out_shapes()), out_shape=[out_shape] + rider.out_shapes(),
        input_output_aliases=rider.aliases(n_in, 1), scratch_shapes=[pltpu.VMEM(acc_shape, F32)] + rider.sems,
        compiler_params=_cparams(("arbitrary", "arbitrary", "arbitrary")),
    )(*args, *rider.operands())


def _sigmoid(x):
    return 1.0 / (1.0 + jnp.exp(-x))


def _silu(x):
    return x * _sigmoid(x)


def _dsilu(x):
    s = _sigmoid(x)
    return s * (1.0 + x * (1.0 - s))


def _softplus(x):
    return jnp.maximum(x, 0.0) + jnp.log(1.0 + jnp.exp(-jnp.abs(x)))


_GELU_C = math.sqrt(2.0 / math.pi)


def _gelu(x):
    return 0.5 * x * (1.0 + jnp.tanh(_GELU_C * (x + 0.044715 * x * x * x)))


def _dgelu(x):
    t = jnp.tanh(_GELU_C * (x + 0.044715 * x * x * x))
    return 0.5 * (1.0 + t) + 0.5 * x * (1.0 - t * t) * _GELU_C * (1.0 + 3.0 * 0.044715 * x * x)


def _acc(ref, i, val):
    @pl.when(i == 0)
    def _():
        ref[...] = val

    @pl.when(i > 0)
    def _():
        ref[...] += val


def _rows(shape):
    return lax.broadcasted_iota(jnp.int32, shape, 0)


def _shift_down(x, k, halo8, first):
    r = pltpu.roll(x, k, 0)
    h = pltpu.roll(halo8, k, 0)
    h = jnp.where(first, 0.0, h)
    head = jnp.where(_rows(h.shape) < k, h, r[:8])
    if x.shape[0] == 8:
        return head
    return jnp.concatenate([head, r[8:]], axis=0)


def _shift_up(x, k, halo8, last):
    T = x.shape[0]
    r = pltpu.roll(x, T - k, 0)
    h = pltpu.roll(halo8, 8 - k, 0)
    h = jnp.where(last, 0.0, h)
    tail = jnp.where(_rows(h.shape) >= 8 - k, h, r[T - 8:])
    return jnp.concatenate([r[:T - 8], tail], axis=0)


def _ln_fwd(name, a, b, g, beta, tile):
    S, Dm = a.shape
    n = S // tile

    def body(a_ref, b_ref, g_ref, be_ref, y_ref, xh_ref, rs_ref):
        z = ALPHA * a_ref[...] + b_ref[...].astype(F32)
        mu = jnp.mean(z, axis=-1, keepdims=True)
        zc = z - mu
        var = jnp.mean(zc * zc, axis=-1, keepdims=True)
        rstd = lax.rsqrt(var + EPS)
        xh = zc * rstd
        y_ref[...] = xh * g_ref[...] + be_ref[...]
        xh_ref[...] = xh
        rs_ref[...] = rstd

    return _pcall(name, body, n,
                  [("row", a, tile, Dm, 0), ("row", b, tile, Dm, 0), ("full", g), ("full", beta)],
                  [("row", S, Dm, F32, tile, Dm, 0), ("row", S, Dm, F32, tile, Dm, 0), ("row", S, 1, F32, tile, 1, 0)])


def _ln_bwd(name, dys, coefs, xh, rstd, g, tile):
    S, Dm = xh.shape
    n = S // tile
    nd = len(dys)

    def body(*refs):
        dy_refs = refs[:nd]
        xh_ref, rs_ref, g_ref, dz_ref, dg_ref, db_ref = refs[nd:]
        i = pl.program_id(0)
        dy = coefs[0] * dy_refs[0][...].astype(F32)
        for k in range(1, nd):
            dy = dy + coefs[k] * dy_refs[k][...].astype(F32)
        xh_v = xh_ref[...]
        dxh = dy * g_ref[...]
        m1 = jnp.mean(dxh, axis=-1, keepdims=True)
        m2 = jnp.mean(dxh * xh_v, axis=-1, keepdims=True)
        dz_ref[...] = rs_ref[...] * (dxh - m1 - xh_v * m2)
        _acc(dg_ref, i, jnp.sum(dy * xh_v, axis=0, keepdims=True))
        _acc(db_ref, i, jnp.sum(dy, axis=0, keepdims=True))

    ins = [("row", d, tile, Dm, 0) for d in dys]
    ins += [("row", xh, tile, Dm, 0), ("row", rstd, tile, 1, 0), ("full", g)]
    return _pcall(name, body, n, ins,
                  [("row", S, Dm, F32, tile, Dm, 0), ("full", (1, Dm), F32), ("full", (1, Dm), F32)])


def _loss_fwd_bwd(name, y, target, tile):
    S, Dm = y.shape
    n = S // tile

    def body(y_ref, t_ref, dy_ref, l_ref):
        i = pl.program_id(0)
        err = y_ref[...] - t_ref[...]
        dy_ref[...] = err * (1.0 / Dm)
        part = jnp.sum(jnp.sum(err * err, axis=-1, keepdims=True), axis=0, keepdims=True) * (0.5 / Dm)
        _acc(l_ref, i, part)

    return _pcall(name, body, n, [("row", y, tile, Dm, 0), ("row", target, tile, Dm, 0)],
                  [("row", S, Dm, F32, tile, Dm, 0), ("full", (1, 1), F32)])


def _swiglu_fwd(name, gu, tile):
    S = gu.shape[0]
    n = S // tile

    def body(g_ref, u_ref, o_ref):
        o_ref[...] = (_silu(g_ref[...].astype(F32)) * u_ref[...].astype(F32)).astype(BF16)

    return _pcall(name, body, n, [("row", gu, tile, D_FF, 0), ("row", gu, tile, D_FF, 1)],
                  [("row", S, D_FF, BF16, tile, D_FF, 0)])[0]


def _swiglu_bwd(name, gu, dh, tile):
    S = gu.shape[0]
    n = S // tile

    def body(g_ref, u_ref, dh_ref, o_ref):
        gv = g_ref[...].astype(F32)
        uv = u_ref[...].astype(F32)
        dv = dh_ref[...].astype(F32)
        dg = dv * uv * _dsilu(gv)
        du = dv * _silu(gv)
        o_ref[...] = jnp.concatenate([dg, du], axis=1).astype(BF16)

    return _pcall(name, body, n, [("row", gu, tile, D_FF, 0), ("row", gu, tile, D_FF, 1), ("row", dh, tile, D_FF, 0)],
                  [("row", S, 2 * D_FF, BF16, tile, 2 * D_FF, 0)])[0]


def _merge_fwd(name, proj, b_gate, pl_, ps_, px_, tile):
    S = proj.shape[0]
    n = S // tile
    Dm = D_MODEL

    def body(l0, l1, l2, bg, p0, p1, p2, o_ref):
        bgv = bg[...]
        acc = _sigmoid(l0[...].astype(F32) + bgv[0:1]) * p0[...].astype(F32)
        acc = acc + _sigmoid(l1[...].astype(F32) + bgv[1:2]) * p1[...].astype(F32)
        acc = acc + _sigmoid(l2[...].astype(F32) + bgv[2:3]) * p2[...].astype(F32)
        o_ref[...] = acc.astype(BF16)

    ins = [("row", proj, tile, *CB_G0), ("row", proj, tile, *CB_G1), ("row", proj, tile, *CB_G2), ("full", b_gate),
           ("row", pl_, tile, Dm, 0), ("row", ps_, tile, Dm, 0), ("row", px_, tile, Dm, 0)]
    return _pcall(name, body, n, ins, [("row", S, Dm, BF16, tile, Dm, 0)])[0]


def _merge_bwd(name, proj, b_gate, pl_, ps_, px_, dmerged, tile):
    S = proj.shape[0]
    n = S // tile
    Dm = D_MODEL

    def body(l0, l1, l2, bg, p0, p1, p2, dm_ref, dproj_ref, d0, d1, d2, db0, db1, db2):
        i = pl.program_id(0)
        bgv = bg[...]
        dm = dm_ref[...].astype(F32)
        dls = []
        for k, (lr, pr, dr, dbr) in enumerate(((l0, p0, d0, db0), (l1, p1, d1, db1), (l2, p2, d2, db2))):
            gk = _sigmoid(lr[...].astype(F32) + bgv[k:k + 1])
            dr[...] = (dm * gk).astype(BF16)
            dl = dm * pr[...].astype(F32) * gk * (1.0 - gk)
            _acc(dbr, i, jnp.sum(dl, axis=0, keepdims=True))
            dls.append(dl)
        dproj_ref[...] = jnp.concatenate(dls, axis=1).astype(BF16)

    ins = [("row", proj, tile, *CB_G0), ("row", proj, tile, *CB_G1), ("row", proj, tile, *CB_G2), ("full", b_gate),
           ("row", pl_, tile, Dm, 0), ("row", ps_, tile, Dm, 0), ("row", px_, tile, Dm, 0), ("row", dmerged, tile, Dm, 0)]
    outs = [("row", S, N_PROJ, BF16, tile, *CB_LOGITS)] + [("row", S, Dm, BF16, tile, Dm, 0)] * 3 + [("full", (1, Dm), F32)] * 3
    return _pcall(name, body, n, ins, outs)


def _conv_taps(xf, halo8, first, w):
    out = xf * w[3:4]
    for k in (1, 2, 3):
        out = out + _shift_down(xf, k, halo8, first) * w[3 - k:4 - k]
    return out


def _conv_fwd(name, src, cb, w, b, act, out_dtype, tile):
    S = src.shape[0]
    C = cb[0]
    n = S // tile

    def body(x_ref, p_ref, w_ref, b_ref, o_ref):
        i = pl.program_id(0)
        xf = x_ref[...].astype(F32)
        halo8 = p_ref[...].astype(F32)[HALO - 8:]
        pre = _conv_taps(xf, halo8, i == 0, w_ref[...]) + b_ref[...]
        o_ref[...] = (_silu(pre) if act else pre).astype(out_dtype)

    return _pcall(name, body, n, [("row", src, tile, *cb), ("prev", src, tile, *cb), ("full", w), ("full", b)],
                  [("row", S, C, out_dtype, tile, C, 0)])[0]


def _conv_bwd(name, src, cb, w, b, act, dout, dproj, tile):
    S = src.shape[0]
    C = cb[0]
    n = S // tile

    def body(x_ref, p_ref, xn_ref, w_ref, b_ref, d_ref, dn_ref, buf_ref, dx_ref, dw0, dw1, dw2, dw3, db_ref):
        i = pl.program_id(0)
        first, last = i == 0, i == n - 1
        xf = x_ref[...].astype(F32)
        halo8 = p_ref[...].astype(F32)[HALO - 8:]
        wv = w_ref[...]
        dv = d_ref[...].astype(F32)
        nx8 = dn_ref[...].astype(F32)[:8]
        if act:
            bv = b_ref[...]
            dv = dv * _dsilu(_conv_taps(xf, halo8, first, wv) + bv)
            xn8 = xn_ref[...].astype(F32)[:8]
            nx8 = nx8 * _dsilu(_conv_taps(xn8, xf[tile - 8:], False, wv) + bv)
        dx = dv * wv[3:4]
        for k in (1, 2, 3):
            dx = dx + _shift_up(dv, k, nx8, last) * wv[3 - k:4 - k]
        dx_ref[...] = dx.astype(BF16)
        _acc(dw3, i, jnp.sum(dv * xf, axis=0, keepdims=True))
        for k, dwr in ((1, dw2), (2, dw1), (3, dw0)):
            _acc(dwr, i, jnp.sum(dv * _shift_down(xf, k, halo8, first), axis=0, keepdims=True))
        _acc(db_ref, i, jnp.sum(dv, axis=0, keepdims=True))

    ins = [("row", src, tile, *cb), ("prev", src, tile, *cb), ("next", src, tile, *cb), ("full", w), ("full", b),
           ("row", dout, tile, C, 0), ("next", dout, tile, C, 0), ("any", dproj)]
    outs = [("row", S, N_PROJ, BF16, tile, *cb)] + [("full", (1, C), F32)] * 5
    return _pcall(name, body, n, ins, outs, aliases={7: 0})


def _lru_gates(xc, wa_ref, wi_ref, ba, bi, lam):
    xb = xc.astype(BF16)
    pa, pi_ = [], []
    for nb in range(LRU_BLOCKS):
        sl = slice(nb * LRU_BLOCK, (nb + 1) * LRU_BLOCK)
        pa.append(jnp.dot(xb[:, sl], wa_ref[nb], preferred_element_type=F32))
        pi_.append(jnp.dot(xb[:, sl], wi_ref[nb], preferred_element_type=F32))
    r = _sigmoid(jnp.concatenate(pa, axis=1) + ba)
    ig = _sigmoid(jnp.concatenate(pi_, axis=1) + bi)
    sp = _softplus(-lam)
    a = jnp.exp(-LRU_C * r * sp)
    m = jnp.sqrt(1.0 - a * a)
    return xb, r, ig, sp, a, m


def _lru_fwd(name, xc, proj, wa, wi, ba, bi, lam, tile, rider=None):
    S = xc.shape[0]
    n = S // tile
    C = D_MODEL

    def body(xc_ref, gate_ref, wa_ref, wi_ref, ba_ref, bi_ref, lam_ref, y_ref, h_ref, carry):
        i = pl.program_id(0)

        @pl.when(i == 0)
        def _():
            carry[...] = jnp.zeros_like(carry)

        xcv = xc_ref[...]
        _, r, ig, sp, a, m = _lru_gates(xcv, wa_ref, wi_ref, ba_ref[...], bi_ref[...], lam_ref[...])
        u = m * (ig * xcv)
        rows = _rows(a.shape)
        d = 1
        while d < tile:
            keep = rows >= d
            a_s = jnp.where(keep, pltpu.roll(a, d, 0), 1.0)
            u_s = jnp.where(keep, pltpu.roll(u, d, 0), 0.0)
            u = a * u_s + u
            a = a * a_s
            d *= 2
        h = u + a * carry[0:1, :]
        h_ref[...] = h
        carry[0:1, :] = h_ref[pl.ds(tile - 1, 1), :]
        y_ref[...] = (_gelu(gate_ref[...].astype(F32)) * h).astype(BF16)

    ins = [("row", xc, tile, C, 0), ("row", proj, tile, *CB_LRU_GATE), ("full", wa), ("full", wi),
           ("full", ba), ("full", bi), ("full", lam)]
    return _pcall(name, body, n, ins, [("row", S, C, BF16, tile, C, 0), ("row", S, C, F32, tile, C, 0)],
                  scratch=[pltpu.VMEM((8, C), F32)], rider=rider)


def _lru_bwd(name, dy, xc, proj, h, wa, wi, ba, bi, lam, dproj, tile, rider=None):
    S = xc.shape[0]
    n = S // tile
    C = D_MODEL

    def body(dy_ref, xc_ref, gate_ref, h_ref, hp_ref, wa_ref, wi_ref, ba_ref, bi_ref, lam_ref, buf_ref,
             dg_ref, dxc_ref, dwa_ref, dwi_ref, dba_ref, dbi_ref, dlam_ref, carry):
        i = pl.program_id(0)
        first_tile = i == n - 1

        @pl.when(i == 0)
        def _():
            carry[...] = jnp.zeros_like(carry)

        xcv = xc_ref[...]
        lamv = lam_ref[...]
        xb, r, ig, sp, a, m = _lru_gates(xcv, wa_ref, wi_ref, ba_ref[...], bi_ref[...], lamv)
        hv = h_ref[...]
        gv = gate_ref[...].astype(F32)
        dyv = dy_ref[...].astype(F32)
        dg_ref[...] = (dyv * hv * _dgelu(gv)).astype(BF16)
        v = dyv * _gelu(gv)
        rows = _rows(a.shape)
        bcoef = jnp.where(rows == tile - 1, 1.0, pltpu.roll(a, tile - 1, 0))
        d = 1
        while d < tile:
            keep = rows < tile - d
            b_s = jnp.where(keep, pltpu.roll(bcoef, tile - d, 0), 1.0)
            v_s = jnp.where(keep, pltpu.roll(v, tile - d, 0), 0.0)
            v = v + bcoef * v_s
            bcoef = bcoef * b_s
            d *= 2
        dH = v + bcoef * carry[0:1, :]
        dxc_ref[...] = dH
        carry[0:1, :] = dxc_ref[pl.ds(0, 1), :] * a[0:1, :]
        halo8 = hp_ref[...][HALO - 8:]
        hprev = _shift_down(hv, 1, halo8, first_tile)
        da = dH * hprev
        ix = ig * xcv
        dm = dH * ix
        di = dH * m * xcv
        dxc = dH * m * ig
        da = da - dm * a / m
        dla = da * a
        dr = dla * (-LRU_C) * sp
        _acc(dlam_ref, i, jnp.sum(dla * (-LRU_C) * r, axis=0, keepdims=True) * (-_sigmoid(-lamv)))
        dpa = dr * r * (1.0 - r)
        dpi = di * ig * (1.0 - ig)
        _acc(dba_ref, i, jnp.sum(dpa, axis=0, keepdims=True))
        _acc(dbi_ref, i, jnp.sum(dpi, axis=0, keepdims=True))
        dpab, dpib = dpa.astype(BF16), dpi.astype(BF16)
        back = []
        for nb in range(LRU_BLOCKS):
            sl = slice(nb * LRU_BLOCK, (nb + 1) * LRU_BLOCK)
            back.append(lax.dot_general(dpab[:, sl], wa_ref[nb], (((1,), (1,)), ((), ())), preferred_element_type=F32)
                        + lax.dot_general(dpib[:, sl], wi_ref[nb], (((1,), (1,)), ((), ())), preferred_element_type=F32))
            ga = lax.dot_general(xb[:, sl], dpab[:, sl], (((0,), (0,)), ((), ())), preferred_element_type=F32)
            gi = lax.dot_general(xb[:, sl], dpib[:, sl], (((0,), (0,)), ((), ())), preferred_element_type=F32)

            @pl.when(i == 0)
            def _(ga=ga, gi=gi, nb=nb):
                dwa_ref[nb] = ga
                dwi_ref[nb] = gi

            @pl.when(i > 0)
            def _(ga=ga, gi=gi, nb=nb):
                dwa_ref[nb] += ga
                dwi_ref[nb] += gi

        dxc_ref[...] = dxc + jnp.concatenate(back, axis=1)

    ins = [("row", dy, tile, C, 0), ("row", xc, tile, C, 0), ("row", proj, tile, *CB_LRU_GATE), ("row", h, tile, C, 0),
           ("prev", h, tile, C, 0), ("full", wa), ("full", wi), ("full", ba), ("full", bi), ("full", lam), ("any", dproj)]
    outs = [("row", S, N_PROJ, BF16, tile, *CB_LRU_GATE), ("row", S, C, F32, tile, C, 0),
            ("full", (LRU_BLOCKS, LRU_BLOCK, LRU_BLOCK), F32), ("full", (LRU_BLOCKS, LRU_BLOCK, LRU_BLOCK), F32),
            ("full", (1, C), F32), ("full", (1, C), F32), ("full", (1, C), F32)]
    return _pcall(name, body, n, ins, outs, scratch=[pltpu.VMEM((8, C), F32)], reverse=True, aliases={10: 0}, rider=rider)


SSD_STEP = 4

def _split3(x):
    h = x.astype(BF16)
    r = x - h.astype(F32)
    m = r.astype(BF16)
    lo = (r - m.astype(F32)).astype(BF16)
    return h, m, lo


def _dot01_r(x, e):
    h = x.astype(BF16)
    m = (x - h.astype(F32)).astype(BF16)
    return jnp.dot(h, e, preferred_element_type=F32) + jnp.dot(m, e, preferred_element_type=F32)


def _dot01_l(e, x):
    h, m, lo = _split3(x)
    return (jnp.dot(e, h, preferred_element_type=F32) + jnp.dot(e, m, preferred_element_type=F32)
            + jnp.dot(e, lo, preferred_element_type=F32))


def _ssd_consts():
    hh = lax.broadcasted_iota(jnp.int32, (DT_PAD, D_SSD), 0)
    cc = lax.broadcasted_iota(jnp.int32, (DT_PAD, D_SSD), 1)
    e = (cc // SSD_HEAD_DIM == hh).astype(BF16)
    rows = SSD_STEP * CHUNK
    li = lax.broadcasted_iota(jnp.int32, (rows, rows), 0)
    si = lax.broadcasted_iota(jnp.int32, (rows, rows), 1)
    ltri = ((li >= si) & (li // CHUNK == si // CHUNK)).astype(BF16)
    l4 = lax.broadcasted_iota(jnp.int32, (CHUNK, 4 * CHUNK), 0)
    s4 = lax.broadcasted_iota(jnp.int32, (CHUNK, 4 * CHUNK), 1) % CHUNK
    itile = (l4 == s4).astype(F32)
    causal = (l4 >= s4).astype(F32)
    j4 = lax.broadcasted_iota(jnp.int32, (8, 4 * CHUNK), 0)
    c4 = lax.broadcasted_iota(jnp.int32, (8, 4 * CHUNK), 1) // CHUNK
    hmask = (j4 == c4).astype(F32)
    return e, e.T, ltri, ltri.T, itile, causal, hmask


def _ssd_chunk_common(xs_ref, bm_ref, cm_ref, dt_ref, dtb_ref, a_ref, e_ref, ltri_ref):
    xs = xs_ref[...].astype(F32)
    raw = dt_ref[...] + dtb_ref[...]
    dtv = _softplus(raw)
    da = dtv * a_ref[...]
    cs = _dot01_l(ltri_ref[...], da)
    e = e_ref[...]
    dte = _dot01_r(dtv, e)
    ce = _dot01_r(cs, e)
    xdt = xs * dte
    ecs = jnp.exp(ce)
    return xs, raw, dtv, cs, dte, ce, xdt, ecs


def _quad_terms(ce_q, cb4, itile, causal):
    cr = jnp.sum(ce_q * itile, axis=0, keepdims=True)
    seg = ce_q - cr
    dec = jnp.where(causal > 0.0, jnp.exp(jnp.minimum(seg, 0.0)), 0.0)
    return dec, cb4 * dec


def _block_diag4(xq, hmask):
    return jnp.concatenate([xq * hmask[j:j + 1].astype(xq.dtype) for j in range(4)], axis=0)


def _ssd_fwd(name, xbc, dt_raw, dt_bias, a_neg, d_exp, consts, rider=None):
    S = xbc.shape[0]
    nc = S // CHUNK
    e, et, ltri, ltri_t, itile, causal, hmask = consts

    def body(xs_ref, bm_ref, cm_ref, dt_ref, dtb_ref, a_ref, dex_ref, e_ref, ltri_ref, it_ref, ca_ref, hm_ref,
             y_ref, prev_ref, hst):
        i = pl.program_id(0)

        @pl.when(i == 0)
        def _():
            hst[...] = jnp.zeros_like(hst)

        xs, raw, dtv, cs, dte, ce_all, xdt_all, ecs_all = _ssd_chunk_common(
            xs_ref, bm_ref, cm_ref, dt_ref, dtb_ref, a_ref, e_ref, ltri_ref)
        itile_v, causal_v, hmask_v = it_ref[...], ca_ref[...], hm_ref[...]
        dskip_all = dex_ref[...] * xs
        for k in range(SSD_STEP):
            rs = slice(k * CHUNK, (k + 1) * CHUNK)
            ce, xdt, ecs, dskip = ce_all[rs], xdt_all[rs], ecs_all[rs], dskip_all[rs]
            cle = ce[CHUNK - 1:CHUNK, :]
            xdtb = xdt.astype(BF16)
            xst = (xdt * jnp.exp(cle - ce)).astype(BF16)
            ecl = jnp.exp(cle)
            bm = bm_ref[rs, :]
            cm = cm_ref[rs, :]
            for g in range(SSD_GROUPS):
                gs = slice(g * 512, (g + 1) * 512)
                ns = slice(g * SSD_STATE, (g + 1) * SSD_STATE)
                bm_g, cm_g = bm[:, ns], cm[:, ns]
                hprev = hst[g]
                hprev_b = hprev.astype(BF16)
                prev_ref[k, g] = hprev_b
                yoff = jnp.dot(cm_g, hprev_b, preferred_element_type=F32) * ecs[:, gs]
                st = lax.dot_general(bm_g, xst[:, gs], (((0,), (0,)), ((), ())), preferred_element_type=F32)
                hst[g] = hprev * ecl[:, gs] + st
                b4 = jnp.concatenate([bm_g] * 4, axis=0)
                cb4 = lax.dot_general(cm_g, b4, (((1,), (1,)), ((), ())), preferred_element_type=F32)
                for q in range(2):
                    cols = slice(g * 512 + q * 256, g * 512 + (q + 1) * 256)
                    _, mq = _quad_terms(ce[:, cols], cb4, itile_v, causal_v)
                    xbd = _block_diag4(xdtb[:, cols], hmask_v)
                    ydiag = jnp.dot(mq.astype(BF16), xbd, preferred_element_type=F32)
                    y_ref[rs, cols] = ydiag + yoff[:, q * 256:(q + 1) * 256] + dskip[:, cols]

    T = SSD_STEP * CHUNK
    ins = [("row", xbc, T, *CB_XS), ("row", xbc, T, *CB_BM), ("row", xbc, T, *CB_CM),
           ("row", dt_raw, T, DT_PAD, 0), ("full", dt_bias), ("full", a_neg), ("full", d_exp),
           ("full", e), ("full", ltri), ("full", itile), ("full", causal), ("full", hmask)]
    outs = [("row", S, D_SSD, F32, T, D_SSD, 0), ("lead", (nc, SSD_GROUPS, SSD_STATE, 512), BF16, SSD_STEP)]
    return _pcall(name, body, S // T, ins, outs, scratch=[pltpu.VMEM((SSD_GROUPS, SSD_STATE, 512), F32)], rider=rider)


def _ssd_bwd(name, xbc, dt_raw, dt_bias, a_neg, d_exp, prev, dy, consts, rider=None):
    S = xbc.shape[0]
    nc = S // CHUNK
    e, et, ltri, ltri_t, itile, causal, hmask = consts

    def body(xs_ref, bm_ref, cm_ref, dt_ref, dtb_ref, a_ref, dex_ref, prev_ref, dy_ref,
             e_ref, et_ref, ltri_ref, ltt_ref, it_ref, ca_ref, hm_ref,
             dx_ref, ddt_ref, da_ref, dbias_ref, dd_ref, dh, dce_ref, dxdt_ref):
        i = pl.program_id(0)

        @pl.when(i == 0)
        def _():
            dh[...] = jnp.zeros_like(dh)

        xs, raw, dtv, cs, dte, ce_all, xdt_all, ecs_all = _ssd_chunk_common(
            xs_ref, bm_ref, cm_ref, dt_ref, dtb_ref, a_ref, e_ref, ltri_ref)
        itile_v, causal_v, hmask_v = it_ref[...], ca_ref[...], hm_ref[...]
        dyv = dy_ref[...]
        last_row = _rows((CHUNK, 512)) == CHUNK - 1
        for k in reversed(range(SSD_STEP)):
            rs = slice(k * CHUNK, (k + 1) * CHUNK)
            ce, xdt, ecs = ce_all[rs], xdt_all[rs], ecs_all[rs]
            cle = ce[CHUNK - 1:CHUNK, :]
            dend = jnp.exp(cle - ce)
            xdtb = xdt.astype(BF16)
            xst = (xdt * dend).astype(BF16)
            ecl = jnp.exp(cle)
            bm = bm_ref[rs, :]
            cm = cm_ref[rs, :]
            for g in range(SSD_GROUPS):
                gs = slice(g * 512, (g + 1) * 512)
                ns = slice(g * SSD_STATE, (g + 1) * SSD_STATE)
                bm_g, cm_g = bm[:, ns], cm[:, ns]
                hprev_b = prev_ref[k, g]
                dhn = dh[g]
                dhn_b = dhn.astype(BF16)
                dy_g = dyv[rs, gs]
                ecs_g, dend_g, xdt_g, ecl_g = ecs[:, gs], dend[:, gs], xdt[:, gs], ecl[:, gs]
                z = jnp.dot(cm_g, hprev_b, preferred_element_type=F32)
                dz = dy_g * ecs_g
                dzb = dz.astype(BF16)
                dce_g = dz * z
                dcm_g = lax.dot_general(dzb, hprev_b, (((1,), (1,)), ((), ())), preferred_element_type=F32)
                dprev = lax.dot_general(cm_g, dzb, (((0,), (0,)), ((), ())), preferred_element_type=F32) + dhn * ecl_g
                dcl = jnp.sum(dhn * hprev_b.astype(F32), axis=0, keepdims=True) * ecl_g
                gmat = jnp.dot(bm_g, dhn_b, preferred_element_type=F32)
                dbm_g = lax.dot_general(xst[:, gs], dhn_b, (((1,), (1,)), ((), ())), preferred_element_type=F32)
                dxdt_g = gmat * dend_g
                t = gmat * xdt_g * dend_g
                dce_g = dce_g - t
                dcl = dcl + jnp.sum(t, axis=0, keepdims=True)
                dce_g = dce_g + jnp.where(last_row, dcl, 0.0)
                dh[g] = dprev
                b4 = jnp.concatenate([bm_g] * 4, axis=0)
                cb4 = lax.dot_general(cm_g, b4, (((1,), (1,)), ((), ())), preferred_element_type=F32)
                for q in range(2):
                    qs = slice(q * 256, (q + 1) * 256)
                    cols = slice(g * 512 + q * 256, g * 512 + (q + 1) * 256)
                    dec, mq = _quad_terms(ce[:, cols], cb4, itile_v, causal_v)
                    mqb = mq.astype(BF16)
                    xbd = _block_diag4(xdtb[:, cols], hmask_v)
                    dyq = dy_g[:, qs].astype(BF16)
                    dm = lax.dot_general(dyq, xbd, (((1,), (1,)), ((), ())), preferred_element_type=F32)
                    rmat = lax.dot_general(mqb, dyq, (((0,), (0,)), ((), ())), preferred_element_type=F32)
                    dxq = rmat[0:64] * hmask_v[0:1]
                    for j in range(1, 4):
                        dxq = dxq + rmat[64 * j:64 * (j + 1)] * hmask_v[j:j + 1]
                    tq = dm * dec
                    tqb = tq.astype(BF16)
                    dcm_g = dcm_g + jnp.dot(tqb, b4, preferred_element_type=F32)
                    rb = lax.dot_general(tqb, cm_g, (((0,), (0,)), ((), ())), preferred_element_type=F32)
                    dbm_g = dbm_g + rb[0:64] + rb[64:128] + rb[128:192] + rb[192:256]
                    dseg = tq * cb4
                    colsum = jnp.sum(dseg, axis=0, keepdims=True)
                    dce_ref[rs, cols] = dce_g[:, qs] + dseg - itile_v * colsum
                    dxdt_ref[rs, cols] = dxdt_g[:, qs] + dxq
                dx_ref[rs, D_SSD + g * SSD_STATE:D_SSD + (g + 1) * SSD_STATE] = dbm_g
                dx_ref[rs, D_SSD + D_BC + g * SSD_STATE:D_SSD + D_BC + (g + 1) * SSD_STATE] = dcm_g
        dxdt = dxdt_ref[...]
        dexv = dex_ref[...]
        dx_ref[:, 0:D_SSD] = dxdt * dte + dyv * dexv
        _acc(dd_ref, i, jnp.sum(dyv * xs, axis=0, keepdims=True))
        etv = et_ref[...]
        dcs = _dot01_r(dce_ref[...], etv)
        dda = _dot01_l(ltt_ref[...], dcs)
        av = a_ref[...]
        ddtv = dda * av + _dot01_r(dxdt * xs, etv)
        _acc(da_ref, i, jnp.sum(dda * dtv, axis=0, keepdims=True))
        draw = ddtv * _sigmoid(raw)
        ddt_ref[...] = draw.astype(BF16)
        _acc(dbias_ref, i, jnp.sum(draw, axis=0, keepdims=True))

    T = SSD_STEP * CHUNK
    ins = [("row", xbc, T, *CB_XS), ("row", xbc, T, *CB_BM), ("row", xbc, T, *CB_CM),
           ("row", dt_raw, T, DT_PAD, 0), ("full", dt_bias), ("full", a_neg), ("full", d_exp),
           ("lead", prev, SSD_STEP), ("row", dy, T, D_SSD, 0),
           ("full", e), ("full", et), ("full", ltri), ("full", ltri_t), ("full", itile), ("full", causal), ("full", hmask)]
    outs = [("row", S, D_XBC, F32, T, D_XBC, 0), ("row", S, DT_PAD, BF16, T, DT_PAD, 0),
            ("full", (1, DT_PAD), F32), ("full", (1, DT_PAD), F32), ("full", (1, D_SSD), F32)]
    scratch = [pltpu.VMEM((SSD_GROUPS, SSD_STATE, 512), F32), pltpu.VMEM((T, D_SSD), F32), pltpu.VMEM((T, D_SSD), F32)]
    return _pcall(name, body, S // T, ins, outs, scratch=scratch, reverse=True, rider=rider)


def _gate_norm_fwd(name, ycore, proj, norm_w, tile):
    S = ycore.shape[0]
    n = S // tile

    def body(y_ref, z_ref, w_ref, o_ref):
        y2 = y_ref[...] * _silu(z_ref[...].astype(F32))
        wv = w_ref[...]
        for g in range(SSD_GROUPS):
            gs = slice(g * 512, (g + 1) * 512)
            seg = y2[:, gs]
            r = lax.rsqrt(jnp.mean(seg * seg, axis=-1, keepdims=True) + EPS)
            o_ref[:, gs] = (seg * r * wv[:, gs]).astype(BF16)

    return _pcall(name, body, n, [("row", ycore, tile, D_SSD, 0), ("row", proj, tile, *CB_Z), ("full", norm_w)],
                  [("row", S, D_SSD, BF16, tile, D_SSD, 0)])[0]


def _gate_norm_bwd(name, dout, ycore, proj, norm_w, dproj, tile):
    S = ycore.shape[0]
    n = S // tile

    def body(do_ref, y_ref, z_ref, w_ref, buf_ref, dz_ref, dy_ref, dw_ref):
        i = pl.program_id(0)
        yv = y_ref[...]
        zv = z_ref[...].astype(F32)
        sz = _silu(zv)
        y2 = yv * sz
        dov = do_ref[...].astype(F32)
        wv = w_ref[...]
        dws, dy2s = [], []
        for g in range(SSD_GROUPS):
            gs = slice(g * 512, (g + 1) * 512)
            seg = y2[:, gs]
            r = lax.rsqrt(jnp.mean(seg * seg, axis=-1, keepdims=True) + EPS)
            yn = seg * r
            dws.append(jnp.sum(dov[:, gs] * yn, axis=0, keepdims=True))
            dyn = dov[:, gs] * wv[:, gs]
            dy2s.append(r * (dyn - yn * jnp.mean(dyn * yn, axis=-1, keepdims=True)))
        dy2 = jnp.concatenate(dy2s, axis=1)
        dy_ref[...] = dy2 * sz
        dz_ref[...] = (dy2 * yv * _dsilu(zv)).astype(BF16)
        _acc(dw_ref, i, jnp.concatenate(dws, axis=1))

    ins = [("row", dout, tile, D_SSD, 0), ("row", ycore, tile, D_SSD, 0), ("row", proj, tile, *CB_Z), ("full", norm_w),
           ("any", dproj)]
    outs = [("row", S, N_PROJ, BF16, tile, *CB_Z), ("row", S, D_SSD, F32, tile, D_SSD, 0), ("full", (1, D_SSD), F32)]
    return _pcall(name, body, n, ins, outs, aliases={4: 0})


_XA_SCALE = XA_HEAD_DIM ** -0.5
_NT = (((1,), (1,)), ((), ()))
_TN = (((0,), (0,)), ((), ()))


def _xa_probs(qh, kh):
    s = lax.dot_general(qh, kh, _NT, preferred_element_type=F32) * _XA_SCALE
    s = s - jnp.max(s, axis=-1, keepdims=True)
    p = jnp.exp(s)
    return p / jnp.sum(p, axis=-1, keepdims=True)


def _xa_fwd(name, proj, kv, tile):
    S = proj.shape[0]
    n = S // tile
    Dh = XA_HEAD_DIM

    def body(q_ref, kv_ref, o_ref):
        for hd in range(XA_HEADS):
            qh = q_ref[:, hd * Dh:(hd + 1) * Dh]
            kh = kv_ref[:, hd * Dh:(hd + 1) * Dh]
            vh = kv_ref[:, D_MODEL + hd * Dh:D_MODEL + (hd + 1) * Dh]
            p = _xa_probs(qh, kh)
            o_ref[:, hd * Dh:(hd + 1) * Dh] = jnp.dot(p.astype(BF16), vh, preferred_element_type=F32).astype(BF16)

    return _pcall(name, body, n, [("row", proj, tile, *CB_XA_Q), ("full", kv)],
                  [("row", S, D_MODEL, BF16, tile, D_MODEL, 0)])[0]


def _xa_bwd(name, proj, kv, dout, dproj, tile):
    S = proj.shape[0]
    n = S // tile
    Dh = XA_HEAD_DIM

    def body(q_ref, kv_ref, do_ref, buf_ref, dq_ref, dkv_ref):
        i = pl.program_id(0)
        for hd in range(XA_HEADS):
            ks_ = slice(hd * Dh, (hd + 1) * Dh)
            vs_ = slice(D_MODEL + hd * Dh, D_MODEL + (hd + 1) * Dh)
            qh = q_ref[:, ks_]
            kh = kv_ref[:, ks_]
            vh = kv_ref[:, vs_]
            doh = do_ref[:, ks_].astype(BF16)
            p = _xa_probs(qh, kh)
            pb = p.astype(BF16)
            dp = lax.dot_general(doh, vh, _NT, preferred_element_type=F32)
            dv = lax.dot_general(pb, doh, _TN, preferred_element_type=F32)
            ds = (p * (dp - jnp.sum(dp * p, axis=-1, keepdims=True)) * _XA_SCALE).astype(BF16)
            dq_ref[:, ks_] = jnp.dot(ds, kh, preferred_element_type=F32).astype(BF16)
            dk = lax.dot_general(ds, qh, _TN, preferred_element_type=F32)

            @pl.when(i == 0)
            def _(dk=dk, dv=dv, ks_=ks_, vs_=vs_):
                dkv_ref[:, ks_] = dk
                dkv_ref[:, vs_] = dv

            @pl.when(i > 0)
            def _(dk=dk, dv=dv, ks_=ks_, vs_=vs_):
                dkv_ref[:, ks_] += dk
                dkv_ref[:, vs_] += dv

    ins = [("row", proj, tile, *CB_XA_Q), ("full", kv), ("row", dout, tile, D_MODEL, 0), ("any", dproj)]
    outs = [("row", S, N_PROJ, BF16, tile, *CB_XA_Q), ("full", (N_MEM, 2 * D_MODEL), F32)]
    return _pcall(name, body, n, ins, outs, aliases={3: 0})


def _adamw(name, w, g, m, v):
    L, R, C = w.shape
    tile = _pick(R, [t for t in (256, 128, 64, 32, 16, 8) if t * C <= 128 * 2048])
    bc1 = 1.0 - ADAM_B1 ** ADAM_STEP
    bc2 = 1.0 - ADAM_B2 ** ADAM_STEP

    def body(w_ref, g_ref, m_ref, v_ref, d_ref, nm_ref, nv_ref):
        gv = g_ref[...]
        mn = ADAM_B1 * m_ref[...] + (1.0 - ADAM_B1) * gv
        vn = ADAM_B2 * v_ref[...] + (1.0 - ADAM_B2) * (gv * gv)
        nm_ref[...] = mn
        nv_ref[...] = vn
        d_ref[...] = -ADAM_LR * ((mn / bc1) / (jnp.sqrt(vn / bc2) + ADAM_EPS) + ADAM_WD * w_ref[...])

    spec = pl.BlockSpec((1, tile, C), lambda l, i: (l, i, 0))
    return pl.pallas_call(
        body, name=name, grid=(L, R // tile), in_specs=[spec] * 4, out_specs=[spec] * 3,
        out_shape=[jax.ShapeDtypeStruct((L, R, C), F32)] * 3,
        compiler_params=_cparams(("arbitrary", "arbitrary")),
    )(w, g, m, v)


def _axpy(name, coef, a, b, tile):
    S, C = a.shape
    n = S // tile

    def body(a_ref, b_ref, o_ref):
        o_ref[...] = coef * a_ref[...].astype(F32) + b_ref[...].astype(F32)

    return _pcall(name, body, n, [("row", a, tile, C, 0), ("row", b, tile, C, 0)], [("row", S, C, F32, tile, C, 0)])[0]


def _sum_terms(name, terms, out_dtype):
    R, C = terms[0].shape
    tile = _pick(R, (512, 256, 128, 64, 32, 16, 8))
    n = R // tile
    nt = len(terms)

    def body(*refs):
        vals = [r[...].astype(F32) for r in refs[:nt]]
        while len(vals) > 1:
            vals = [vals[k] + vals[k + 1] for k in range(0, len(vals), 2)]
        refs[nt][...] = vals[0].astype(out_dtype)

    return _pcall(name, body, n, [("row", t, tile, C, 0) for t in terms], [("row", R, C, out_dtype, tile, C, 0)])[0]


def _me():
    return lax.axis_index("x"), lax.axis_index("y"), lax.axis_index("c")


def _other_chips(x, y):
    return [(1 - x, y), (x, 1 - y), (1 - x, 1 - y)]


_ANY = pl.BlockSpec(memory_space=pl.ANY)


def _swap_sibling(name, src):
    def body(src_ref, out_ref, send_sem, recv_sem):
        x, y, c = _me()
        cp = pltpu.make_async_remote_copy(src_ref=src_ref, dst_ref=out_ref, send_sem=send_sem, recv_sem=recv_sem,
                                          device_id=(x, y, 1 - c), device_id_type=MESH)
        cp.start()
        cp.wait()

    return pl.pallas_call(
        body, name=name, in_specs=[_ANY], out_specs=_ANY, out_shape=jax.ShapeDtypeStruct(src.shape, src.dtype),
        scratch_shapes=[pltpu.SemaphoreType.DMA, pltpu.SemaphoreType.DMA],
    )(src)


def _send_chips(name, src, per_chip):
    shape = src.shape[1:] if per_chip else src.shape

    def body(src_ref, out_ref, send_sems, recv_sems):
        x, y, c = _me()
        cps = []
        for k, (cx, cy) in enumerate(_other_chips(x, y)):
            s = src_ref.at[2 * cx + cy] if per_chip else src_ref
            cps.append(pltpu.make_async_remote_copy(src_ref=s, dst_ref=out_ref.at[k], send_sem=send_sems.at[k],
                                                    recv_sem=recv_sems.at[k], device_id=(cx, cy, c), device_id_type=MESH))
        for cp in cps:
            cp.start()
        for cp in cps:
            cp.wait()

    return pl.pallas_call(
        body, name=name, in_specs=[_ANY], out_specs=_ANY, out_shape=jax.ShapeDtypeStruct((3,) + shape, src.dtype),
        scratch_shapes=[pltpu.SemaphoreType.DMA((3,)), pltpu.SemaphoreType.DMA((3,))],
    )(src)


W_IN_SHARD = N_IN // 4
W_IN_INNER = W_IN_SHARD - 8


def _win_rows(chip_x, chip_y):
    start = (2 * chip_x + chip_y) * W_IN_SHARD + 8 * chip_y
    return pl.ds(pl.multiple_of(start, 2 * 8), W_IN_INNER)


def _dma_sems(*shape):
    return [pltpu.SemaphoreType.DMA(shape), pltpu.SemaphoreType.DMA(shape)]


def _gather_rider(srcs, kinds, bufs, part):
    n = len(srcs)

    def copy(refs, i, k, cx, cy, to, own=False):
        src_refs, out_refs, _, (send_sems, recv_sems) = refs
        d = out_refs[i].at[2 * cx + cy] if kinds[i] == "blk" else out_refs[i].at[_win_rows(cx, cy)]
        return pltpu.make_async_remote_copy(src_ref=src_refs[i].at[part] if own else d, dst_ref=d,
                                            send_sem=send_sems.at[i, k], recv_sem=recv_sems.at[i, k],
                                            device_id=to, device_id_type=MESH)

    def start(*refs):
        x, y, c = _me()

        @pl.when(c == part)
        def _():
            for k, (cx, cy) in enumerate(_other_chips(x, y)):
                for i in range(n):
                    copy(refs, i, k, x, y, (cx, cy, part), own=True).start()

    def mid(*refs):
        x, y, c = _me()

        @pl.when(c == part)
        def _():
            for k, (cx, cy) in enumerate(_other_chips(x, y)):
                for i in range(n):
                    copy(refs, i, k, cx, cy, (x, y, part)).wait_recv()
                    copy(refs, i, 3 + k, cx, cy, (x, y, 1 - part)).start()

    def finish(*refs):
        x, y, c = _me()
        chips = _other_chips(x, y)

        @pl.when(c == part)
        def _():
            for k, (cx, cy) in enumerate(chips):
                for i in range(n):
                    copy(refs, i, k, x, y, (cx, cy, part), own=True).wait_send()
                    copy(refs, i, 3 + k, cx, cy, (x, y, 1 - part)).wait_send()

        @pl.when(c != part)
        def _():
            for k, (cx, cy) in enumerate(chips):
                for i in range(n):
                    copy(refs, i, 3 + k, cx, cy, (x, y, 1 - part)).wait_recv()

    return _Rider(srcs, bufs, [], _dma_sems(n, 6), start, finish, mid)


def _pair_swap_rider(gs, layer):
    n = len(gs)

    def copy(refs, i):
        in_refs, _, out_refs, (send_sems, recv_sems) = refs
        x, y, _c = _me()
        return pltpu.make_async_remote_copy(src_ref=in_refs[i], dst_ref=out_refs[i], send_sem=send_sems.at[i],
                                            recv_sem=recv_sems.at[i], device_id=(x, y, layer), device_id_type=MESH)

    def start(*refs):
        @pl.when(_me()[2] != layer)
        def _():
            for i in range(n):
                copy(refs, i).start()

    def finish(*refs):
        @pl.when(_me()[2] != layer)
        def _():
            for i in range(n):
                copy(refs, i).wait_send()

        @pl.when(_me()[2] == layer)
        def _():
            for i in range(n):
                copy(refs, i).wait_recv()

    return _Rider(gs, [], [jax.ShapeDtypeStruct(g.shape, g.dtype) for g in gs], _dma_sems(n), start, finish)


def _chip_send_rider(pairs, layer):
    n = len(pairs)

    def copies(refs):
        in_refs, _, out_refs, (send_sems, recv_sems) = refs
        x, y, _c = _me()
        return [pltpu.make_async_remote_copy(src_ref=in_refs[i].at[2 * cx + cy], dst_ref=out_refs[i].at[k],
                                             send_sem=send_sems.at[i, k], recv_sem=recv_sems.at[i, k],
                                             device_id=(cx, cy, layer), device_id_type=MESH)
                for k, (cx, cy) in enumerate(_other_chips(x, y)) for i in range(n)]

    def start(*refs):
        @pl.when(_me()[2] == layer)
        def _():
            for cp in copies(refs):
                cp.start()

    def finish(*refs):
        @pl.when(_me()[2] == layer)
        def _():
            for cp in copies(refs):
                cp.wait()

    outs = [jax.ShapeDtypeStruct((3,) + p.shape[1:], p.dtype) for p in pairs]
    return _Rider(pairs, [], outs, _dma_sems(n, 3), start, finish)


def _join_parts_multi(name, bufs):
    n = len(bufs)

    def body(*refs):
        out_refs = refs[n:2 * n]
        send_sems, recv_sems = refs[2 * n:]
        x, y, c = _me()

        def copy(i, part):
            return pltpu.make_async_remote_copy(src_ref=out_refs[i].at[part], dst_ref=out_refs[i].at[part],
                                                send_sem=send_sems.at[i], recv_sem=recv_sems.at[i],
                                                device_id=(x, y, 1 - c), device_id_type=MESH)

        for i in range(n):
            copy(i, c).start()
        for i in range(n):
            copy(i, c).wait_send()
            copy(i, 1 - c).wait_recv()

    return pl.pallas_call(
        body, name=name, in_specs=[_ANY] * n, out_specs=[_ANY] * n,
        out_shape=[jax.ShapeDtypeStruct(b.shape, b.dtype) for b in bufs],
        input_output_aliases={i: i for i in range(n)},
        scratch_shapes=[pltpu.SemaphoreType.DMA((n,)), pltpu.SemaphoreType.DMA((n,))],
    )(*bufs)


def _col_tiles(R, C):
    tr = _pick(R, (512, 256, 128))
    if tr != R:
        return tr, C
    if R * C <= 512 * 1024:
        return R, C
    return R, _pick(C, (256, 128))


def _active(layer):
    return (_me()[2] == layer).astype(jnp.int32).reshape(1)


def _pair_sum(name, g, theirs, layer):
    n4, R, C = g.shape
    tr, tc = _col_tiles(R, C)

    def body(a_ref, g_ref, t_ref, o_ref, ob_ref):
        s = g_ref[...] + t_ref[...]
        o_ref[...] = s
        ob_ref[...] = s.astype(BF16)

    spec = pl.BlockSpec((1, tr, tc), lambda d, i, k, a: (d * a[0], i * a[0], k * a[0]))
    return pl.pallas_call(
        body, name=name,
        grid_spec=pltpu.PrefetchScalarGridSpec(num_scalar_prefetch=1, grid=(n4, R // tr, C // tc),
                                               in_specs=[spec, spec], out_specs=[spec, spec]),
        out_shape=[jax.ShapeDtypeStruct((n4, R, C), F32), jax.ShapeDtypeStruct((n4, R, C), BF16)],
        compiler_params=_cparams(("arbitrary", "arbitrary", "arbitrary")),
    )(_active(layer), g, theirs)


def _chip_sum(name, pair, got, j, layer, buf=None):
    _, R, C = pair.shape
    tr, tc = _col_tiles(R, C)

    def body(ja_ref, p_ref, g0, g1, g2, *rest):
        rest[-1][0] = (p_ref[0] + g2[0].astype(F32)) + (g0[0].astype(F32) + g1[0].astype(F32))

    def gspec(k):
        return pl.BlockSpec((1, tr, tc), lambda i, q, ja, k=k: (k, i * ja[1], q * ja[1]))

    in_specs = [pl.BlockSpec((1, tr, tc), lambda i, q, ja: (ja[0], i * ja[1], q * ja[1])), gspec(0), gspec(1), gspec(2)]
    args = [jnp.concatenate([j.reshape(1).astype(jnp.int32), _active(layer)]), pair, got, got, got]
    aliases = {}
    if buf is not None:
        in_specs.append(_ANY)
        args.append(buf)
        aliases = {5: 0}
    return pl.pallas_call(
        body, name=name,
        grid_spec=pltpu.PrefetchScalarGridSpec(
            num_scalar_prefetch=1, grid=(R // tr, C // tc), in_specs=in_specs,
            out_specs=pl.BlockSpec((1, tr, tc), lambda i, q, ja: (layer, i * ja[1], q * ja[1]))),
        out_shape=jax.ShapeDtypeStruct((2, R, C), F32), input_output_aliases=aliases,
        compiler_params=_cparams(("arbitrary", "arbitrary")),
    )(*args)


LANES = 1024
_BIG = (("w_in", "col"), ("ffn_w_in", "col"), ("mem_w_kv", "col"),
        ("w_br_lru", "row"), ("w_br_ssd", "row"), ("w_br_xa", "row"), ("w_out", "row"), ("ffn_w_down", "row"))
_SMALL_SHARDED = ("b_gate", "lru_conv_w", "ssd_conv_w")
_SMALL = ("b_gate", "lru_conv_w", "lru_conv_b", "lru_w_a", "lru_b_a", "lru_w_i", "lru_b_i", "lru_lambda",
          "ssd_conv_w", "ssd_conv_b", "ssd_dt_bias", "ssd_a_log", "ssd_d", "ssd_norm_w",
          "ln1_g", "ln1_b", "ln2_g", "ln2_b")
_W_NAMES = ("w_in", "b_gate", "lru_conv_w", "lru_conv_b", "lru_w_a", "lru_b_a", "lru_w_i", "lru_b_i", "lru_lambda",
            "ssd_conv_w", "ssd_conv_b", "ssd_dt_bias", "ssd_a_log", "ssd_d", "ssd_norm_w", "mem_w_kv", "w_br_lru",
            "w_br_ssd", "w_br_xa", "w_out", "ln1_g", "ln1_b", "ffn_w_in", "ffn_w_down", "ln2_g", "ln2_b")
_IN_ORDER = ((4096, 7168), (8224, 11296), (2048, 4096), (0, 1024), (1024, 2048), (7200, 8224))
_IN_DT = (7168, 7200)


def _flat_rows(parts, row_multiple):
    flat = jnp.concatenate([p.reshape(-1) for p in parts])
    rows = -(-flat.size // LANES)
    rows = -(-rows // row_multiple) * row_multiple
    return jnp.pad(flat, (0, rows * LANES - flat.size)).reshape(rows, LANES)


def _take_parts(flat, shapes):
    out, off = [], 0
    for shp in shapes:
        size = math.prod(shp)
        out.append(flat[off:off + size].reshape(shp))
        off += size
    return out


_GATHER_NAMES = ("w_in", "w_in_edge") + tuple(nm for nm, _ in _BIG[1:]) + _SMALL_SHARDED
_GATHER_KINDS = ("rows",) + ("blk",) * (len(_GATHER_NAMES) - 1)


def _gather_sources(w):
    x, y, _ = _me()
    j = 2 * x + y
    wt = jnp.swapaxes(w["w_in"], 1, 2)
    inner = lax.dynamic_slice_in_dim(wt, 8 * y, W_IN_INNER, axis=1).astype(BF16)
    edge = lax.dynamic_slice_in_dim(wt, (1 - y) * W_IN_INNER, 8, axis=1)
    srcs = [inner, edge] + [w[nm].astype(BF16) for nm, _ in _BIG[1:]] + [w[nm] for nm in _SMALL_SHARDED]
    bufs = []
    for l in range(DEPTH):
        row = []
        for s, kd in zip(srcs, _GATHER_KINDS):
            if kd == "blk":
                row.append(lax.dynamic_update_slice_in_dim(lax.empty((4,) + s.shape[1:], s.dtype), s[l][None], j, axis=0))
            else:
                row.append(lax.dynamic_update_slice_in_dim(lax.empty((N_IN, D_MODEL), s.dtype), s[l],
                                                           j * W_IN_SHARD + 8 * y, axis=0))
        bufs.append(row)
    return srcs, bufs


def _layer_weights(names, got):
    g = dict(zip(names, got))
    full = {}
    if "w_in" in g:
        wt_all, edges = g["w_in"], g["w_in_edge"]
        for a, b in ((0, 1), (2, 3)):
            tile = jnp.concatenate([edges[a], edges[b]], axis=0).astype(BF16)
            wt_all = lax.dynamic_update_slice_in_dim(wt_all, tile, b * W_IN_SHARD - 8, axis=0)
        full["w_re_t"] = jnp.concatenate([wt_all[lo:hi] for lo, hi in _IN_ORDER], axis=0)
        full["w_dt_t"] = jnp.pad(wt_all[_IN_DT[0]:_IN_DT[1]], ((0, DT_PAD - SSD_HEADS), (0, 0)))
    kinds = dict(_BIG)
    for nm in names:
        if nm in ("w_in", "w_in_edge"):
            continue
        _, r, cdim = g[nm].shape
        if kinds.get(nm, "col") == "col":
            full[nm] = jnp.moveaxis(g[nm], 0, 1).reshape(r, 4 * cdim)
        else:
            full[nm] = g[nm].reshape(4 * r, cdim)
    return full


def _per_chip(g, nm):
    r, cdim = g.shape
    if nm == "w_in" or dict(_BIG)[nm] == "row":
        return g.reshape(4, r // 4, cdim)
    return jnp.moveaxis(g.reshape(r, 4, cdim // 4), 1, 0)


def _reduce_small_grads(parts):
    shapes = [p.shape for p in parts]
    buf = _flat_rows(parts, 8)
    sib = _swap_sibling("small_pair_swap", buf)
    pair = _sum_terms("small_pair_sum", [buf, sib], F32)
    got = _send_chips("small_chip_send", pair, False)
    total = _sum_terms("small_chip_sum", [pair, got[2], got[0], got[1]], F32)
    return _take_parts(total.reshape(-1), shapes)


def _layer_fwd(xin, mem2, W, P, consts, riders=None, late=None):
    S = xin.shape[0]
    T, TX = min(256, S), min(512, S)
    sv = {"xin": xin}
    riders = riders or {}
    rout = {}
    proj = _mm("proj", xin, W["w_re_t"], "nt", BF16, rider=riders.get("proj"))
    if "proj" in riders:
        proj, rout["proj"] = proj[0], proj[1:]
    if late is not None:
        w_late, p_late = late(rout)
        W, P = {**W, **w_late}, {**P, **p_late}
    dt_raw = _mm("proj_dt", xin, W["w_dt_t"], "nt", F32)
    xc = _conv_fwd("lru_conv", proj, CB_LRU_X, P["lru_conv_w"], P["lru_conv_b"], False, F32, T)
    res = _lru_fwd("lru_scan", xc, proj, P["lru_w_a"], P["lru_w_i"], P["lru_b_a"], P["lru_b_i"], P["lru_lambda"], T,
                   rider=riders.get("lru_scan"))
    y_lru, h = res[0], res[1]
    if "lru_scan" in riders:
        rout["lru_scan"] = res[2:]
    xact = _conv_fwd("ssd_conv", proj, CB_XBC, P["ssd_conv_w"], P["ssd_conv_b"], True, BF16, T)
    res = _ssd_fwd("ssd_scan", xact, dt_raw, P["dt_bias"], P["a_neg"], P["d_exp"], consts, rider=riders.get("ssd_scan"))
    ycore, prev = res[0], res[1]
    if "ssd_scan" in riders:
        rout["ssd_scan"] = res[2:]
    y_ssd = _gate_norm_fwd("ssd_norm", ycore, proj, P["ssd_norm_w"], T)
    kv = _mm("mem_kv", mem2, W["w_kv"], "nn", BF16)
    y_xa = _xa_fwd("xattn", proj, kv, TX)
    p_l = _mm("br_lru", y_lru, W["w_l"], "nn", BF16)
    p_s = _mm("br_ssd", y_ssd, W["w_s"], "nn", BF16)
    p_x = _mm("br_xa", y_xa, W["w_x"], "nn", BF16)
    merged = _merge_fwd("merge", proj, P["b_gate"], p_l, p_s, p_x, T)
    mix = _mm("mix_out", merged, W["w_o"], "nn", F32)
    x1, xh1, rs1 = _ln_fwd("ln_fwd", xin, mix, P["ln1_g"], P["ln1_b"], T)
    gu = _mm("ffn_in", x1, W["w_fi"], "nn", BF16, rider=riders.get("ffn_in"))
    if "ffn_in" in riders:
        gu, rout["ffn_in"] = gu[0], gu[1:]
    hmid = _swiglu_fwd("swiglu", gu, T)
    f = _mm("ffn_down", hmid, W["w_fd"], "nn", F32)
    x2, xh2, rs2 = _ln_fwd("ln_fwd", x1, f, P["ln2_g"], P["ln2_b"], T)
    sv.update(proj=proj, dt_raw=dt_raw, xc=xc, h=h, y_lru=y_lru, xact=xact, ycore=ycore, prev=prev, y_ssd=y_ssd, kv=kv,
              y_xa=y_xa, p_l=p_l, p_s=p_s, p_x=p_x, merged=merged, x1=x1, xh1=xh1, rs1=rs1, gu=gu, hmid=hmid,
              xh2=xh2, rs2=rs2)
    return x2, sv, rout


def _layer_bwd(dys, coefs, sv, mem2, W, P, consts, riders=None):
    S = sv["xin"].shape[0]
    T, TX = min(256, S), min(512, S)
    proj = sv["proj"]
    g = {}
    rout = {}

    def ride(host):
        return riders[host](g, rout) if riders and host in riders else None

    dz2, g["ln2_g"], g["ln2_b"] = _ln_bwd("ln_bwd_%d" % len(dys), dys, coefs, sv["xh2"], sv["rs2"], P["ln2_g"], T)
    dhmid = _mm("d_hmid", dz2, W["w_fd"], "nt", BF16)
    g["ffn_w_down"] = _mm("dw_ffn_down", sv["hmid"], dz2, "tn", F32)
    dgu = _swiglu_bwd("swiglu_bwd", sv["gu"], dhmid, T)
    dx1f = _mm("d_x1", dgu, W["w_fi"], "nt", F32)
    g["ffn_w_in"] = _mm("dw_ffn_in", sv["x1"], dgu, "tn", F32)
    dz1, g["ln1_g"], g["ln1_b"] = _ln_bwd("ln_bwd_2", [dz2, dx1f], [ALPHA, 1.0], sv["xh1"], sv["rs1"], P["ln1_g"], T)
    dmerged = _mm("d_merged", dz1, W["w_o"], "nt", BF16)
    g["w_out"] = _mm("dw_out", sv["merged"], dz1, "tn", F32)
    dproj, dpl, dps, dpx, dbg0, dbg1, dbg2 = _merge_bwd("merge_bwd", proj, P["b_gate"], sv["p_l"], sv["p_s"], sv["p_x"],
                                                         dmerged, T)
    g["b_gate"] = jnp.concatenate([dbg0, dbg1, dbg2], axis=0)
    dy_lru = _mm("d_ylru", dpl, W["w_l"], "nt", BF16)
    g["w_br_lru"] = _mm("dw_br_lru", sv["y_lru"], dpl, "tn", F32)
    dy_ssd = _mm("d_yssd", dps, W["w_s"], "nt", BF16)
    g["w_br_ssd"] = _mm("dw_br_ssd", sv["y_ssd"], dps, "tn", F32)
    dy_xa = _mm("d_yxa", dpx, W["w_x"], "nt", BF16)
    g["w_br_xa"] = _mm("dw_br_xa", sv["y_xa"], dpx, "tn", F32)
    dproj, dkv = _xa_bwd("xattn_bwd", proj, sv["kv"], dy_xa, dproj, TX)
    g["mem_w_kv"] = _mm("dw_kv", mem2, dkv, "tn", F32)
    dproj, dycore, g["ssd_norm_w"] = _gate_norm_bwd("ssd_norm_bwd", dy_ssd, sv["ycore"], proj, P["ssd_norm_w"], dproj, T)
    res = _ssd_bwd("ssd_scan_bwd", sv["xact"], sv["dt_raw"], P["dt_bias"], P["a_neg"], P["d_exp"],
                   sv["prev"], dycore, consts, rider=ride("ssd_scan_bwd"))
    dxact, ddt, d_a, g_dtb, d_dexp = res[:5]
    rout["ssd_scan_bwd"] = res[5:]
    g["ssd_dt_bias"] = g_dtb[:, :SSD_HEADS]
    g["ssd_a_log"] = d_a[:, :SSD_HEADS] * P["a_neg"][:, :SSD_HEADS]
    g["ssd_d"] = jnp.sum(d_dexp.reshape(SSD_HEADS, SSD_HEAD_DIM), axis=-1)
    dproj, w0, w1, w2, w3, g["ssd_conv_b"] = _conv_bwd("ssd_conv_bwd", proj, CB_XBC, P["ssd_conv_w"], P["ssd_conv_b"], True,
                                                       dxact, dproj, T)
    g["ssd_conv_w"] = jnp.concatenate([w0, w1, w2, w3], axis=0)
    res = _lru_bwd("lru_scan_bwd", dy_lru, sv["xc"], proj, sv["h"], P["lru_w_a"], P["lru_w_i"], P["lru_b_a"], P["lru_b_i"],
                   P["lru_lambda"], dproj, T, rider=ride("lru_scan_bwd"))
    dproj, dxc, g["lru_w_a"], g["lru_w_i"], g["lru_b_a"], g["lru_b_i"], g["lru_lambda"] = res[:7]
    rout["lru_scan_bwd"] = res[7:]
    dproj, w0, w1, w2, w3, g["lru_conv_b"] = _conv_bwd("lru_conv_bwd", proj, CB_LRU_X, P["lru_conv_w"], P["lru_conv_b"], False,
                                                       dxc, dproj, T)
    g["lru_conv_w"] = jnp.concatenate([w0, w1, w2, w3], axis=0)
    xin = sv["xin"]
    rd = ride("dw_in")
    dw_re_t = _mm("dw_in", dproj, xin, "tn", F32, rider=rd)
    if rd is not None:
        dw_re_t, rout["dw_in"] = dw_re_t[0], dw_re_t[1:]
    dw_dt_t = _mm("dw_in_dt", ddt, xin, "tn", F32)
    pieces = {rng: dw_re_t[off:off + rng[1] - rng[0]]
              for rng, off in zip(_IN_ORDER, (0, 3072, 6144, 8192, 9216, 10240))}
    pieces[_IN_DT] = dw_dt_t[:SSD_HEADS]
    g["w_in"] = jnp.concatenate([pieces[k] for k in sorted(pieces)], axis=0)
    rd = ride("d_xin")
    dxp = _mm("d_xin", dproj, W["w_re_t"], "nn", F32, rider=rd)
    if rd is not None:
        dxp, rout["d_xin"] = dxp[0], dxp[1:]
    dxs = _mm("d_xin_dt", ddt, W["w_dt_t"], "nn", F32, add=dxp)
    return [dz1, dxs], [ALPHA, 1.0], g, rout


def _step(a):
    x2d, mem2, target = a["x"][0], a["mem"][0], a["loss_target"][0]
    S = x2d.shape[0]
    T = min(256, S)
    xi, yi, ci = _me()
    j = 2 * xi + yi
    w = {nm: a[nm] for nm in _W_NAMES}
    consts = _ssd_consts()
    row = lambda v: v.reshape(1, -1)
    pad_h = lambda v: jnp.pad(v.reshape(1, -1), ((0, 0), (0, DT_PAD - SSD_HEADS)))

    w_keys = dict(w_re_t="w_re_t", w_dt_t="w_dt_t", w_fi="ffn_w_in", w_kv="mem_w_kv", w_l="w_br_lru", w_s="w_br_ssd",
                  w_x="w_br_xa", w_o="w_out", w_fd="ffn_w_down")

    def gathered_params(full):
        return ({k: full[v] for k, v in w_keys.items() if v in full}, {k: full[k] for k in _SMALL_SHARDED if k in full})

    def local_params(l):
        return dict(
            lru_conv_b=row(w["lru_conv_b"][l]), lru_w_a=w["lru_w_a"][l].astype(BF16), lru_w_i=w["lru_w_i"][l].astype(BF16),
            lru_b_a=row(w["lru_b_a"][l]), lru_b_i=row(w["lru_b_i"][l]), lru_lambda=row(w["lru_lambda"][l]),
            ssd_conv_b=row(w["ssd_conv_b"][l]), dt_bias=pad_h(w["ssd_dt_bias"][l]), a_neg=pad_h(-jnp.exp(w["ssd_a_log"][l])),
            d_exp=jnp.broadcast_to(w["ssd_d"][l][:, None], (SSD_HEADS, SSD_HEAD_DIM)).reshape(1, D_SSD),
            ssd_norm_w=row(w["ssd_norm_w"][l]), ln1_g=row(w["ln1_g"][l]), ln1_b=row(w["ln1_b"][l]),
            ln2_g=row(w["ln2_g"][l]), ln2_b=row(w["ln2_b"][l]))

    srcs, bufs = _gather_sources(w)
    names, kinds = list(_GATHER_NAMES), list(_GATHER_KINDS)
    groups = {"w_in": [0, 1], "mixer": [3, 4, 5, 6, 7, 9, 10, 11], "ffn": [2, 8]}

    def gather(group, layer):
        idx = groups[group]
        return _gather_rider([srcs[i] for i in idx], [kinds[i] for i in idx], [bufs[layer][i] for i in idx], layer)

    def weights_of(group, got):
        return _layer_weights([names[i] for i in groups[group]], got)

    rest = groups["mixer"] + groups["ffn"]
    got = _run_rider("gather_weights", gather("w_in", 0))
    W0, _ = gathered_params(weights_of("w_in", got))

    def late0(rout_):
        wl, pl_ = gathered_params(_layer_weights([names[i] for i in rest], rout_["proj"]))
        return wl, pl_

    rest_rider = _gather_rider([srcs[i] for i in rest], [kinds[i] for i in rest], [bufs[0][i] for i in rest], 0)
    x1, sv0, rout = _layer_fwd(x2d, mem2, W0, local_params(0), consts, late=late0, riders={
        "proj": rest_rider, "lru_scan": gather("mixer", 1), "ssd_scan": gather("w_in", 1), "ffn_in": gather("ffn", 1)})
    full1 = {**weights_of("w_in", rout["ssd_scan"]), **weights_of("mixer", rout["lru_scan"]), **weights_of("ffn", rout["ffn_in"])}
    W1, P1g = gathered_params(full1)
    P1 = {**local_params(1), **P1g}
    w0_late, p0_late = late0(rout)
    W0, P0 = {**W0, **w0_late}, {**local_params(0), **p0_late}
    xcur, sv1, _ = _layer_fwd(x1, mem2, W1, P1, consts)
    dy, loss_part = _loss_fwd_bwd("loss", xcur, target, T)
    loss = lax.psum(loss_part[0, 0], ("x", "y", "c"))

    big_names = [nm for nm, _ in _BIG]
    rest_names = big_names[1:]
    st = {}

    def swap_rider(key, names, layer):
        def build(g, rout_):
            st[key] = [_per_chip(g[nm], nm) for nm in names]
            return _pair_swap_rider(st[key], layer)
        return build

    def pair_sums(key, theirs, layer):
        return [_pair_sum("grads_pair_sum", g_, t_, layer) for g_, t_ in zip(st[key], theirs)]

    dys, coefs, g1, rout1 = _layer_bwd([dy], [1.0], sv1, mem2, W1, P1, consts,
                                       riders={"d_xin": swap_rider("g1", big_names, 1)})
    pairs1 = pair_sums("g1", rout1["d_xin"], 1)

    def send_rest0(g, rout_):
        st["pairs0"] = pair_sums("g0", rout_["lru_scan_bwd"], 0)
        return _chip_send_rider([pb for _, pb in st["pairs0"]], 0)

    dys, coefs, g0, rout0 = _layer_bwd(dys, coefs, sv0, mem2, W0, P0, consts, riders={
        "ssd_scan_bwd": lambda g, r: _chip_send_rider([pb for _, pb in pairs1], 1),
        "lru_scan_bwd": swap_rider("g0", rest_names, 0),
        "dw_in": send_rest0,
        "d_xin": swap_rider("g0_in", ["w_in"], 0)})
    grad_x = _axpy("grad_x", coefs[0], dys[0], dys[1], T)[None]
    pairs0_in = pair_sums("g0_in", rout0["d_xin"], 0)
    got0_in = _run_rider("grads_chip_send", _chip_send_rider([pb for _, pb in pairs0_in], 0))
    pairs0 = pairs0_in + st["pairs0"]
    gots0 = list(got0_in) + list(rout0["dw_in"])
    halves = []
    for (p1, _), gt1, (p0, _), gt0 in zip(pairs1, rout0["ssd_scan_bwd"], pairs0, gots0):
        buf = _chip_sum("grads_chip_sum", p1, gt1, j, 1)
        halves.append(_chip_sum("grads_chip_sum", p0, gt0, j, 0, buf))
    big = dict(zip(big_names, _join_parts_multi("grads_join", halves)))
    big["w_in"] = jnp.swapaxes(big["w_in"], 1, 2)
    layer_grads = [g0, g1]
    stacked = {nm: jnp.stack([layer_grads[l][nm] for l in range(DEPTH)]) for nm in _SMALL}
    small_parts = [stacked[nm].reshape((DEPTH,) + tuple(sh)) for nm, sh in
                   ((nm, (3, D_MODEL) if nm == "b_gate" else (4, D_MODEL) if nm == "lru_conv_w" else
                     (4, D_XBC) if nm == "ssd_conv_w" else w[nm].shape[1:]) for nm in _SMALL)]
    small = dict(zip(_SMALL, _reduce_small_grads(small_parts)))
    for nm in _SMALL_SHARDED:
        cs = w[nm].shape[2]
        small[nm] = lax.dynamic_slice_in_dim(small[nm], j * cs, cs, axis=2)
    grads = {**big, **small}

    delta, new_m, new_v = {}, {}, {}
    for nm, _ in _BIG:
        delta[nm], new_m[nm], new_v[nm] = _adamw("adamw_" + nm, w[nm], grads[nm], a["m_" + nm], a["v_" + nm])
    shapes = [w[nm].shape for nm in _SMALL]
    packs = [_flat_rows([src[nm] for nm in _SMALL], 8)[None] for src in
             (w, grads, {nm: a["m_" + nm] for nm in _SMALL}, {nm: a["v_" + nm] for nm in _SMALL})]
    d_, m_, v_ = _adamw("adamw_small", *packs)
    for dst, buf in ((delta, d_), (new_m, m_), (new_v, v_)):
        dst.update(zip(_SMALL, _take_parts(buf.reshape(-1), shapes)))

    outs = [loss, grad_x]
    for group in (grads, delta, new_m, new_v):
        outs += [group[nm] for nm in _W_NAMES]
    return tuple(outs)


def kernel(x, mem, w_in, b_gate, lru_conv_w, lru_conv_b, lru_w_a, lru_b_a, lru_w_i, lru_b_i, lru_lambda, ssd_conv_w, ssd_conv_b, ssd_dt_bias, ssd_a_log, ssd_d, ssd_norm_w, mem_w_kv, w_br_lru, w_br_ssd, w_br_xa, w_out, ln1_g, ln1_b, ffn_w_in, ffn_w_down, ln2_g, ln2_b, loss_target, m_w_in, m_b_gate, m_lru_conv_w, m_lru_conv_b, m_lru_w_a, m_lru_b_a, m_lru_w_i, m_lru_b_i, m_lru_lambda, m_ssd_conv_w, m_ssd_conv_b, m_ssd_dt_bias, m_ssd_a_log, m_ssd_d, m_ssd_norm_w, m_mem_w_kv, m_w_br_lru, m_w_br_ssd, m_w_br_xa, m_w_out, m_ln1_g, m_ln1_b, m_ffn_w_in, m_ffn_w_down, m_ln2_g, m_ln2_b, v_w_in, v_b_gate, v_lru_conv_w, v_lru_conv_b, v_lru_w_a, v_lru_b_a, v_lru_w_i, v_lru_b_i, v_lru_lambda, v_ssd_conv_w, v_ssd_conv_b, v_ssd_dt_bias, v_ssd_a_log, v_ssd_d, v_ssd_norm_w, v_mem_w_kv, v_w_br_lru, v_w_br_ssd, v_w_br_xa, v_w_out, v_ln1_g, v_ln1_b, v_ffn_w_in, v_ffn_w_down, v_ln2_g, v_ln2_b):
    return _step(dict(locals()))
```

```python
import functools
import math

import jax
import jax.numpy as jnp
from jax import lax
from jax.experimental import pallas as pl
from jax.experimental.pallas import tpu as pltpu

F32, BF16 = jnp.float32, jnp.bfloat16
MESH = pl.DeviceIdType.MESH
VMEM_LIMIT_BYTES = 56 * 2**20
HALO = 16

D_MODEL = 1024
DEPTH = 2
CHUNK = 64
N_MEM = 256
LRU_BLOCKS = 8
LRU_BLOCK = 128
LRU_C = 8.0
D_SSD = 2048
SSD_HEADS = 32
SSD_HEAD_DIM = 64
SSD_GROUPS = 4
SSD_STATE = 128
D_BC = SSD_GROUPS * SSD_STATE
D_XBC = D_SSD + 2 * D_BC
XA_HEADS = 4
XA_HEAD_DIM = 256
D_FF = 2816
ALPHA = (2 * DEPTH) ** 0.25
EPS = 1e-5
N_IN = 11296
N_PROJ = 11264
DT_PAD = 128

ADAM_LR, ADAM_B1, ADAM_B2, ADAM_EPS, ADAM_WD, ADAM_STEP = 0.001, 0.9, 0.999, 1e-08, 0.01, 10

CB_XBC = (3072, 0)
CB_XS, CB_BM, CB_CM = (2048, 0), (512, 4), (512, 5)
CB_LOGITS = (3072, 1)
CB_G0, CB_G1, CB_G2 = (1024, 3), (1024, 4), (1024, 5)
CB_Z = (2048, 3)
CB_LRU_X, CB_LRU_GATE, CB_XA_Q = (1024, 8), (1024, 9), (1024, 10)


def _cparams(sem):
    return pltpu.CompilerParams(dimension_semantics=sem, vmem_limit_bytes=VMEM_LIMIT_BYTES)


MID_STEP_PCT = 65


class _Rider:
    def __init__(self, ins, alias, outs, sems, start, finish, mid=None):
        self.ins, self.alias, self.outs, self.sems = list(ins), list(alias), list(outs), list(sems)
        self.start, self.finish = start, finish
        self.mid = mid

    def operands(self):
        return self.ins + self.alias

    def out_shapes(self):
        return [jax.ShapeDtypeStruct(a.shape, a.dtype) for a in self.alias] + self.outs

    def aliases(self, n_in, n_out):
        return {n_in + len(self.ins) + q: n_out + q for q in range(len(self.alias))}

    def split(self, refs, n_in, n_out, n_scratch):
        ni, na, no = len(self.ins), len(self.alias), len(self.outs)
        main = list(refs[:n_in])
        r_in = refs[n_in:n_in + ni]
        p = n_in + ni + na
        main += refs[p:p + n_out]
        r_alias = refs[p + n_out:p + n_out + na]
        r_out = refs[p + n_out + na:p + n_out + na + no]
        p = p + n_out + na + no
        main += refs[p:p + n_scratch]
        sems = refs[p + n_scratch:]
        return main, (r_in, r_alias, r_out, sems)


def _run_rider(name, rider):
    def body(*refs):
        _, parts = rider.split(refs, 0, 0, 0)
        rider.start(*parts)
        if rider.mid is not None:
            rider.mid(*parts)
        rider.finish(*parts)

    n_ops = len(rider.operands())
    return pl.pallas_call(
        body, name=name, in_specs=[_ANY] * n_ops, out_specs=[_ANY] * len(rider.out_shapes()),
        out_shape=rider.out_shapes(), input_output_aliases=rider.aliases(0, 0), scratch_shapes=rider.sems,
    )(*rider.operands())


def _pcall(name, body, n, ins, outs, scratch=(), reverse=False, aliases=None, rider=None):
    def ridx(i):
        return (n - 1 - i) if reverse else i

    in_specs, args = [], []
    for sp in ins:
        kind, arr = sp[0], sp[1]
        if kind == "row":
            _, _, tile, width, cb = sp
            in_specs.append(pl.BlockSpec((tile, width), lambda i, cb=cb: (ridx(i), cb)))
        elif kind == "prev":
            _, _, tile, width, cb = sp
            t = tile // HALO
            in_specs.append(pl.BlockSpec((HALO, width), lambda i, cb=cb, t=t: (jnp.maximum(ridx(i) * t - 1, 0), cb)))
        elif kind == "next":
            _, _, tile, width, cb = sp
            t = tile // HALO
            last = arr.shape[0] // HALO - 1
            in_specs.append(pl.BlockSpec((HALO, width), lambda i, cb=cb, t=t, last=last: (jnp.minimum((ridx(i) + 1) * t, last), cb)))
        elif kind == "lead":
            nd = arr.ndim
            in_specs.append(pl.BlockSpec((sp[2],) + arr.shape[1:], lambda i, nd=nd: (ridx(i),) + (0,) * (nd - 1)))
        elif kind == "full":
            nd = arr.ndim
            in_specs.append(pl.BlockSpec(arr.shape, lambda i, nd=nd: (0,) * nd))
        elif kind == "any":
            in_specs.append(pl.BlockSpec(memory_space=pl.ANY))
        else:
            raise ValueError(kind)
        args.append(arr)
    out_specs, out_shape = [], []
    for sp in outs:
        kind = sp[0]
        if kind == "row":
            _, rows, cols, dtype, tile, width, cb = sp
            out_shape.append(jax.ShapeDtypeStruct((rows, cols), dtype))
            out_specs.append(pl.BlockSpec((tile, width), lambda i, cb=cb: (ridx(i), cb)))
        elif kind == "lead":
            _, shape, dtype, lead = sp
            nd = len(shape)
            out_shape.append(jax.ShapeDtypeStruct(shape, dtype))
            out_specs.append(pl.BlockSpec((lead,) + tuple(shape[1:]), lambda i, nd=nd: (ridx(i),) + (0,) * (nd - 1)))
        elif kind == "full":
            _, shape, dtype = sp
            nd = len(shape)
            out_shape.append(jax.ShapeDtypeStruct(shape, dtype))
            out_specs.append(pl.BlockSpec(tuple(shape), lambda i, nd=nd: (0,) * nd))
        else:
            raise ValueError(kind)
    aliases = dict(aliases or {})
    scratch = list(scratch)
    kernel_body = body
    if rider is not None:
        n_in, n_out, n_scratch = len(args), len(out_shape), len(scratch)

        def kernel_body(*refs):
            main, parts = rider.split(refs, n_in, n_out, n_scratch)
            i = pl.program_id(0)

            @pl.when(i == 0)
            def _():
                rider.start(*parts)

            if rider.mid is not None:
                @pl.when(i == (n * MID_STEP_PCT) // 100)
                def _():
                    rider.mid(*parts)

            body(*main)

            @pl.when(i == n - 1)
            def _():
                rider.finish(*parts)

        aliases.update(rider.aliases(n_in, n_out))
        args += rider.operands()
        in_specs += [_ANY] * len(rider.operands())
        out_shape += rider.out_shapes()
        out_specs += [_ANY] * len(rider.out_shapes())
        scratch += rider.sems
    res = pl.pallas_call(
        kernel_body, name=name, grid=(n,), in_specs=in_specs, out_specs=out_specs, out_shape=out_shape,
        scratch_shapes=scratch, input_output_aliases=aliases,
        compiler_params=_cparams(("arbitrary",)),
    )(*args)
    return res


def _pick(n, cands):
    for c in cands:
        if n % c == 0:
            return c
    return n


_MM_TILES = (1024, 1408, 512, 256, 128)


def _mm(name, a, b, mode, out_dtype, add=None, rider=None):
    if mode == "nn":
        (M, K), (K2, N) = a.shape, b.shape
    elif mode == "nt":
        (M, K), (N, K2) = a.shape, b.shape
    else:
        (K, M), (K2, N) = a.shape, b.shape
    assert K == K2, (name, a.shape, b.shape)
    tm = _pick(M, _MM_TILES)
    tn = _pick(N, _MM_TILES)
    if mode == "tn":
        tk = _pick(K, (1024, 512, 256))
    else:
        tk = K if K <= 2816 else _pick(K, _MM_TILES)
    nk = K // tk
    has_add = add is not None

    def body(*refs):
        if has_add:
            a_ref, b_ref, add_ref, o_ref, acc_ref = refs
        else:
            a_ref, b_ref, o_ref, acc_ref = refs
        k = pl.program_id(2)
        av = a_ref[...].astype(BF16)
        bv = b_ref[...].astype(BF16)
        if mode == "nn":
            p = jnp.dot(av, bv, preferred_element_type=F32)
        elif mode == "nt":
            p = lax.dot_general(av, bv, (((1,), (1,)), ((), ())), preferred_element_type=F32)
        else:
            p = lax.dot_general(av, bv, (((0,), (0,)), ((), ())), preferred_element_type=F32)

        def fin(v):
            if has_add:
                v = v + add_ref[...].astype(F32)
            o_ref[...] = v.astype(out_dtype)

        if nk == 1:
            fin(p)
        else:
            @pl.when(k == 0)
            def _():
                acc_ref[...] = p

            @pl.when(k > 0)
            def _():
                acc_ref[...] += p

            @pl.when(k == nk - 1)
            def _():
                fin(acc_ref[...])

    if mode == "nn":
        specs = [pl.BlockSpec((tm, tk), lambda i, j, k: (i, k)), pl.BlockSpec((tk, tn), lambda i, j, k: (k, j))]
    elif mode == "nt":
        specs = [pl.BlockSpec((tm, tk), lambda i, j, k: (i, k)), pl.BlockSpec((tn, tk), lambda i, j, k: (j, k))]
    else:
        specs = [pl.BlockSpec((tk, tm), lambda i, j, k: (k, i)), pl.BlockSpec((tk, tn), lambda i, j, k: (k, j))]
    args = [a, b]
    if has_add:
        specs.append(pl.BlockSpec((tm, tn), lambda i, j, k: (i, j)))
        args.append(add)
    acc_shape = (tm, tn) if nk > 1 else (8, 128)
    grid = (M // tm, N // tn, nk)
    out_spec = pl.BlockSpec((tm, tn), lambda i, j, k: (i, j))
    out_shape = jax.ShapeDtypeStruct((M, N), out_dtype)
    if rider is None:
        return pl.pallas_call(
            body, name=name, grid=grid, in_specs=specs, out_specs=out_spec, out_shape=out_shape,
            scratch_shapes=[pltpu.VMEM(acc_shape, F32)],
            compiler_params=_cparams(("parallel", "parallel", "arbitrary")),
        )(*args)
    n_in = len(args)

    def kernel_body(*refs):
        main, parts = rider.split(refs, n_in, 1, 1)
        i, j, k = pl.program_id(0), pl.program_id(1), pl.program_id(2)

        @pl.when((i == 0) & (j == 0) & (k == 0))
        def _():
            rider.start(*parts)

        if rider.mid is not None:
            flat = (grid[0] * grid[1] * grid[2] * MID_STEP_PCT) // 100
            mi, mj, mk = flat // (grid[1] * grid[2]), (flat // grid[2]) % grid[1], flat % grid[2]

            @pl.when((i == mi) & (j == mj) & (k == mk))
            def _():
                rider.mid(*parts)

        body(*main)

        @pl.when((i == grid[0] - 1) & (j == grid[1] - 1) & (k == grid[2] - 1))
        def _():
            rider.finish(*parts)

    return pl.pallas_call(
        kernel_body, name=name, grid=grid, in_specs=specs + [_ANY] * len(rider.operands()),
        out_specs=[out_spec] + [_ANY] * len(rider.out_shapes()), out_shape=[out_shape] + rider.out_shapes(),
        input_output_aliases=rider.aliases(n_in, 1), scratch_shapes=[pltpu.VMEM(acc_shape, F32)] + rider.sems,
        compiler_params=_cparams(("arbitrary", "arbitrary", "arbitrary")),
    )(*args, *rider.operands())


def _sigmoid(x):
    return 1.0 / (1.0 + jnp.exp(-x))


def _silu(x):
    return x * _sigmoid(x)


def _dsilu(x):
    s = _sigmoid(x)
    return s * (1.0 + x * (1.0 - s))


def _softplus(x):
    return jnp.maximum(x, 0.0) + jnp.log(1.0 + jnp.exp(-jnp.abs(x)))


_GELU_C = math.sqrt(2.0 / math.pi)


def _gelu(x):
    return 0.5 * x * (1.0 + jnp.tanh(_GELU_C * (x + 0.044715 * x * x * x)))


def _dgelu(x):
    t = jnp.tanh(_GELU_C * (x + 0.044715 * x * x * x))
    return 0.5 * (1.0 + t) + 0.5 * x * (1.0 - t * t) * _GELU_C * (1.0 + 3.0 * 0.044715 * x * x)


def _acc(ref, i, val):
    @pl.when(i == 0)
    def _():
        ref[...] = val

    @pl.when(i > 0)
    def _():
        ref[...] += val


def _rows(shape):
    return lax.broadcasted_iota(jnp.int32, shape, 0)


def _shift_down(x, k, halo8, first):
    r = pltpu.roll(x, k, 0)
    h = pltpu.roll(halo8, k, 0)
    h = jnp.where(first, 0.0, h)
    head = jnp.where(_rows(h.shape) < k, h, r[:8])
    if x.shape[0] == 8:
        return head
    return jnp.concatenate([head, r[8:]], axis=0)


def _shift_up(x, k, halo8, last):
    T = x.shape[0]
    r = pltpu.roll(x, T - k, 0)
    h = pltpu.roll(halo8, 8 - k, 0)
    h = jnp.where(last, 0.0, h)
    tail = jnp.where(_rows(h.shape) >= 8 - k, h, r[T - 8:])
    return jnp.concatenate([r[:T - 8], tail], axis=0)


def _ln_fwd(name, a, b, g, beta, tile):
    S, Dm = a.shape
    n = S // tile

    def body(a_ref, b_ref, g_ref, be_ref, y_ref, xh_ref, rs_ref):
        z = ALPHA * a_ref[...] + b_ref[...].astype(F32)
        mu = jnp.mean(z, axis=-1, keepdims=True)
        zc = z - mu
        var = jnp.mean(zc * zc, axis=-1, keepdims=True)
        rstd = lax.rsqrt(var + EPS)
        xh = zc * rstd
        y_ref[...] = xh * g_ref[...] + be_ref[...]
        xh_ref[...] = xh
        rs_ref[...] = rstd

    return _pcall(name, body, n,
                  [("row", a, tile, Dm, 0), ("row", b, tile, Dm, 0), ("full", g), ("full", beta)],
                  [("row", S, Dm, F32, tile, Dm, 0), ("row", S, Dm, F32, tile, Dm, 0), ("row", S, 1, F32, tile, 1, 0)])


def _ln_bwd(name, dys, coefs, xh, rstd, g, tile):
    S, Dm = xh.shape
    n = S // tile
    nd = len(dys)

    def body(*refs):
        dy_refs = refs[:nd]
        xh_ref, rs_ref, g_ref, dz_ref, dg_ref, db_ref = refs[nd:]
        i = pl.program_id(0)
        dy = coefs[0] * dy_refs[0][...].astype(F32)
        for k in range(1, nd):
            dy = dy + coefs[k] * dy_refs[k][...].astype(F32)
        xh_v = xh_ref[...]
        dxh = dy * g_ref[...]
        m1 = jnp.mean(dxh, axis=-1, keepdims=True)
        m2 = jnp.mean(dxh * xh_v, axis=-1, keepdims=True)
        dz_ref[...] = rs_ref[...] * (dxh - m1 - xh_v * m2)
        _acc(dg_ref, i, jnp.sum(dy * xh_v, axis=0, keepdims=True))
        _acc(db_ref, i, jnp.sum(dy, axis=0, keepdims=True))

    ins = [("row", d, tile, Dm, 0) for d in dys]
    ins += [("row", xh, tile, Dm, 0), ("row", rstd, tile, 1, 0), ("full", g)]
    return _pcall(name, body, n, ins,
                  [("row", S, Dm, F32, tile, Dm, 0), ("full", (1, Dm), F32), ("full", (1, Dm), F32)])


def _loss_fwd_bwd(name, y, target, tile):
    S, Dm = y.shape
    n = S // tile

    def body(y_ref, t_ref, dy_ref, l_ref):
        i = pl.program_id(0)
        err = y_ref[...] - t_ref[...]
        dy_ref[...] = err * (1.0 / Dm)
        part = jnp.sum(jnp.sum(err * err, axis=-1, keepdims=True), axis=0, keepdims=True) * (0.5 / Dm)
        _acc(l_ref, i, part)

    return _pcall(name, body, n, [("row", y, tile, Dm, 0), ("row", target, tile, Dm, 0)],
                  [("row", S, Dm, F32, tile, Dm, 0), ("full", (1, 1), F32)])


def _swiglu_fwd(name, gu, tile):
    S = gu.shape[0]
    n = S // tile

    def body(g_ref, u_ref, o_ref):
        o_ref[...] = (_silu(g_ref[...].astype(F32)) * u_ref[...].astype(F32)).astype(BF16)

    return _pcall(name, body, n, [("row", gu, tile, D_FF, 0), ("row", gu, tile, D_FF, 1)],
                  [("row", S, D_FF, BF16, tile, D_FF, 0)])[0]


def _swiglu_bwd(name, gu, dh, tile):
    S = gu.shape[0]
    n = S // tile

    def body(g_ref, u_ref, dh_ref, o_ref):
        gv = g_ref[...].astype(F32)
        uv = u_ref[...].astype(F32)
        dv = dh_ref[...].astype(F32)
        dg = dv * uv * _dsilu(gv)
        du = dv * _silu(gv)
        o_ref[...] = jnp.concatenate([dg, du], axis=1).astype(BF16)

    return _pcall(name, body, n, [("row", gu, tile, D_FF, 0), ("row", gu, tile, D_FF, 1), ("row", dh, tile, D_FF, 0)],
                  [("row", S, 2 * D_FF, BF16, tile, 2 * D_FF, 0)])[0]


def _merge_fwd(name, proj, b_gate, pl_, ps_, px_, tile):
    S = proj.shape[0]
    n = S // tile
    Dm = D_MODEL

    def body(l0, l1, l2, bg, p0, p1, p2, o_ref):
        bgv = bg[...]
        acc = _sigmoid(l0[...].astype(F32) + bgv[0:1]) * p0[...].astype(F32)
        acc = acc + _sigmoid(l1[...].astype(F32) + bgv[1:2]) * p1[...].astype(F32)
        acc = acc + _sigmoid(l2[...].astype(F32) + bgv[2:3]) * p2[...].astype(F32)
        o_ref[...] = acc.astype(BF16)

    ins = [("row", proj, tile, *CB_G0), ("row", proj, tile, *CB_G1), ("row", proj, tile, *CB_G2), ("full", b_gate),
           ("row", pl_, tile, Dm, 0), ("row", ps_, tile, Dm, 0), ("row", px_, tile, Dm, 0)]
    return _pcall(name, body, n, ins, [("row", S, Dm, BF16, tile, Dm, 0)])[0]


def _merge_bwd(name, proj, b_gate, pl_, ps_, px_, dmerged, tile):
    S = proj.shape[0]
    n = S // tile
    Dm = D_MODEL

    def body(l0, l1, l2, bg, p0, p1, p2, dm_ref, dproj_ref, d0, d1, d2, db0, db1, db2):
        i = pl.program_id(0)
        bgv = bg[...]
        dm = dm_ref[...].astype(F32)
        dls = []
        for k, (lr, pr, dr, dbr) in enumerate(((l0, p0, d0, db0), (l1, p1, d1, db1), (l2, p2, d2, db2))):
            gk = _sigmoid(lr[...].astype(F32) + bgv[k:k + 1])
            dr[...] = (dm * gk).astype(BF16)
            dl = dm * pr[...].astype(F32) * gk * (1.0 - gk)
            _acc(dbr, i, jnp.sum(dl, axis=0, keepdims=True))
            dls.append(dl)
        dproj_ref[...] = jnp.concatenate(dls, axis=1).astype(BF16)

    ins = [("row", proj, tile, *CB_G0), ("row", proj, tile, *CB_G1), ("row", proj, tile, *CB_G2), ("full", b_gate),
           ("row", pl_, tile, Dm, 0), ("row", ps_, tile, Dm, 0), ("row", px_, tile, Dm, 0), ("row", dmerged, tile, Dm, 0)]
    outs = [("row", S, N_PROJ, BF16, tile, *CB_LOGITS)] + [("row", S, Dm, BF16, tile, Dm, 0)] * 3 + [("full", (1, Dm), F32)] * 3
    return _pcall(name, body, n, ins, outs)


def _conv_taps(xf, halo8, first, w):
    out = xf * w[3:4]
    for k in (1, 2, 3):
        out = out + _shift_down(xf, k, halo8, first) * w[3 - k:4 - k]
    return out


def _conv_fwd(name, src, cb, w, b, act, out_dtype, tile):
    S = src.shape[0]
    C = cb[0]
    n = S // tile

    def body(x_ref, p_ref, w_ref, b_ref, o_ref):
        i = pl.program_id(0)
        xf = x_ref[...].astype(F32)
        halo8 = p_ref[...].astype(F32)[HALO - 8:]
        pre = _conv_taps(xf, halo8, i == 0, w_ref[...]) + b_ref[...]
        o_ref[...] = (_silu(pre) if act else pre).astype(out_dtype)

    return _pcall(name, body, n, [("row", src, tile, *cb), ("prev", src, tile, *cb), ("full", w), ("full", b)],
                  [("row", S, C, out_dtype, tile, C, 0)])[0]


def _conv_bwd(name, src, cb, w, b, act, dout, dproj, tile, rider=None):
    S = src.shape[0]
    C = cb[0]
    n = S // tile

    def body(x_ref, p_ref, xn_ref, w_ref, b_ref, d_ref, dn_ref, buf_ref, dx_ref, dw0, dw1, dw2, dw3, db_ref):
        i = pl.program_id(0)
        first, last = i == 0, i == n - 1
        xf = x_ref[...].astype(F32)
        halo8 = p_ref[...].astype(F32)[HALO - 8:]
        wv = w_ref[...]
        dv = d_ref[...].astype(F32)
        nx8 = dn_ref[...].astype(F32)[:8]
        xs = [xf] + [_shift_down(xf, k, halo8, first) for k in (1, 2, 3)]
        if act:
            bv = b_ref[...]
            pre = xs[0] * wv[3:4] + xs[1] * wv[2:3] + xs[2] * wv[1:2] + xs[3] * wv[0:1] + bv
            dv = dv * _dsilu(pre)
            xn8 = xn_ref[...].astype(F32)[:8]
            nx8 = nx8 * _dsilu(_conv_taps(xn8, xf[tile - 8:], False, wv) + bv)
        dx = dv * wv[3:4]
        for k in (1, 2, 3):
            dx = dx + _shift_up(dv, k, nx8, last) * wv[3 - k:4 - k]
        dx_ref[...] = dx.astype(BF16)
        for k, dwr in ((0, dw3), (1, dw2), (2, dw1), (3, dw0)):
            _acc(dwr, i, jnp.sum(dv * xs[k], axis=0, keepdims=True))
        _acc(db_ref, i, jnp.sum(dv, axis=0, keepdims=True))

    ins = [("row", src, tile, *cb), ("prev", src, tile, *cb), ("next", src, tile, *cb), ("full", w), ("full", b),
           ("row", dout, tile, C, 0), ("next", dout, tile, C, 0), ("any", dproj)]
    outs = [("row", S, N_PROJ, BF16, tile, *cb)] + [("full", (1, C), F32)] * 5
    return _pcall(name, body, n, ins, outs, aliases={7: 0}, rider=rider)


def _lru_gates(xc, wa_ref, wi_ref, ba, bi, lam):
    xb = xc.astype(BF16)
    pa, pi_ = [], []
    for nb in range(LRU_BLOCKS):
        sl = slice(nb * LRU_BLOCK, (nb + 1) * LRU_BLOCK)
        pa.append(jnp.dot(xb[:, sl], wa_ref[nb], preferred_element_type=F32))
        pi_.append(jnp.dot(xb[:, sl], wi_ref[nb], preferred_element_type=F32))
    r = _sigmoid(jnp.concatenate(pa, axis=1) + ba)
    ig = _sigmoid(jnp.concatenate(pi_, axis=1) + bi)
    sp = _softplus(-lam)
    a = jnp.exp(-LRU_C * r * sp)
    m = jnp.sqrt(1.0 - a * a)
    return xb, r, ig, sp, a, m


def _lru_fwd(name, xc, proj, wa, wi, ba, bi, lam, tile, rider=None):
    S = xc.shape[0]
    n = S // tile
    C = D_MODEL

    def body(xc_ref, gate_ref, wa_ref, wi_ref, ba_ref, bi_ref, lam_ref, y_ref, h_ref, carry):
        i = pl.program_id(0)

        @pl.when(i == 0)
        def _():
            carry[...] = jnp.zeros_like(carry)

        xcv = xc_ref[...]
        _, r, ig, sp, a, m = _lru_gates(xcv, wa_ref, wi_ref, ba_ref[...], bi_ref[...], lam_ref[...])
        u = m * (ig * xcv)
        rows = _rows(a.shape)
        d = 1
        while d < tile:
            keep = rows >= d
            a_s = jnp.where(keep, pltpu.roll(a, d, 0), 1.0)
            u_s = jnp.where(keep, pltpu.roll(u, d, 0), 0.0)
            u = a * u_s + u
            a = a * a_s
            d *= 2
        h = u + a * carry[0:1, :]
        h_ref[...] = h
        carry[0:1, :] = h_ref[pl.ds(tile - 1, 1), :]
        y_ref[...] = (_gelu(gate_ref[...].astype(F32)) * h).astype(BF16)

    ins = [("row", xc, tile, C, 0), ("row", proj, tile, *CB_LRU_GATE), ("full", wa), ("full", wi),
           ("full", ba), ("full", bi), ("full", lam)]
    return _pcall(name, body, n, ins, [("row", S, C, BF16, tile, C, 0), ("row", S, C, F32, tile, C, 0)],
                  scratch=[pltpu.VMEM((8, C), F32)], rider=rider)


def _lru_bwd(name, dy, xc, proj, h, wa, wi, ba, bi, lam, dproj, tile, rider=None):
    S = xc.shape[0]
    n = S // tile
    C = D_MODEL

    def body(dy_ref, xc_ref, gate_ref, h_ref, hp_ref, wa_ref, wi_ref, ba_ref, bi_ref, lam_ref, buf_ref,
             dg_ref, dxc_ref, dwa_ref, dwi_ref, dba_ref, dbi_ref, dlam_ref, carry):
        i = pl.program_id(0)
        first_tile = i == n - 1

        @pl.when(i == 0)
        def _():
            carry[...] = jnp.zeros_like(carry)

        xcv = xc_ref[...]
        lamv = lam_ref[...]
        xb, r, ig, sp, a, m = _lru_gates(xcv, wa_ref, wi_ref, ba_ref[...], bi_ref[...], lamv)
        hv = h_ref[...]
        gv = gate_ref[...].astype(F32)
        dyv = dy_ref[...].astype(F32)
        dg_ref[...] = (dyv * hv * _dgelu(gv)).astype(BF16)
        v = dyv * _gelu(gv)
        rows = _rows(a.shape)
        bcoef = jnp.where(rows == tile - 1, 1.0, pltpu.roll(a, tile - 1, 0))
        d = 1
        while d < tile:
            keep = rows < tile - d
            b_s = jnp.where(keep, pltpu.roll(bcoef, tile - d, 0), 1.0)
            v_s = jnp.where(keep, pltpu.roll(v, tile - d, 0), 0.0)
            v = v + bcoef * v_s
            bcoef = bcoef * b_s
            d *= 2
        dH = v + bcoef * carry[0:1, :]
        dxc_ref[...] = dH
        carry[0:1, :] = dxc_ref[pl.ds(0, 1), :] * a[0:1, :]
        halo8 = hp_ref[...][HALO - 8:]
        hprev = _shift_down(hv, 1, halo8, first_tile)
        da = dH * hprev
        ix = ig * xcv
        dm = dH * ix
        di = dH * m * xcv
        dxc = dH * m * ig
        da = da - dm * a / m
        dla = da * a
        dr = dla * (-LRU_C) * sp
        _acc(dlam_ref, i, jnp.sum(dla * (-LRU_C) * r, axis=0, keepdims=True) * (-_sigmoid(-lamv)))
        dpa = dr * r * (1.0 - r)
        dpi = di * ig * (1.0 - ig)
        _acc(dba_ref, i, jnp.sum(dpa, axis=0, keepdims=True))
        _acc(dbi_ref, i, jnp.sum(dpi, axis=0, keepdims=True))
        dpab, dpib = dpa.astype(BF16), dpi.astype(BF16)
        back = []
        for nb in range(LRU_BLOCKS):
            sl = slice(nb * LRU_BLOCK, (nb + 1) * LRU_BLOCK)
            back.append(lax.dot_general(dpab[:, sl], wa_ref[nb], (((1,), (1,)), ((), ())), preferred_element_type=F32)
                        + lax.dot_general(dpib[:, sl], wi_ref[nb], (((1,), (1,)), ((), ())), preferred_element_type=F32))
            ga = lax.dot_general(xb[:, sl], dpab[:, sl], (((0,), (0,)), ((), ())), preferred_element_type=F32)
            gi = lax.dot_general(xb[:, sl], dpib[:, sl], (((0,), (0,)), ((), ())), preferred_element_type=F32)

            @pl.when(i == 0)
            def _(ga=ga, gi=gi, nb=nb):
                dwa_ref[nb] = ga
                dwi_ref[nb] = gi

            @pl.when(i > 0)
            def _(ga=ga, gi=gi, nb=nb):
                dwa_ref[nb] += ga
                dwi_ref[nb] += gi

        dxc_ref[...] = dxc + jnp.concatenate(back, axis=1)

    ins = [("row", dy, tile, C, 0), ("row", xc, tile, C, 0), ("row", proj, tile, *CB_LRU_GATE), ("row", h, tile, C, 0),
           ("prev", h, tile, C, 0), ("full", wa), ("full", wi), ("full", ba), ("full", bi), ("full", lam), ("any", dproj)]
    outs = [("row", S, N_PROJ, BF16, tile, *CB_LRU_GATE), ("row", S, C, F32, tile, C, 0),
            ("full", (LRU_BLOCKS, LRU_BLOCK, LRU_BLOCK), F32), ("full", (LRU_BLOCKS, LRU_BLOCK, LRU_BLOCK), F32),
            ("full", (1, C), F32), ("full", (1, C), F32), ("full", (1, C), F32)]
    return _pcall(name, body, n, ins, outs, scratch=[pltpu.VMEM((8, C), F32)], reverse=True, aliases={10: 0}, rider=rider)


SSD_STEP = 4

def _split3(x):
    h = x.astype(BF16)
    r = x - h.astype(F32)
    m = r.astype(BF16)
    lo = (r - m.astype(F32)).astype(BF16)
    return h, m, lo


def _dot01_r(x, e):
    h = x.astype(BF16)
    m = (x - h.astype(F32)).astype(BF16)
    return jnp.dot(h, e, preferred_element_type=F32) + jnp.dot(m, e, preferred_element_type=F32)


def _dot01_l(e, x):
    h, m, lo = _split3(x)
    return (jnp.dot(e, h, preferred_element_type=F32) + jnp.dot(e, m, preferred_element_type=F32)
            + jnp.dot(e, lo, preferred_element_type=F32))


def _ssd_consts():
    hh = lax.broadcasted_iota(jnp.int32, (DT_PAD, D_SSD), 0)
    cc = lax.broadcasted_iota(jnp.int32, (DT_PAD, D_SSD), 1)
    e = (cc // SSD_HEAD_DIM == hh).astype(BF16)
    rows = SSD_STEP * CHUNK
    li = lax.broadcasted_iota(jnp.int32, (rows, rows), 0)
    si = lax.broadcasted_iota(jnp.int32, (rows, rows), 1)
    ltri = ((li >= si) & (li // CHUNK == si // CHUNK)).astype(BF16)
    l4 = lax.broadcasted_iota(jnp.int32, (CHUNK, 4 * CHUNK), 0)
    s4 = lax.broadcasted_iota(jnp.int32, (CHUNK, 4 * CHUNK), 1) % CHUNK
    itile = (l4 == s4).astype(F32)
    causal = (l4 >= s4).astype(F32)
    j4 = lax.broadcasted_iota(jnp.int32, (8, 4 * CHUNK), 0)
    c4 = lax.broadcasted_iota(jnp.int32, (8, 4 * CHUNK), 1) // CHUNK
    hmask = (j4 == c4).astype(F32)
    return e, e.T, ltri, ltri.T, itile, causal, hmask


def _ssd_chunk_common(xs_ref, bm_ref, cm_ref, dt_ref, dtb_ref, a_ref, e_ref, ltri_ref):
    xs = xs_ref[...].astype(F32)
    raw = dt_ref[...] + dtb_ref[...]
    dtv = _softplus(raw)
    da = dtv * a_ref[...]
    cs = _dot01_l(ltri_ref[...], da)
    e = e_ref[...]
    dte = _dot01_r(dtv, e)
    ce = _dot01_r(cs, e)
    xdt = xs * dte
    ecs = jnp.exp(ce)
    return xs, raw, dtv, cs, dte, ce, xdt, ecs


def _quad_terms(ce_q, cb4, itile, causal):
    cr = jnp.sum(ce_q * itile, axis=0, keepdims=True)
    seg = ce_q - cr
    dec = jnp.where(causal > 0.0, jnp.exp(jnp.minimum(seg, 0.0)), 0.0)
    return dec, cb4 * dec


def _block_diag4(xq, hmask):
    return jnp.concatenate([xq * hmask[j:j + 1].astype(xq.dtype) for j in range(4)], axis=0)


def _ssd_fwd(name, xbc, dt_raw, dt_bias, a_neg, d_exp, consts, rider=None):
    S = xbc.shape[0]
    nc = S // CHUNK
    e, et, ltri, ltri_t, itile, causal, hmask = consts

    def body(xs_ref, bm_ref, cm_ref, dt_ref, dtb_ref, a_ref, dex_ref, e_ref, ltri_ref, it_ref, ca_ref, hm_ref,
             y_ref, prev_ref, hst):
        i = pl.program_id(0)

        @pl.when(i == 0)
        def _():
            hst[...] = jnp.zeros_like(hst)

        xs, raw, dtv, cs, dte, ce_all, xdt_all, ecs_all = _ssd_chunk_common(
            xs_ref, bm_ref, cm_ref, dt_ref, dtb_ref, a_ref, e_ref, ltri_ref)
        itile_v, causal_v, hmask_v = it_ref[...], ca_ref[...], hm_ref[...]
        dskip_all = dex_ref[...] * xs
        for k in range(SSD_STEP):
            rs = slice(k * CHUNK, (k + 1) * CHUNK)
            ce, xdt, ecs, dskip = ce_all[rs], xdt_all[rs], ecs_all[rs], dskip_all[rs]
            cle = ce[CHUNK - 1:CHUNK, :]
            xdtb = xdt.astype(BF16)
            xst = (xdt * jnp.exp(cle - ce)).astype(BF16)
            ecl = jnp.exp(cle)
            bm = bm_ref[rs, :]
            cm = cm_ref[rs, :]
            for g in range(SSD_GROUPS):
                gs = slice(g * 512, (g + 1) * 512)
                ns = slice(g * SSD_STATE, (g + 1) * SSD_STATE)
                bm_g, cm_g = bm[:, ns], cm[:, ns]
                hprev = hst[g]
                hprev_b = hprev.astype(BF16)
                prev_ref[k, g] = hprev_b
                yoff = jnp.dot(cm_g, hprev_b, preferred_element_type=F32) * ecs[:, gs]
                st = lax.dot_general(bm_g, xst[:, gs], (((0,), (0,)), ((), ())), preferred_element_type=F32)
                hst[g] = hprev * ecl[:, gs] + st
                b4 = jnp.concatenate([bm_g] * 4, axis=0)
                cb4 = lax.dot_general(cm_g, b4, (((1,), (1,)), ((), ())), preferred_element_type=F32)
                for q in range(2):
                    cols = slice(g * 512 + q * 256, g * 512 + (q + 1) * 256)
                    _, mq = _quad_terms(ce[:, cols], cb4, itile_v, causal_v)
                    xbd = _block_diag4(xdtb[:, cols], hmask_v)
                    ydiag = jnp.dot(mq.astype(BF16), xbd, preferred_element_type=F32)
                    y_ref[rs, cols] = ydiag + yoff[:, q * 256:(q + 1) * 256] + dskip[:, cols]

    T = SSD_STEP * CHUNK
    ins = [("row", xbc, T, *CB_XS), ("row", xbc, T, *CB_BM), ("row", xbc, T, *CB_CM),
           ("row", dt_raw, T, DT_PAD, 0), ("full", dt_bias), ("full", a_neg), ("full", d_exp),
           ("full", e), ("full", ltri), ("full", itile), ("full", causal), ("full", hmask)]
    outs = [("row", S, D_SSD, F32, T, D_SSD, 0), ("lead", (nc, SSD_GROUPS, SSD_STATE, 512), BF16, SSD_STEP)]
    return _pcall(name, body, S // T, ins, outs, scratch=[pltpu.VMEM((SSD_GROUPS, SSD_STATE, 512), F32)], rider=rider)


def _ssd_bwd(name, xbc, dt_raw, dt_bias, a_neg, d_exp, prev, dy, consts, rider=None):
    S = xbc.shape[0]
    nc = S // CHUNK
    e, et, ltri, ltri_t, itile, causal, hmask = consts

    def body(xs_ref, bm_ref, cm_ref, dt_ref, dtb_ref, a_ref, dex_ref, prev_ref, dy_ref,
             e_ref, et_ref, ltri_ref, ltt_ref, it_ref, ca_ref, hm_ref,
             dx_ref, ddt_ref, da_ref, dbias_ref, dd_ref, dh, dce_ref, dxdt_ref):
        i = pl.program_id(0)

        @pl.when(i == 0)
        def _():
            dh[...] = jnp.zeros_like(dh)

        xs, raw, dtv, cs, dte, ce_all, xdt_all, ecs_all = _ssd_chunk_common(
            xs_ref, bm_ref, cm_ref, dt_ref, dtb_ref, a_ref, e_ref, ltri_ref)
        itile_v, causal_v, hmask_v = it_ref[...], ca_ref[...], hm_ref[...]
        dyv = dy_ref[...]
        last_row = _rows((CHUNK, 512)) == CHUNK - 1
        for k in reversed(range(SSD_STEP)):
            rs = slice(k * CHUNK, (k + 1) * CHUNK)
            ce, xdt, ecs = ce_all[rs], xdt_all[rs], ecs_all[rs]
            cle = ce[CHUNK - 1:CHUNK, :]
            dend = jnp.exp(cle - ce)
            xdtb = xdt.astype(BF16)
            xst = (xdt * dend).astype(BF16)
            ecl = jnp.exp(cle)
            bm = bm_ref[rs, :]
            cm = cm_ref[rs, :]
            for g in range(SSD_GROUPS):
                gs = slice(g * 512, (g + 1) * 512)
                ns = slice(g * SSD_STATE, (g + 1) * SSD_STATE)
                bm_g, cm_g = bm[:, ns], cm[:, ns]
                hprev_b = prev_ref[k, g]
                dhn = dh[g]
                dhn_b = dhn.astype(BF16)
                dy_g = dyv[rs, gs]
                ecs_g, dend_g, xdt_g, ecl_g = ecs[:, gs], dend[:, gs], xdt[:, gs], ecl[:, gs]
                z = jnp.dot(cm_g, hprev_b, preferred_element_type=F32)
                dz = dy_g * ecs_g
                dzb = dz.astype(BF16)
                dce_g = dz * z
                dcm_g = lax.dot_general(dzb, hprev_b, (((1,), (1,)), ((), ())), preferred_element_type=F32)
                dprev = lax.dot_general(cm_g, dzb, (((0,), (0,)), ((), ())), preferred_element_type=F32) + dhn * ecl_g
                dcl = jnp.sum(dhn * hprev_b.astype(F32), axis=0, keepdims=True) * ecl_g
                gmat = jnp.dot(bm_g, dhn_b, preferred_element_type=F32)
                dbm_g = lax.dot_general(xst[:, gs], dhn_b, (((1,), (1,)), ((), ())), preferred_element_type=F32)
                dxdt_g = gmat * dend_g
                t = gmat * xdt_g * dend_g
                dce_g = dce_g - t
                dcl = dcl + jnp.sum(t, axis=0, keepdims=True)
                dce_g = dce_g + jnp.where(last_row, dcl, 0.0)
                dh[g] = dprev
                b4 = jnp.concatenate([bm_g] * 4, axis=0)
                cb4 = lax.dot_general(cm_g, b4, (((1,), (1,)), ((), ())), preferred_element_type=F32)
                for q in range(2):
                    qs = slice(q * 256, (q + 1) * 256)
                    cols = slice(g * 512 + q * 256, g * 512 + (q + 1) * 256)
                    dec, mq = _quad_terms(ce[:, cols], cb4, itile_v, causal_v)
                    mqb = mq.astype(BF16)
                    xbd = _block_diag4(xdtb[:, cols], hmask_v)
                    dyq = dy_g[:, qs].astype(BF16)
                    dm = lax.dot_general(dyq, xbd, (((1,), (1,)), ((), ())), preferred_element_type=F32)
                    rmat = lax.dot_general(mqb, dyq, (((0,), (0,)), ((), ())), preferred_element_type=F32)
                    dxq = rmat[0:64] * hmask_v[0:1]
                    for j in range(1, 4):
                        dxq = dxq + rmat[64 * j:64 * (j + 1)] * hmask_v[j:j + 1]
                    tq = dm * dec
                    tqb = tq.astype(BF16)
                    dcm_g = dcm_g + jnp.dot(tqb, b4, preferred_element_type=F32)
                    rb = lax.dot_general(tqb, cm_g, (((0,), (0,)), ((), ())), preferred_element_type=F32)
                    dbm_g = dbm_g + rb[0:64] + rb[64:128] + rb[128:192] + rb[192:256]
                    dseg = tq * cb4
                    colsum = jnp.sum(dseg, axis=0, keepdims=True)
                    dce_ref[rs, cols] = dce_g[:, qs] + dseg - itile_v * colsum
                    dxdt_ref[rs, cols] = dxdt_g[:, qs] + dxq
                dx_ref[rs, D_SSD + g * SSD_STATE:D_SSD + (g + 1) * SSD_STATE] = dbm_g
                dx_ref[rs, D_SSD + D_BC + g * SSD_STATE:D_SSD + D_BC + (g + 1) * SSD_STATE] = dcm_g
        dxdt = dxdt_ref[...]
        dexv = dex_ref[...]
        dx_ref[:, 0:D_SSD] = dxdt * dte + dyv * dexv
        _acc(dd_ref, i, jnp.sum(dyv * xs, axis=0, keepdims=True))
        etv = et_ref[...]
        dcs = _dot01_r(dce_ref[...], etv)
        dda = _dot01_l(ltt_ref[...], dcs)
        av = a_ref[...]
        ddtv = dda * av + _dot01_r(dxdt * xs, etv)
        _acc(da_ref, i, jnp.sum(dda * dtv, axis=0, keepdims=True))
        draw = ddtv * _sigmoid(raw)
        ddt_ref[...] = draw.astype(BF16)
        _acc(dbias_ref, i, jnp.sum(draw, axis=0, keepdims=True))

    T = SSD_STEP * CHUNK
    ins = [("row", xbc, T, *CB_XS), ("row", xbc, T, *CB_BM), ("row", xbc, T, *CB_CM),
           ("row", dt_raw, T, DT_PAD, 0), ("full", dt_bias), ("full", a_neg), ("full", d_exp),
           ("lead", prev, SSD_STEP), ("row", dy, T, D_SSD, 0),
           ("full", e), ("full", et), ("full", ltri), ("full", ltri_t), ("full", itile), ("full", causal), ("full", hmask)]
    outs = [("row", S, D_XBC, F32, T, D_XBC, 0), ("row", S, DT_PAD, BF16, T, DT_PAD, 0),
            ("full", (1, DT_PAD), F32), ("full", (1, DT_PAD), F32), ("full", (1, D_SSD), F32)]
    scratch = [pltpu.VMEM((SSD_GROUPS, SSD_STATE, 512), F32), pltpu.VMEM((T, D_SSD), F32), pltpu.VMEM((T, D_SSD), F32)]
    return _pcall(name, body, S // T, ins, outs, scratch=scratch, reverse=True, rider=rider)


def _gate_norm_fwd(name, ycore, proj, norm_w, tile):
    S = ycore.shape[0]
    n = S // tile

    def body(y_ref, z_ref, w_ref, o_ref):
        y2 = y_ref[...] * _silu(z_ref[...].astype(F32))
        wv = w_ref[...]
        for g in range(SSD_GROUPS):
            gs = slice(g * 512, (g + 1) * 512)
            seg = y2[:, gs]
            r = lax.rsqrt(jnp.mean(seg * seg, axis=-1, keepdims=True) + EPS)
            o_ref[:, gs] = (seg * r * wv[:, gs]).astype(BF16)

    return _pcall(name, body, n, [("row", ycore, tile, D_SSD, 0), ("row", proj, tile, *CB_Z), ("full", norm_w)],
                  [("row", S, D_SSD, BF16, tile, D_SSD, 0)])[0]


def _gate_norm_bwd(name, dout, ycore, proj, norm_w, dproj, tile):
    S = ycore.shape[0]
    n = S // tile

    def body(do_ref, y_ref, z_ref, w_ref, buf_ref, dz_ref, dy_ref, dw_ref):
        i = pl.program_id(0)
        yv = y_ref[...]
        zv = z_ref[...].astype(F32)
        sz = _silu(zv)
        y2 = yv * sz
        dov = do_ref[...].astype(F32)
        wv = w_ref[...]
        dws, dy2s = [], []
        for g in range(SSD_GROUPS):
            gs = slice(g * 512, (g + 1) * 512)
            seg = y2[:, gs]
            r = lax.rsqrt(jnp.mean(seg * seg, axis=-1, keepdims=True) + EPS)
            yn = seg * r
            dws.append(jnp.sum(dov[:, gs] * yn, axis=0, keepdims=True))
            dyn = dov[:, gs] * wv[:, gs]
            dy2s.append(r * (dyn - yn * jnp.mean(dyn * yn, axis=-1, keepdims=True)))
        dy2 = jnp.concatenate(dy2s, axis=1)
        dy_ref[...] = dy2 * sz
        dz_ref[...] = (dy2 * yv * _dsilu(zv)).astype(BF16)
        _acc(dw_ref, i, jnp.concatenate(dws, axis=1))

    ins = [("row", dout, tile, D_SSD, 0), ("row", ycore, tile, D_SSD, 0), ("row", proj, tile, *CB_Z), ("full", norm_w),
           ("any", dproj)]
    outs = [("row", S, N_PROJ, BF16, tile, *CB_Z), ("row", S, D_SSD, F32, tile, D_SSD, 0), ("full", (1, D_SSD), F32)]
    return _pcall(name, body, n, ins, outs, aliases={4: 0})


_XA_SCALE = XA_HEAD_DIM ** -0.5
_NT = (((1,), (1,)), ((), ()))
_TN = (((0,), (0,)), ((), ()))


def _xa_probs(qh, kh):
    s = lax.dot_general(qh, kh, _NT, preferred_element_type=F32) * _XA_SCALE
    s = s - jnp.max(s, axis=-1, keepdims=True)
    p = jnp.exp(s)
    return p / jnp.sum(p, axis=-1, keepdims=True)


def _xa_fwd(name, proj, kv, tile):
    S = proj.shape[0]
    n = S // tile
    Dh = XA_HEAD_DIM

    def body(q_ref, kv_ref, o_ref):
        for hd in range(XA_HEADS):
            qh = q_ref[:, hd * Dh:(hd + 1) * Dh]
            kh = kv_ref[:, hd * Dh:(hd + 1) * Dh]
            vh = kv_ref[:, D_MODEL + hd * Dh:D_MODEL + (hd + 1) * Dh]
            p = _xa_probs(qh, kh)
            o_ref[:, hd * Dh:(hd + 1) * Dh] = jnp.dot(p.astype(BF16), vh, preferred_element_type=F32).astype(BF16)

    return _pcall(name, body, n, [("row", proj, tile, *CB_XA_Q), ("full", kv)],
                  [("row", S, D_MODEL, BF16, tile, D_MODEL, 0)])[0]


def _xa_bwd(name, proj, kv, dout, dproj, tile):
    S = proj.shape[0]
    n = S // tile
    Dh = XA_HEAD_DIM

    def body(q_ref, kv_ref, do_ref, buf_ref, dq_ref, dkv_ref):
        i = pl.program_id(0)
        for hd in range(XA_HEADS):
            ks_ = slice(hd * Dh, (hd + 1) * Dh)
            vs_ = slice(D_MODEL + hd * Dh, D_MODEL + (hd + 1) * Dh)
            qh = q_ref[:, ks_]
            kh = kv_ref[:, ks_]
            vh = kv_ref[:, vs_]
            doh = do_ref[:, ks_].astype(BF16)
            p = _xa_probs(qh, kh)
            pb = p.astype(BF16)
            dp = lax.dot_general(doh, vh, _NT, preferred_element_type=F32)
            dv = lax.dot_general(pb, doh, _TN, preferred_element_type=F32)
            ds = (p * (dp - jnp.sum(dp * p, axis=-1, keepdims=True)) * _XA_SCALE).astype(BF16)
            dq_ref[:, ks_] = jnp.dot(ds, kh, preferred_element_type=F32).astype(BF16)
            dk = lax.dot_general(ds, qh, _TN, preferred_element_type=F32)

            @pl.when(i == 0)
            def _(dk=dk, dv=dv, ks_=ks_, vs_=vs_):
                dkv_ref[:, ks_] = dk
                dkv_ref[:, vs_] = dv

            @pl.when(i > 0)
            def _(dk=dk, dv=dv, ks_=ks_, vs_=vs_):
                dkv_ref[:, ks_] += dk
                dkv_ref[:, vs_] += dv

    ins = [("row", proj, tile, *CB_XA_Q), ("full", kv), ("row", dout, tile, D_MODEL, 0), ("any", dproj)]
    outs = [("row", S, N_PROJ, BF16, tile, *CB_XA_Q), ("full", (N_MEM, 2 * D_MODEL), F32)]
    return _pcall(name, body, n, ins, outs, aliases={3: 0})


def _adamw(name, w, g, m, v):
    L, R, C = w.shape
    tile = _pick(R, [t for t in (256, 128, 64, 32, 16, 8) if t * C <= 128 * 2048])
    bc1 = 1.0 - ADAM_B1 ** ADAM_STEP
    bc2 = 1.0 - ADAM_B2 ** ADAM_STEP

    def body(w_ref, g_ref, m_ref, v_ref, d_ref, nm_ref, nv_ref):
        gv = g_ref[...]
        mn = ADAM_B1 * m_ref[...] + (1.0 - ADAM_B1) * gv
        vn = ADAM_B2 * v_ref[...] + (1.0 - ADAM_B2) * (gv * gv)
        nm_ref[...] = mn
        nv_ref[...] = vn
        d_ref[...] = -ADAM_LR * ((mn / bc1) / (jnp.sqrt(vn / bc2) + ADAM_EPS) + ADAM_WD * w_ref[...])

    spec = pl.BlockSpec((1, tile, C), lambda l, i: (l, i, 0))
    return pl.pallas_call(
        body, name=name, grid=(L, R // tile), in_specs=[spec] * 4, out_specs=[spec] * 3,
        out_shape=[jax.ShapeDtypeStruct((L, R, C), F32)] * 3,
        compiler_params=_cparams(("arbitrary", "arbitrary")),
    )(w, g, m, v)


def _axpy(name, coef, a, b, tile):
    S, C = a.shape
    n = S // tile

    def body(a_ref, b_ref, o_ref):
        o_ref[...] = coef * a_ref[...].astype(F32) + b_ref[...].astype(F32)

    return _pcall(name, body, n, [("row", a, tile, C, 0), ("row", b, tile, C, 0)], [("row", S, C, F32, tile, C, 0)])[0]


def _sum_terms(name, terms, out_dtype):
    R, C = terms[0].shape
    tile = _pick(R, (512, 256, 128, 64, 32, 16, 8))
    n = R // tile
    nt = len(terms)

    def body(*refs):
        vals = [r[...].astype(F32) for r in refs[:nt]]
        while len(vals) > 1:
            vals = [vals[k] + vals[k + 1] for k in range(0, len(vals), 2)]
        refs[nt][...] = vals[0].astype(out_dtype)

    return _pcall(name, body, n, [("row", t, tile, C, 0) for t in terms], [("row", R, C, out_dtype, tile, C, 0)])[0]


def _me():
    return lax.axis_index("x"), lax.axis_index("y"), lax.axis_index("c")


def _other_chips(x, y):
    return [(1 - x, y), (x, 1 - y), (1 - x, 1 - y)]


_ANY = pl.BlockSpec(memory_space=pl.ANY)


def _swap_sibling(name, src):
    def body(src_ref, out_ref, send_sem, recv_sem):
        x, y, c = _me()
        cp = pltpu.make_async_remote_copy(src_ref=src_ref, dst_ref=out_ref, send_sem=send_sem, recv_sem=recv_sem,
                                          device_id=(x, y, 1 - c), device_id_type=MESH)
        cp.start()
        cp.wait()

    return pl.pallas_call(
        body, name=name, in_specs=[_ANY], out_specs=_ANY, out_shape=jax.ShapeDtypeStruct(src.shape, src.dtype),
        scratch_shapes=[pltpu.SemaphoreType.DMA, pltpu.SemaphoreType.DMA],
    )(src)


def _send_chips(name, src, per_chip):
    shape = src.shape[1:] if per_chip else src.shape

    def body(src_ref, out_ref, send_sems, recv_sems):
        x, y, c = _me()
        cps = []
        for k, (cx, cy) in enumerate(_other_chips(x, y)):
            s = src_ref.at[2 * cx + cy] if per_chip else src_ref
            cps.append(pltpu.make_async_remote_copy(src_ref=s, dst_ref=out_ref.at[k], send_sem=send_sems.at[k],
                                                    recv_sem=recv_sems.at[k], device_id=(cx, cy, c), device_id_type=MESH))
        for cp in cps:
            cp.start()
        for cp in cps:
            cp.wait()

    return pl.pallas_call(
        body, name=name, in_specs=[_ANY], out_specs=_ANY, out_shape=jax.ShapeDtypeStruct((3,) + shape, src.dtype),
        scratch_shapes=[pltpu.SemaphoreType.DMA((3,)), pltpu.SemaphoreType.DMA((3,))],
    )(src)


W_IN_SHARD = N_IN // 4
W_IN_INNER = W_IN_SHARD - 8


def _win_rows(chip_x, chip_y):
    start = (2 * chip_x + chip_y) * W_IN_SHARD + 8 * chip_y
    return pl.ds(pl.multiple_of(start, 2 * 8), W_IN_INNER)


def _dma_sems(*shape):
    return [pltpu.SemaphoreType.DMA(shape), pltpu.SemaphoreType.DMA(shape)]


def _gather_rider(srcs, kinds, bufs, part):
    n = len(srcs)

    def copy(refs, i, k, cx, cy, to, own=False):
        src_refs, out_refs, _, (send_sems, recv_sems) = refs
        d = out_refs[i].at[2 * cx + cy] if kinds[i] == "blk" else out_refs[i].at[_win_rows(cx, cy)]
        return pltpu.make_async_remote_copy(src_ref=src_refs[i].at[part] if own else d, dst_ref=d,
                                            send_sem=send_sems.at[i, k], recv_sem=recv_sems.at[i, k],
                                            device_id=to, device_id_type=MESH)

    def start(*refs):
        x, y, c = _me()

        @pl.when(c == part)
        def _():
            for k, (cx, cy) in enumerate(_other_chips(x, y)):
                for i in range(n):
                    copy(refs, i, k, x, y, (cx, cy, part), own=True).start()

    def mid(*refs):
        x, y, c = _me()

        @pl.when(c == part)
        def _():
            for k, (cx, cy) in enumerate(_other_chips(x, y)):
                for i in range(n):
                    copy(refs, i, k, cx, cy, (x, y, part)).wait_recv()
                    copy(refs, i, 3 + k, cx, cy, (x, y, 1 - part)).start()

    def finish(*refs):
        x, y, c = _me()
        chips = _other_chips(x, y)

        @pl.when(c == part)
        def _():
            for k, (cx, cy) in enumerate(chips):
                for i in range(n):
                    copy(refs, i, k, x, y, (cx, cy, part), own=True).wait_send()
                    copy(refs, i, 3 + k, cx, cy, (x, y, 1 - part)).wait_send()

        @pl.when(c != part)
        def _():
            for k, (cx, cy) in enumerate(chips):
                for i in range(n):
                    copy(refs, i, 3 + k, cx, cy, (x, y, 1 - part)).wait_recv()

    return _Rider(srcs, bufs, [], _dma_sems(n, 6), start, finish, mid)


def _pair_swap_rider(gs, layer):
    n = len(gs)

    def copy(refs, i):
        in_refs, _, out_refs, (send_sems, recv_sems) = refs
        x, y, _c = _me()
        return pltpu.make_async_remote_copy(src_ref=in_refs[i], dst_ref=out_refs[i], send_sem=send_sems.at[i],
                                            recv_sem=recv_sems.at[i], device_id=(x, y, layer), device_id_type=MESH)

    def start(*refs):
        @pl.when(_me()[2] != layer)
        def _():
            for i in range(n):
                copy(refs, i).start()

    def finish(*refs):
        @pl.when(_me()[2] != layer)
        def _():
            for i in range(n):
                copy(refs, i).wait_send()

        @pl.when(_me()[2] == layer)
        def _():
            for i in range(n):
                copy(refs, i).wait_recv()

    return _Rider(gs, [], [jax.ShapeDtypeStruct(g.shape, g.dtype) for g in gs], _dma_sems(n), start, finish)


def _chip_send_rider(pairs, layer):
    n = len(pairs)

    def copies(refs):
        in_refs, _, out_refs, (send_sems, recv_sems) = refs
        x, y, _c = _me()
        return [pltpu.make_async_remote_copy(src_ref=in_refs[i].at[2 * cx + cy], dst_ref=out_refs[i].at[k],
                                             send_sem=send_sems.at[i, k], recv_sem=recv_sems.at[i, k],
                                             device_id=(cx, cy, layer), device_id_type=MESH)
                for k, (cx, cy) in enumerate(_other_chips(x, y)) for i in range(n)]

    def start(*refs):
        @pl.when(_me()[2] == layer)
        def _():
            for cp in copies(refs):
                cp.start()

    def finish(*refs):
        @pl.when(_me()[2] == layer)
        def _():
            for cp in copies(refs):
                cp.wait()

    outs = [jax.ShapeDtypeStruct((3,) + p.shape[1:], p.dtype) for p in pairs]
    return _Rider(pairs, [], outs, _dma_sems(n, 3), start, finish)


def _join_parts_multi(name, bufs):
    n = len(bufs)

    def body(*refs):
        out_refs = refs[n:2 * n]
        send_sems, recv_sems = refs[2 * n:]
        x, y, c = _me()

        def copy(i, part):
            return pltpu.make_async_remote_copy(src_ref=out_refs[i].at[part], dst_ref=out_refs[i].at[part],
                                                send_sem=send_sems.at[i], recv_sem=recv_sems.at[i],
                                                device_id=(x, y, 1 - c), device_id_type=MESH)

        for i in range(n):
            copy(i, c).start()
        for i in range(n):
            copy(i, c).wait_send()
            copy(i, 1 - c).wait_recv()

    return pl.pallas_call(
        body, name=name, in_specs=[_ANY] * n, out_specs=[_ANY] * n,
        out_shape=[jax.ShapeDtypeStruct(b.shape, b.dtype) for b in bufs],
        input_output_aliases={i: i for i in range(n)},
        scratch_shapes=[pltpu.SemaphoreType.DMA((n,)), pltpu.SemaphoreType.DMA((n,))],
    )(*bufs)


def _col_tiles(R, C):
    tr = _pick(R, (512, 256, 128))
    if tr != R:
        return tr, C
    if R * C <= 512 * 1024:
        return R, C
    return R, _pick(C, (256, 128))


def _active(layer):
    return (_me()[2] == layer).astype(jnp.int32).reshape(1)


def _pair_sum(name, g, theirs, layer):
    n4, R, C = g.shape
    tr, tc = _col_tiles(R, C)

    def body(a_ref, g_ref, t_ref, o_ref, ob_ref):
        s = g_ref[...] + t_ref[...]
        o_ref[...] = s
        ob_ref[...] = s.astype(BF16)

    spec = pl.BlockSpec((1, tr, tc), lambda d, i, k, a: (d * a[0], i * a[0], k * a[0]))
    return pl.pallas_call(
        body, name=name,
        grid_spec=pltpu.PrefetchScalarGridSpec(num_scalar_prefetch=1, grid=(n4, R // tr, C // tc),
                                               in_specs=[spec, spec], out_specs=[spec, spec]),
        out_shape=[jax.ShapeDtypeStruct((n4, R, C), F32), jax.ShapeDtypeStruct((n4, R, C), BF16)],
        compiler_params=_cparams(("arbitrary", "arbitrary", "arbitrary")),
    )(_active(layer), g, theirs)


def _chip_sum(name, pair, got, j, layer, buf=None):
    _, R, C = pair.shape
    tr, tc = _col_tiles(R, C)

    def body(ja_ref, p_ref, g0, g1, g2, *rest):
        rest[-1][0] = (p_ref[0] + g2[0].astype(F32)) + (g0[0].astype(F32) + g1[0].astype(F32))

    def gspec(k):
        return pl.BlockSpec((1, tr, tc), lambda i, q, ja, k=k: (k, i * ja[1], q * ja[1]))

    in_specs = [pl.BlockSpec((1, tr, tc), lambda i, q, ja: (ja[0], i * ja[1], q * ja[1])), gspec(0), gspec(1), gspec(2)]
    args = [jnp.concatenate([j.reshape(1).astype(jnp.int32), _active(layer)]), pair, got, got, got]
    aliases = {}
    if buf is not None:
        in_specs.append(_ANY)
        args.append(buf)
        aliases = {5: 0}
    return pl.pallas_call(
        body, name=name,
        grid_spec=pltpu.PrefetchScalarGridSpec(
            num_scalar_prefetch=1, grid=(R // tr, C // tc), in_specs=in_specs,
            out_specs=pl.BlockSpec((1, tr, tc), lambda i, q, ja: (layer, i * ja[1], q * ja[1]))),
        out_shape=jax.ShapeDtypeStruct((2, R, C), F32), input_output_aliases=aliases,
        compiler_params=_cparams(("arbitrary", "arbitrary")),
    )(*args)


LANES = 1024
_BIG = (("w_in", "col"), ("ffn_w_in", "col"), ("mem_w_kv", "col"),
        ("w_br_lru", "row"), ("w_br_ssd", "row"), ("w_br_xa", "row"), ("w_out", "row"), ("ffn_w_down", "row"))
_SMALL_SHARDED = ("b_gate", "lru_conv_w", "ssd_conv_w")
_SMALL = ("b_gate", "lru_conv_w", "lru_conv_b", "lru_w_a", "lru_b_a", "lru_w_i", "lru_b_i", "lru_lambda",
          "ssd_conv_w", "ssd_conv_b", "ssd_dt_bias", "ssd_a_log", "ssd_d", "ssd_norm_w",
          "ln1_g", "ln1_b", "ln2_g", "ln2_b")
_W_NAMES = ("w_in", "b_gate", "lru_conv_w", "lru_conv_b", "lru_w_a", "lru_b_a", "lru_w_i", "lru_b_i", "lru_lambda",
            "ssd_conv_w", "ssd_conv_b", "ssd_dt_bias", "ssd_a_log", "ssd_d", "ssd_norm_w", "mem_w_kv", "w_br_lru",
            "w_br_ssd", "w_br_xa", "w_out", "ln1_g", "ln1_b", "ffn_w_in", "ffn_w_down", "ln2_g", "ln2_b")
_IN_ORDER = ((4096, 7168), (8224, 11296), (2048, 4096), (0, 1024), (1024, 2048), (7200, 8224))
_IN_DT = (7168, 7200)


def _flat_rows(parts, row_multiple):
    flat = jnp.concatenate([p.reshape(-1) for p in parts])
    rows = -(-flat.size // LANES)
    rows = -(-rows // row_multiple) * row_multiple
    return jnp.pad(flat, (0, rows * LANES - flat.size)).reshape(rows, LANES)


def _take_parts(flat, shapes):
    out, off = [], 0
    for shp in shapes:
        size = math.prod(shp)
        out.append(flat[off:off + size].reshape(shp))
        off += size
    return out


_GATHER_NAMES = ("w_in", "w_in_edge") + tuple(nm for nm, _ in _BIG[1:]) + _SMALL_SHARDED
_GATHER_KINDS = ("rows",) + ("blk",) * (len(_GATHER_NAMES) - 1)


def _gather_sources(w):
    x, y, _ = _me()
    j = 2 * x + y
    wt = jnp.swapaxes(w["w_in"], 1, 2)
    inner = lax.dynamic_slice_in_dim(wt, 8 * y, W_IN_INNER, axis=1).astype(BF16)
    edge = lax.dynamic_slice_in_dim(wt, (1 - y) * W_IN_INNER, 8, axis=1)
    srcs = [inner, edge] + [w[nm].astype(BF16) for nm, _ in _BIG[1:]] + [w[nm] for nm in _SMALL_SHARDED]
    bufs = []
    for l in range(DEPTH):
        row = []
        for s, kd in zip(srcs, _GATHER_KINDS):
            if kd == "blk":
                row.append(lax.dynamic_update_slice_in_dim(lax.empty((4,) + s.shape[1:], s.dtype), s[l][None], j, axis=0))
            else:
                row.append(lax.dynamic_update_slice_in_dim(lax.empty((N_IN, D_MODEL), s.dtype), s[l],
                                                           j * W_IN_SHARD + 8 * y, axis=0))
        bufs.append(row)
    return srcs, bufs


def _layer_weights(names, got):
    g = dict(zip(names, got))
    full = {}
    if "w_in" in g:
        wt_all, edges = g["w_in"], g["w_in_edge"]
        for a, b in ((0, 1), (2, 3)):
            tile = jnp.concatenate([edges[a], edges[b]], axis=0).astype(BF16)
            wt_all = lax.dynamic_update_slice_in_dim(wt_all, tile, b * W_IN_SHARD - 8, axis=0)
        full["w_re_t"] = jnp.concatenate([wt_all[lo:hi] for lo, hi in _IN_ORDER], axis=0)
        full["w_dt_t"] = jnp.pad(wt_all[_IN_DT[0]:_IN_DT[1]], ((0, DT_PAD - SSD_HEADS), (0, 0)))
    kinds = dict(_BIG)
    for nm in names:
        if nm in ("w_in", "w_in_edge"):
            continue
        _, r, cdim = g[nm].shape
        if kinds.get(nm, "col") == "col":
            full[nm] = jnp.moveaxis(g[nm], 0, 1).reshape(r, 4 * cdim)
        else:
            full[nm] = g[nm].reshape(4 * r, cdim)
    return full


def _per_chip(g, nm):
    r, cdim = g.shape
    if nm == "w_in" or dict(_BIG)[nm] == "row":
        return g.reshape(4, r // 4, cdim)
    return jnp.moveaxis(g.reshape(r, 4, cdim // 4), 1, 0)


def _reduce_small_grads(parts):
    shapes = [p.shape for p in parts]
    buf = _flat_rows(parts, 8)
    sib = _swap_sibling("small_pair_swap", buf)
    pair = _sum_terms("small_pair_sum", [buf, sib], F32)
    got = _send_chips("small_chip_send", pair, False)
    total = _sum_terms("small_chip_sum", [pair, got[2], got[0], got[1]], F32)
    return _take_parts(total.reshape(-1), shapes)


def _layer_fwd(xin, mem2, W, P, consts, riders=None, late=None):
    S = xin.shape[0]
    T, TX = min(256, S), min(512, S)
    sv = {"xin": xin}
    riders = riders or {}
    rout = {}
    proj = _mm("proj", xin, W["w_re_t"], "nt", BF16, rider=riders.get("proj"))
    if "proj" in riders:
        proj, rout["proj"] = proj[0], proj[1:]
    if late is not None:
        w_late, p_late = late(rout)
        W, P = {**W, **w_late}, {**P, **p_late}
    dt_raw = _mm("proj_dt", xin, W["w_dt_t"], "nt", F32)
    xc = _conv_fwd("lru_conv", proj, CB_LRU_X, P["lru_conv_w"], P["lru_conv_b"], False, F32, T)
    res = _lru_fwd("lru_scan", xc, proj, P["lru_w_a"], P["lru_w_i"], P["lru_b_a"], P["lru_b_i"], P["lru_lambda"], T,
                   rider=riders.get("lru_scan"))
    y_lru, h = res[0], res[1]
    if "lru_scan" in riders:
        rout["lru_scan"] = res[2:]
    xact = _conv_fwd("ssd_conv", proj, CB_XBC, P["ssd_conv_w"], P["ssd_conv_b"], True, BF16, T)
    res = _ssd_fwd("ssd_scan", xact, dt_raw, P["dt_bias"], P["a_neg"], P["d_exp"], consts, rider=riders.get("ssd_scan"))
    ycore, prev = res[0], res[1]
    if "ssd_scan" in riders:
        rout["ssd_scan"] = res[2:]
    y_ssd = _gate_norm_fwd("ssd_norm", ycore, proj, P["ssd_norm_w"], T)
    kv = _mm("mem_kv", mem2, W["w_kv"], "nn", BF16)
    y_xa = _xa_fwd("xattn", proj, kv, TX)
    p_l = _mm("br_lru", y_lru, W["w_l"], "nn", BF16)
    p_s = _mm("br_ssd", y_ssd, W["w_s"], "nn", BF16)
    p_x = _mm("br_xa", y_xa, W["w_x"], "nn", BF16)
    merged = _merge_fwd("merge", proj, P["b_gate"], p_l, p_s, p_x, T)
    mix = _mm("mix_out", merged, W["w_o"], "nn", F32)
    x1, xh1, rs1 = _ln_fwd("ln_fwd", xin, mix, P["ln1_g"], P["ln1_b"], T)
    gu = _mm("ffn_in", x1, W["w_fi"], "nn", BF16, rider=riders.get("ffn_in"))
    if "ffn_in" in riders:
        gu, rout["ffn_in"] = gu[0], gu[1:]
    hmid = _swiglu_fwd("swiglu", gu, T)
    f = _mm("ffn_down", hmid, W["w_fd"], "nn", F32)
    x2, xh2, rs2 = _ln_fwd("ln_fwd", x1, f, P["ln2_g"], P["ln2_b"], T)
    sv.update(proj=proj, dt_raw=dt_raw, xc=xc, h=h, y_lru=y_lru, xact=xact, ycore=ycore, prev=prev, y_ssd=y_ssd, kv=kv,
              y_xa=y_xa, p_l=p_l, p_s=p_s, p_x=p_x, merged=merged, x1=x1, xh1=xh1, rs1=rs1, gu=gu, hmid=hmid,
              xh2=xh2, rs2=rs2)
    return x2, sv, rout


def _layer_bwd(dys, coefs, sv, mem2, W, P, consts, riders=None):
    S = sv["xin"].shape[0]
    T, TX = min(256, S), min(512, S)
    proj = sv["proj"]
    g = {}
    rout = {}

    def ride(host):
        return riders[host](g, rout) if riders and host in riders else None

    dz2, g["ln2_g"], g["ln2_b"] = _ln_bwd("ln_bwd_%d" % len(dys), dys, coefs, sv["xh2"], sv["rs2"], P["ln2_g"], T)
    dhmid = _mm("d_hmid", dz2, W["w_fd"], "nt", BF16)
    g["ffn_w_down"] = _mm("dw_ffn_down", sv["hmid"], dz2, "tn", F32)
    dgu = _swiglu_bwd("swiglu_bwd", sv["gu"], dhmid, T)
    dx1f = _mm("d_x1", dgu, W["w_fi"], "nt", F32)
    g["ffn_w_in"] = _mm("dw_ffn_in", sv["x1"], dgu, "tn", F32)
    dz1, g["ln1_g"], g["ln1_b"] = _ln_bwd("ln_bwd_2", [dz2, dx1f], [ALPHA, 1.0], sv["xh1"], sv["rs1"], P["ln1_g"], T)
    dmerged = _mm("d_merged", dz1, W["w_o"], "nt", BF16)
    g["w_out"] = _mm("dw_out", sv["merged"], dz1, "tn", F32)
    dproj, dpl, dps, dpx, dbg0, dbg1, dbg2 = _merge_bwd("merge_bwd", proj, P["b_gate"], sv["p_l"], sv["p_s"], sv["p_x"],
                                                         dmerged, T)
    g["b_gate"] = jnp.concatenate([dbg0, dbg1, dbg2], axis=0)
    dy_lru = _mm("d_ylru", dpl, W["w_l"], "nt", BF16)
    g["w_br_lru"] = _mm("dw_br_lru", sv["y_lru"], dpl, "tn", F32)
    dy_ssd = _mm("d_yssd", dps, W["w_s"], "nt", BF16)
    g["w_br_ssd"] = _mm("dw_br_ssd", sv["y_ssd"], dps, "tn", F32)
    dy_xa = _mm("d_yxa", dpx, W["w_x"], "nt", BF16)
    g["w_br_xa"] = _mm("dw_br_xa", sv["y_xa"], dpx, "tn", F32)
    dproj, dkv = _xa_bwd("xattn_bwd", proj, sv["kv"], dy_xa, dproj, TX)
    g["mem_w_kv"] = _mm("dw_kv", mem2, dkv, "tn", F32)
    dproj, dycore, g["ssd_norm_w"] = _gate_norm_bwd("ssd_norm_bwd", dy_ssd, sv["ycore"], proj, P["ssd_norm_w"], dproj, T)
    res = _ssd_bwd("ssd_scan_bwd", sv["xact"], sv["dt_raw"], P["dt_bias"], P["a_neg"], P["d_exp"],
                   sv["prev"], dycore, consts, rider=ride("ssd_scan_bwd"))
    dxact, ddt, d_a, g_dtb, d_dexp = res[:5]
    rout["ssd_scan_bwd"] = res[5:]
    g["ssd_dt_bias"] = g_dtb[:, :SSD_HEADS]
    g["ssd_a_log"] = d_a[:, :SSD_HEADS] * P["a_neg"][:, :SSD_HEADS]
    g["ssd_d"] = jnp.sum(d_dexp.reshape(SSD_HEADS, SSD_HEAD_DIM), axis=-1)
    res = _conv_bwd("ssd_conv_bwd", proj, CB_XBC, P["ssd_conv_w"], P["ssd_conv_b"], True, dxact, dproj, T,
                    rider=ride("ssd_conv_bwd"))
    dproj, w0, w1, w2, w3, g["ssd_conv_b"] = res[:6]
    rout["ssd_conv_bwd"] = res[6:]
    g["ssd_conv_w"] = jnp.concatenate([w0, w1, w2, w3], axis=0)
    res = _lru_bwd("lru_scan_bwd", dy_lru, sv["xc"], proj, sv["h"], P["lru_w_a"], P["lru_w_i"], P["lru_b_a"], P["lru_b_i"],
                   P["lru_lambda"], dproj, T, rider=ride("lru_scan_bwd"))
    dproj, dxc, g["lru_w_a"], g["lru_w_i"], g["lru_b_a"], g["lru_b_i"], g["lru_lambda"] = res[:7]
    rout["lru_scan_bwd"] = res[7:]
    dproj, w0, w1, w2, w3, g["lru_conv_b"] = _conv_bwd("lru_conv_bwd", proj, CB_LRU_X, P["lru_conv_w"], P["lru_conv_b"], False,
                                                       dxc, dproj, T)
    g["lru_conv_w"] = jnp.concatenate([w0, w1, w2, w3], axis=0)
    xin = sv["xin"]
    rd = ride("dw_in")
    dw_re_t = _mm("dw_in", dproj, xin, "tn", F32, rider=rd)
    if rd is not None:
        dw_re_t, rout["dw_in"] = dw_re_t[0], dw_re_t[1:]
    dw_dt_t = _mm("dw_in_dt", ddt, xin, "tn", F32)
    pieces = {rng: dw_re_t[off:off + rng[1] - rng[0]]
              for rng, off in zip(_IN_ORDER, (0, 3072, 6144, 8192, 9216, 10240))}
    pieces[_IN_DT] = dw_dt_t[:SSD_HEADS]
    g["w_in"] = jnp.concatenate([pieces[k] for k in sorted(pieces)], axis=0)
    rd = ride("d_xin")
    dxp = _mm("d_xin", dproj, W["w_re_t"], "nn", F32, rider=rd)
    if rd is not None:
        dxp, rout["d_xin"] = dxp[0], dxp[1:]
    dxs = _mm("d_xin_dt", ddt, W["w_dt_t"], "nn", F32, add=dxp)
    return [dz1, dxs], [ALPHA, 1.0], g, rout


def _step(a):
    x2d, mem2, target = a["x"][0], a["mem"][0], a["loss_target"][0]
    S = x2d.shape[0]
    T = min(256, S)
    xi, yi, ci = _me()
    j = 2 * xi + yi
    w = {nm: a[nm] for nm in _W_NAMES}
    consts = _ssd_consts()
    row = lambda v: v.reshape(1, -1)
    pad_h = lambda v: jnp.pad(v.reshape(1, -1), ((0, 0), (0, DT_PAD - SSD_HEADS)))

    w_keys = dict(w_re_t="w_re_t", w_dt_t="w_dt_t", w_fi="ffn_w_in", w_kv="mem_w_kv", w_l="w_br_lru", w_s="w_br_ssd",
                  w_x="w_br_xa", w_o="w_out", w_fd="ffn_w_down")

    def gathered_params(full):
        return ({k: full[v] for k, v in w_keys.items() if v in full}, {k: full[k] for k in _SMALL_SHARDED if k in full})

    def local_params(l):
        return dict(
            lru_conv_b=row(w["lru_conv_b"][l]), lru_w_a=w["lru_w_a"][l].astype(BF16), lru_w_i=w["lru_w_i"][l].astype(BF16),
            lru_b_a=row(w["lru_b_a"][l]), lru_b_i=row(w["lru_b_i"][l]), lru_lambda=row(w["lru_lambda"][l]),
            ssd_conv_b=row(w["ssd_conv_b"][l]), dt_bias=pad_h(w["ssd_dt_bias"][l]), a_neg=pad_h(-jnp.exp(w["ssd_a_log"][l])),
            d_exp=jnp.broadcast_to(w["ssd_d"][l][:, None], (SSD_HEADS, SSD_HEAD_DIM)).reshape(1, D_SSD),
            ssd_norm_w=row(w["ssd_norm_w"][l]), ln1_g=row(w["ln1_g"][l]), ln1_b=row(w["ln1_b"][l]),
            ln2_g=row(w["ln2_g"][l]), ln2_b=row(w["ln2_b"][l]))

    srcs, bufs = _gather_sources(w)
    names, kinds = list(_GATHER_NAMES), list(_GATHER_KINDS)
    groups = {"w_in": [0, 1], "mixer": [3, 4, 5, 6, 7, 9, 10, 11], "ffn": [2, 8]}

    def gather(group, layer):
        idx = groups[group]
        return _gather_rider([srcs[i] for i in idx], [kinds[i] for i in idx], [bufs[layer][i] for i in idx], layer)

    def weights_of(group, got):
        return _layer_weights([names[i] for i in groups[group]], got)

    rest = groups["mixer"] + groups["ffn"]
    got = _run_rider("gather_weights", gather("w_in", 0))
    W0, _ = gathered_params(weights_of("w_in", got))

    def late0(rout_):
        wl, pl_ = gathered_params(_layer_weights([names[i] for i in rest], rout_["proj"]))
        return wl, pl_

    rest_rider = _gather_rider([srcs[i] for i in rest], [kinds[i] for i in rest], [bufs[0][i] for i in rest], 0)
    x1, sv0, rout = _layer_fwd(x2d, mem2, W0, local_params(0), consts, late=late0, riders={
        "proj": rest_rider, "lru_scan": gather("mixer", 1), "ssd_scan": gather("w_in", 1), "ffn_in": gather("ffn", 1)})
    full1 = {**weights_of("w_in", rout["ssd_scan"]), **weights_of("mixer", rout["lru_scan"]), **weights_of("ffn", rout["ffn_in"])}
    W1, P1g = gathered_params(full1)
    P1 = {**local_params(1), **P1g}
    w0_late, p0_late = late0(rout)
    W0, P0 = {**W0, **w0_late}, {**local_params(0), **p0_late}
    xcur, sv1, _ = _layer_fwd(x1, mem2, W1, P1, consts)
    dy, loss_part = _loss_fwd_bwd("loss", xcur, target, T)
    loss = lax.psum(loss_part[0, 0], ("x", "y", "c"))

    big_names = [nm for nm, _ in _BIG]
    rest_names = big_names[1:]
    st = {}

    def swap_rider(key, names, layer):
        def build(g, rout_):
            st[key] = [_per_chip(g[nm], nm) for nm in names]
            return _pair_swap_rider(st[key], layer)
        return build

    def pair_sums(key, theirs, layer):
        return [_pair_sum("grads_pair_sum", g_, t_, layer) for g_, t_ in zip(st[key], theirs)]

    dys, coefs, g1, rout1 = _layer_bwd([dy], [1.0], sv1, mem2, W1, P1, consts,
                                       riders={"d_xin": swap_rider("g1", big_names, 1)})
    pairs1 = pair_sums("g1", rout1["d_xin"], 1)

    def send_rest0(g, rout_):
        st["pairs0"] = pair_sums("g0", rout_["lru_scan_bwd"], 0)
        return _chip_send_rider([pb for _, pb in st["pairs0"]], 0)

    dys, coefs, g0, rout0 = _layer_bwd(dys, coefs, sv0, mem2, W0, P0, consts, riders={
        "ssd_scan_bwd": lambda g, r: _chip_send_rider([pb for _, pb in pairs1[:1]], 1),
        "ssd_conv_bwd": lambda g, r: _chip_send_rider([pb for _, pb in pairs1[1:]], 1),
        "lru_scan_bwd": swap_rider("g0", rest_names, 0),
        "dw_in": send_rest0,
        "d_xin": swap_rider("g0_in", ["w_in"], 0)})
    grad_x = _axpy("grad_x", coefs[0], dys[0], dys[1], T)[None]
    pairs0_in = pair_sums("g0_in", rout0["d_xin"], 0)
    got0_in = _run_rider("grads_chip_send", _chip_send_rider([pb for _, pb in pairs0_in], 0))
    pairs0 = pairs0_in + st["pairs0"]
    gots0 = list(got0_in) + list(rout0["dw_in"])
    halves = []
    gots1 = list(rout0["ssd_scan_bwd"]) + list(rout0["ssd_conv_bwd"])
    for (p1, _), gt1, (p0, _), gt0 in zip(pairs1, gots1, pairs0, gots0):
        buf = _chip_sum("grads_chip_sum", p1, gt1, j, 1)
        halves.append(_chip_sum("grads_chip_sum", p0, gt0, j, 0, buf))
    big = dict(zip(big_names, _join_parts_multi("grads_join", halves)))
    big["w_in"] = jnp.swapaxes(big["w_in"], 1, 2)
    layer_grads = [g0, g1]
    stacked = {nm: jnp.stack([layer_grads[l][nm] for l in range(DEPTH)]) for nm in _SMALL}
    small_parts = [stacked[nm].reshape((DEPTH,) + tuple(sh)) for nm, sh in
                   ((nm, (3, D_MODEL) if nm == "b_gate" else (4, D_MODEL) if nm == "lru_conv_w" else
                     (4, D_XBC) if nm == "ssd_conv_w" else w[nm].shape[1:]) for nm in _SMALL)]
    small = dict(zip(_SMALL, _reduce_small_grads(small_parts)))
    for nm in _SMALL_SHARDED:
        cs = w[nm].shape[2]
        small[nm] = lax.dynamic_slice_in_dim(small[nm], j * cs, cs, axis=2)
    grads = {**big, **small}

    delta, new_m, new_v = {}, {}, {}
    for nm, _ in _BIG:
        delta[nm], new_m[nm], new_v[nm] = _adamw("adamw_" + nm, w[nm], grads[nm], a["m_" + nm], a["v_" + nm])
    shapes = [w[nm].shape for nm in _SMALL]
    packs = [_flat_rows([src[nm] for nm in _SMALL], 8)[None] for src in
             (w, grads, {nm: a["m_" + nm] for nm in _SMALL}, {nm: a["v_" + nm] for nm in _SMALL})]
    d_, m_, v_ = _adamw("adamw_small", *packs)
    for dst, buf in ((delta, d_), (new_m, m_), (new_v, v_)):
        dst.update(zip(_SMALL, _take_parts(buf.reshape(-1), shapes)))

    outs = [loss, grad_x]
    for group in (grads, delta, new_m, new_v):
        outs += [group[nm] for nm in _W_NAMES]
    return tuple(outs)


def kernel(x, mem, w_in, b_gate, lru_conv_w, lru_conv_b, lru_w_a, lru_b_a, lru_w_i, lru_b_i, lru_lambda, ssd_conv_w, ssd_conv_b, ssd_dt_bias, ssd_a_log, ssd_d, ssd_norm_w, mem_w_kv, w_br_lru, w_br_ssd, w_br_xa, w_out, ln1_g, ln1_b, ffn_w_in, ffn_w_down, ln2_g, ln2_b, loss_target, m_w_in, m_b_gate, m_lru_conv_w, m_lru_conv_b, m_lru_w_a, m_lru_b_a, m_lru_w_i, m_lru_b_i, m_lru_lambda, m_ssd_conv_w, m_ssd_conv_b, m_ssd_dt_bias, m_ssd_a_log, m_ssd_d, m_ssd_norm_w, m_mem_w_kv, m_w_br_lru, m_w_br_ssd, m_w_br_xa, m_w_out, m_ln1_g, m_ln1_b, m_ffn_w_in, m_ffn_w_down, m_ln2_g, m_ln2_b, v_w_in, v_b_gate, v_lru_conv_w, v_lru_conv_b, v_lru_w_a, v_lru_b_a, v_lru_w_i, v_lru_b_i, v_lru_lambda, v_ssd_conv_w, v_ssd_conv_b, v_ssd_dt_bias, v_ssd_a_log, v_ssd_d, v_ssd_norm_w, v_mem_w_kv, v_w_br_lru, v_w_br_ssd, v_w_br_xa, v_w_out, v_ln1_g, v_ln1_b, v_ffn_w_in, v_ffn_w_down, v_ln2_g, v_ln2_b):
    return _step(dict(locals()))
```

```python
import functools
import math

import jax
import jax.numpy as jnp
from jax import lax
from jax.experimental import pallas as pl
from jax.experimental.pallas import tpu as pltpu

F32, BF16 = jnp.float32, jnp.bfloat16
MESH = pl.DeviceIdType.MESH
VMEM_LIMIT_BYTES = 56 * 2**20
HALO = 16

D_MODEL = 1024
DEPTH = 2
CHUNK = 64
N_MEM = 256
LRU_BLOCKS = 8
LRU_BLOCK = 128
LRU_C = 8.0
D_SSD = 2048
SSD_HEADS = 32
SSD_HEAD_DIM = 64
SSD_GROUPS = 4
SSD_STATE = 128
D_BC = SSD_GROUPS * SSD_STATE
D_XBC = D_SSD + 2 * D_BC
XA_HEADS = 4
XA_HEAD_DIM = 256
D_FF = 2816
ALPHA = (2 * DEPTH) ** 0.25
EPS = 1e-5
N_IN = 11296
N_PROJ = 11264
DT_PAD = 128

ADAM_LR, ADAM_B1, ADAM_B2, ADAM_EPS, ADAM_WD, ADAM_STEP = 0.001, 0.9, 0.999, 1e-08, 0.01, 10

CB_XBC = (3072, 0)
CB_XS, CB_BM, CB_CM = (2048, 0), (512, 4), (512, 5)
CB_LOGITS = (3072, 1)
CB_G0, CB_G1, CB_G2 = (1024, 3), (1024, 4), (1024, 5)
CB_Z = (2048, 3)
CB_LRU_X, CB_LRU_GATE, CB_XA_Q = (1024, 8), (1024, 9), (1024, 10)


def _cparams(sem):
    return pltpu.CompilerParams(dimension_semantics=sem, vmem_limit_bytes=VMEM_LIMIT_BYTES)


MID_STEP_PCT = 65


class _Rider:
    def __init__(self, ins, alias, outs, sems, start, finish, mid=None):
        self.ins, self.alias, self.outs, self.sems = list(ins), list(alias), list(outs), list(sems)
        self.start, self.finish = start, finish
        self.mid = mid

    def operands(self):
        return self.ins + self.alias

    def out_shapes(self):
        return [jax.ShapeDtypeStruct(a.shape, a.dtype) for a in self.alias] + self.outs

    def aliases(self, n_in, n_out):
        return {n_in + len(self.ins) + q: n_out + q for q in range(len(self.alias))}

    def split(self, refs, n_in, n_out, n_scratch):
        ni, na, no = len(self.ins), len(self.alias), len(self.outs)
        main = list(refs[:n_in])
        r_in = refs[n_in:n_in + ni]
        p = n_in + ni + na
        main += refs[p:p + n_out]
        r_alias = refs[p + n_out:p + n_out + na]
        r_out = refs[p + n_out + na:p + n_out + na + no]
        p = p + n_out + na + no
        main += refs[p:p + n_scratch]
        sems = refs[p + n_scratch:]
        return main, (r_in, r_alias, r_out, sems)


def _run_rider(name, rider):
    def body(*refs):
        _, parts = rider.split(refs, 0, 0, 0)
        rider.start(*parts)
        if rider.mid is not None:
            rider.mid(*parts)
        rider.finish(*parts)

    n_ops = len(rider.operands())
    return pl.pallas_call(
        body, name=name, in_specs=[_ANY] * n_ops, out_specs=[_ANY] * len(rider.out_shapes()),
        out_shape=rider.out_shapes(), input_output_aliases=rider.aliases(0, 0), scratch_shapes=rider.sems,
    )(*rider.operands())


def _pcall(name, body, n, ins, outs, scratch=(), reverse=False, aliases=None, rider=None):
    def ridx(i):
        return (n - 1 - i) if reverse else i

    in_specs, args = [], []
    for sp in ins:
        kind, arr = sp[0], sp[1]
        if kind == "row":
            _, _, tile, width, cb = sp
            in_specs.append(pl.BlockSpec((tile, width), lambda i, cb=cb: (ridx(i), cb)))
        elif kind == "prev":
            _, _, tile, width, cb = sp
            t = tile // HALO
            in_specs.append(pl.BlockSpec((HALO, width), lambda i, cb=cb, t=t: (jnp.maximum(ridx(i) * t - 1, 0), cb)))
        elif kind == "next":
            _, _, tile, width, cb = sp
            t = tile // HALO
            last = arr.shape[0] // HALO - 1
            in_specs.append(pl.BlockSpec((HALO, width), lambda i, cb=cb, t=t, last=last: (jnp.minimum((ridx(i) + 1) * t, last), cb)))
        elif kind == "lead":
            nd = arr.ndim
            in_specs.append(pl.BlockSpec((sp[2],) + arr.shape[1:], lambda i, nd=nd: (ridx(i),) + (0,) * (nd - 1)))
        elif kind == "full":
            nd = arr.ndim
            in_specs.append(pl.BlockSpec(arr.shape, lambda i, nd=nd: (0,) * nd))
        elif kind == "any":
            in_specs.append(pl.BlockSpec(memory_space=pl.ANY))
        else:
            raise ValueError(kind)
        args.append(arr)
    out_specs, out_shape = [], []
    for sp in outs:
        kind = sp[0]
        if kind == "row":
            _, rows, cols, dtype, tile, width, cb = sp
            out_shape.append(jax.ShapeDtypeStruct((rows, cols), dtype))
            out_specs.append(pl.BlockSpec((tile, width), lambda i, cb=cb: (ridx(i), cb)))
        elif kind == "lead":
            _, shape, dtype, lead = sp
            nd = len(shape)
            out_shape.append(jax.ShapeDtypeStruct(shape, dtype))
            out_specs.append(pl.BlockSpec((lead,) + tuple(shape[1:]), lambda i, nd=nd: (ridx(i),) + (0,) * (nd - 1)))
        elif kind == "full":
            _, shape, dtype = sp
            nd = len(shape)
            out_shape.append(jax.ShapeDtypeStruct(shape, dtype))
            out_specs.append(pl.BlockSpec(tuple(shape), lambda i, nd=nd: (0,) * nd))
        else:
            raise ValueError(kind)
    aliases = dict(aliases or {})
    scratch = list(scratch)
    kernel_body = body
    if rider is not None:
        n_in, n_out, n_scratch = len(args), len(out_shape), len(scratch)

        def kernel_body(*refs):
            main, parts = rider.split(refs, n_in, n_out, n_scratch)
            i = pl.program_id(0)

            @pl.when(i == 0)
            def _():
                rider.start(*parts)

            if rider.mid is not None:
                @pl.when(i == (n * MID_STEP_PCT) // 100)
                def _():
                    rider.mid(*parts)

            body(*main)

            @pl.when(i == n - 1)
            def _():
                rider.finish(*parts)

        aliases.update(rider.aliases(n_in, n_out))
        args += rider.operands()
        in_specs += [_ANY] * len(rider.operands())
        out_shape += rider.out_shapes()
        out_specs += [_ANY] * len(rider.out_shapes())
        scratch += rider.sems
    res = pl.pallas_call(
        kernel_body, name=name, grid=(n,), in_specs=in_specs, out_specs=out_specs, out_shape=out_shape,
        scratch_shapes=scratch, input_output_aliases=aliases,
        compiler_params=_cparams(("arbitrary",)),
    )(*args)
    return res


def _pick(n, cands):
    for c in cands:
        if n % c == 0:
            return c
    return n


_MM_TILES = (1024, 1408, 512, 256, 128)


def _mm(name, a, b, mode, out_dtype, add=None, rider=None):
    if mode == "nn":
        (M, K), (K2, N) = a.shape, b.shape
    elif mode == "nt":
        (M, K), (N, K2) = a.shape, b.shape
    else:
        (K, M), (K2, N) = a.shape, b.shape
    assert K == K2, (name, a.shape, b.shape)
    tm = _pick(M, _MM_TILES)
    tn = _pick(N, _MM_TILES)
    if mode == "tn":
        tk = _pick(K, (1024, 512, 256))
    else:
        tk = K if K <= 2816 else _pick(K, _MM_TILES)
    nk = K // tk
    has_add = add is not None

    def body(*refs):
        if has_add:
            a_ref, b_ref, add_ref, o_ref, acc_ref = refs
        else:
            a_ref, b_ref, o_ref, acc_ref = refs
        k = pl.program_id(2)
        av = a_ref[...].astype(BF16)
        bv = b_ref[...].astype(BF16)
        if mode == "nn":
            p = jnp.dot(av, bv, preferred_element_type=F32)
        elif mode == "nt":
            p = lax.dot_general(av, bv, (((1,), (1,)), ((), ())), preferred_element_type=F32)
        else:
            p = lax.dot_general(av, bv, (((0,), (0,)), ((), ())), preferred_element_type=F32)

        def fin(v):
            if has_add:
                v = v + add_ref[...].astype(F32)
            o_ref[...] = v.astype(out_dtype)

        if nk == 1:
            fin(p)
        else:
            @pl.when(k == 0)
            def _():
                acc_ref[...] = p

            @pl.when(k > 0)
            def _():
                acc_ref[...] += p

            @pl.when(k == nk - 1)
            def _():
                fin(acc_ref[...])

    if mode == "nn":
        specs = [pl.BlockSpec((tm, tk), lambda i, j, k: (i, k)), pl.BlockSpec((tk, tn), lambda i, j, k: (k, j))]
    elif mode == "nt":
        specs = [pl.BlockSpec((tm, tk), lambda i, j, k: (i, k)), pl.BlockSpec((tn, tk), lambda i, j, k: (j, k))]
    else:
        specs = [pl.BlockSpec((tk, tm), lambda i, j, k: (k, i)), pl.BlockSpec((tk, tn), lambda i, j, k: (k, j))]
    args = [a, b]
    if has_add:
        specs.append(pl.BlockSpec((tm, tn), lambda i, j, k: (i, j)))
        args.append(add)
    acc_shape = (tm, tn) if nk > 1 else (8, 128)
    grid = (M // tm, N // tn, nk)
    out_spec = pl.BlockSpec((tm, tn), lambda i, j, k: (i, j))
    out_shape = jax.ShapeDtypeStruct((M, N), out_dtype)
    if rider is None:
        return pl.pallas_call(
            body, name=name, grid=grid, in_specs=specs, out_specs=out_spec, out_shape=out_shape,
            scratch_shapes=[pltpu.VMEM(acc_shape, F32)],
            compiler_params=_cparams(("parallel", "parallel", "arbitrary")),
        )(*args)
    n_in = len(args)

    def kernel_body(*refs):
        main, parts = rider.split(refs, n_in, 1, 1)
        i, j, k = pl.program_id(0), pl.program_id(1), pl.program_id(2)

        @pl.when((i == 0) & (j == 0) & (k == 0))
        def _():
            rider.start(*parts)

        if rider.mid is not None:
            flat = (grid[0] * grid[1] * grid[2] * MID_STEP_PCT) // 100
            mi, mj, mk = flat // (grid[1] * grid[2]), (flat // grid[2]) % grid[1], flat % grid[2]

            @pl.when((i == mi) & (j == mj) & (k == mk))
            def _():
                rider.mid(*parts)

        body(*main)

        @pl.when((i == grid[0] - 1) & (j == grid[1] - 1) & (k == grid[2] - 1))
        def _():
            rider.finish(*parts)

    return pl.pallas_call(
        kernel_body, name=name, grid=grid, in_specs=specs + [_ANY] * len(rider.operands()),
        out_specs=[out_spec] + [_ANY] * len(rider.out_shapes()), out_shape=[out_shape] + rider.out_shapes(),
        input_output_aliases=rider.aliases(n_in, 1), scratch_shapes=[pltpu.VMEM(acc_shape, F32)] + rider.sems,
        compiler_params=_cparams(("arbitrary", "arbitrary", "arbitrary")),
    )(*args, *rider.operands())


def _sigmoid(x):
    return 1.0 / (1.0 + jnp.exp(-x))


def _silu(x):
    return x * _sigmoid(x)


def _dsilu(x):
    s = _sigmoid(x)
    return s * (1.0 + x * (1.0 - s))


def _softplus(x):
    return jnp.maximum(x, 0.0) + jnp.log(1.0 + jnp.exp(-jnp.abs(x)))


_GELU_C = math.sqrt(2.0 / math.pi)


def _gelu(x):
    return 0.5 * x * (1.0 + jnp.tanh(_GELU_C * (x + 0.044715 * x * x * x)))


def _dgelu(x):
    t = jnp.tanh(_GELU_C * (x + 0.044715 * x * x * x))
    return 0.5 * (1.0 + t) + 0.5 * x * (1.0 - t * t) * _GELU_C * (1.0 + 3.0 * 0.044715 * x * x)


def _acc(ref, i, val):
    @pl.when(i == 0)
    def _():
        ref[...] = val

    @pl.when(i > 0)
    def _():
        ref[...] += val


def _rows(shape):
    return lax.broadcasted_iota(jnp.int32, shape, 0)


def _shift_down(x, k, halo8, first):
    r = pltpu.roll(x, k, 0)
    h = pltpu.roll(halo8, k, 0)
    h = jnp.where(first, 0.0, h)
    head = jnp.where(_rows(h.shape) < k, h, r[:8])
    if x.shape[0] == 8:
        return head
    return jnp.concatenate([head, r[8:]], axis=0)


def _shift_up(x, k, halo8, last):
    T = x.shape[0]
    r = pltpu.roll(x, T - k, 0)
    h = pltpu.roll(halo8, 8 - k, 0)
    h = jnp.where(last, 0.0, h)
    tail = jnp.where(_rows(h.shape) >= 8 - k, h, r[T - 8:])
    return jnp.concatenate([r[:T - 8], tail], axis=0)


def _ln_fwd(name, a, b, g, beta, tile):
    S, Dm = a.shape
    n = S // tile

    def body(a_ref, b_ref, g_ref, be_ref, y_ref, xh_ref, rs_ref):
        z = ALPHA * a_ref[...] + b_ref[...].astype(F32)
        mu = jnp.mean(z, axis=-1, keepdims=True)
        zc = z - mu
        var = jnp.mean(zc * zc, axis=-1, keepdims=True)
        rstd = lax.rsqrt(var + EPS)
        xh = zc * rstd
        y_ref[...] = xh * g_ref[...] + be_ref[...]
        xh_ref[...] = xh
        rs_ref[...] = rstd

    return _pcall(name, body, n,
                  [("row", a, tile, Dm, 0), ("row", b, tile, Dm, 0), ("full", g), ("full", beta)],
                  [("row", S, Dm, F32, tile, Dm, 0), ("row", S, Dm, F32, tile, Dm, 0), ("row", S, 1, F32, tile, 1, 0)])


def _ln_bwd(name, dys, coefs, xh, rstd, g, tile):
    S, Dm = xh.shape
    n = S // tile
    nd = len(dys)

    def body(*refs):
        dy_refs = refs[:nd]
        xh_ref, rs_ref, g_ref, dz_ref, dg_ref, db_ref = refs[nd:]
        i = pl.program_id(0)
        dy = coefs[0] * dy_refs[0][...].astype(F32)
        for k in range(1, nd):
            dy = dy + coefs[k] * dy_refs[k][...].astype(F32)
        xh_v = xh_ref[...]
        dxh = dy * g_ref[...]
        m1 = jnp.mean(dxh, axis=-1, keepdims=True)
        m2 = jnp.mean(dxh * xh_v, axis=-1, keepdims=True)
        dz_ref[...] = rs_ref[...] * (dxh - m1 - xh_v * m2)
        _acc(dg_ref, i, jnp.sum(dy * xh_v, axis=0, keepdims=True))
        _acc(db_ref, i, jnp.sum(dy, axis=0, keepdims=True))

    ins = [("row", d, tile, Dm, 0) for d in dys]
    ins += [("row", xh, tile, Dm, 0), ("row", rstd, tile, 1, 0), ("full", g)]
    return _pcall(name, body, n, ins,
                  [("row", S, Dm, F32, tile, Dm, 0), ("full", (1, Dm), F32), ("full", (1, Dm), F32)])


def _loss_fwd_bwd(name, y, target, tile):
    S, Dm = y.shape
    n = S // tile

    def body(y_ref, t_ref, dy_ref, l_ref):
        i = pl.program_id(0)
        err = y_ref[...] - t_ref[...]
        dy_ref[...] = err * (1.0 / Dm)
        part = jnp.sum(jnp.sum(err * err, axis=-1, keepdims=True), axis=0, keepdims=True) * (0.5 / Dm)
        _acc(l_ref, i, part)

    return _pcall(name, body, n, [("row", y, tile, Dm, 0), ("row", target, tile, Dm, 0)],
                  [("row", S, Dm, F32, tile, Dm, 0), ("full", (1, 1), F32)])


def _swiglu_fwd(name, gu, tile):
    S = gu.shape[0]
    n = S // tile

    def body(g_ref, u_ref, o_ref):
        o_ref[...] = (_silu(g_ref[...].astype(F32)) * u_ref[...].astype(F32)).astype(BF16)

    return _pcall(name, body, n, [("row", gu, tile, D_FF, 0), ("row", gu, tile, D_FF, 1)],
                  [("row", S, D_FF, BF16, tile, D_FF, 0)])[0]


def _swiglu_bwd(name, gu, dh, tile):
    S = gu.shape[0]
    n = S // tile

    def body(g_ref, u_ref, dh_ref, o_ref):
        gv = g_ref[...].astype(F32)
        uv = u_ref[...].astype(F32)
        dv = dh_ref[...].astype(F32)
        dg = dv * uv * _dsilu(gv)
        du = dv * _silu(gv)
        o_ref[...] = jnp.concatenate([dg, du], axis=1).astype(BF16)

    return _pcall(name, body, n, [("row", gu, tile, D_FF, 0), ("row", gu, tile, D_FF, 1), ("row", dh, tile, D_FF, 0)],
                  [("row", S, 2 * D_FF, BF16, tile, 2 * D_FF, 0)])[0]


def _merge_fwd(name, proj, b_gate, pl_, ps_, px_, tile):
    S = proj.shape[0]
    n = S // tile
    Dm = D_MODEL

    def body(l0, l1, l2, bg, p0, p1, p2, o_ref):
        bgv = bg[...]
        acc = _sigmoid(l0[...].astype(F32) + bgv[0:1]) * p0[...].astype(F32)
        acc = acc + _sigmoid(l1[...].astype(F32) + bgv[1:2]) * p1[...].astype(F32)
        acc = acc + _sigmoid(l2[...].astype(F32) + bgv[2:3]) * p2[...].astype(F32)
        o_ref[...] = acc.astype(BF16)

    ins = [("row", proj, tile, *CB_G0), ("row", proj, tile, *CB_G1), ("row", proj, tile, *CB_G2), ("full", b_gate),
           ("row", pl_, tile, Dm, 0), ("row", ps_, tile, Dm, 0), ("row", px_, tile, Dm, 0)]
    return _pcall(name, body, n, ins, [("row", S, Dm, BF16, tile, Dm, 0)])[0]


def _merge_bwd(name, proj, b_gate, pl_, ps_, px_, dmerged, tile):
    S = proj.shape[0]
    n = S // tile
    Dm = D_MODEL

    def body(l0, l1, l2, bg, p0, p1, p2, dm_ref, dproj_ref, d0, d1, d2, db0, db1, db2):
        i = pl.program_id(0)
        bgv = bg[...]
        dm = dm_ref[...].astype(F32)
        dls = []
        for k, (lr, pr, dr, dbr) in enumerate(((l0, p0, d0, db0), (l1, p1, d1, db1), (l2, p2, d2, db2))):
            gk = _sigmoid(lr[...].astype(F32) + bgv[k:k + 1])
            dr[...] = (dm * gk).astype(BF16)
            dl = dm * pr[...].astype(F32) * gk * (1.0 - gk)
            _acc(dbr, i, jnp.sum(dl, axis=0, keepdims=True))
            dls.append(dl)
        dproj_ref[...] = jnp.concatenate(dls, axis=1).astype(BF16)

    ins = [("row", proj, tile, *CB_G0), ("row", proj, tile, *CB_G1), ("row", proj, tile, *CB_G2), ("full", b_gate),
           ("row", pl_, tile, Dm, 0), ("row", ps_, tile, Dm, 0), ("row", px_, tile, Dm, 0), ("row", dmerged, tile, Dm, 0)]
    outs = [("row", S, N_PROJ, BF16, tile, *CB_LOGITS)] + [("row", S, Dm, BF16, tile, Dm, 0)] * 3 + [("full", (1, Dm), F32)] * 3
    return _pcall(name, body, n, ins, outs)


def _conv_taps(xf, halo8, first, w):
    out = xf * w[3:4]
    for k in (1, 2, 3):
        out = out + _shift_down(xf, k, halo8, first) * w[3 - k:4 - k]
    return out


def _conv_fwd(name, src, cb, w, b, act, out_dtype, tile):
    S = src.shape[0]
    C = cb[0]
    n = S // tile

    def body(x_ref, p_ref, w_ref, b_ref, o_ref):
        i = pl.program_id(0)
        xf = x_ref[...].astype(F32)
        halo8 = p_ref[...].astype(F32)[HALO - 8:]
        pre = _conv_taps(xf, halo8, i == 0, w_ref[...]) + b_ref[...]
        o_ref[...] = (_silu(pre) if act else pre).astype(out_dtype)

    return _pcall(name, body, n, [("row", src, tile, *cb), ("prev", src, tile, *cb), ("full", w), ("full", b)],
                  [("row", S, C, out_dtype, tile, C, 0)])[0]


def _conv_bwd(name, src, cb, w, b, act, dout, dproj, tile, rider=None):
    S = src.shape[0]
    C = cb[0]
    n = S // tile

    def body(x_ref, p_ref, xn_ref, w_ref, b_ref, d_ref, dn_ref, buf_ref, dx_ref, dw0, dw1, dw2, dw3, db_ref):
        i = pl.program_id(0)
        first, last = i == 0, i == n - 1
        xf = x_ref[...].astype(F32)
        halo8 = p_ref[...].astype(F32)[HALO - 8:]
        wv = w_ref[...]
        dv = d_ref[...].astype(F32)
        nx8 = dn_ref[...].astype(F32)[:8]
        xs = [xf] + [_shift_down(xf, k, halo8, first) for k in (1, 2, 3)]
        if act:
            bv = b_ref[...]
            pre = xs[0] * wv[3:4] + xs[1] * wv[2:3] + xs[2] * wv[1:2] + xs[3] * wv[0:1] + bv
            dv = dv * _dsilu(pre)
            xn8 = xn_ref[...].astype(F32)[:8]
            nx8 = nx8 * _dsilu(_conv_taps(xn8, xf[tile - 8:], False, wv) + bv)
        dx = dv * wv[3:4]
        for k in (1, 2, 3):
            dx = dx + _shift_up(dv, k, nx8, last) * wv[3 - k:4 - k]
        dx_ref[...] = dx.astype(BF16)
        for k, dwr in ((0, dw3), (1, dw2), (2, dw1), (3, dw0)):
            _acc(dwr, i, jnp.sum(dv * xs[k], axis=0, keepdims=True))
        _acc(db_ref, i, jnp.sum(dv, axis=0, keepdims=True))

    ins = [("row", src, tile, *cb), ("prev", src, tile, *cb), ("next", src, tile, *cb), ("full", w), ("full", b),
           ("row", dout, tile, C, 0), ("next", dout, tile, C, 0), ("any", dproj)]
    outs = [("row", S, N_PROJ, BF16, tile, *cb)] + [("full", (1, C), F32)] * 5
    return _pcall(name, body, n, ins, outs, aliases={7: 0}, rider=rider)


def _lru_gates(xc, wa_ref, wi_ref, ba, bi, lam):
    xb = xc.astype(BF16)
    pa, pi_ = [], []
    for nb in range(LRU_BLOCKS):
        sl = slice(nb * LRU_BLOCK, (nb + 1) * LRU_BLOCK)
        pa.append(jnp.dot(xb[:, sl], wa_ref[nb], preferred_element_type=F32))
        pi_.append(jnp.dot(xb[:, sl], wi_ref[nb], preferred_element_type=F32))
    r = _sigmoid(jnp.concatenate(pa, axis=1) + ba)
    ig = _sigmoid(jnp.concatenate(pi_, axis=1) + bi)
    sp = _softplus(-lam)
    a = jnp.exp(-LRU_C * r * sp)
    m = jnp.sqrt(1.0 - a * a)
    return xb, r, ig, sp, a, m


def _lru_fwd(name, xc, proj, wa, wi, ba, bi, lam, tile, rider=None):
    S = xc.shape[0]
    n = S // tile
    C = D_MODEL

    def body(xc_ref, gate_ref, wa_ref, wi_ref, ba_ref, bi_ref, lam_ref, y_ref, h_ref, carry):
        i = pl.program_id(0)

        @pl.when(i == 0)
        def _():
            carry[...] = jnp.zeros_like(carry)

        xcv = xc_ref[...]
        _, r, ig, sp, a, m = _lru_gates(xcv, wa_ref, wi_ref, ba_ref[...], bi_ref[...], lam_ref[...])
        u = m * (ig * xcv)
        rows = _rows(a.shape)
        d = 1
        while d < tile:
            keep = rows >= d
            a_s = jnp.where(keep, pltpu.roll(a, d, 0), 1.0)
            u_s = jnp.where(keep, pltpu.roll(u, d, 0), 0.0)
            u = a * u_s + u
            a = a * a_s
            d *= 2
        h = u + a * carry[0:1, :]
        h_ref[...] = h
        carry[0:1, :] = h_ref[pl.ds(tile - 1, 1), :]
        y_ref[...] = (_gelu(gate_ref[...].astype(F32)) * h).astype(BF16)

    ins = [("row", xc, tile, C, 0), ("row", proj, tile, *CB_LRU_GATE), ("full", wa), ("full", wi),
           ("full", ba), ("full", bi), ("full", lam)]
    return _pcall(name, body, n, ins, [("row", S, C, BF16, tile, C, 0), ("row", S, C, F32, tile, C, 0)],
                  scratch=[pltpu.VMEM((8, C), F32)], rider=rider)


def _lru_bwd(name, dy, xc, proj, h, wa, wi, ba, bi, lam, dproj, tile, rider=None):
    S = xc.shape[0]
    n = S // tile
    C = D_MODEL

    def body(dy_ref, xc_ref, gate_ref, h_ref, hp_ref, wa_ref, wi_ref, ba_ref, bi_ref, lam_ref, buf_ref,
             dg_ref, dxc_ref, dwa_ref, dwi_ref, dba_ref, dbi_ref, dlam_ref, carry):
        i = pl.program_id(0)
        first_tile = i == n - 1

        @pl.when(i == 0)
        def _():
            carry[...] = jnp.zeros_like(carry)

        xcv = xc_ref[...]
        lamv = lam_ref[...]
        xb, r, ig, sp, a, m = _lru_gates(xcv, wa_ref, wi_ref, ba_ref[...], bi_ref[...], lamv)
        hv = h_ref[...]
        gv = gate_ref[...].astype(F32)
        dyv = dy_ref[...].astype(F32)
        dg_ref[...] = (dyv * hv * _dgelu(gv)).astype(BF16)
        v = dyv * _gelu(gv)
        rows = _rows(a.shape)
        bcoef = jnp.where(rows == tile - 1, 1.0, pltpu.roll(a, tile - 1, 0))
        d = 1
        while d < tile:
            keep = rows < tile - d
            b_s = jnp.where(keep, pltpu.roll(bcoef, tile - d, 0), 1.0)
            v_s = jnp.where(keep, pltpu.roll(v, tile - d, 0), 0.0)
            v = v + bcoef * v_s
            bcoef = bcoef * b_s
            d *= 2
        dH = v + bcoef * carry[0:1, :]
        dxc_ref[...] = dH
        carry[0:1, :] = dxc_ref[pl.ds(0, 1), :] * a[0:1, :]
        halo8 = hp_ref[...][HALO - 8:]
        hprev = _shift_down(hv, 1, halo8, first_tile)
        da = dH * hprev
        ix = ig * xcv
        dm = dH * ix
        di = dH * m * xcv
        dxc = dH * m * ig
        da = da - dm * a / m
        dla = da * a
        dr = dla * (-LRU_C) * sp
        _acc(dlam_ref, i, jnp.sum(dla * (-LRU_C) * r, axis=0, keepdims=True) * (-_sigmoid(-lamv)))
        dpa = dr * r * (1.0 - r)
        dpi = di * ig * (1.0 - ig)
        _acc(dba_ref, i, jnp.sum(dpa, axis=0, keepdims=True))
        _acc(dbi_ref, i, jnp.sum(dpi, axis=0, keepdims=True))
        dpab, dpib = dpa.astype(BF16), dpi.astype(BF16)
        back = []
        for nb in range(LRU_BLOCKS):
            sl = slice(nb * LRU_BLOCK, (nb + 1) * LRU_BLOCK)
            back.append(lax.dot_general(dpab[:, sl], wa_ref[nb], (((1,), (1,)), ((), ())), preferred_element_type=F32)
                        + lax.dot_general(dpib[:, sl], wi_ref[nb], (((1,), (1,)), ((), ())), preferred_element_type=F32))
            ga = lax.dot_general(xb[:, sl], dpab[:, sl], (((0,), (0,)), ((), ())), preferred_element_type=F32)
            gi = lax.dot_general(xb[:, sl], dpib[:, sl], (((0,), (0,)), ((), ())), preferred_element_type=F32)

            @pl.when(i == 0)
            def _(ga=ga, gi=gi, nb=nb):
                dwa_ref[nb] = ga
                dwi_ref[nb] = gi

            @pl.when(i > 0)
            def _(ga=ga, gi=gi, nb=nb):
                dwa_ref[nb] += ga
                dwi_ref[nb] += gi

        dxc_ref[...] = dxc + jnp.concatenate(back, axis=1)

    ins = [("row", dy, tile, C, 0), ("row", xc, tile, C, 0), ("row", proj, tile, *CB_LRU_GATE), ("row", h, tile, C, 0),
           ("prev", h, tile, C, 0), ("full", wa), ("full", wi), ("full", ba), ("full", bi), ("full", lam), ("any", dproj)]
    outs = [("row", S, N_PROJ, BF16, tile, *CB_LRU_GATE), ("row", S, C, F32, tile, C, 0),
            ("full", (LRU_BLOCKS, LRU_BLOCK, LRU_BLOCK), F32), ("full", (LRU_BLOCKS, LRU_BLOCK, LRU_BLOCK), F32),
            ("full", (1, C), F32), ("full", (1, C), F32), ("full", (1, C), F32)]
    return _pcall(name, body, n, ins, outs, scratch=[pltpu.VMEM((8, C), F32)], reverse=True, aliases={10: 0}, rider=rider)


SSD_STEP = 4

def _split3(x):
    h = x.astype(BF16)
    r = x - h.astype(F32)
    m = r.astype(BF16)
    lo = (r - m.astype(F32)).astype(BF16)
    return h, m, lo


def _dot01_r(x, e):
    h = x.astype(BF16)
    m = (x - h.astype(F32)).astype(BF16)
    return jnp.dot(h, e, preferred_element_type=F32) + jnp.dot(m, e, preferred_element_type=F32)


def _dot01_l(e, x):
    h, m, lo = _split3(x)
    return (jnp.dot(e, h, preferred_element_type=F32) + jnp.dot(e, m, preferred_element_type=F32)
            + jnp.dot(e, lo, preferred_element_type=F32))


def _ssd_consts():
    hh = lax.broadcasted_iota(jnp.int32, (DT_PAD, D_SSD), 0)
    cc = lax.broadcasted_iota(jnp.int32, (DT_PAD, D_SSD), 1)
    e = (cc // SSD_HEAD_DIM == hh).astype(BF16)
    rows = SSD_STEP * CHUNK
    li = lax.broadcasted_iota(jnp.int32, (rows, rows), 0)
    si = lax.broadcasted_iota(jnp.int32, (rows, rows), 1)
    ltri = ((li >= si) & (li // CHUNK == si // CHUNK)).astype(BF16)
    l4 = lax.broadcasted_iota(jnp.int32, (CHUNK, 4 * CHUNK), 0)
    s4 = lax.broadcasted_iota(jnp.int32, (CHUNK, 4 * CHUNK), 1) % CHUNK
    itile = (l4 == s4).astype(F32)
    causal = (l4 >= s4).astype(F32)
    j4 = lax.broadcasted_iota(jnp.int32, (8, 4 * CHUNK), 0)
    c4 = lax.broadcasted_iota(jnp.int32, (8, 4 * CHUNK), 1) // CHUNK
    hmask = (j4 == c4).astype(F32)
    return e, e.T, ltri, ltri.T, itile, causal, hmask


def _ssd_chunk_common(xs_ref, bm_ref, cm_ref, dt_ref, dtb_ref, a_ref, e_ref, ltri_ref):
    xs = xs_ref[...].astype(F32)
    raw = dt_ref[...] + dtb_ref[...]
    dtv = _softplus(raw)
    da = dtv * a_ref[...]
    cs = _dot01_l(ltri_ref[...], da)
    e = e_ref[...]
    dte = _dot01_r(dtv, e)
    ce = _dot01_r(cs, e)
    xdt = xs * dte
    ecs = jnp.exp(ce)
    return xs, raw, dtv, cs, dte, ce, xdt, ecs


def _quad_terms(ce_q, cb4, itile, causal):
    cr = jnp.sum(ce_q * itile, axis=0, keepdims=True)
    seg = ce_q - cr
    dec = jnp.where(causal > 0.0, jnp.exp(jnp.minimum(seg, 0.0)), 0.0)
    return dec, cb4 * dec


def _block_diag4(xq, hmask):
    return jnp.concatenate([xq * hmask[j:j + 1].astype(xq.dtype) for j in range(4)], axis=0)


def _ssd_fwd(name, xbc, dt_raw, dt_bias, a_neg, d_exp, consts, rider=None):
    S = xbc.shape[0]
    nc = S // CHUNK
    e, et, ltri, ltri_t, itile, causal, hmask = consts

    def body(xs_ref, bm_ref, cm_ref, dt_ref, dtb_ref, a_ref, dex_ref, e_ref, ltri_ref, it_ref, ca_ref, hm_ref,
             y_ref, prev_ref, hst):
        i = pl.program_id(0)

        @pl.when(i == 0)
        def _():
            hst[...] = jnp.zeros_like(hst)

        xs, raw, dtv, cs, dte, ce_all, xdt_all, ecs_all = _ssd_chunk_common(
            xs_ref, bm_ref, cm_ref, dt_ref, dtb_ref, a_ref, e_ref, ltri_ref)
        itile_v, causal_v, hmask_v = it_ref[...], ca_ref[...], hm_ref[...]
        dskip_all = dex_ref[...] * xs
        for k in range(SSD_STEP):
            rs = slice(k * CHUNK, (k + 1) * CHUNK)
            ce, xdt, ecs, dskip = ce_all[rs], xdt_all[rs], ecs_all[rs], dskip_all[rs]
            cle = ce[CHUNK - 1:CHUNK, :]
            xdtb = xdt.astype(BF16)
            xst = (xdt * jnp.exp(cle - ce)).astype(BF16)
            ecl = jnp.exp(cle)
            bm = bm_ref[rs, :]
            cm = cm_ref[rs, :]
            for g in range(SSD_GROUPS):
                gs = slice(g * 512, (g + 1) * 512)
                ns = slice(g * SSD_STATE, (g + 1) * SSD_STATE)
                bm_g, cm_g = bm[:, ns], cm[:, ns]
                hprev = hst[g]
                hprev_b = hprev.astype(BF16)
                prev_ref[k, g] = hprev_b
                yoff = jnp.dot(cm_g, hprev_b, preferred_element_type=F32) * ecs[:, gs]
                st = lax.dot_general(bm_g, xst[:, gs], (((0,), (0,)), ((), ())), preferred_element_type=F32)
                hst[g] = hprev * ecl[:, gs] + st
                b4 = jnp.concatenate([bm_g] * 4, axis=0)
                cb4 = lax.dot_general(cm_g, b4, (((1,), (1,)), ((), ())), preferred_element_type=F32)
                for q in range(2):
                    cols = slice(g * 512 + q * 256, g * 512 + (q + 1) * 256)
                    _, mq = _quad_terms(ce[:, cols], cb4, itile_v, causal_v)
                    xbd = _block_diag4(xdtb[:, cols], hmask_v)
                    ydiag = jnp.dot(mq.astype(BF16), xbd, preferred_element_type=F32)
                    y_ref[rs, cols] = ydiag + yoff[:, q * 256:(q + 1) * 256] + dskip[:, cols]

    T = SSD_STEP * CHUNK
    ins = [("row", xbc, T, *CB_XS), ("row", xbc, T, *CB_BM), ("row", xbc, T, *CB_CM),
           ("row", dt_raw, T, DT_PAD, 0), ("full", dt_bias), ("full", a_neg), ("full", d_exp),
           ("full", e), ("full", ltri), ("full", itile), ("full", causal), ("full", hmask)]
    outs = [("row", S, D_SSD, F32, T, D_SSD, 0), ("lead", (nc, SSD_GROUPS, SSD_STATE, 512), BF16, SSD_STEP)]
    return _pcall(name, body, S // T, ins, outs, scratch=[pltpu.VMEM((SSD_GROUPS, SSD_STATE, 512), F32)], rider=rider)


def _ssd_bwd(name, xbc, dt_raw, dt_bias, a_neg, d_exp, prev, dy, consts, rider=None):
    S = xbc.shape[0]
    nc = S // CHUNK
    e, et, ltri, ltri_t, itile, causal, hmask = consts

    def body(xs_ref, bm_ref, cm_ref, dt_ref, dtb_ref, a_ref, dex_ref, prev_ref, dy_ref,
             e_ref, et_ref, ltri_ref, ltt_ref, it_ref, ca_ref, hm_ref,
             dx_ref, ddt_ref, da_ref, dbias_ref, dd_ref, dh, dce_ref, dxdt_ref):
        i = pl.program_id(0)

        @pl.when(i == 0)
        def _():
            dh[...] = jnp.zeros_like(dh)

        xs, raw, dtv, cs, dte, ce_all, xdt_all, ecs_all = _ssd_chunk_common(
            xs_ref, bm_ref, cm_ref, dt_ref, dtb_ref, a_ref, e_ref, ltri_ref)
        itile_v, causal_v, hmask_v = it_ref[...], ca_ref[...], hm_ref[...]
        dyv = dy_ref[...]
        last_row = _rows((CHUNK, 512)) == CHUNK - 1
        for k in reversed(range(SSD_STEP)):
            rs = slice(k * CHUNK, (k + 1) * CHUNK)
            ce, xdt, ecs = ce_all[rs], xdt_all[rs], ecs_all[rs]
            cle = ce[CHUNK - 1:CHUNK, :]
            dend = jnp.exp(cle - ce)
            xdtb = xdt.astype(BF16)
            xst = (xdt * dend).astype(BF16)
            ecl = jnp.exp(cle)
            bm = bm_ref[rs, :]
            cm = cm_ref[rs, :]
            for g in range(SSD_GROUPS):
                gs = slice(g * 512, (g + 1) * 512)
                ns = slice(g * SSD_STATE, (g + 1) * SSD_STATE)
                bm_g, cm_g = bm[:, ns], cm[:, ns]
                hprev_b = prev_ref[k, g]
                dhn = dh[g]
                dhn_b = dhn.astype(BF16)
                dy_g = dyv[rs, gs]
                ecs_g, dend_g, xdt_g, ecl_g = ecs[:, gs], dend[:, gs], xdt[:, gs], ecl[:, gs]
                z = jnp.dot(cm_g, hprev_b, preferred_element_type=F32)
                dz = dy_g * ecs_g
                dzb = dz.astype(BF16)
                dce_g = dz * z
                dcm_g = lax.dot_general(dzb, hprev_b, (((1,), (1,)), ((), ())), preferred_element_type=F32)
                dprev = lax.dot_general(cm_g, dzb, (((0,), (0,)), ((), ())), preferred_element_type=F32) + dhn * ecl_g
                dcl = jnp.sum(dhn * hprev_b.astype(F32), axis=0, keepdims=True) * ecl_g
                gmat = jnp.dot(bm_g, dhn_b, preferred_element_type=F32)
                dbm_g = lax.dot_general(xst[:, gs], dhn_b, (((1,), (1,)), ((), ())), preferred_element_type=F32)
                dxdt_g = gmat * dend_g
                t = gmat * xdt_g * dend_g
                dce_g = dce_g - t
                dcl = dcl + jnp.sum(t, axis=0, keepdims=True)
                dce_g = dce_g + jnp.where(last_row, dcl, 0.0)
                dh[g] = dprev
                b4 = jnp.concatenate([bm_g] * 4, axis=0)
                cb4 = lax.dot_general(cm_g, b4, (((1,), (1,)), ((), ())), preferred_element_type=F32)
                for q in range(2):
                    qs = slice(q * 256, (q + 1) * 256)
                    cols = slice(g * 512 + q * 256, g * 512 + (q + 1) * 256)
                    dec, mq = _quad_terms(ce[:, cols], cb4, itile_v, causal_v)
                    mqb = mq.astype(BF16)
                    xbd = _block_diag4(xdtb[:, cols], hmask_v)
                    dyq = dy_g[:, qs].astype(BF16)
                    dm = lax.dot_general(dyq, xbd, (((1,), (1,)), ((), ())), preferred_element_type=F32)
                    rmat = lax.dot_general(mqb, dyq, (((0,), (0,)), ((), ())), preferred_element_type=F32)
                    dxq = rmat[0:64] * hmask_v[0:1]
                    for j in range(1, 4):
                        dxq = dxq + rmat[64 * j:64 * (j + 1)] * hmask_v[j:j + 1]
                    tq = dm * dec
                    tqb = tq.astype(BF16)
                    dcm_g = dcm_g + jnp.dot(tqb, b4, preferred_element_type=F32)
                    rb = lax.dot_general(tqb, cm_g, (((0,), (0,)), ((), ())), preferred_element_type=F32)
                    dbm_g = dbm_g + rb[0:64] + rb[64:128] + rb[128:192] + rb[192:256]
                    dseg = tq * cb4
                    colsum = jnp.sum(dseg, axis=0, keepdims=True)
                    dce_ref[rs, cols] = dce_g[:, qs] + dseg - itile_v * colsum
                    dxdt_ref[rs, cols] = dxdt_g[:, qs] + dxq
                dx_ref[rs, D_SSD + g * SSD_STATE:D_SSD + (g + 1) * SSD_STATE] = dbm_g
                dx_ref[rs, D_SSD + D_BC + g * SSD_STATE:D_SSD + D_BC + (g + 1) * SSD_STATE] = dcm_g
        dxdt = dxdt_ref[...]
        dexv = dex_ref[...]
        dx_ref[:, 0:D_SSD] = dxdt * dte + dyv * dexv
        _acc(dd_ref, i, jnp.sum(dyv * xs, axis=0, keepdims=True))
        etv = et_ref[...]
        dcs = _dot01_r(dce_ref[...], etv)
        dda = _dot01_l(ltt_ref[...], dcs)
        av = a_ref[...]
        ddtv = dda * av + _dot01_r(dxdt * xs, etv)
        _acc(da_ref, i, jnp.sum(dda * dtv, axis=0, keepdims=True))
        draw = ddtv * _sigmoid(raw)
        ddt_ref[...] = draw.astype(BF16)
        _acc(dbias_ref, i, jnp.sum(draw, axis=0, keepdims=True))

    T = SSD_STEP * CHUNK
    ins = [("row", xbc, T, *CB_XS), ("row", xbc, T, *CB_BM), ("row", xbc, T, *CB_CM),
           ("row", dt_raw, T, DT_PAD, 0), ("full", dt_bias), ("full", a_neg), ("full", d_exp),
           ("lead", prev, SSD_STEP), ("row", dy, T, D_SSD, 0),
           ("full", e), ("full", et), ("full", ltri), ("full", ltri_t), ("full", itile), ("full", causal), ("full", hmask)]
    outs = [("row", S, D_XBC, F32, T, D_XBC, 0), ("row", S, DT_PAD, BF16, T, DT_PAD, 0),
            ("full", (1, DT_PAD), F32), ("full", (1, DT_PAD), F32), ("full", (1, D_SSD), F32)]
    scratch = [pltpu.VMEM((SSD_GROUPS, SSD_STATE, 512), F32), pltpu.VMEM((T, D_SSD), F32), pltpu.VMEM((T, D_SSD), F32)]
    return _pcall(name, body, S // T, ins, outs, scratch=scratch, reverse=True, rider=rider)


def _gate_norm_fwd(name, ycore, proj, norm_w, tile):
    S = ycore.shape[0]
    n = S // tile

    def body(y_ref, z_ref, w_ref, o_ref):
        y2 = y_ref[...] * _silu(z_ref[...].astype(F32))
        wv = w_ref[...]
        for g in range(SSD_GROUPS):
            gs = slice(g * 512, (g + 1) * 512)
            seg = y2[:, gs]
            r = lax.rsqrt(jnp.mean(seg * seg, axis=-1, keepdims=True) + EPS)
            o_ref[:, gs] = (seg * r * wv[:, gs]).astype(BF16)

    return _pcall(name, body, n, [("row", ycore, tile, D_SSD, 0), ("row", proj, tile, *CB_Z), ("full", norm_w)],
                  [("row", S, D_SSD, BF16, tile, D_SSD, 0)])[0]


def _gate_norm_bwd(name, dout, ycore, proj, norm_w, dproj, tile):
    S = ycore.shape[0]
    n = S // tile

    def body(do_ref, y_ref, z_ref, w_ref, buf_ref, dz_ref, dy_ref, dw_ref):
        i = pl.program_id(0)
        yv = y_ref[...]
        zv = z_ref[...].astype(F32)
        sz = _silu(zv)
        y2 = yv * sz
        dov = do_ref[...].astype(F32)
        wv = w_ref[...]
        dws, dy2s = [], []
        for g in range(SSD_GROUPS):
            gs = slice(g * 512, (g + 1) * 512)
            seg = y2[:, gs]
            r = lax.rsqrt(jnp.mean(seg * seg, axis=-1, keepdims=True) + EPS)
            yn = seg * r
            dws.append(jnp.sum(dov[:, gs] * yn, axis=0, keepdims=True))
            dyn = dov[:, gs] * wv[:, gs]
            dy2s.append(r * (dyn - yn * jnp.mean(dyn * yn, axis=-1, keepdims=True)))
        dy2 = jnp.concatenate(dy2s, axis=1)
        dy_ref[...] = dy2 * sz
        dz_ref[...] = (dy2 * yv * _dsilu(zv)).astype(BF16)
        _acc(dw_ref, i, jnp.concatenate(dws, axis=1))

    ins = [("row", dout, tile, D_SSD, 0), ("row", ycore, tile, D_SSD, 0), ("row", proj, tile, *CB_Z), ("full", norm_w),
           ("any", dproj)]
    outs = [("row", S, N_PROJ, BF16, tile, *CB_Z), ("row", S, D_SSD, F32, tile, D_SSD, 0), ("full", (1, D_SSD), F32)]
    return _pcall(name, body, n, ins, outs, aliases={4: 0})


_XA_SCALE = XA_HEAD_DIM ** -0.5
_NT = (((1,), (1,)), ((), ()))
_TN = (((0,), (0,)), ((), ()))


def _xa_probs(qh, kh):
    s = lax.dot_general(qh, kh, _NT, preferred_element_type=F32) * _XA_SCALE
    s = s - jnp.max(s, axis=-1, keepdims=True)
    p = jnp.exp(s)
    return p / jnp.sum(p, axis=-1, keepdims=True)


def _xa_fwd(name, proj, kv, tile):
    S = proj.shape[0]
    n = S // tile
    Dh = XA_HEAD_DIM

    def body(q_ref, kv_ref, o_ref):
        for hd in range(XA_HEADS):
            qh = q_ref[:, hd * Dh:(hd + 1) * Dh]
            kh = kv_ref[:, hd * Dh:(hd + 1) * Dh]
            vh = kv_ref[:, D_MODEL + hd * Dh:D_MODEL + (hd + 1) * Dh]
            p = _xa_probs(qh, kh)
            o_ref[:, hd * Dh:(hd + 1) * Dh] = jnp.dot(p.astype(BF16), vh, preferred_element_type=F32).astype(BF16)

    return _pcall(name, body, n, [("row", proj, tile, *CB_XA_Q), ("full", kv)],
                  [("row", S, D_MODEL, BF16, tile, D_MODEL, 0)])[0]


def _xa_bwd(name, proj, kv, dout, dproj, tile):
    S = proj.shape[0]
    n = S // tile
    Dh = XA_HEAD_DIM

    def body(q_ref, kv_ref, do_ref, buf_ref, dq_ref, dkv_ref):
        i = pl.program_id(0)
        for hd in range(XA_HEADS):
            ks_ = slice(hd * Dh, (hd + 1) * Dh)
            vs_ = slice(D_MODEL + hd * Dh, D_MODEL + (hd + 1) * Dh)
            qh = q_ref[:, ks_]
            kh = kv_ref[:, ks_]
            vh = kv_ref[:, vs_]
            doh = do_ref[:, ks_].astype(BF16)
            p = _xa_probs(qh, kh)
            pb = p.astype(BF16)
            dp = lax.dot_general(doh, vh, _NT, preferred_element_type=F32)
            dv = lax.dot_general(pb, doh, _TN, preferred_element_type=F32)
            ds = (p * (dp - jnp.sum(dp * p, axis=-1, keepdims=True)) * _XA_SCALE).astype(BF16)
            dq_ref[:, ks_] = jnp.dot(ds, kh, preferred_element_type=F32).astype(BF16)
            dk = lax.dot_general(ds, qh, _TN, preferred_element_type=F32)

            @pl.when(i == 0)
            def _(dk=dk, dv=dv, ks_=ks_, vs_=vs_):
                dkv_ref[:, ks_] = dk
                dkv_ref[:, vs_] = dv

            @pl.when(i > 0)
            def _(dk=dk, dv=dv, ks_=ks_, vs_=vs_):
                dkv_ref[:, ks_] += dk
                dkv_ref[:, vs_] += dv

    ins = [("row", proj, tile, *CB_XA_Q), ("full", kv), ("row", dout, tile, D_MODEL, 0), ("any", dproj)]
    outs = [("row", S, N_PROJ, BF16, tile, *CB_XA_Q), ("full", (N_MEM, 2 * D_MODEL), F32)]
    return _pcall(name, body, n, ins, outs, aliases={3: 0})


def _adamw(name, w, g, m, v):
    L, R, C = w.shape
    tile = _pick(R, [t for t in (256, 128, 64, 32, 16, 8) if t * C <= 128 * 2048])
    bc1 = 1.0 - ADAM_B1 ** ADAM_STEP
    bc2 = 1.0 - ADAM_B2 ** ADAM_STEP

    def body(w_ref, g_ref, m_ref, v_ref, d_ref, nm_ref, nv_ref):
        gv = g_ref[...]
        mn = ADAM_B1 * m_ref[...] + (1.0 - ADAM_B1) * gv
        vn = ADAM_B2 * v_ref[...] + (1.0 - ADAM_B2) * (gv * gv)
        nm_ref[...] = mn
        nv_ref[...] = vn
        d_ref[...] = -ADAM_LR * ((mn / bc1) / (jnp.sqrt(vn / bc2) + ADAM_EPS) + ADAM_WD * w_ref[...])

    spec = pl.BlockSpec((1, tile, C), lambda l, i: (l, i, 0))
    return pl.pallas_call(
        body, name=name, grid=(L, R // tile), in_specs=[spec] * 4, out_specs=[spec] * 3,
        out_shape=[jax.ShapeDtypeStruct((L, R, C), F32)] * 3,
        compiler_params=_cparams(("arbitrary", "arbitrary")),
    )(w, g, m, v)


def _axpy(name, coef, a, b, tile):
    S, C = a.shape
    n = S // tile

    def body(a_ref, b_ref, o_ref):
        o_ref[...] = coef * a_ref[...].astype(F32) + b_ref[...].astype(F32)

    return _pcall(name, body, n, [("row", a, tile, C, 0), ("row", b, tile, C, 0)], [("row", S, C, F32, tile, C, 0)])[0]


def _sum_terms(name, terms, out_dtype):
    R, C = terms[0].shape
    tile = _pick(R, (512, 256, 128, 64, 32, 16, 8))
    n = R // tile
    nt = len(terms)

    def body(*refs):
        vals = [r[...].astype(F32) for r in refs[:nt]]
        while len(vals) > 1:
            vals = [vals[k] + vals[k + 1] for k in range(0, len(vals), 2)]
        refs[nt][...] = vals[0].astype(out_dtype)

    return _pcall(name, body, n, [("row", t, tile, C, 0) for t in terms], [("row", R, C, out_dtype, tile, C, 0)])[0]


def _me():
    return lax.axis_index("x"), lax.axis_index("y"), lax.axis_index("c")


def _other_chips(x, y):
    return [(1 - x, y), (x, 1 - y), (1 - x, 1 - y)]


_ANY = pl.BlockSpec(memory_space=pl.ANY)


def _swap_sibling(name, src):
    def body(src_ref, out_ref, send_sem, recv_sem):
        x, y, c = _me()
        cp = pltpu.make_async_remote_copy(src_ref=src_ref, dst_ref=out_ref, send_sem=send_sem, recv_sem=recv_sem,
                                          device_id=(x, y, 1 - c), device_id_type=MESH)
        cp.start()
        cp.wait()

    return pl.pallas_call(
        body, name=name, in_specs=[_ANY], out_specs=_ANY, out_shape=jax.ShapeDtypeStruct(src.shape, src.dtype),
        scratch_shapes=[pltpu.SemaphoreType.DMA, pltpu.SemaphoreType.DMA],
    )(src)


W_IN_SHARD = N_IN // 4
W_IN_INNER = W_IN_SHARD - 8


def _win_rows(chip_x, chip_y):
    start = (2 * chip_x + chip_y) * W_IN_SHARD + 8 * chip_y
    return pl.ds(pl.multiple_of(start, 2 * 8), W_IN_INNER)


def _dma_sems(*shape):
    return [pltpu.SemaphoreType.DMA(shape), pltpu.SemaphoreType.DMA(shape)]


def _gather_rider(srcs, kinds, bufs, part):
    n = len(srcs)

    def copy(refs, i, k, cx, cy, to, own=False):
        src_refs, out_refs, _, (send_sems, recv_sems) = refs
        d = out_refs[i].at[2 * cx + cy] if kinds[i] == "blk" else out_refs[i].at[_win_rows(cx, cy)]
        return pltpu.make_async_remote_copy(src_ref=src_refs[i].at[part] if own else d, dst_ref=d,
                                            send_sem=send_sems.at[i, k], recv_sem=recv_sems.at[i, k],
                                            device_id=to, device_id_type=MESH)

    def start(*refs):
        x, y, c = _me()

        @pl.when(c == part)
        def _():
            for k, (cx, cy) in enumerate(_other_chips(x, y)):
                for i in range(n):
                    copy(refs, i, k, x, y, (cx, cy, part), own=True).start()

    def mid(*refs):
        x, y, c = _me()

        @pl.when(c == part)
        def _():
            for k, (cx, cy) in enumerate(_other_chips(x, y)):
                for i in range(n):
                    copy(refs, i, k, cx, cy, (x, y, part)).wait_recv()
                    copy(refs, i, 3 + k, cx, cy, (x, y, 1 - part)).start()

    def finish(*refs):
        x, y, c = _me()
        chips = _other_chips(x, y)

        @pl.when(c == part)
        def _():
            for k, (cx, cy) in enumerate(chips):
                for i in range(n):
                    copy(refs, i, k, x, y, (cx, cy, part), own=True).wait_send()
                    copy(refs, i, 3 + k, cx, cy, (x, y, 1 - part)).wait_send()

        @pl.when(c != part)
        def _():
            for k, (cx, cy) in enumerate(chips):
                for i in range(n):
                    copy(refs, i, 3 + k, cx, cy, (x, y, 1 - part)).wait_recv()

    return _Rider(srcs, bufs, [], _dma_sems(n, 6), start, finish, mid)


def _pair_swap_rider(gs, layer):
    n = len(gs)

    def copy(refs, i):
        in_refs, _, out_refs, (send_sems, recv_sems) = refs
        x, y, _c = _me()
        return pltpu.make_async_remote_copy(src_ref=in_refs[i], dst_ref=out_refs[i], send_sem=send_sems.at[i],
                                            recv_sem=recv_sems.at[i], device_id=(x, y, layer), device_id_type=MESH)

    def start(*refs):
        @pl.when(_me()[2] != layer)
        def _():
            for i in range(n):
                copy(refs, i).start()

    def finish(*refs):
        @pl.when(_me()[2] != layer)
        def _():
            for i in range(n):
                copy(refs, i).wait_send()

        @pl.when(_me()[2] == layer)
        def _():
            for i in range(n):
                copy(refs, i).wait_recv()

    return _Rider(gs, [], [jax.ShapeDtypeStruct(g.shape, g.dtype) for g in gs], _dma_sems(n), start, finish)


def _chip_send_rider(pairs, layer):
    n = len(pairs)

    def copies(refs):
        in_refs, _, out_refs, (send_sems, recv_sems) = refs
        x, y, _c = _me()
        return [pltpu.make_async_remote_copy(src_ref=in_refs[i].at[2 * cx + cy], dst_ref=out_refs[i].at[k],
                                             send_sem=send_sems.at[i, k], recv_sem=recv_sems.at[i, k],
                                             device_id=(cx, cy, layer), device_id_type=MESH)
                for k, (cx, cy) in enumerate(_other_chips(x, y)) for i in range(n)]

    def start(*refs):
        @pl.when(_me()[2] == layer)
        def _():
            for cp in copies(refs):
                cp.start()

    def finish(*refs):
        @pl.when(_me()[2] == layer)
        def _():
            for cp in copies(refs):
                cp.wait()

    outs = [jax.ShapeDtypeStruct((3,) + p.shape[1:], p.dtype) for p in pairs]
    return _Rider(pairs, [], outs, _dma_sems(n, 3), start, finish)


def _bcast_rider(srcs):
    n = len(srcs)

    def copies(refs):
        in_refs, _, out_refs, (send_sems, recv_sems) = refs
        x, y, c = _me()
        return [pltpu.make_async_remote_copy(src_ref=in_refs[i], dst_ref=out_refs[i].at[k], send_sem=send_sems.at[i, k],
                                             recv_sem=recv_sems.at[i, k], device_id=(cx, cy, c), device_id_type=MESH)
                for k, (cx, cy) in enumerate(_other_chips(x, y)) for i in range(n)]

    def start(*refs):
        for cp in copies(refs):
            cp.start()

    def finish(*refs):
        for cp in copies(refs):
            cp.wait()

    return _Rider(srcs, [], [jax.ShapeDtypeStruct((3,) + s.shape, s.dtype) for s in srcs], _dma_sems(n, 3), start, finish)


def _merge_riders(a, b):
    assert not a.alias and not b.alias and a.mid is None and b.mid is None

    def both(fa, fb):
        def run(r_in, r_alias, r_out, sems):
            fa(r_in[:len(a.ins)], (), r_out[:len(a.outs)], sems[:len(a.sems)])
            fb(r_in[len(a.ins):], (), r_out[len(a.outs):], sems[len(a.sems):])
        return run

    return _Rider(a.ins + b.ins, [], a.outs + b.outs, a.sems + b.sems, both(a.start, b.start), both(a.finish, b.finish))


def _join_parts_multi(name, bufs):
    n = len(bufs)

    def body(*refs):
        out_refs = refs[n:2 * n]
        send_sems, recv_sems = refs[2 * n:]
        x, y, c = _me()

        def copy(i, part):
            return pltpu.make_async_remote_copy(src_ref=out_refs[i].at[part], dst_ref=out_refs[i].at[part],
                                                send_sem=send_sems.at[i], recv_sem=recv_sems.at[i],
                                                device_id=(x, y, 1 - c), device_id_type=MESH)

        for i in range(n):
            copy(i, c).start()
        for i in range(n):
            copy(i, c).wait_send()
            copy(i, 1 - c).wait_recv()

    return pl.pallas_call(
        body, name=name, in_specs=[_ANY] * n, out_specs=[_ANY] * n,
        out_shape=[jax.ShapeDtypeStruct(b.shape, b.dtype) for b in bufs],
        input_output_aliases={i: i for i in range(n)},
        scratch_shapes=[pltpu.SemaphoreType.DMA((n,)), pltpu.SemaphoreType.DMA((n,))],
    )(*bufs)


def _col_tiles(R, C):
    tr = _pick(R, (512, 256, 128))
    if tr != R:
        return tr, C
    if R * C <= 512 * 1024:
        return R, C
    return R, _pick(C, (256, 128))


def _active(layer):
    return (_me()[2] == layer).astype(jnp.int32).reshape(1)


def _pair_sum(name, g, theirs, layer):
    n4, R, C = g.shape
    tr, tc = _col_tiles(R, C)

    def body(a_ref, g_ref, t_ref, o_ref, ob_ref):
        s = g_ref[...] + t_ref[...]
        o_ref[...] = s
        ob_ref[...] = s.astype(BF16)

    spec = pl.BlockSpec((1, tr, tc), lambda d, i, k, a: (d * a[0], i * a[0], k * a[0]))
    return pl.pallas_call(
        body, name=name,
        grid_spec=pltpu.PrefetchScalarGridSpec(num_scalar_prefetch=1, grid=(n4, R // tr, C // tc),
                                               in_specs=[spec, spec], out_specs=[spec, spec]),
        out_shape=[jax.ShapeDtypeStruct((n4, R, C), F32), jax.ShapeDtypeStruct((n4, R, C), BF16)],
        compiler_params=_cparams(("arbitrary", "arbitrary", "arbitrary")),
    )(_active(layer), g, theirs)


def _chip_sum(name, pair, got, j, layer, buf=None):
    _, R, C = pair.shape
    tr, tc = _col_tiles(R, C)

    def body(ja_ref, p_ref, g0, g1, g2, *rest):
        rest[-1][0] = (p_ref[0] + g2[0].astype(F32)) + (g0[0].astype(F32) + g1[0].astype(F32))

    def gspec(k):
        return pl.BlockSpec((1, tr, tc), lambda i, q, ja, k=k: (k, i * ja[1], q * ja[1]))

    in_specs = [pl.BlockSpec((1, tr, tc), lambda i, q, ja: (ja[0], i * ja[1], q * ja[1])), gspec(0), gspec(1), gspec(2)]
    args = [jnp.concatenate([j.reshape(1).astype(jnp.int32), _active(layer)]), pair, got, got, got]
    aliases = {}
    if buf is not None:
        in_specs.append(_ANY)
        args.append(buf)
        aliases = {5: 0}
    return pl.pallas_call(
        body, name=name,
        grid_spec=pltpu.PrefetchScalarGridSpec(
            num_scalar_prefetch=1, grid=(R // tr, C // tc), in_specs=in_specs,
            out_specs=pl.BlockSpec((1, tr, tc), lambda i, q, ja: (layer, i * ja[1], q * ja[1]))),
        out_shape=jax.ShapeDtypeStruct((2, R, C), F32), input_output_aliases=aliases,
        compiler_params=_cparams(("arbitrary", "arbitrary")),
    )(*args)


LANES = 1024
_BIG = (("w_in", "col"), ("ffn_w_in", "col"), ("mem_w_kv", "col"),
        ("w_br_lru", "row"), ("w_br_ssd", "row"), ("w_br_xa", "row"), ("w_out", "row"), ("ffn_w_down", "row"))
_SMALL_SHARDED = ("b_gate", "lru_conv_w", "ssd_conv_w")
_SMALL = ("b_gate", "lru_conv_w", "lru_conv_b", "lru_w_a", "lru_b_a", "lru_w_i", "lru_b_i", "lru_lambda",
          "ssd_conv_w", "ssd_conv_b", "ssd_dt_bias", "ssd_a_log", "ssd_d", "ssd_norm_w",
          "ln1_g", "ln1_b", "ln2_g", "ln2_b")
_W_NAMES = ("w_in", "b_gate", "lru_conv_w", "lru_conv_b", "lru_w_a", "lru_b_a", "lru_w_i", "lru_b_i", "lru_lambda",
            "ssd_conv_w", "ssd_conv_b", "ssd_dt_bias", "ssd_a_log", "ssd_d", "ssd_norm_w", "mem_w_kv", "w_br_lru",
            "w_br_ssd", "w_br_xa", "w_out", "ln1_g", "ln1_b", "ffn_w_in", "ffn_w_down", "ln2_g", "ln2_b")
_IN_ORDER = ((4096, 7168), (8224, 11296), (2048, 4096), (0, 1024), (1024, 2048), (7200, 8224))
_IN_DT = (7168, 7200)


def _flat_rows(parts, row_multiple):
    flat = jnp.concatenate([p.reshape(-1) for p in parts])
    rows = -(-flat.size // LANES)
    rows = -(-rows // row_multiple) * row_multiple
    return jnp.pad(flat, (0, rows * LANES - flat.size)).reshape(rows, LANES)


def _take_parts(flat, shapes):
    out, off = [], 0
    for shp in shapes:
        size = math.prod(shp)
        out.append(flat[off:off + size].reshape(shp))
        off += size
    return out


_GATHER_NAMES = ("w_in", "w_in_edge") + tuple(nm for nm, _ in _BIG[1:]) + _SMALL_SHARDED
_GATHER_KINDS = ("rows",) + ("blk",) * (len(_GATHER_NAMES) - 1)


def _gather_sources(w):
    x, y, _ = _me()
    j = 2 * x + y
    wt = jnp.swapaxes(w["w_in"], 1, 2)
    inner = lax.dynamic_slice_in_dim(wt, 8 * y, W_IN_INNER, axis=1).astype(BF16)
    edge = lax.dynamic_slice_in_dim(wt, (1 - y) * W_IN_INNER, 8, axis=1)
    srcs = [inner, edge] + [w[nm].astype(BF16) for nm, _ in _BIG[1:]] + [w[nm] for nm in _SMALL_SHARDED]
    bufs = []
    for l in range(DEPTH):
        row = []
        for s, kd in zip(srcs, _GATHER_KINDS):
            if kd == "blk":
                row.append(lax.dynamic_update_slice_in_dim(lax.empty((4,) + s.shape[1:], s.dtype), s[l][None], j, axis=0))
            else:
                row.append(lax.dynamic_update_slice_in_dim(lax.empty((N_IN, D_MODEL), s.dtype), s[l],
                                                           j * W_IN_SHARD + 8 * y, axis=0))
        bufs.append(row)
    return srcs, bufs


def _layer_weights(names, got):
    g = dict(zip(names, got))
    full = {}
    if "w_in" in g:
        wt_all, edges = g["w_in"], g["w_in_edge"]
        for a, b in ((0, 1), (2, 3)):
            tile = jnp.concatenate([edges[a], edges[b]], axis=0).astype(BF16)
            wt_all = lax.dynamic_update_slice_in_dim(wt_all, tile, b * W_IN_SHARD - 8, axis=0)
        full["w_re_t"] = jnp.concatenate([wt_all[lo:hi] for lo, hi in _IN_ORDER], axis=0)
        full["w_dt_t"] = jnp.pad(wt_all[_IN_DT[0]:_IN_DT[1]], ((0, DT_PAD - SSD_HEADS), (0, 0)))
    kinds = dict(_BIG)
    for nm in names:
        if nm in ("w_in", "w_in_edge"):
            continue
        _, r, cdim = g[nm].shape
        if kinds.get(nm, "col") == "col":
            full[nm] = jnp.moveaxis(g[nm], 0, 1).reshape(r, 4 * cdim)
        else:
            full[nm] = g[nm].reshape(4 * r, cdim)
    return full


def _per_chip(g, nm):
    r, cdim = g.shape
    if nm == "w_in" or dict(_BIG)[nm] == "row":
        return g.reshape(4, r // 4, cdim)
    return jnp.moveaxis(g.reshape(r, 4, cdim // 4), 1, 0)


def _layer_fwd(xin, mem2, W, P, consts, riders=None, late=None):
    S = xin.shape[0]
    T, TX = min(256, S), min(512, S)
    sv = {"xin": xin}
    riders = riders or {}
    rout = {}
    proj = _mm("proj", xin, W["w_re_t"], "nt", BF16, rider=riders.get("proj"))
    if "proj" in riders:
        proj, rout["proj"] = proj[0], proj[1:]
    if late is not None:
        w_late, p_late = late(rout)
        W, P = {**W, **w_late}, {**P, **p_late}
    dt_raw = _mm("proj_dt", xin, W["w_dt_t"], "nt", F32)
    xc = _conv_fwd("lru_conv", proj, CB_LRU_X, P["lru_conv_w"], P["lru_conv_b"], False, F32, T)
    res = _lru_fwd("lru_scan", xc, proj, P["lru_w_a"], P["lru_w_i"], P["lru_b_a"], P["lru_b_i"], P["lru_lambda"], T,
                   rider=riders.get("lru_scan"))
    y_lru, h = res[0], res[1]
    if "lru_scan" in riders:
        rout["lru_scan"] = res[2:]
    xact = _conv_fwd("ssd_conv", proj, CB_XBC, P["ssd_conv_w"], P["ssd_conv_b"], True, BF16, T)
    res = _ssd_fwd("ssd_scan", xact, dt_raw, P["dt_bias"], P["a_neg"], P["d_exp"], consts, rider=riders.get("ssd_scan"))
    ycore, prev = res[0], res[1]
    if "ssd_scan" in riders:
        rout["ssd_scan"] = res[2:]
    y_ssd = _gate_norm_fwd("ssd_norm", ycore, proj, P["ssd_norm_w"], T)
    kv = _mm("mem_kv", mem2, W["w_kv"], "nn", BF16)
    y_xa = _xa_fwd("xattn", proj, kv, TX)
    p_l = _mm("br_lru", y_lru, W["w_l"], "nn", BF16)
    p_s = _mm("br_ssd", y_ssd, W["w_s"], "nn", BF16)
    p_x = _mm("br_xa", y_xa, W["w_x"], "nn", BF16)
    merged = _merge_fwd("merge", proj, P["b_gate"], p_l, p_s, p_x, T)
    mix = _mm("mix_out", merged, W["w_o"], "nn", F32)
    x1, xh1, rs1 = _ln_fwd("ln_fwd", xin, mix, P["ln1_g"], P["ln1_b"], T)
    gu = _mm("ffn_in", x1, W["w_fi"], "nn", BF16, rider=riders.get("ffn_in"))
    if "ffn_in" in riders:
        gu, rout["ffn_in"] = gu[0], gu[1:]
    hmid = _swiglu_fwd("swiglu", gu, T)
    f = _mm("ffn_down", hmid, W["w_fd"], "nn", F32)
    x2, xh2, rs2 = _ln_fwd("ln_fwd", x1, f, P["ln2_g"], P["ln2_b"], T)
    sv.update(proj=proj, dt_raw=dt_raw, xc=xc, h=h, y_lru=y_lru, xact=xact, ycore=ycore, prev=prev, y_ssd=y_ssd, kv=kv,
              y_xa=y_xa, p_l=p_l, p_s=p_s, p_x=p_x, merged=merged, x1=x1, xh1=xh1, rs1=rs1, gu=gu, hmid=hmid,
              xh2=xh2, rs2=rs2)
    return x2, sv, rout


def _layer_bwd(dys, coefs, sv, mem2, W, P, consts, riders=None):
    S = sv["xin"].shape[0]
    T, TX = min(256, S), min(512, S)
    proj = sv["proj"]
    g = {}
    rout = {}

    def ride(host):
        return riders[host](g, rout) if riders and host in riders else None

    dz2, g["ln2_g"], g["ln2_b"] = _ln_bwd("ln_bwd_%d" % len(dys), dys, coefs, sv["xh2"], sv["rs2"], P["ln2_g"], T)
    dhmid = _mm("d_hmid", dz2, W["w_fd"], "nt", BF16)
    g["ffn_w_down"] = _mm("dw_ffn_down", sv["hmid"], dz2, "tn", F32)
    dgu = _swiglu_bwd("swiglu_bwd", sv["gu"], dhmid, T)
    dx1f = _mm("d_x1", dgu, W["w_fi"], "nt", F32)
    g["ffn_w_in"] = _mm("dw_ffn_in", sv["x1"], dgu, "tn", F32)
    dz1, g["ln1_g"], g["ln1_b"] = _ln_bwd("ln_bwd_2", [dz2, dx1f], [ALPHA, 1.0], sv["xh1"], sv["rs1"], P["ln1_g"], T)
    dmerged = _mm("d_merged", dz1, W["w_o"], "nt", BF16)
    g["w_out"] = _mm("dw_out", sv["merged"], dz1, "tn", F32)
    dproj, dpl, dps, dpx, dbg0, dbg1, dbg2 = _merge_bwd("merge_bwd", proj, P["b_gate"], sv["p_l"], sv["p_s"], sv["p_x"],
                                                         dmerged, T)
    g["b_gate"] = jnp.concatenate([dbg0, dbg1, dbg2], axis=0)
    dy_lru = _mm("d_ylru", dpl, W["w_l"], "nt", BF16)
    g["w_br_lru"] = _mm("dw_br_lru", sv["y_lru"], dpl, "tn", F32)
    dy_ssd = _mm("d_yssd", dps, W["w_s"], "nt", BF16)
    g["w_br_ssd"] = _mm("dw_br_ssd", sv["y_ssd"], dps, "tn", F32)
    dy_xa = _mm("d_yxa", dpx, W["w_x"], "nt", BF16)
    g["w_br_xa"] = _mm("dw_br_xa", sv["y_xa"], dpx, "tn", F32)
    dproj, dkv = _xa_bwd("xattn_bwd", proj, sv["kv"], dy_xa, dproj, TX)
    g["mem_w_kv"] = _mm("dw_kv", mem2, dkv, "tn", F32)
    dproj, dycore, g["ssd_norm_w"] = _gate_norm_bwd("ssd_norm_bwd", dy_ssd, sv["ycore"], proj, P["ssd_norm_w"], dproj, T)
    res = _ssd_bwd("ssd_scan_bwd", sv["xact"], sv["dt_raw"], P["dt_bias"], P["a_neg"], P["d_exp"],
                   sv["prev"], dycore, consts, rider=ride("ssd_scan_bwd"))
    dxact, ddt, d_a, g_dtb, d_dexp = res[:5]
    rout["ssd_scan_bwd"] = res[5:]
    g["ssd_dt_bias"] = g_dtb[:, :SSD_HEADS]
    g["ssd_a_log"] = d_a[:, :SSD_HEADS] * P["a_neg"][:, :SSD_HEADS]
    g["ssd_d"] = jnp.sum(d_dexp.reshape(SSD_HEADS, SSD_HEAD_DIM), axis=-1)
    res = _conv_bwd("ssd_conv_bwd", proj, CB_XBC, P["ssd_conv_w"], P["ssd_conv_b"], True, dxact, dproj, T,
                    rider=ride("ssd_conv_bwd"))
    dproj, w0, w1, w2, w3, g["ssd_conv_b"] = res[:6]
    rout["ssd_conv_bwd"] = res[6:]
    g["ssd_conv_w"] = jnp.concatenate([w0, w1, w2, w3], axis=0)
    res = _lru_bwd("lru_scan_bwd", dy_lru, sv["xc"], proj, sv["h"], P["lru_w_a"], P["lru_w_i"], P["lru_b_a"], P["lru_b_i"],
                   P["lru_lambda"], dproj, T, rider=ride("lru_scan_bwd"))
    dproj, dxc, g["lru_w_a"], g["lru_w_i"], g["lru_b_a"], g["lru_b_i"], g["lru_lambda"] = res[:7]
    rout["lru_scan_bwd"] = res[7:]
    dproj, w0, w1, w2, w3, g["lru_conv_b"] = _conv_bwd("lru_conv_bwd", proj, CB_LRU_X, P["lru_conv_w"], P["lru_conv_b"], False,
                                                       dxc, dproj, T)
    g["lru_conv_w"] = jnp.concatenate([w0, w1, w2, w3], axis=0)
    xin = sv["xin"]
    rd = ride("dw_in")
    dw_re_t = _mm("dw_in", dproj, xin, "tn", F32, rider=rd)
    if rd is not None:
        dw_re_t, rout["dw_in"] = dw_re_t[0], dw_re_t[1:]
    dw_dt_t = _mm("dw_in_dt", ddt, xin, "tn", F32)
    pieces = {rng: dw_re_t[off:off + rng[1] - rng[0]]
              for rng, off in zip(_IN_ORDER, (0, 3072, 6144, 8192, 9216, 10240))}
    pieces[_IN_DT] = dw_dt_t[:SSD_HEADS]
    g["w_in"] = jnp.concatenate([pieces[k] for k in sorted(pieces)], axis=0)
    rd = ride("d_xin")
    dxp = _mm("d_xin", dproj, W["w_re_t"], "nn", F32, rider=rd)
    if rd is not None:
        dxp, rout["d_xin"] = dxp[0], dxp[1:]
    dxs = _mm("d_xin_dt", ddt, W["w_dt_t"], "nn", F32, add=dxp)
    return [dz1, dxs], [ALPHA, 1.0], g, rout


def _step(a):
    x2d, mem2, target = a["x"][0], a["mem"][0], a["loss_target"][0]
    S = x2d.shape[0]
    T = min(256, S)
    xi, yi, ci = _me()
    j = 2 * xi + yi
    w = {nm: a[nm] for nm in _W_NAMES}
    consts = _ssd_consts()
    row = lambda v: v.reshape(1, -1)
    pad_h = lambda v: jnp.pad(v.reshape(1, -1), ((0, 0), (0, DT_PAD - SSD_HEADS)))

    w_keys = dict(w_re_t="w_re_t", w_dt_t="w_dt_t", w_fi="ffn_w_in", w_kv="mem_w_kv", w_l="w_br_lru", w_s="w_br_ssd",
                  w_x="w_br_xa", w_o="w_out", w_fd="ffn_w_down")

    def gathered_params(full):
        return ({k: full[v] for k, v in w_keys.items() if v in full}, {k: full[k] for k in _SMALL_SHARDED if k in full})

    def local_params(l):
        return dict(
            lru_conv_b=row(w["lru_conv_b"][l]), lru_w_a=w["lru_w_a"][l].astype(BF16), lru_w_i=w["lru_w_i"][l].astype(BF16),
            lru_b_a=row(w["lru_b_a"][l]), lru_b_i=row(w["lru_b_i"][l]), lru_lambda=row(w["lru_lambda"][l]),
            ssd_conv_b=row(w["ssd_conv_b"][l]), dt_bias=pad_h(w["ssd_dt_bias"][l]), a_neg=pad_h(-jnp.exp(w["ssd_a_log"][l])),
            d_exp=jnp.broadcast_to(w["ssd_d"][l][:, None], (SSD_HEADS, SSD_HEAD_DIM)).reshape(1, D_SSD),
            ssd_norm_w=row(w["ssd_norm_w"][l]), ln1_g=row(w["ln1_g"][l]), ln1_b=row(w["ln1_b"][l]),
            ln2_g=row(w["ln2_g"][l]), ln2_b=row(w["ln2_b"][l]))

    srcs, bufs = _gather_sources(w)
    names, kinds = list(_GATHER_NAMES), list(_GATHER_KINDS)
    groups = {"w_in": [0, 1], "mixer": [3, 4, 5, 6, 7, 9, 10, 11], "ffn": [2, 8]}

    def gather(group, layer):
        idx = groups[group]
        return _gather_rider([srcs[i] for i in idx], [kinds[i] for i in idx], [bufs[layer][i] for i in idx], layer)

    def weights_of(group, got):
        return _layer_weights([names[i] for i in groups[group]], got)

    rest = groups["mixer"] + groups["ffn"]
    got = _run_rider("gather_weights", gather("w_in", 0))
    W0, _ = gathered_params(weights_of("w_in", got))

    def late0(rout_):
        wl, pl_ = gathered_params(_layer_weights([names[i] for i in rest], rout_["proj"]))
        return wl, pl_

    rest_rider = _gather_rider([srcs[i] for i in rest], [kinds[i] for i in rest], [bufs[0][i] for i in rest], 0)
    x1, sv0, rout = _layer_fwd(x2d, mem2, W0, local_params(0), consts, late=late0, riders={
        "proj": rest_rider, "lru_scan": gather("mixer", 1), "ssd_scan": gather("w_in", 1), "ffn_in": gather("ffn", 1)})
    full1 = {**weights_of("w_in", rout["ssd_scan"]), **weights_of("mixer", rout["lru_scan"]), **weights_of("ffn", rout["ffn_in"])}
    W1, P1g = gathered_params(full1)
    P1 = {**local_params(1), **P1g}
    w0_late, p0_late = late0(rout)
    W0, P0 = {**W0, **w0_late}, {**local_params(0), **p0_late}
    xcur, sv1, _ = _layer_fwd(x1, mem2, W1, P1, consts)
    dy, loss_part = _loss_fwd_bwd("loss", xcur, target, T)
    loss = lax.psum(loss_part[0, 0], ("x", "y", "c"))

    big_names = [nm for nm, _ in _BIG]
    rest_names = big_names[1:]
    st = {}

    def swap_rider(key, names, layer):
        def build(g, rout_):
            st[key] = [_per_chip(g[nm], nm) for nm in names]
            return _pair_swap_rider(st[key], layer)
        return build

    def pair_sums(key, theirs, layer):
        return [_pair_sum("grads_pair_sum", g_, t_, layer) for g_, t_ in zip(st[key], theirs)]

    dys, coefs, g1, rout1 = _layer_bwd([dy], [1.0], sv1, mem2, W1, P1, consts,
                                       riders={"d_xin": swap_rider("g1", big_names, 1)})
    pairs1 = pair_sums("g1", rout1["d_xin"], 1)

    def send_rest0(g, rout_):
        st["pairs0"] = pair_sums("g0", rout_["lru_scan_bwd"], 0)
        return _chip_send_rider([pb for _, pb in st["pairs0"]], 0)

    dys, coefs, g0, rout0 = _layer_bwd(dys, coefs, sv0, mem2, W0, P0, consts, riders={
        "ssd_scan_bwd": lambda g, r: _chip_send_rider([pb for _, pb in pairs1[:1]], 1),
        "ssd_conv_bwd": lambda g, r: _chip_send_rider([pb for _, pb in pairs1[1:]], 1),
        "lru_scan_bwd": swap_rider("g0", rest_names, 0),
        "dw_in": send_rest0,
        "d_xin": swap_rider("g0_in", ["w_in"], 0)})
    grad_x = _axpy("grad_x", coefs[0], dys[0], dys[1], T)[None]
    pairs0_in = pair_sums("g0_in", rout0["d_xin"], 0)
    layer_grads = [g0, g1]
    stacked = {nm: jnp.stack([layer_grads[l][nm] for l in range(DEPTH)]) for nm in _SMALL}
    small_parts = [stacked[nm].reshape((DEPTH,) + tuple(sh)) for nm, sh in
                   ((nm, (3, D_MODEL) if nm == "b_gate" else (4, D_MODEL) if nm == "lru_conv_w" else
                     (4, D_XBC) if nm == "ssd_conv_w" else w[nm].shape[1:]) for nm in _SMALL)]
    small_buf = _flat_rows(small_parts, 8)
    small_pair = _sum_terms("small_pair_sum", [small_buf, _swap_sibling("small_pair_swap", small_buf)], F32)
    sent = _run_rider("grads_chip_send", _merge_riders(_chip_send_rider([pb for _, pb in pairs0_in], 0),
                                                        _bcast_rider([small_pair])))
    got0_in, small_got = sent[:1], sent[1]
    small_total = _sum_terms("small_chip_sum", [small_pair, small_got[2], small_got[0], small_got[1]], F32)
    small = dict(zip(_SMALL, _take_parts(small_total.reshape(-1), [p.shape for p in small_parts])))
    pairs0 = pairs0_in + st["pairs0"]
    gots0 = list(got0_in) + list(rout0["dw_in"])
    halves = []
    gots1 = list(rout0["ssd_scan_bwd"]) + list(rout0["ssd_conv_bwd"])
    for (p1, _), gt1, (p0, _), gt0 in zip(pairs1, gots1, pairs0, gots0):
        buf = _chip_sum("grads_chip_sum", p1, gt1, j, 1)
        halves.append(_chip_sum("grads_chip_sum", p0, gt0, j, 0, buf))
    big = dict(zip(big_names, _join_parts_multi("grads_join", halves)))
    big["w_in"] = jnp.swapaxes(big["w_in"], 1, 2)
    for nm in _SMALL_SHARDED:
        cs = w[nm].shape[2]
        small[nm] = lax.dynamic_slice_in_dim(small[nm], j * cs, cs, axis=2)
    grads = {**big, **small}

    delta, new_m, new_v = {}, {}, {}
    for nm, _ in _BIG:
        delta[nm], new_m[nm], new_v[nm] = _adamw("adamw_" + nm, w[nm], grads[nm], a["m_" + nm], a["v_" + nm])
    shapes = [w[nm].shape for nm in _SMALL]
    packs = [_flat_rows([src[nm] for nm in _SMALL], 8)[None] for src in
             (w, grads, {nm: a["m_" + nm] for nm in _SMALL}, {nm: a["v_" + nm] for nm in _SMALL})]
    d_, m_, v_ = _adamw("adamw_small", *packs)
    for dst, buf in ((delta, d_), (new_m, m_), (new_v, v_)):
        dst.update(zip(_SMALL, _take_parts(buf.reshape(-1), shapes)))

    outs = [loss, grad_x]
    for group in (grads, delta, new_m, new_v):
        outs += [group[nm] for nm in _W_NAMES]
    return tuple(outs)


def kernel(x, mem, w_in, b_gate, lru_conv_w, lru_conv_b, lru_w_a, lru_b_a, lru_w_i, lru_b_i, lru_lambda, ssd_conv_w, ssd_conv_b, ssd_dt_bias, ssd_a_log, ssd_d, ssd_norm_w, mem_w_kv, w_br_lru, w_br_ssd, w_br_xa, w_out, ln1_g, ln1_b, ffn_w_in, ffn_w_down, ln2_g, ln2_b, loss_target, m_w_in, m_b_gate, m_lru_conv_w, m_lru_conv_b, m_lru_w_a, m_lru_b_a, m_lru_w_i, m_lru_b_i, m_lru_lambda, m_ssd_conv_w, m_ssd_conv_b, m_ssd_dt_bias, m_ssd_a_log, m_ssd_d, m_ssd_norm_w, m_mem_w_kv, m_w_br_lru, m_w_br_ssd, m_w_br_xa, m_w_out, m_ln1_g, m_ln1_b, m_ffn_w_in, m_ffn_w_down, m_ln2_g, m_ln2_b, v_w_in, v_b_gate, v_lru_conv_w, v_lru_conv_b, v_lru_w_a, v_lru_b_a, v_lru_w_i, v_lru_b_i, v_lru_lambda, v_ssd_conv_w, v_ssd_conv_b, v_ssd_dt_bias, v_ssd_a_log, v_ssd_d, v_ssd_norm_w, v_mem_w_kv, v_w_br_lru, v_w_br_ssd, v_w_br_xa, v_w_out, v_ln1_g, v_ln1_b, v_ffn_w_in, v_ffn_w_down, v_ln2_g, v_ln2_b):
    return _step(dict(locals()))
```

```python
import functools
import math

import jax
import jax.numpy as jnp
from jax import lax
from jax.experimental import pallas as pl
from jax.experimental.pallas import tpu as pltpu

F32, BF16 = jnp.float32, jnp.bfloat16
MESH = pl.DeviceIdType.MESH
VMEM_LIMIT_BYTES = 56 * 2**20
HALO = 16

D_MODEL = 1024
DEPTH = 2
CHUNK = 64
N_MEM = 256
LRU_BLOCKS = 8
LRU_BLOCK = 128
LRU_C = 8.0
D_SSD = 2048
SSD_HEADS = 32
SSD_HEAD_DIM = 64
SSD_GROUPS = 4
SSD_STATE = 128
D_BC = SSD_GROUPS * SSD_STATE
D_XBC = D_SSD + 2 * D_BC
XA_HEADS = 4
XA_HEAD_DIM = 256
D_FF = 2816
ALPHA = (2 * DEPTH) ** 0.25
EPS = 1e-5
N_IN = 11296
N_PROJ = 11264
DT_PAD = 128

ADAM_LR, ADAM_B1, ADAM_B2, ADAM_EPS, ADAM_WD, ADAM_STEP = 0.001, 0.9, 0.999, 1e-08, 0.01, 10

CB_XBC = (3072, 0)
CB_XS, CB_BM, CB_CM = (2048, 0), (512, 4), (512, 5)
CB_LOGITS = (3072, 1)
CB_G0, CB_G1, CB_G2 = (1024, 3), (1024, 4), (1024, 5)
CB_Z = (2048, 3)
CB_LRU_X, CB_LRU_GATE, CB_XA_Q = (1024, 8), (1024, 9), (1024, 10)


def _cparams(sem):
    return pltpu.CompilerParams(dimension_semantics=sem, vmem_limit_bytes=VMEM_LIMIT_BYTES)


MID_STEP_PCT = 65


class _Rider:
    def __init__(self, ins, alias, outs, sems, start, finish, mid=None):
        self.ins, self.alias, self.outs, self.sems = list(ins), list(alias), list(outs), list(sems)
        self.start, self.finish = start, finish
        self.mid = mid

    def operands(self):
        return self.ins + self.alias

    def out_shapes(self):
        return [jax.ShapeDtypeStruct(a.shape, a.dtype) for a in self.alias] + self.outs

    def aliases(self, n_in, n_out):
        return {n_in + len(self.ins) + q: n_out + q for q in range(len(self.alias))}

    def split(self, refs, n_in, n_out, n_scratch):
        ni, na, no = len(self.ins), len(self.alias), len(self.outs)
        main = list(refs[:n_in])
        r_in = refs[n_in:n_in + ni]
        p = n_in + ni + na
        main += refs[p:p + n_out]
        r_alias = refs[p + n_out:p + n_out + na]
        r_out = refs[p + n_out + na:p + n_out + na + no]
        p = p + n_out + na + no
        main += refs[p:p + n_scratch]
        sems = refs[p + n_scratch:]
        return main, (r_in, r_alias, r_out, sems)


def _run_rider(name, rider):
    def body(*refs):
        _, parts = rider.split(refs, 0, 0, 0)
        rider.start(*parts)
        if rider.mid is not None:
            rider.mid(*parts)
        rider.finish(*parts)

    n_ops = len(rider.operands())
    return pl.pallas_call(
        body, name=name, in_specs=[_ANY] * n_ops, out_specs=[_ANY] * len(rider.out_shapes()),
        out_shape=rider.out_shapes(), input_output_aliases=rider.aliases(0, 0), scratch_shapes=rider.sems,
    )(*rider.operands())


def _pcall(name, body, n, ins, outs, scratch=(), reverse=False, aliases=None, rider=None):
    def ridx(i):
        return (n - 1 - i) if reverse else i

    in_specs, args = [], []
    for sp in ins:
        kind, arr = sp[0], sp[1]
        if kind == "row":
            _, _, tile, width, cb = sp
            in_specs.append(pl.BlockSpec((tile, width), lambda i, cb=cb: (ridx(i), cb)))
        elif kind == "prev":
            _, _, tile, width, cb = sp
            t = tile // HALO
            in_specs.append(pl.BlockSpec((HALO, width), lambda i, cb=cb, t=t: (jnp.maximum(ridx(i) * t - 1, 0), cb)))
        elif kind == "next":
            _, _, tile, width, cb = sp
            t = tile // HALO
            last = arr.shape[0] // HALO - 1
            in_specs.append(pl.BlockSpec((HALO, width), lambda i, cb=cb, t=t, last=last: (jnp.minimum((ridx(i) + 1) * t, last), cb)))
        elif kind == "lead":
            nd = arr.ndim
            in_specs.append(pl.BlockSpec((sp[2],) + arr.shape[1:], lambda i, nd=nd: (ridx(i),) + (0,) * (nd - 1)))
        elif kind == "full":
            nd = arr.ndim
            in_specs.append(pl.BlockSpec(arr.shape, lambda i, nd=nd: (0,) * nd))
        elif kind == "any":
            in_specs.append(pl.BlockSpec(memory_space=pl.ANY))
        else:
            raise ValueError(kind)
        args.append(arr)
    out_specs, out_shape = [], []
    for sp in outs:
        kind = sp[0]
        if kind == "row":
            _, rows, cols, dtype, tile, width, cb = sp
            out_shape.append(jax.ShapeDtypeStruct((rows, cols), dtype))
            out_specs.append(pl.BlockSpec((tile, width), lambda i, cb=cb: (ridx(i), cb)))
        elif kind == "lead":
            _, shape, dtype, lead = sp
            nd = len(shape)
            out_shape.append(jax.ShapeDtypeStruct(shape, dtype))
            out_specs.append(pl.BlockSpec((lead,) + tuple(shape[1:]), lambda i, nd=nd: (ridx(i),) + (0,) * (nd - 1)))
        elif kind == "full":
            _, shape, dtype = sp
            nd = len(shape)
            out_shape.append(jax.ShapeDtypeStruct(shape, dtype))
            out_specs.append(pl.BlockSpec(tuple(shape), lambda i, nd=nd: (0,) * nd))
        else:
            raise ValueError(kind)
    aliases = dict(aliases or {})
    scratch = list(scratch)
    kernel_body = body
    if rider is not None:
        n_in, n_out, n_scratch = len(args), len(out_shape), len(scratch)

        def kernel_body(*refs):
            main, parts = rider.split(refs, n_in, n_out, n_scratch)
            i = pl.program_id(0)

            @pl.when(i == 0)
            def _():
                rider.start(*parts)

            if rider.mid is not None:
                @pl.when(i == (n * MID_STEP_PCT) // 100)
                def _():
                    rider.mid(*parts)

            body(*main)

            @pl.when(i == n - 1)
            def _():
                rider.finish(*parts)

        aliases.update(rider.aliases(n_in, n_out))
        args += rider.operands()
        in_specs += [_ANY] * len(rider.operands())
        out_shape += rider.out_shapes()
        out_specs += [_ANY] * len(rider.out_shapes())
        scratch += rider.sems
    res = pl.pallas_call(
        kernel_body, name=name, grid=(n,), in_specs=in_specs, out_specs=out_specs, out_shape=out_shape,
        scratch_shapes=scratch, input_output_aliases=aliases,
        compiler_params=_cparams(("arbitrary",)),
    )(*args)
    return res


def _pick(n, cands):
    for c in cands:
        if n % c == 0:
            return c
    return n


_MM_TILES = (1024, 1408, 512, 256, 128)


def _mm(name, a, b, mode, out_dtype, add=None, rider=None):
    if mode == "nn":
        (M, K), (K2, N) = a.shape, b.shape
    elif mode == "nt":
        (M, K), (N, K2) = a.shape, b.shape
    else:
        (K, M), (K2, N) = a.shape, b.shape
    assert K == K2, (name, a.shape, b.shape)
    tm = _pick(M, _MM_TILES)
    tn = _pick(N, _MM_TILES)
    if mode == "tn":
        tk = _pick(K, (1024, 512, 256))
    else:
        tk = K if K <= 2816 else _pick(K, _MM_TILES)
    nk = K // tk
    has_add = add is not None

    def body(*refs):
        if has_add:
            a_ref, b_ref, add_ref, o_ref, acc_ref = refs
        else:
            a_ref, b_ref, o_ref, acc_ref = refs
        k = pl.program_id(2)
        av = a_ref[...].astype(BF16)
        bv = b_ref[...].astype(BF16)
        if mode == "nn":
            p = jnp.dot(av, bv, preferred_element_type=F32)
        elif mode == "nt":
            p = lax.dot_general(av, bv, (((1,), (1,)), ((), ())), preferred_element_type=F32)
        else:
            p = lax.dot_general(av, bv, (((0,), (0,)), ((), ())), preferred_element_type=F32)

        def fin(v):
            if has_add:
                v = v + add_ref[...].astype(F32)
            o_ref[...] = v.astype(out_dtype)

        if nk == 1:
            fin(p)
        else:
            @pl.when(k == 0)
            def _():
                acc_ref[...] = p

            @pl.when(k > 0)
            def _():
                acc_ref[...] += p

            @pl.when(k == nk - 1)
            def _():
                fin(acc_ref[...])

    if mode == "nn":
        specs = [pl.BlockSpec((tm, tk), lambda i, j, k: (i, k)), pl.BlockSpec((tk, tn), lambda i, j, k: (k, j))]
    elif mode == "nt":
        specs = [pl.BlockSpec((tm, tk), lambda i, j, k: (i, k)), pl.BlockSpec((tn, tk), lambda i, j, k: (j, k))]
    else:
        specs = [pl.BlockSpec((tk, tm), lambda i, j, k: (k, i)), pl.BlockSpec((tk, tn), lambda i, j, k: (k, j))]
    args = [a, b]
    if has_add:
        specs.append(pl.BlockSpec((tm, tn), lambda i, j, k: (i, j)))
        args.append(add)
    acc_shape = (tm, tn) if nk > 1 else (8, 128)
    grid = (M // tm, N // tn, nk)
    out_spec = pl.BlockSpec((tm, tn), lambda i, j, k: (i, j))
    out_shape = jax.ShapeDtypeStruct((M, N), out_dtype)
    if rider is None:
        return pl.pallas_call(
            body, name=name, grid=grid, in_specs=specs, out_specs=out_spec, out_shape=out_shape,
            scratch_shapes=[pltpu.VMEM(acc_shape, F32)],
            compiler_params=_cparams(("parallel", "parallel", "arbitrary")),
        )(*args)
    n_in = len(args)

    def kernel_body(*refs):
        main, parts = rider.split(refs, n_in, 1, 1)
        i, j, k = pl.program_id(0), pl.program_id(1), pl.program_id(2)

        @pl.when((i == 0) & (j == 0) & (k == 0))
        def _():
            rider.start(*parts)

        if rider.mid is not None:
            flat = (grid[0] * grid[1] * grid[2] * MID_STEP_PCT) // 100
            mi, mj, mk = flat // (grid[1] * grid[2]), (flat // grid[2]) % grid[1], flat % grid[2]

            @pl.when((i == mi) & (j == mj) & (k == mk))
            def _():
                rider.mid(*parts)

        body(*main)

        @pl.when((i == grid[0] - 1) & (j == grid[1] - 1) & (k == grid[2] - 1))
        def _():
            rider.finish(*parts)

    return pl.pallas_call(
        kernel_body, name=name, grid=grid, in_specs=specs + [_ANY] * len(rider.operands()),
        out_specs=[out_spec] + [_ANY] * len(rider.out_shapes()), out_shape=[out_shape] + rider.out_shapes(),
        input_output_aliases=rider.aliases(n_in, 1), scratch_shapes=[pltpu.VMEM(acc_shape, F32)] + rider.sems,
        compiler_params=_cparams(("arbitrary", "arbitrary", "arbitrary")),
    )(*args, *rider.operands())


def _sigmoid(x):
    return 1.0 / (1.0 + jnp.exp(-x))


def _silu(x):
    return x * _sigmoid(x)


def _dsilu(x):
    s = _sigmoid(x)
    return s * (1.0 + x * (1.0 - s))


def _softplus(x):
    return jnp.maximum(x, 0.0) + jnp.log(1.0 + jnp.exp(-jnp.abs(x)))


_GELU_C = math.sqrt(2.0 / math.pi)


def _gelu(x):
    return 0.5 * x * (1.0 + jnp.tanh(_GELU_C * (x + 0.044715 * x * x * x)))


def _dgelu(x):
    t = jnp.tanh(_GELU_C * (x + 0.044715 * x * x * x))
    return 0.5 * (1.0 + t) + 0.5 * x * (1.0 - t * t) * _GELU_C * (1.0 + 3.0 * 0.044715 * x * x)


def _acc(ref, i, val):
    @pl.when(i == 0)
    def _():
        ref[...] = val

    @pl.when(i > 0)
    def _():
        ref[...] += val


def _rows(shape):
    return lax.broadcasted_iota(jnp.int32, shape, 0)


def _shift_down(x, k, halo8, first):
    r = pltpu.roll(x, k, 0)
    h = pltpu.roll(halo8, k, 0)
    h = jnp.where(first, 0.0, h)
    head = jnp.where(_rows(h.shape) < k, h, r[:8])
    if x.shape[0] == 8:
        return head
    return jnp.concatenate([head, r[8:]], axis=0)


def _shift_up(x, k, halo8, last):
    T = x.shape[0]
    r = pltpu.roll(x, T - k, 0)
    h = pltpu.roll(halo8, 8 - k, 0)
    h = jnp.where(last, 0.0, h)
    tail = jnp.where(_rows(h.shape) >= 8 - k, h, r[T - 8:])
    return jnp.concatenate([r[:T - 8], tail], axis=0)


def _ln_fwd(name, a, b, g, beta, tile):
    S, Dm = a.shape
    n = S // tile

    def body(a_ref, b_ref, g_ref, be_ref, y_ref, xh_ref, rs_ref):
        z = ALPHA * a_ref[...] + b_ref[...].astype(F32)
        mu = jnp.mean(z, axis=-1, keepdims=True)
        zc = z - mu
        var = jnp.mean(zc * zc, axis=-1, keepdims=True)
        rstd = lax.rsqrt(var + EPS)
        xh = zc * rstd
        y_ref[...] = xh * g_ref[...] + be_ref[...]
        xh_ref[...] = xh
        rs_ref[...] = rstd

    return _pcall(name, body, n,
                  [("row", a, tile, Dm, 0), ("row", b, tile, Dm, 0), ("full", g), ("full", beta)],
                  [("row", S, Dm, F32, tile, Dm, 0), ("row", S, Dm, F32, tile, Dm, 0), ("row", S, 1, F32, tile, 1, 0)])


def _ln_bwd(name, dys, coefs, xh, rstd, g, tile):
    S, Dm = xh.shape
    n = S // tile
    nd = len(dys)

    def body(*refs):
        dy_refs = refs[:nd]
        xh_ref, rs_ref, g_ref, dz_ref, dg_ref, db_ref = refs[nd:]
        i = pl.program_id(0)
        dy = coefs[0] * dy_refs[0][...].astype(F32)
        for k in range(1, nd):
            dy = dy + coefs[k] * dy_refs[k][...].astype(F32)
        xh_v = xh_ref[...]
        dxh = dy * g_ref[...]
        m1 = jnp.mean(dxh, axis=-1, keepdims=True)
        m2 = jnp.mean(dxh * xh_v, axis=-1, keepdims=True)
        dz_ref[...] = rs_ref[...] * (dxh - m1 - xh_v * m2)
        _acc(dg_ref, i, jnp.sum(dy * xh_v, axis=0, keepdims=True))
        _acc(db_ref, i, jnp.sum(dy, axis=0, keepdims=True))

    ins = [("row", d, tile, Dm, 0) for d in dys]
    ins += [("row", xh, tile, Dm, 0), ("row", rstd, tile, 1, 0), ("full", g)]
    return _pcall(name, body, n, ins,
                  [("row", S, Dm, F32, tile, Dm, 0), ("full", (1, Dm), F32), ("full", (1, Dm), F32)])


def _loss_fwd_bwd(name, y, target, tile):
    S, Dm = y.shape
    n = S // tile

    def body(y_ref, t_ref, dy_ref, l_ref):
        i = pl.program_id(0)
        err = y_ref[...] - t_ref[...]
        dy_ref[...] = err * (1.0 / Dm)
        part = jnp.sum(jnp.sum(err * err, axis=-1, keepdims=True), axis=0, keepdims=True) * (0.5 / Dm)
        _acc(l_ref, i, part)

    return _pcall(name, body, n, [("row", y, tile, Dm, 0), ("row", target, tile, Dm, 0)],
                  [("row", S, Dm, F32, tile, Dm, 0), ("full", (1, 1), F32)])


def _swiglu_fwd(name, gu, tile):
    S = gu.shape[0]
    n = S // tile

    def body(g_ref, u_ref, o_ref):
        o_ref[...] = (_silu(g_ref[...].astype(F32)) * u_ref[...].astype(F32)).astype(BF16)

    return _pcall(name, body, n, [("row", gu, tile, D_FF, 0), ("row", gu, tile, D_FF, 1)],
                  [("row", S, D_FF, BF16, tile, D_FF, 0)])[0]


def _swiglu_bwd(name, gu, dh, tile):
    S = gu.shape[0]
    n = S // tile

    def body(g_ref, u_ref, dh_ref, o_ref):
        gv = g_ref[...].astype(F32)
        uv = u_ref[...].astype(F32)
        dv = dh_ref[...].astype(F32)
        dg = dv * uv * _dsilu(gv)
        du = dv * _silu(gv)
        o_ref[...] = jnp.concatenate([dg, du], axis=1).astype(BF16)

    return _pcall(name, body, n, [("row", gu, tile, D_FF, 0), ("row", gu, tile, D_FF, 1), ("row", dh, tile, D_FF, 0)],
                  [("row", S, 2 * D_FF, BF16, tile, 2 * D_FF, 0)])[0]


def _merge_fwd(name, proj, b_gate, pl_, ps_, px_, tile):
    S = proj.shape[0]
    n = S // tile
    Dm = D_MODEL

    def body(l0, l1, l2, bg, p0, p1, p2, o_ref):
        bgv = bg[...]
        acc = _sigmoid(l0[...].astype(F32) + bgv[0:1]) * p0[...].astype(F32)
        acc = acc + _sigmoid(l1[...].astype(F32) + bgv[1:2]) * p1[...].astype(F32)
        acc = acc + _sigmoid(l2[...].astype(F32) + bgv[2:3]) * p2[...].astype(F32)
        o_ref[...] = acc.astype(BF16)

    ins = [("row", proj, tile, *CB_G0), ("row", proj, tile, *CB_G1), ("row", proj, tile, *CB_G2), ("full", b_gate),
           ("row", pl_, tile, Dm, 0), ("row", ps_, tile, Dm, 0), ("row", px_, tile, Dm, 0)]
    return _pcall(name, body, n, ins, [("row", S, Dm, BF16, tile, Dm, 0)])[0]


def _merge_bwd(name, proj, b_gate, pl_, ps_, px_, dmerged, tile):
    S = proj.shape[0]
    n = S // tile
    Dm = D_MODEL

    def body(l0, l1, l2, bg, p0, p1, p2, dm_ref, dproj_ref, d0, d1, d2, db0, db1, db2):
        i = pl.program_id(0)
        bgv = bg[...]
        dm = dm_ref[...].astype(F32)
        dls = []
        for k, (lr, pr, dr, dbr) in enumerate(((l0, p0, d0, db0), (l1, p1, d1, db1), (l2, p2, d2, db2))):
            gk = _sigmoid(lr[...].astype(F32) + bgv[k:k + 1])
            dr[...] = (dm * gk).astype(BF16)
            dl = dm * pr[...].astype(F32) * gk * (1.0 - gk)
            _acc(dbr, i, jnp.sum(dl, axis=0, keepdims=True))
            dls.append(dl)
        dproj_ref[...] = jnp.concatenate(dls, axis=1).astype(BF16)

    ins = [("row", proj, tile, *CB_G0), ("row", proj, tile, *CB_G1), ("row", proj, tile, *CB_G2), ("full", b_gate),
           ("row", pl_, tile, Dm, 0), ("row", ps_, tile, Dm, 0), ("row", px_, tile, Dm, 0), ("row", dmerged, tile, Dm, 0)]
    outs = [("row", S, N_PROJ, BF16, tile, *CB_LOGITS)] + [("row", S, Dm, BF16, tile, Dm, 0)] * 3 + [("full", (1, Dm), F32)] * 3
    return _pcall(name, body, n, ins, outs)


def _conv_taps(xf, halo8, first, w):
    out = xf * w[3:4]
    for k in (1, 2, 3):
        out = out + _shift_down(xf, k, halo8, first) * w[3 - k:4 - k]
    return out


def _conv_fwd(name, src, cb, w, b, act, out_dtype, tile):
    S = src.shape[0]
    C = cb[0]
    n = S // tile

    def body(x_ref, p_ref, w_ref, b_ref, o_ref):
        i = pl.program_id(0)
        xf = x_ref[...].astype(F32)
        halo8 = p_ref[...].astype(F32)[HALO - 8:]
        pre = _conv_taps(xf, halo8, i == 0, w_ref[...]) + b_ref[...]
        o_ref[...] = (_silu(pre) if act else pre).astype(out_dtype)

    return _pcall(name, body, n, [("row", src, tile, *cb), ("prev", src, tile, *cb), ("full", w), ("full", b)],
                  [("row", S, C, out_dtype, tile, C, 0)])[0]


def _conv_bwd(name, src, cb, w, b, act, dout, dproj, tile, rider=None):
    S = src.shape[0]
    C = cb[0]
    n = S // tile

    def body(x_ref, p_ref, xn_ref, w_ref, b_ref, d_ref, dn_ref, buf_ref, dx_ref, dw0, dw1, dw2, dw3, db_ref):
        i = pl.program_id(0)
        first, last = i == 0, i == n - 1
        xf = x_ref[...].astype(F32)
        halo8 = p_ref[...].astype(F32)[HALO - 8:]
        wv = w_ref[...]
        dv = d_ref[...].astype(F32)
        nx8 = dn_ref[...].astype(F32)[:8]
        xs = [xf] + [_shift_down(xf, k, halo8, first) for k in (1, 2, 3)]
        if act:
            bv = b_ref[...]
            pre = xs[0] * wv[3:4] + xs[1] * wv[2:3] + xs[2] * wv[1:2] + xs[3] * wv[0:1] + bv
            dv = dv * _dsilu(pre)
            xn8 = xn_ref[...].astype(F32)[:8]
            nx8 = nx8 * _dsilu(_conv_taps(xn8, xf[tile - 8:], False, wv) + bv)
        dx = dv * wv[3:4]
        for k in (1, 2, 3):
            dx = dx + _shift_up(dv, k, nx8, last) * wv[3 - k:4 - k]
        dx_ref[...] = dx.astype(BF16)
        for k, dwr in ((0, dw3), (1, dw2), (2, dw1), (3, dw0)):
            _acc(dwr, i, jnp.sum(dv * xs[k], axis=0, keepdims=True))
        _acc(db_ref, i, jnp.sum(dv, axis=0, keepdims=True))

    ins = [("row", src, tile, *cb), ("prev", src, tile, *cb), ("next", src, tile, *cb), ("full", w), ("full", b),
           ("row", dout, tile, C, 0), ("next", dout, tile, C, 0), ("any", dproj)]
    outs = [("row", S, N_PROJ, BF16, tile, *cb)] + [("full", (1, C), F32)] * 5
    return _pcall(name, body, n, ins, outs, aliases={7: 0}, rider=rider)


def _lru_gates(xc, wa_ref, wi_ref, ba, bi, lam):
    xb = xc.astype(BF16)
    pa, pi_ = [], []
    for nb in range(LRU_BLOCKS):
        sl = slice(nb * LRU_BLOCK, (nb + 1) * LRU_BLOCK)
        pa.append(jnp.dot(xb[:, sl], wa_ref[nb], preferred_element_type=F32))
        pi_.append(jnp.dot(xb[:, sl], wi_ref[nb], preferred_element_type=F32))
    r = _sigmoid(jnp.concatenate(pa, axis=1) + ba)
    ig = _sigmoid(jnp.concatenate(pi_, axis=1) + bi)
    sp = _softplus(-lam)
    a = jnp.exp(-LRU_C * r * sp)
    m = jnp.sqrt(1.0 - a * a)
    return xb, r, ig, sp, a, m


def _lru_fwd(name, xc, proj, wa, wi, ba, bi, lam, tile, rider=None):
    S = xc.shape[0]
    n = S // tile
    C = D_MODEL

    def body(xc_ref, gate_ref, wa_ref, wi_ref, ba_ref, bi_ref, lam_ref, y_ref, h_ref, carry):
        i = pl.program_id(0)

        @pl.when(i == 0)
        def _():
            carry[...] = jnp.zeros_like(carry)

        xcv = xc_ref[...]
        _, r, ig, sp, a, m = _lru_gates(xcv, wa_ref, wi_ref, ba_ref[...], bi_ref[...], lam_ref[...])
        u = m * (ig * xcv)
        rows = _rows(a.shape)
        d = 1
        while d < tile:
            keep = rows >= d
            a_s = jnp.where(keep, pltpu.roll(a, d, 0), 1.0)
            u_s = jnp.where(keep, pltpu.roll(u, d, 0), 0.0)
            u = a * u_s + u
            a = a * a_s
            d *= 2
        h = u + a * carry[0:1, :]
        h_ref[...] = h
        carry[0:1, :] = h_ref[pl.ds(tile - 1, 1), :]
        y_ref[...] = (_gelu(gate_ref[...].astype(F32)) * h).astype(BF16)

    ins = [("row", xc, tile, C, 0), ("row", proj, tile, *CB_LRU_GATE), ("full", wa), ("full", wi),
           ("full", ba), ("full", bi), ("full", lam)]
    return _pcall(name, body, n, ins, [("row", S, C, BF16, tile, C, 0), ("row", S, C, F32, tile, C, 0)],
                  scratch=[pltpu.VMEM((8, C), F32)], rider=rider)


def _lru_bwd(name, dy, xc, proj, h, wa, wi, ba, bi, lam, dproj, tile, rider=None):
    S = xc.shape[0]
    n = S // tile
    C = D_MODEL

    def body(dy_ref, xc_ref, gate_ref, h_ref, hp_ref, wa_ref, wi_ref, ba_ref, bi_ref, lam_ref, buf_ref,
             dg_ref, dxc_ref, dwa_ref, dwi_ref, dba_ref, dbi_ref, dlam_ref, carry):
        i = pl.program_id(0)
        first_tile = i == n - 1

        @pl.when(i == 0)
        def _():
            carry[...] = jnp.zeros_like(carry)

        xcv = xc_ref[...]
        lamv = lam_ref[...]
        xb, r, ig, sp, a, m = _lru_gates(xcv, wa_ref, wi_ref, ba_ref[...], bi_ref[...], lamv)
        hv = h_ref[...]
        gv = gate_ref[...].astype(F32)
        dyv = dy_ref[...].astype(F32)
        dg_ref[...] = (dyv * hv * _dgelu(gv)).astype(BF16)
        v = dyv * _gelu(gv)
        rows = _rows(a.shape)
        bcoef = jnp.where(rows == tile - 1, 1.0, pltpu.roll(a, tile - 1, 0))
        d = 1
        while d < tile:
            keep = rows < tile - d
            b_s = jnp.where(keep, pltpu.roll(bcoef, tile - d, 0), 1.0)
            v_s = jnp.where(keep, pltpu.roll(v, tile - d, 0), 0.0)
            v = v + bcoef * v_s
            bcoef = bcoef * b_s
            d *= 2
        dH = v + bcoef * carry[0:1, :]
        dxc_ref[...] = dH
        carry[0:1, :] = dxc_ref[pl.ds(0, 1), :] * a[0:1, :]
        halo8 = hp_ref[...][HALO - 8:]
        hprev = _shift_down(hv, 1, halo8, first_tile)
        da = dH * hprev
        ix = ig * xcv
        dm = dH * ix
        di = dH * m * xcv
        dxc = dH * m * ig
        da = da - dm * a / m
        dla = da * a
        dr = dla * (-LRU_C) * sp
        _acc(dlam_ref, i, jnp.sum(dla * (-LRU_C) * r, axis=0, keepdims=True) * (-_sigmoid(-lamv)))
        dpa = dr * r * (1.0 - r)
        dpi = di * ig * (1.0 - ig)
        _acc(dba_ref, i, jnp.sum(dpa, axis=0, keepdims=True))
        _acc(dbi_ref, i, jnp.sum(dpi, axis=0, keepdims=True))
        dpab, dpib = dpa.astype(BF16), dpi.astype(BF16)
        back = []
        for nb in range(LRU_BLOCKS):
            sl = slice(nb * LRU_BLOCK, (nb + 1) * LRU_BLOCK)
            back.append(lax.dot_general(dpab[:, sl], wa_ref[nb], (((1,), (1,)), ((), ())), preferred_element_type=F32)
                        + lax.dot_general(dpib[:, sl], wi_ref[nb], (((1,), (1,)), ((), ())), preferred_element_type=F32))
            ga = lax.dot_general(xb[:, sl], dpab[:, sl], (((0,), (0,)), ((), ())), preferred_element_type=F32)
            gi = lax.dot_general(xb[:, sl], dpib[:, sl], (((0,), (0,)), ((), ())), preferred_element_type=F32)

            @pl.when(i == 0)
            def _(ga=ga, gi=gi, nb=nb):
                dwa_ref[nb] = ga
                dwi_ref[nb] = gi

            @pl.when(i > 0)
            def _(ga=ga, gi=gi, nb=nb):
                dwa_ref[nb] += ga
                dwi_ref[nb] += gi

        dxc_ref[...] = dxc + jnp.concatenate(back, axis=1)

    ins = [("row", dy, tile, C, 0), ("row", xc, tile, C, 0), ("row", proj, tile, *CB_LRU_GATE), ("row", h, tile, C, 0),
           ("prev", h, tile, C, 0), ("full", wa), ("full", wi), ("full", ba), ("full", bi), ("full", lam), ("any", dproj)]
    outs = [("row", S, N_PROJ, BF16, tile, *CB_LRU_GATE), ("row", S, C, F32, tile, C, 0),
            ("full", (LRU_BLOCKS, LRU_BLOCK, LRU_BLOCK), F32), ("full", (LRU_BLOCKS, LRU_BLOCK, LRU_BLOCK), F32),
            ("full", (1, C), F32), ("full", (1, C), F32), ("full", (1, C), F32)]
    return _pcall(name, body, n, ins, outs, scratch=[pltpu.VMEM((8, C), F32)], reverse=True, aliases={10: 0}, rider=rider)


SSD_STEP = 4

def _split3(x):
    h = x.astype(BF16)
    r = x - h.astype(F32)
    m = r.astype(BF16)
    lo = (r - m.astype(F32)).astype(BF16)
    return h, m, lo


def _dot01_r(x, e):
    h = x.astype(BF16)
    m = (x - h.astype(F32)).astype(BF16)
    return jnp.dot(h, e, preferred_element_type=F32) + jnp.dot(m, e, preferred_element_type=F32)


def _dot01_l(e, x):
    h, m, lo = _split3(x)
    return (jnp.dot(e, h, preferred_element_type=F32) + jnp.dot(e, m, preferred_element_type=F32)
            + jnp.dot(e, lo, preferred_element_type=F32))


def _ssd_consts():
    hh = lax.broadcasted_iota(jnp.int32, (DT_PAD, D_SSD), 0)
    cc = lax.broadcasted_iota(jnp.int32, (DT_PAD, D_SSD), 1)
    e = (cc // SSD_HEAD_DIM == hh).astype(BF16)
    rows = SSD_STEP * CHUNK
    li = lax.broadcasted_iota(jnp.int32, (rows, rows), 0)
    si = lax.broadcasted_iota(jnp.int32, (rows, rows), 1)
    ltri = ((li >= si) & (li // CHUNK == si // CHUNK)).astype(BF16)
    l4 = lax.broadcasted_iota(jnp.int32, (CHUNK, 4 * CHUNK), 0)
    s4 = lax.broadcasted_iota(jnp.int32, (CHUNK, 4 * CHUNK), 1) % CHUNK
    itile = (l4 == s4).astype(F32)
    causal = (l4 >= s4).astype(F32)
    j4 = lax.broadcasted_iota(jnp.int32, (8, 4 * CHUNK), 0)
    c4 = lax.broadcasted_iota(jnp.int32, (8, 4 * CHUNK), 1) // CHUNK
    hmask = (j4 == c4).astype(F32)
    return e, e.T, ltri, ltri.T, itile, causal, hmask


def _ssd_chunk_common(xs_ref, bm_ref, cm_ref, dt_ref, dtb_ref, a_ref, e_ref, ltri_ref):
    xs = xs_ref[...].astype(F32)
    raw = dt_ref[...] + dtb_ref[...]
    dtv = _softplus(raw)
    da = dtv * a_ref[...]
    cs = _dot01_l(ltri_ref[...], da)
    e = e_ref[...]
    dte = _dot01_r(dtv, e)
    ce = _dot01_r(cs, e)
    xdt = xs * dte
    ecs = jnp.exp(ce)
    return xs, raw, dtv, cs, dte, ce, xdt, ecs


def _quad_terms(ce_q, cb4, itile, causal):
    cr = jnp.sum(ce_q * itile, axis=0, keepdims=True)
    seg = ce_q - cr
    dec = jnp.where(causal > 0.0, jnp.exp(jnp.minimum(seg, 0.0)), 0.0)
    return dec, cb4 * dec


def _block_diag4(xq, hmask):
    return jnp.concatenate([xq * hmask[j:j + 1].astype(xq.dtype) for j in range(4)], axis=0)


def _ssd_fwd(name, xbc, dt_raw, dt_bias, a_neg, d_exp, consts, rider=None):
    S = xbc.shape[0]
    nc = S // CHUNK
    e, et, ltri, ltri_t, itile, causal, hmask = consts

    def body(xs_ref, bm_ref, cm_ref, dt_ref, dtb_ref, a_ref, dex_ref, e_ref, ltri_ref, it_ref, ca_ref, hm_ref,
             y_ref, prev_ref, hst):
        i = pl.program_id(0)

        @pl.when(i == 0)
        def _():
            hst[...] = jnp.zeros_like(hst)

        xs, raw, dtv, cs, dte, ce_all, xdt_all, ecs_all = _ssd_chunk_common(
            xs_ref, bm_ref, cm_ref, dt_ref, dtb_ref, a_ref, e_ref, ltri_ref)
        itile_v, causal_v, hmask_v = it_ref[...], ca_ref[...], hm_ref[...]
        dskip_all = dex_ref[...] * xs
        for k in range(SSD_STEP):
            rs = slice(k * CHUNK, (k + 1) * CHUNK)
            ce, xdt, ecs, dskip = ce_all[rs], xdt_all[rs], ecs_all[rs], dskip_all[rs]
            cle = ce[CHUNK - 1:CHUNK, :]
            xdtb = xdt.astype(BF16)
            xst = (xdt * jnp.exp(cle - ce)).astype(BF16)
            ecl = jnp.exp(cle)
            bm = bm_ref[rs, :]
            cm = cm_ref[rs, :]
            for g in range(SSD_GROUPS):
                gs = slice(g * 512, (g + 1) * 512)
                ns = slice(g * SSD_STATE, (g + 1) * SSD_STATE)
                bm_g, cm_g = bm[:, ns], cm[:, ns]
                hprev = hst[g]
                hprev_b = hprev.astype(BF16)
                prev_ref[k, g] = hprev_b
                yoff = jnp.dot(cm_g, hprev_b, preferred_element_type=F32) * ecs[:, gs]
                st = lax.dot_general(bm_g, xst[:, gs], (((0,), (0,)), ((), ())), preferred_element_type=F32)
                hst[g] = hprev * ecl[:, gs] + st
                b4 = jnp.concatenate([bm_g] * 4, axis=0)
                cb4 = lax.dot_general(cm_g, b4, (((1,), (1,)), ((), ())), preferred_element_type=F32)
                for q in range(2):
                    cols = slice(g * 512 + q * 256, g * 512 + (q + 1) * 256)
                    _, mq = _quad_terms(ce[:, cols], cb4, itile_v, causal_v)
                    xbd = _block_diag4(xdtb[:, cols], hmask_v)
                    ydiag = jnp.dot(mq.astype(BF16), xbd, preferred_element_type=F32)
                    y_ref[rs, cols] = ydiag + yoff[:, q * 256:(q + 1) * 256] + dskip[:, cols]

    T = SSD_STEP * CHUNK
    ins = [("row", xbc, T, *CB_XS), ("row", xbc, T, *CB_BM), ("row", xbc, T, *CB_CM),
           ("row", dt_raw, T, DT_PAD, 0), ("full", dt_bias), ("full", a_neg), ("full", d_exp),
           ("full", e), ("full", ltri), ("full", itile), ("full", causal), ("full", hmask)]
    outs = [("row", S, D_SSD, F32, T, D_SSD, 0), ("lead", (nc, SSD_GROUPS, SSD_STATE, 512), BF16, SSD_STEP)]
    return _pcall(name, body, S // T, ins, outs, scratch=[pltpu.VMEM((SSD_GROUPS, SSD_STATE, 512), F32)], rider=rider)


def _ssd_bwd(name, xbc, dt_raw, dt_bias, a_neg, d_exp, prev, dy, consts, rider=None):
    S = xbc.shape[0]
    nc = S // CHUNK
    e, et, ltri, ltri_t, itile, causal, hmask = consts

    def body(xs_ref, bm_ref, cm_ref, dt_ref, dtb_ref, a_ref, dex_ref, prev_ref, dy_ref,
             e_ref, et_ref, ltri_ref, ltt_ref, it_ref, ca_ref, hm_ref,
             dx_ref, ddt_ref, da_ref, dbias_ref, dd_ref, dh, dce_ref, dxdt_ref):
        i = pl.program_id(0)

        @pl.when(i == 0)
        def _():
            dh[...] = jnp.zeros_like(dh)

        xs, raw, dtv, cs, dte, ce_all, xdt_all, ecs_all = _ssd_chunk_common(
            xs_ref, bm_ref, cm_ref, dt_ref, dtb_ref, a_ref, e_ref, ltri_ref)
        itile_v, causal_v, hmask_v = it_ref[...], ca_ref[...], hm_ref[...]
        dyv = dy_ref[...]
        last_row = _rows((CHUNK, 512)) == CHUNK - 1
        for k in reversed(range(SSD_STEP)):
            rs = slice(k * CHUNK, (k + 1) * CHUNK)
            ce, xdt, ecs = ce_all[rs], xdt_all[rs], ecs_all[rs]
            cle = ce[CHUNK - 1:CHUNK, :]
            dend = jnp.exp(cle - ce)
            xdtb = xdt.astype(BF16)
            xst = (xdt * dend).astype(BF16)
            ecl = jnp.exp(cle)
            bm = bm_ref[rs, :]
            cm = cm_ref[rs, :]
            for g in range(SSD_GROUPS):
                gs = slice(g * 512, (g + 1) * 512)
                ns = slice(g * SSD_STATE, (g + 1) * SSD_STATE)
                bm_g, cm_g = bm[:, ns], cm[:, ns]
                hprev_b = prev_ref[k, g]
                dhn = dh[g]
                dhn_b = dhn.astype(BF16)
                dy_g = dyv[rs, gs]
                ecs_g, dend_g, xdt_g, ecl_g = ecs[:, gs], dend[:, gs], xdt[:, gs], ecl[:, gs]
                z = jnp.dot(cm_g, hprev_b, preferred_element_type=F32)
                dz = dy_g * ecs_g
                dzb = dz.astype(BF16)
                dce_g = dz * z
                dcm_g = lax.dot_general(dzb, hprev_b, (((1,), (1,)), ((), ())), preferred_element_type=F32)
                dprev = lax.dot_general(cm_g, dzb, (((0,), (0,)), ((), ())), preferred_element_type=F32) + dhn * ecl_g
                dcl = jnp.sum(dhn * hprev_b.astype(F32), axis=0, keepdims=True) * ecl_g
                gmat = jnp.dot(bm_g, dhn_b, preferred_element_type=F32)
                dbm_g = lax.dot_general(xst[:, gs], dhn_b, (((1,), (1,)), ((), ())), preferred_element_type=F32)
                dxdt_g = gmat * dend_g
                t = gmat * xdt_g * dend_g
                dce_g = dce_g - t
                dcl = dcl + jnp.sum(t, axis=0, keepdims=True)
                dce_g = dce_g + jnp.where(last_row, dcl, 0.0)
                dh[g] = dprev
                b4 = jnp.concatenate([bm_g] * 4, axis=0)
                cb4 = lax.dot_general(cm_g, b4, (((1,), (1,)), ((), ())), preferred_element_type=F32)
                for q in range(2):
                    qs = slice(q * 256, (q + 1) * 256)
                    cols = slice(g * 512 + q * 256, g * 512 + (q + 1) * 256)
                    dec, mq = _quad_terms(ce[:, cols], cb4, itile_v, causal_v)
                    mqb = mq.astype(BF16)
                    xbd = _block_diag4(xdtb[:, cols], hmask_v)
                    dyq = dy_g[:, qs].astype(BF16)
                    dm = lax.dot_general(dyq, xbd, (((1,), (1,)), ((), ())), preferred_element_type=F32)
                    rmat = lax.dot_general(mqb, dyq, (((0,), (0,)), ((), ())), preferred_element_type=F32)
                    dxq = rmat[0:64] * hmask_v[0:1]
                    for j in range(1, 4):
                        dxq = dxq + rmat[64 * j:64 * (j + 1)] * hmask_v[j:j + 1]
                    tq = dm * dec
                    tqb = tq.astype(BF16)
                    dcm_g = dcm_g + jnp.dot(tqb, b4, preferred_element_type=F32)
                    rb = lax.dot_general(tqb, cm_g, (((0,), (0,)), ((), ())), preferred_element_type=F32)
                    dbm_g = dbm_g + rb[0:64] + rb[64:128] + rb[128:192] + rb[192:256]
                    dseg = tq * cb4
                    colsum = jnp.sum(dseg, axis=0, keepdims=True)
                    dce_ref[rs, cols] = dce_g[:, qs] + dseg - itile_v * colsum
                    dxdt_ref[rs, cols] = dxdt_g[:, qs] + dxq
                dx_ref[rs, D_SSD + g * SSD_STATE:D_SSD + (g + 1) * SSD_STATE] = dbm_g
                dx_ref[rs, D_SSD + D_BC + g * SSD_STATE:D_SSD + D_BC + (g + 1) * SSD_STATE] = dcm_g
        dxdt = dxdt_ref[...]
        dexv = dex_ref[...]
        dx_ref[:, 0:D_SSD] = dxdt * dte + dyv * dexv
        _acc(dd_ref, i, jnp.sum(dyv * xs, axis=0, keepdims=True))
        etv = et_ref[...]
        dcs = _dot01_r(dce_ref[...], etv)
        dda = _dot01_l(ltt_ref[...], dcs)
        av = a_ref[...]
        ddtv = dda * av + _dot01_r(dxdt * xs, etv)
        _acc(da_ref, i, jnp.sum(dda * dtv, axis=0, keepdims=True))
        draw = ddtv * _sigmoid(raw)
        ddt_ref[...] = draw.astype(BF16)
        _acc(dbias_ref, i, jnp.sum(draw, axis=0, keepdims=True))

    T = SSD_STEP * CHUNK
    ins = [("row", xbc, T, *CB_XS), ("row", xbc, T, *CB_BM), ("row", xbc, T, *CB_CM),
           ("row", dt_raw, T, DT_PAD, 0), ("full", dt_bias), ("full", a_neg), ("full", d_exp),
           ("lead", prev, SSD_STEP), ("row", dy, T, D_SSD, 0),
           ("full", e), ("full", et), ("full", ltri), ("full", ltri_t), ("full", itile), ("full", causal), ("full", hmask)]
    outs = [("row", S, D_XBC, F32, T, D_XBC, 0), ("row", S, DT_PAD, BF16, T, DT_PAD, 0),
            ("full", (1, DT_PAD), F32), ("full", (1, DT_PAD), F32), ("full", (1, D_SSD), F32)]
    scratch = [pltpu.VMEM((SSD_GROUPS, SSD_STATE, 512), F32), pltpu.VMEM((T, D_SSD), F32), pltpu.VMEM((T, D_SSD), F32)]
    return _pcall(name, body, S // T, ins, outs, scratch=scratch, reverse=True, rider=rider)


def _gate_norm_fwd(name, ycore, proj, norm_w, tile):
    S = ycore.shape[0]
    n = S // tile

    def body(y_ref, z_ref, w_ref, o_ref):
        y2 = y_ref[...] * _silu(z_ref[...].astype(F32))
        wv = w_ref[...]
        for g in range(SSD_GROUPS):
            gs = slice(g * 512, (g + 1) * 512)
            seg = y2[:, gs]
            r = lax.rsqrt(jnp.mean(seg * seg, axis=-1, keepdims=True) + EPS)
            o_ref[:, gs] = (seg * r * wv[:, gs]).astype(BF16)

    return _pcall(name, body, n, [("row", ycore, tile, D_SSD, 0), ("row", proj, tile, *CB_Z), ("full", norm_w)],
                  [("row", S, D_SSD, BF16, tile, D_SSD, 0)])[0]


def _gate_norm_bwd(name, dout, ycore, proj, norm_w, dproj, tile):
    S = ycore.shape[0]
    n = S // tile

    def body(do_ref, y_ref, z_ref, w_ref, buf_ref, dz_ref, dy_ref, dw_ref):
        i = pl.program_id(0)
        yv = y_ref[...]
        zv = z_ref[...].astype(F32)
        sz = _silu(zv)
        y2 = yv * sz
        dov = do_ref[...].astype(F32)
        wv = w_ref[...]
        dws, dy2s = [], []
        for g in range(SSD_GROUPS):
            gs = slice(g * 512, (g + 1) * 512)
            seg = y2[:, gs]
            r = lax.rsqrt(jnp.mean(seg * seg, axis=-1, keepdims=True) + EPS)
            yn = seg * r
            dws.append(jnp.sum(dov[:, gs] * yn, axis=0, keepdims=True))
            dyn = dov[:, gs] * wv[:, gs]
            dy2s.append(r * (dyn - yn * jnp.mean(dyn * yn, axis=-1, keepdims=True)))
        dy2 = jnp.concatenate(dy2s, axis=1)
        dy_ref[...] = dy2 * sz
        dz_ref[...] = (dy2 * yv * _dsilu(zv)).astype(BF16)
        _acc(dw_ref, i, jnp.concatenate(dws, axis=1))

    ins = [("row", dout, tile, D_SSD, 0), ("row", ycore, tile, D_SSD, 0), ("row", proj, tile, *CB_Z), ("full", norm_w),
           ("any", dproj)]
    outs = [("row", S, N_PROJ, BF16, tile, *CB_Z), ("row", S, D_SSD, F32, tile, D_SSD, 0), ("full", (1, D_SSD), F32)]
    return _pcall(name, body, n, ins, outs, aliases={4: 0})


_XA_SCALE = XA_HEAD_DIM ** -0.5
_NT = (((1,), (1,)), ((), ()))
_TN = (((0,), (0,)), ((), ()))


def _xa_probs(qh, kh):
    s = lax.dot_general(qh, kh, _NT, preferred_element_type=F32) * _XA_SCALE
    s = s - jnp.max(s, axis=-1, keepdims=True)
    p = jnp.exp(s)
    return p / jnp.sum(p, axis=-1, keepdims=True)


def _xa_fwd(name, proj, kv, tile):
    S = proj.shape[0]
    n = S // tile
    Dh = XA_HEAD_DIM

    def body(q_ref, kv_ref, o_ref):
        for hd in range(XA_HEADS):
            qh = q_ref[:, hd * Dh:(hd + 1) * Dh]
            kh = kv_ref[:, hd * Dh:(hd + 1) * Dh]
            vh = kv_ref[:, D_MODEL + hd * Dh:D_MODEL + (hd + 1) * Dh]
            p = _xa_probs(qh, kh)
            o_ref[:, hd * Dh:(hd + 1) * Dh] = jnp.dot(p.astype(BF16), vh, preferred_element_type=F32).astype(BF16)

    return _pcall(name, body, n, [("row", proj, tile, *CB_XA_Q), ("full", kv)],
                  [("row", S, D_MODEL, BF16, tile, D_MODEL, 0)])[0]


def _xa_bwd(name, proj, kv, dout, dproj, tile):
    S = proj.shape[0]
    n = S // tile
    Dh = XA_HEAD_DIM

    def body(q_ref, kv_ref, do_ref, buf_ref, dq_ref, dkv_ref):
        i = pl.program_id(0)
        for hd in range(XA_HEADS):
            ks_ = slice(hd * Dh, (hd + 1) * Dh)
            vs_ = slice(D_MODEL + hd * Dh, D_MODEL + (hd + 1) * Dh)
            qh = q_ref[:, ks_]
            kh = kv_ref[:, ks_]
            vh = kv_ref[:, vs_]
            doh = do_ref[:, ks_].astype(BF16)
            p = _xa_probs(qh, kh)
            pb = p.astype(BF16)
            dp = lax.dot_general(doh, vh, _NT, preferred_element_type=F32)
            dv = lax.dot_general(pb, doh, _TN, preferred_element_type=F32)
            ds = (p * (dp - jnp.sum(dp * p, axis=-1, keepdims=True)) * _XA_SCALE).astype(BF16)
            dq_ref[:, ks_] = jnp.dot(ds, kh, preferred_element_type=F32).astype(BF16)
            dk = lax.dot_general(ds, qh, _TN, preferred_element_type=F32)

            @pl.when(i == 0)
            def _(dk=dk, dv=dv, ks_=ks_, vs_=vs_):
                dkv_ref[:, ks_] = dk
                dkv_ref[:, vs_] = dv

            @pl.when(i > 0)
            def _(dk=dk, dv=dv, ks_=ks_, vs_=vs_):
                dkv_ref[:, ks_] += dk
                dkv_ref[:, vs_] += dv

    ins = [("row", proj, tile, *CB_XA_Q), ("full", kv), ("row", dout, tile, D_MODEL, 0), ("any", dproj)]
    outs = [("row", S, N_PROJ, BF16, tile, *CB_XA_Q), ("full", (N_MEM, 2 * D_MODEL), F32)]
    return _pcall(name, body, n, ins, outs, aliases={3: 0})


def _adamw(name, w, g, m, v):
    L, R, C = w.shape
    tile = _pick(R, [t for t in (256, 128, 64, 32, 16, 8) if t * C <= 128 * 2048])
    bc1 = 1.0 - ADAM_B1 ** ADAM_STEP
    bc2 = 1.0 - ADAM_B2 ** ADAM_STEP

    def body(w_ref, g_ref, m_ref, v_ref, d_ref, nm_ref, nv_ref):
        gv = g_ref[...]
        mn = ADAM_B1 * m_ref[...] + (1.0 - ADAM_B1) * gv
        vn = ADAM_B2 * v_ref[...] + (1.0 - ADAM_B2) * (gv * gv)
        nm_ref[...] = mn
        nv_ref[...] = vn
        d_ref[...] = -ADAM_LR * ((mn / bc1) / (jnp.sqrt(vn / bc2) + ADAM_EPS) + ADAM_WD * w_ref[...])

    spec = pl.BlockSpec((1, tile, C), lambda l, i: (l, i, 0))
    return pl.pallas_call(
        body, name=name, grid=(L, R // tile), in_specs=[spec] * 4, out_specs=[spec] * 3,
        out_shape=[jax.ShapeDtypeStruct((L, R, C), F32)] * 3,
        compiler_params=_cparams(("arbitrary", "arbitrary")),
    )(w, g, m, v)


def _axpy(name, coef, a, b, tile):
    S, C = a.shape
    n = S // tile

    def body(a_ref, b_ref, o_ref):
        o_ref[...] = coef * a_ref[...].astype(F32) + b_ref[...].astype(F32)

    return _pcall(name, body, n, [("row", a, tile, C, 0), ("row", b, tile, C, 0)], [("row", S, C, F32, tile, C, 0)])[0]


def _sum_terms(name, terms, out_dtype):
    R, C = terms[0].shape
    tile = _pick(R, (512, 256, 128, 64, 32, 16, 8))
    n = R // tile
    nt = len(terms)

    def body(*refs):
        vals = [r[...].astype(F32) for r in refs[:nt]]
        while len(vals) > 1:
            vals = [vals[k] + vals[k + 1] for k in range(0, len(vals), 2)]
        refs[nt][...] = vals[0].astype(out_dtype)

    return _pcall(name, body, n, [("row", t, tile, C, 0) for t in terms], [("row", R, C, out_dtype, tile, C, 0)])[0]


def _me():
    return lax.axis_index("x"), lax.axis_index("y"), lax.axis_index("c")


def _other_chips(x, y):
    return [(1 - x, y), (x, 1 - y), (1 - x, 1 - y)]


_ANY = pl.BlockSpec(memory_space=pl.ANY)


def _swap_sibling(name, src):
    def body(src_ref, out_ref, send_sem, recv_sem):
        x, y, c = _me()
        cp = pltpu.make_async_remote_copy(src_ref=src_ref, dst_ref=out_ref, send_sem=send_sem, recv_sem=recv_sem,
                                          device_id=(x, y, 1 - c), device_id_type=MESH)
        cp.start()
        cp.wait()

    return pl.pallas_call(
        body, name=name, in_specs=[_ANY], out_specs=_ANY, out_shape=jax.ShapeDtypeStruct(src.shape, src.dtype),
        scratch_shapes=[pltpu.SemaphoreType.DMA, pltpu.SemaphoreType.DMA],
    )(src)


W_IN_SHARD = N_IN // 4
W_IN_INNER = W_IN_SHARD - 8


def _win_rows(chip_x, chip_y):
    start = (2 * chip_x + chip_y) * W_IN_SHARD + 8 * chip_y
    return pl.ds(pl.multiple_of(start, 2 * 8), W_IN_INNER)


def _dma_sems(*shape):
    return [pltpu.SemaphoreType.DMA(shape), pltpu.SemaphoreType.DMA(shape)]


def _gather_rider(srcs, kinds, bufs, part):
    n = len(srcs)

    def copy(refs, i, k, cx, cy, to, own=False):
        src_refs, out_refs, _, (send_sems, recv_sems) = refs
        d = out_refs[i].at[2 * cx + cy] if kinds[i] == "blk" else out_refs[i].at[_win_rows(cx, cy)]
        return pltpu.make_async_remote_copy(src_ref=src_refs[i].at[part] if own else d, dst_ref=d,
                                            send_sem=send_sems.at[i, k], recv_sem=recv_sems.at[i, k],
                                            device_id=to, device_id_type=MESH)

    def start(*refs):
        x, y, c = _me()

        @pl.when(c == part)
        def _():
            for k, (cx, cy) in enumerate(_other_chips(x, y)):
                for i in range(n):
                    copy(refs, i, k, x, y, (cx, cy, part), own=True).start()

    def mid(*refs):
        x, y, c = _me()

        @pl.when(c == part)
        def _():
            for k, (cx, cy) in enumerate(_other_chips(x, y)):
                for i in range(n):
                    copy(refs, i, k, cx, cy, (x, y, part)).wait_recv()
                    copy(refs, i, 3 + k, cx, cy, (x, y, 1 - part)).start()

    def finish(*refs):
        x, y, c = _me()
        chips = _other_chips(x, y)

        @pl.when(c == part)
        def _():
            for k, (cx, cy) in enumerate(chips):
                for i in range(n):
                    copy(refs, i, k, x, y, (cx, cy, part), own=True).wait_send()
                    copy(refs, i, 3 + k, cx, cy, (x, y, 1 - part)).wait_send()

        @pl.when(c != part)
        def _():
            for k, (cx, cy) in enumerate(chips):
                for i in range(n):
                    copy(refs, i, 3 + k, cx, cy, (x, y, 1 - part)).wait_recv()

    return _Rider(srcs, bufs, [], _dma_sems(n, 6), start, finish, mid)


def _pair_swap_rider(gs, layer):
    n = len(gs)

    def copy(refs, i):
        in_refs, _, out_refs, (send_sems, recv_sems) = refs
        x, y, _c = _me()
        return pltpu.make_async_remote_copy(src_ref=in_refs[i], dst_ref=out_refs[i], send_sem=send_sems.at[i],
                                            recv_sem=recv_sems.at[i], device_id=(x, y, layer), device_id_type=MESH)

    def start(*refs):
        @pl.when(_me()[2] != layer)
        def _():
            for i in range(n):
                copy(refs, i).start()

    def finish(*refs):
        @pl.when(_me()[2] != layer)
        def _():
            for i in range(n):
                copy(refs, i).wait_send()

        @pl.when(_me()[2] == layer)
        def _():
            for i in range(n):
                copy(refs, i).wait_recv()

    return _Rider(gs, [], [jax.ShapeDtypeStruct(g.shape, g.dtype) for g in gs], _dma_sems(n), start, finish)


def _chip_send_rider(pairs, layer):
    n = len(pairs)

    def copies(refs):
        in_refs, _, out_refs, (send_sems, recv_sems) = refs
        x, y, _c = _me()
        return [pltpu.make_async_remote_copy(src_ref=in_refs[i].at[2 * cx + cy], dst_ref=out_refs[i].at[k],
                                             send_sem=send_sems.at[i, k], recv_sem=recv_sems.at[i, k],
                                             device_id=(cx, cy, layer), device_id_type=MESH)
                for k, (cx, cy) in enumerate(_other_chips(x, y)) for i in range(n)]

    def start(*refs):
        @pl.when(_me()[2] == layer)
        def _():
            for cp in copies(refs):
                cp.start()

    def finish(*refs):
        @pl.when(_me()[2] == layer)
        def _():
            for cp in copies(refs):
                cp.wait()

    outs = [jax.ShapeDtypeStruct((3,) + p.shape[1:], p.dtype) for p in pairs]
    return _Rider(pairs, [], outs, _dma_sems(n, 3), start, finish)


def _bcast_rider(srcs):
    n = len(srcs)

    def copies(refs):
        in_refs, _, out_refs, (send_sems, recv_sems) = refs
        x, y, c = _me()
        return [pltpu.make_async_remote_copy(src_ref=in_refs[i], dst_ref=out_refs[i].at[k], send_sem=send_sems.at[i, k],
                                             recv_sem=recv_sems.at[i, k], device_id=(cx, cy, c), device_id_type=MESH)
                for k, (cx, cy) in enumerate(_other_chips(x, y)) for i in range(n)]

    def start(*refs):
        for cp in copies(refs):
            cp.start()

    def finish(*refs):
        for cp in copies(refs):
            cp.wait()

    return _Rider(srcs, [], [jax.ShapeDtypeStruct((3,) + s.shape, s.dtype) for s in srcs], _dma_sems(n, 3), start, finish)


def _merge_riders(a, b):
    assert not a.alias and not b.alias and a.mid is None and b.mid is None

    def both(fa, fb):
        def run(r_in, r_alias, r_out, sems):
            fa(r_in[:len(a.ins)], (), r_out[:len(a.outs)], sems[:len(a.sems)])
            fb(r_in[len(a.ins):], (), r_out[len(a.outs):], sems[len(a.sems):])
        return run

    return _Rider(a.ins + b.ins, [], a.outs + b.outs, a.sems + b.sems, both(a.start, b.start), both(a.finish, b.finish))


def _join_parts_multi(name, bufs):
    n = len(bufs)

    def body(*refs):
        out_refs = refs[n:2 * n]
        send_sems, recv_sems = refs[2 * n:]
        x, y, c = _me()

        def copy(i, part):
            return pltpu.make_async_remote_copy(src_ref=out_refs[i].at[part], dst_ref=out_refs[i].at[part],
                                                send_sem=send_sems.at[i], recv_sem=recv_sems.at[i],
                                                device_id=(x, y, 1 - c), device_id_type=MESH)

        for i in range(n):
            copy(i, c).start()
        for i in range(n):
            copy(i, c).wait_send()
            copy(i, 1 - c).wait_recv()

    return pl.pallas_call(
        body, name=name, in_specs=[_ANY] * n, out_specs=[_ANY] * n,
        out_shape=[jax.ShapeDtypeStruct(b.shape, b.dtype) for b in bufs],
        input_output_aliases={i: i for i in range(n)},
        scratch_shapes=[pltpu.SemaphoreType.DMA((n,)), pltpu.SemaphoreType.DMA((n,))],
    )(*bufs)


def _col_tiles(R, C):
    tr = _pick(R, (512, 256, 128))
    if tr != R:
        return tr, C
    if R * C <= 512 * 1024:
        return R, C
    return R, _pick(C, (256, 128))


def _active(layer):
    return (_me()[2] == layer).astype(jnp.int32).reshape(1)


def _pair_sum(name, g, theirs, layer):
    n4, R, C = g.shape
    tr, tc = _col_tiles(R, C)

    def body(a_ref, g_ref, t_ref, o_ref, ob_ref):
        s = g_ref[...] + t_ref[...]
        o_ref[...] = s
        ob_ref[...] = s.astype(BF16)

    spec = pl.BlockSpec((1, tr, tc), lambda d, i, k, a: (d * a[0], i * a[0], k * a[0]))
    return pl.pallas_call(
        body, name=name,
        grid_spec=pltpu.PrefetchScalarGridSpec(num_scalar_prefetch=1, grid=(n4, R // tr, C // tc),
                                               in_specs=[spec, spec], out_specs=[spec, spec]),
        out_shape=[jax.ShapeDtypeStruct((n4, R, C), F32), jax.ShapeDtypeStruct((n4, R, C), BF16)],
        compiler_params=_cparams(("arbitrary", "arbitrary", "arbitrary")),
    )(_active(layer), g, theirs)


def _chip_sum(name, pair, got, j, layer, buf=None):
    _, R, C = pair.shape
    tr, tc = _col_tiles(R, C)

    def body(ja_ref, p_ref, g0, g1, g2, *rest):
        rest[-1][0] = (p_ref[0] + g2[0].astype(F32)) + (g0[0].astype(F32) + g1[0].astype(F32))

    def gspec(k):
        return pl.BlockSpec((1, tr, tc), lambda i, q, ja, k=k: (k, i * ja[1], q * ja[1]))

    in_specs = [pl.BlockSpec((1, tr, tc), lambda i, q, ja: (ja[0], i * ja[1], q * ja[1])), gspec(0), gspec(1), gspec(2)]
    args = [jnp.concatenate([j.reshape(1).astype(jnp.int32), _active(layer)]), pair, got, got, got]
    aliases = {}
    if buf is not None:
        in_specs.append(_ANY)
        args.append(buf)
        aliases = {5: 0}
    return pl.pallas_call(
        body, name=name,
        grid_spec=pltpu.PrefetchScalarGridSpec(
            num_scalar_prefetch=1, grid=(R // tr, C // tc), in_specs=in_specs,
            out_specs=pl.BlockSpec((1, tr, tc), lambda i, q, ja: (layer, i * ja[1], q * ja[1]))),
        out_shape=jax.ShapeDtypeStruct((2, R, C), F32), input_output_aliases=aliases,
        compiler_params=_cparams(("arbitrary", "arbitrary")),
    )(*args)


LANES = 1024
_BIG = (("w_in", "col"), ("ffn_w_in", "col"), ("mem_w_kv", "col"),
        ("w_br_lru", "row"), ("w_br_ssd", "row"), ("w_br_xa", "row"), ("w_out", "row"), ("ffn_w_down", "row"))
_SMALL_SHARDED = ("b_gate", "lru_conv_w", "ssd_conv_w")
_SMALL = ("b_gate", "lru_conv_w", "lru_conv_b", "lru_w_a", "lru_b_a", "lru_w_i", "lru_b_i", "lru_lambda",
          "ssd_conv_w", "ssd_conv_b", "ssd_dt_bias", "ssd_a_log", "ssd_d", "ssd_norm_w",
          "ln1_g", "ln1_b", "ln2_g", "ln2_b")
_W_NAMES = ("w_in", "b_gate", "lru_conv_w", "lru_conv_b", "lru_w_a", "lru_b_a", "lru_w_i", "lru_b_i", "lru_lambda",
            "ssd_conv_w", "ssd_conv_b", "ssd_dt_bias", "ssd_a_log", "ssd_d", "ssd_norm_w", "mem_w_kv", "w_br_lru",
            "w_br_ssd", "w_br_xa", "w_out", "ln1_g", "ln1_b", "ffn_w_in", "ffn_w_down", "ln2_g", "ln2_b")
_IN_ORDER = ((4096, 7168), (8224, 11296), (2048, 4096), (0, 1024), (1024, 2048), (7200, 8224))
_IN_DT = (7168, 7200)


def _flat_rows(parts, row_multiple):
    flat = jnp.concatenate([p.reshape(-1) for p in parts])
    rows = -(-flat.size // LANES)
    rows = -(-rows // row_multiple) * row_multiple
    return jnp.pad(flat, (0, rows * LANES - flat.size)).reshape(rows, LANES)


def _take_parts(flat, shapes):
    out, off = [], 0
    for shp in shapes:
        size = math.prod(shp)
        out.append(flat[off:off + size].reshape(shp))
        off += size
    return out


_GATHER_NAMES = ("w_in", "w_in_edge") + tuple(nm for nm, _ in _BIG[1:]) + _SMALL_SHARDED
_GATHER_KINDS = ("rows",) + ("blk",) * (len(_GATHER_NAMES) - 1)


def _gather_sources(w):
    x, y, _ = _me()
    j = 2 * x + y
    wt = jnp.swapaxes(w["w_in"], 1, 2)
    inner = lax.dynamic_slice_in_dim(wt, 8 * y, W_IN_INNER, axis=1).astype(BF16)
    edge = lax.dynamic_slice_in_dim(wt, (1 - y) * W_IN_INNER, 8, axis=1)
    srcs = [inner, edge] + [w[nm].astype(BF16) for nm, _ in _BIG[1:]] + [w[nm] for nm in _SMALL_SHARDED]
    bufs = []
    for l in range(DEPTH):
        row = []
        for s, kd in zip(srcs, _GATHER_KINDS):
            if kd == "blk":
                row.append(lax.dynamic_update_slice_in_dim(lax.empty((4,) + s.shape[1:], s.dtype), s[l][None], j, axis=0))
            else:
                row.append(lax.dynamic_update_slice_in_dim(lax.empty((N_IN, D_MODEL), s.dtype), s[l],
                                                           j * W_IN_SHARD + 8 * y, axis=0))
        bufs.append(row)
    return srcs, bufs


def _layer_weights(names, got):
    g = dict(zip(names, got))
    full = {}
    if "w_in" in g:
        wt_all, edges = g["w_in"], g["w_in_edge"]
        for a, b in ((0, 1), (2, 3)):
            tile = jnp.concatenate([edges[a], edges[b]], axis=0).astype(BF16)
            wt_all = lax.dynamic_update_slice_in_dim(wt_all, tile, b * W_IN_SHARD - 8, axis=0)
        full["w_re_t"] = jnp.concatenate([wt_all[lo:hi] for lo, hi in _IN_ORDER], axis=0)
        full["w_dt_t"] = jnp.pad(wt_all[_IN_DT[0]:_IN_DT[1]], ((0, DT_PAD - SSD_HEADS), (0, 0)))
    kinds = dict(_BIG)
    for nm in names:
        if nm in ("w_in", "w_in_edge"):
            continue
        _, r, cdim = g[nm].shape
        if kinds.get(nm, "col") == "col":
            full[nm] = jnp.moveaxis(g[nm], 0, 1).reshape(r, 4 * cdim)
        else:
            full[nm] = g[nm].reshape(4 * r, cdim)
    return full


def _per_chip(g, nm):
    r, cdim = g.shape
    if nm == "w_in" or dict(_BIG)[nm] == "row":
        return g.reshape(4, r // 4, cdim)
    return jnp.moveaxis(g.reshape(r, 4, cdim // 4), 1, 0)


def _layer_fwd(xin, mem2, W, P, consts, riders=None, late=None):
    S = xin.shape[0]
    T, TX = min(256, S), min(512, S)
    sv = {"xin": xin}
    riders = riders or {}
    rout = {}
    proj = _mm("proj", xin, W["w_re_t"], "nt", BF16, rider=riders.get("proj"))
    if "proj" in riders:
        proj, rout["proj"] = proj[0], proj[1:]
    if late is not None:
        w_late, p_late = late(rout)
        W, P = {**W, **w_late}, {**P, **p_late}
    dt_raw = _mm("proj_dt", xin, W["w_dt_t"], "nt", F32)
    xc = _conv_fwd("lru_conv", proj, CB_LRU_X, P["lru_conv_w"], P["lru_conv_b"], False, F32, T)
    res = _lru_fwd("lru_scan", xc, proj, P["lru_w_a"], P["lru_w_i"], P["lru_b_a"], P["lru_b_i"], P["lru_lambda"], T,
                   rider=riders.get("lru_scan"))
    y_lru, h = res[0], res[1]
    if "lru_scan" in riders:
        rout["lru_scan"] = res[2:]
    xact = _conv_fwd("ssd_conv", proj, CB_XBC, P["ssd_conv_w"], P["ssd_conv_b"], True, BF16, T)
    res = _ssd_fwd("ssd_scan", xact, dt_raw, P["dt_bias"], P["a_neg"], P["d_exp"], consts, rider=riders.get("ssd_scan"))
    ycore, prev = res[0], res[1]
    if "ssd_scan" in riders:
        rout["ssd_scan"] = res[2:]
    y_ssd = _gate_norm_fwd("ssd_norm", ycore, proj, P["ssd_norm_w"], TX)
    kv = _mm("mem_kv", mem2, W["w_kv"], "nn", BF16)
    y_xa = _xa_fwd("xattn", proj, kv, TX)
    p_l = _mm("br_lru", y_lru, W["w_l"], "nn", BF16)
    p_s = _mm("br_ssd", y_ssd, W["w_s"], "nn", BF16)
    p_x = _mm("br_xa", y_xa, W["w_x"], "nn", BF16)
    merged = _merge_fwd("merge", proj, P["b_gate"], p_l, p_s, p_x, TX)
    mix = _mm("mix_out", merged, W["w_o"], "nn", F32)
    x1, xh1, rs1 = _ln_fwd("ln_fwd", xin, mix, P["ln1_g"], P["ln1_b"], TX)
    gu = _mm("ffn_in", x1, W["w_fi"], "nn", BF16, rider=riders.get("ffn_in"))
    if "ffn_in" in riders:
        gu, rout["ffn_in"] = gu[0], gu[1:]
    hmid = _swiglu_fwd("swiglu", gu, TX)
    f = _mm("ffn_down", hmid, W["w_fd"], "nn", F32)
    x2, xh2, rs2 = _ln_fwd("ln_fwd", x1, f, P["ln2_g"], P["ln2_b"], TX)
    sv.update(proj=proj, dt_raw=dt_raw, xc=xc, h=h, y_lru=y_lru, xact=xact, ycore=ycore, prev=prev, y_ssd=y_ssd, kv=kv,
              y_xa=y_xa, p_l=p_l, p_s=p_s, p_x=p_x, merged=merged, x1=x1, xh1=xh1, rs1=rs1, gu=gu, hmid=hmid,
              xh2=xh2, rs2=rs2)
    return x2, sv, rout


def _layer_bwd(dys, coefs, sv, mem2, W, P, consts, riders=None):
    S = sv["xin"].shape[0]
    T, TX = min(256, S), min(512, S)
    proj = sv["proj"]
    g = {}
    rout = {}

    def ride(host):
        return riders[host](g, rout) if riders and host in riders else None

    dz2, g["ln2_g"], g["ln2_b"] = _ln_bwd("ln_bwd_%d" % len(dys), dys, coefs, sv["xh2"], sv["rs2"], P["ln2_g"], TX)
    dhmid = _mm("d_hmid", dz2, W["w_fd"], "nt", BF16)
    g["ffn_w_down"] = _mm("dw_ffn_down", sv["hmid"], dz2, "tn", F32)
    dgu = _swiglu_bwd("swiglu_bwd", sv["gu"], dhmid, T)
    dx1f = _mm("d_x1", dgu, W["w_fi"], "nt", F32)
    g["ffn_w_in"] = _mm("dw_ffn_in", sv["x1"], dgu, "tn", F32)
    dz1, g["ln1_g"], g["ln1_b"] = _ln_bwd("ln_bwd_2", [dz2, dx1f], [ALPHA, 1.0], sv["xh1"], sv["rs1"], P["ln1_g"], TX)
    dmerged = _mm("d_merged", dz1, W["w_o"], "nt", BF16)
    g["w_out"] = _mm("dw_out", sv["merged"], dz1, "tn", F32)
    dproj, dpl, dps, dpx, dbg0, dbg1, dbg2 = _merge_bwd("merge_bwd", proj, P["b_gate"], sv["p_l"], sv["p_s"], sv["p_x"],
                                                         dmerged, TX)
    g["b_gate"] = jnp.concatenate([dbg0, dbg1, dbg2], axis=0)
    dy_lru = _mm("d_ylru", dpl, W["w_l"], "nt", BF16)
    g["w_br_lru"] = _mm("dw_br_lru", sv["y_lru"], dpl, "tn", F32)
    dy_ssd = _mm("d_yssd", dps, W["w_s"], "nt", BF16)
    g["w_br_ssd"] = _mm("dw_br_ssd", sv["y_ssd"], dps, "tn", F32)
    dy_xa = _mm("d_yxa", dpx, W["w_x"], "nt", BF16)
    g["w_br_xa"] = _mm("dw_br_xa", sv["y_xa"], dpx, "tn", F32)
    dproj, dkv = _xa_bwd("xattn_bwd", proj, sv["kv"], dy_xa, dproj, TX)
    g["mem_w_kv"] = _mm("dw_kv", mem2, dkv, "tn", F32)
    dproj, dycore, g["ssd_norm_w"] = _gate_norm_bwd("ssd_norm_bwd", dy_ssd, sv["ycore"], proj, P["ssd_norm_w"], dproj, T)
    res = _ssd_bwd("ssd_scan_bwd", sv["xact"], sv["dt_raw"], P["dt_bias"], P["a_neg"], P["d_exp"],
                   sv["prev"], dycore, consts, rider=ride("ssd_scan_bwd"))
    dxact, ddt, d_a, g_dtb, d_dexp = res[:5]
    rout["ssd_scan_bwd"] = res[5:]
    g["ssd_dt_bias"] = g_dtb[:, :SSD_HEADS]
    g["ssd_a_log"] = d_a[:, :SSD_HEADS] * P["a_neg"][:, :SSD_HEADS]
    g["ssd_d"] = jnp.sum(d_dexp.reshape(SSD_HEADS, SSD_HEAD_DIM), axis=-1)
    res = _conv_bwd("ssd_conv_bwd", proj, CB_XBC, P["ssd_conv_w"], P["ssd_conv_b"], True, dxact, dproj, T,
                    rider=ride("ssd_conv_bwd"))
    dproj, w0, w1, w2, w3, g["ssd_conv_b"] = res[:6]
    rout["ssd_conv_bwd"] = res[6:]
    g["ssd_conv_w"] = jnp.concatenate([w0, w1, w2, w3], axis=0)
    res = _lru_bwd("lru_scan_bwd", dy_lru, sv["xc"], proj, sv["h"], P["lru_w_a"], P["lru_w_i"], P["lru_b_a"], P["lru_b_i"],
                   P["lru_lambda"], dproj, T, rider=ride("lru_scan_bwd"))
    dproj, dxc, g["lru_w_a"], g["lru_w_i"], g["lru_b_a"], g["lru_b_i"], g["lru_lambda"] = res[:7]
    rout["lru_scan_bwd"] = res[7:]
    dproj, w0, w1, w2, w3, g["lru_conv_b"] = _conv_bwd("lru_conv_bwd", proj, CB_LRU_X, P["lru_conv_w"], P["lru_conv_b"], False,
                                                       dxc, dproj, T)
    g["lru_conv_w"] = jnp.concatenate([w0, w1, w2, w3], axis=0)
    xin = sv["xin"]
    rd = ride("dw_in")
    dw_re_t = _mm("dw_in", dproj, xin, "tn", F32, rider=rd)
    if rd is not None:
        dw_re_t, rout["dw_in"] = dw_re_t[0], dw_re_t[1:]
    dw_dt_t = _mm("dw_in_dt", ddt, xin, "tn", F32)
    pieces = {rng: dw_re_t[off:off + rng[1] - rng[0]]
              for rng, off in zip(_IN_ORDER, (0, 3072, 6144, 8192, 9216, 10240))}
    pieces[_IN_DT] = dw_dt_t[:SSD_HEADS]
    g["w_in"] = jnp.concatenate([pieces[k] for k in sorted(pieces)], axis=0)
    rd = ride("d_xin")
    dxp = _mm("d_xin", dproj, W["w_re_t"], "nn", F32, rider=rd)
    if rd is not None:
        dxp, rout["d_xin"] = dxp[0], dxp[1:]
    dxs = _mm("d_xin_dt", ddt, W["w_dt_t"], "nn", F32, add=dxp)
    return [dz1, dxs], [ALPHA, 1.0], g, rout


def _step(a):
    x2d, mem2, target = a["x"][0], a["mem"][0], a["loss_target"][0]
    S = x2d.shape[0]
    T = min(256, S)
    xi, yi, ci = _me()
    j = 2 * xi + yi
    w = {nm: a[nm] for nm in _W_NAMES}
    consts = _ssd_consts()
    row = lambda v: v.reshape(1, -1)
    pad_h = lambda v: jnp.pad(v.reshape(1, -1), ((0, 0), (0, DT_PAD - SSD_HEADS)))

    w_keys = dict(w_re_t="w_re_t", w_dt_t="w_dt_t", w_fi="ffn_w_in", w_kv="mem_w_kv", w_l="w_br_lru", w_s="w_br_ssd",
                  w_x="w_br_xa", w_o="w_out", w_fd="ffn_w_down")

    def gathered_params(full):
        return ({k: full[v] for k, v in w_keys.items() if v in full}, {k: full[k] for k in _SMALL_SHARDED if k in full})

    def local_params(l):
        return dict(
            lru_conv_b=row(w["lru_conv_b"][l]), lru_w_a=w["lru_w_a"][l].astype(BF16), lru_w_i=w["lru_w_i"][l].astype(BF16),
            lru_b_a=row(w["lru_b_a"][l]), lru_b_i=row(w["lru_b_i"][l]), lru_lambda=row(w["lru_lambda"][l]),
            ssd_conv_b=row(w["ssd_conv_b"][l]), dt_bias=pad_h(w["ssd_dt_bias"][l]), a_neg=pad_h(-jnp.exp(w["ssd_a_log"][l])),
            d_exp=jnp.broadcast_to(w["ssd_d"][l][:, None], (SSD_HEADS, SSD_HEAD_DIM)).reshape(1, D_SSD),
            ssd_norm_w=row(w["ssd_norm_w"][l]), ln1_g=row(w["ln1_g"][l]), ln1_b=row(w["ln1_b"][l]),
            ln2_g=row(w["ln2_g"][l]), ln2_b=row(w["ln2_b"][l]))

    srcs, bufs = _gather_sources(w)
    names, kinds = list(_GATHER_NAMES), list(_GATHER_KINDS)
    groups = {"w_in": [0, 1], "mixer": [3, 4, 5, 6, 7, 9, 10, 11], "ffn": [2, 8]}

    def gather(group, layer):
        idx = groups[group]
        return _gather_rider([srcs[i] for i in idx], [kinds[i] for i in idx], [bufs[layer][i] for i in idx], layer)

    def weights_of(group, got):
        return _layer_weights([names[i] for i in groups[group]], got)

    rest = groups["mixer"] + groups["ffn"]
    got = _run_rider("gather_weights", gather("w_in", 0))
    W0, _ = gathered_params(weights_of("w_in", got))

    def late0(rout_):
        wl, pl_ = gathered_params(_layer_weights([names[i] for i in rest], rout_["proj"]))
        return wl, pl_

    rest_rider = _gather_rider([srcs[i] for i in rest], [kinds[i] for i in rest], [bufs[0][i] for i in rest], 0)
    x1, sv0, rout = _layer_fwd(x2d, mem2, W0, local_params(0), consts, late=late0, riders={
        "proj": rest_rider, "lru_scan": gather("mixer", 1), "ssd_scan": gather("w_in", 1), "ffn_in": gather("ffn", 1)})
    full1 = {**weights_of("w_in", rout["ssd_scan"]), **weights_of("mixer", rout["lru_scan"]), **weights_of("ffn", rout["ffn_in"])}
    W1, P1g = gathered_params(full1)
    P1 = {**local_params(1), **P1g}
    w0_late, p0_late = late0(rout)
    W0, P0 = {**W0, **w0_late}, {**local_params(0), **p0_late}
    xcur, sv1, _ = _layer_fwd(x1, mem2, W1, P1, consts)
    dy, loss_part = _loss_fwd_bwd("loss", xcur, target, min(512, S))
    loss = lax.psum(loss_part[0, 0], ("x", "y", "c"))

    big_names = [nm for nm, _ in _BIG]
    rest_names = big_names[1:]
    st = {}

    def swap_rider(key, names, layer):
        def build(g, rout_):
            st[key] = [_per_chip(g[nm], nm) for nm in names]
            return _pair_swap_rider(st[key], layer)
        return build

    def pair_sums(key, theirs, layer):
        return [_pair_sum("grads_pair_sum", g_, t_, layer) for g_, t_ in zip(st[key], theirs)]

    dys, coefs, g1, rout1 = _layer_bwd([dy], [1.0], sv1, mem2, W1, P1, consts,
                                       riders={"d_xin": swap_rider("g1", big_names, 1)})
    pairs1 = pair_sums("g1", rout1["d_xin"], 1)

    def send_rest0(g, rout_):
        st["pairs0"] = pair_sums("g0", rout_["lru_scan_bwd"], 0)
        return _chip_send_rider([pb for _, pb in st["pairs0"]], 0)

    dys, coefs, g0, rout0 = _layer_bwd(dys, coefs, sv0, mem2, W0, P0, consts, riders={
        "ssd_scan_bwd": lambda g, r: _chip_send_rider([pb for _, pb in pairs1[:1]], 1),
        "ssd_conv_bwd": lambda g, r: _chip_send_rider([pb for _, pb in pairs1[1:]], 1),
        "lru_scan_bwd": swap_rider("g0", rest_names, 0),
        "dw_in": send_rest0,
        "d_xin": swap_rider("g0_in", ["w_in"], 0)})
    grad_x = _axpy("grad_x", coefs[0], dys[0], dys[1], min(512, S))[None]
    pairs0_in = pair_sums("g0_in", rout0["d_xin"], 0)
    layer_grads = [g0, g1]
    stacked = {nm: jnp.stack([layer_grads[l][nm] for l in range(DEPTH)]) for nm in _SMALL}
    small_parts = [stacked[nm].reshape((DEPTH,) + tuple(sh)) for nm, sh in
                   ((nm, (3, D_MODEL) if nm == "b_gate" else (4, D_MODEL) if nm == "lru_conv_w" else
                     (4, D_XBC) if nm == "ssd_conv_w" else w[nm].shape[1:]) for nm in _SMALL)]
    small_buf = _flat_rows(small_parts, 8)
    small_pair = _sum_terms("small_pair_sum", [small_buf, _swap_sibling("small_pair_swap", small_buf)], F32)
    sent = _run_rider("grads_chip_send", _merge_riders(_chip_send_rider([pb for _, pb in pairs0_in], 0),
                                                        _bcast_rider([small_pair])))
    got0_in, small_got = sent[:1], sent[1]
    small_total = _sum_terms("small_chip_sum", [small_pair, small_got[2], small_got[0], small_got[1]], F32)
    small = dict(zip(_SMALL, _take_parts(small_total.reshape(-1), [p.shape for p in small_parts])))
    pairs0 = pairs0_in + st["pairs0"]
    gots0 = list(got0_in) + list(rout0["dw_in"])
    halves = []
    gots1 = list(rout0["ssd_scan_bwd"]) + list(rout0["ssd_conv_bwd"])
    for (p1, _), gt1, (p0, _), gt0 in zip(pairs1, gots1, pairs0, gots0):
        buf = _chip_sum("grads_chip_sum", p1, gt1, j, 1)
        halves.append(_chip_sum("grads_chip_sum", p0, gt0, j, 0, buf))
    big = dict(zip(big_names, _join_parts_multi("grads_join", halves)))
    big["w_in"] = jnp.swapaxes(big["w_in"], 1, 2)
    for nm in _SMALL_SHARDED:
        cs = w[nm].shape[2]
        small[nm] = lax.dynamic_slice_in_dim(small[nm], j * cs, cs, axis=2)
    grads = {**big, **small}

    delta, new_m, new_v = {}, {}, {}
    for nm, _ in _BIG:
        delta[nm], new_m[nm], new_v[nm] = _adamw("adamw_" + nm, w[nm], grads[nm], a["m_" + nm], a["v_" + nm])
    shapes = [w[nm].shape for nm in _SMALL]
    packs = [_flat_rows([src[nm] for nm in _SMALL], 8)[None] for src in
             (w, grads, {nm: a["m_" + nm] for nm in _SMALL}, {nm: a["v_" + nm] for nm in _SMALL})]
    d_, m_, v_ = _adamw("adamw_small", *packs)
    for dst, buf in ((delta, d_), (new_m, m_), (new_v, v_)):
        dst.update(zip(_SMALL, _take_parts(buf.reshape(-1), shapes)))

    outs = [loss, grad_x]
    for group in (grads, delta, new_m, new_v):
        outs += [group[nm] for nm in _W_NAMES]
    return tuple(outs)


def kernel(x, mem, w_in, b_gate, lru_conv_w, lru_conv_b, lru_w_a, lru_b_a, lru_w_i, lru_b_i, lru_lambda, ssd_conv_w, ssd_conv_b, ssd_dt_bias, ssd_a_log, ssd_d, ssd_norm_w, mem_w_kv, w_br_lru, w_br_ssd, w_br_xa, w_out, ln1_g, ln1_b, ffn_w_in, ffn_w_down, ln2_g, ln2_b, loss_target, m_w_in, m_b_gate, m_lru_conv_w, m_lru_conv_b, m_lru_w_a, m_lru_b_a, m_lru_w_i, m_lru_b_i, m_lru_lambda, m_ssd_conv_w, m_ssd_conv_b, m_ssd_dt_bias, m_ssd_a_log, m_ssd_d, m_ssd_norm_w, m_mem_w_kv, m_w_br_lru, m_w_br_ssd, m_w_br_xa, m_w_out, m_ln1_g, m_ln1_b, m_ffn_w_in, m_ffn_w_down, m_ln2_g, m_ln2_b, v_w_in, v_b_gate, v_lru_conv_w, v_lru_conv_b, v_lru_w_a, v_lru_b_a, v_lru_w_i, v_lru_b_i, v_lru_lambda, v_ssd_conv_w, v_ssd_conv_b, v_ssd_dt_bias, v_ssd_a_log, v_ssd_d, v_ssd_norm_w, v_mem_w_kv, v_w_br_lru, v_w_br_ssd, v_w_br_xa, v_w_out, v_ln1_g, v_ln1_b, v_ffn_w_in, v_ffn_w_down, v_ln2_g, v_ln2_b):
    return _step(dict(locals()))
```

```python
import functools
import math

import jax
import jax.numpy as jnp
from jax import lax
from jax.experimental import pallas as pl
from jax.experimental.pallas import tpu as pltpu

F32, BF16 = jnp.float32, jnp.bfloat16
MESH = pl.DeviceIdType.MESH
VMEM_LIMIT_BYTES = 56 * 2**20
HALO = 16

D_MODEL = 1024
DEPTH = 2
CHUNK = 64
N_MEM = 256
LRU_BLOCKS = 8
LRU_BLOCK = 128
LRU_C = 8.0
D_SSD = 2048
SSD_HEADS = 32
SSD_HEAD_DIM = 64
SSD_GROUPS = 4
SSD_STATE = 128
D_BC = SSD_GROUPS * SSD_STATE
D_XBC = D_SSD + 2 * D_BC
XA_HEADS = 4
XA_HEAD_DIM = 256
D_FF = 2816
ALPHA = (2 * DEPTH) ** 0.25
EPS = 1e-5
N_IN = 11296
N_PROJ = 11264
DT_PAD = 128

ADAM_LR, ADAM_B1, ADAM_B2, ADAM_EPS, ADAM_WD, ADAM_STEP = 0.001, 0.9, 0.999, 1e-08, 0.01, 10

CB_XBC = (3072, 0)
CB_XS, CB_BM, CB_CM = (2048, 0), (512, 4), (512, 5)
CB_LOGITS = (3072, 1)
CB_G0, CB_G1, CB_G2 = (1024, 3), (1024, 4), (1024, 5)
CB_Z = (2048, 3)
CB_LRU_X, CB_LRU_GATE, CB_XA_Q = (1024, 8), (1024, 9), (1024, 10)


def _cparams(sem):
    return pltpu.CompilerParams(dimension_semantics=sem, vmem_limit_bytes=VMEM_LIMIT_BYTES)


MID_STEP_PCT = 65


class _Rider:
    def __init__(self, ins, alias, outs, sems, start, finish, mid=None):
        self.ins, self.alias, self.outs, self.sems = list(ins), list(alias), list(outs), list(sems)
        self.start, self.finish = start, finish
        self.mid = mid

    def operands(self):
        return self.ins + self.alias

    def out_shapes(self):
        return [jax.ShapeDtypeStruct(a.shape, a.dtype) for a in self.alias] + self.outs

    def aliases(self, n_in, n_out):
        return {n_in + len(self.ins) + q: n_out + q for q in range(len(self.alias))}

    def split(self, refs, n_in, n_out, n_scratch):
        ni, na, no = len(self.ins), len(self.alias), len(self.outs)
        main = list(refs[:n_in])
        r_in = refs[n_in:n_in + ni]
        p = n_in + ni + na
        main += refs[p:p + n_out]
        r_alias = refs[p + n_out:p + n_out + na]
        r_out = refs[p + n_out + na:p + n_out + na + no]
        p = p + n_out + na + no
        main += refs[p:p + n_scratch]
        sems = refs[p + n_scratch:]
        return main, (r_in, r_alias, r_out, sems)


def _run_rider(name, rider):
    def body(*refs):
        _, parts = rider.split(refs, 0, 0, 0)
        rider.start(*parts)
        if rider.mid is not None:
            rider.mid(*parts)
        rider.finish(*parts)

    n_ops = len(rider.operands())
    return pl.pallas_call(
        body, name=name, in_specs=[_ANY] * n_ops, out_specs=[_ANY] * len(rider.out_shapes()),
        out_shape=rider.out_shapes(), input_output_aliases=rider.aliases(0, 0), scratch_shapes=rider.sems,
    )(*rider.operands())


def _pcall(name, body, n, ins, outs, scratch=(), reverse=False, aliases=None, rider=None):
    def ridx(i):
        return (n - 1 - i) if reverse else i

    in_specs, args = [], []
    for sp in ins:
        kind, arr = sp[0], sp[1]
        if kind == "row":
            _, _, tile, width, cb = sp
            in_specs.append(pl.BlockSpec((tile, width), lambda i, cb=cb: (ridx(i), cb)))
        elif kind == "prev":
            _, _, tile, width, cb = sp
            t = tile // HALO
            in_specs.append(pl.BlockSpec((HALO, width), lambda i, cb=cb, t=t: (jnp.maximum(ridx(i) * t - 1, 0), cb)))
        elif kind == "next":
            _, _, tile, width, cb = sp
            t = tile // HALO
            last = arr.shape[0] // HALO - 1
            in_specs.append(pl.BlockSpec((HALO, width), lambda i, cb=cb, t=t, last=last: (jnp.minimum((ridx(i) + 1) * t, last), cb)))
        elif kind == "lead":
            nd = arr.ndim
            in_specs.append(pl.BlockSpec((sp[2],) + arr.shape[1:], lambda i, nd=nd: (ridx(i),) + (0,) * (nd - 1)))
        elif kind == "full":
            nd = arr.ndim
            in_specs.append(pl.BlockSpec(arr.shape, lambda i, nd=nd: (0,) * nd))
        elif kind == "any":
            in_specs.append(pl.BlockSpec(memory_space=pl.ANY))
        else:
            raise ValueError(kind)
        args.append(arr)
    out_specs, out_shape = [], []
    for sp in outs:
        kind = sp[0]
        if kind == "row":
            _, rows, cols, dtype, tile, width, cb = sp
            out_shape.append(jax.ShapeDtypeStruct((rows, cols), dtype))
            out_specs.append(pl.BlockSpec((tile, width), lambda i, cb=cb: (ridx(i), cb)))
        elif kind == "lead":
            _, shape, dtype, lead = sp
            nd = len(shape)
            out_shape.append(jax.ShapeDtypeStruct(shape, dtype))
            out_specs.append(pl.BlockSpec((lead,) + tuple(shape[1:]), lambda i, nd=nd: (ridx(i),) + (0,) * (nd - 1)))
        elif kind == "full":
            _, shape, dtype = sp
            nd = len(shape)
            out_shape.append(jax.ShapeDtypeStruct(shape, dtype))
            out_specs.append(pl.BlockSpec(tuple(shape), lambda i, nd=nd: (0,) * nd))
        else:
            raise ValueError(kind)
    aliases = dict(aliases or {})
    scratch = list(scratch)
    kernel_body = body
    if rider is not None:
        n_in, n_out, n_scratch = len(args), len(out_shape), len(scratch)

        def kernel_body(*refs):
            main, parts = rider.split(refs, n_in, n_out, n_scratch)
            i = pl.program_id(0)

            @pl.when(i == 0)
            def _():
                rider.start(*parts)

            if rider.mid is not None:
                @pl.when(i == (n * MID_STEP_PCT) // 100)
                def _():
                    rider.mid(*parts)

            body(*main)

            @pl.when(i == n - 1)
            def _():
                rider.finish(*parts)

        aliases.update(rider.aliases(n_in, n_out))
        args += rider.operands()
        in_specs += [_ANY] * len(rider.operands())
        out_shape += rider.out_shapes()
        out_specs += [_ANY] * len(rider.out_shapes())
        scratch += rider.sems
    res = pl.pallas_call(
        kernel_body, name=name, grid=(n,), in_specs=in_specs, out_specs=out_specs, out_shape=out_shape,
        scratch_shapes=scratch, input_output_aliases=aliases,
        compiler_params=_cparams(("arbitrary",)),
    )(*args)
    return res


def _pick(n, cands):
    for c in cands:
        if n % c == 0:
            return c
    return n


_MM_TILES = (1024, 1408, 512, 256, 128)


def _mm(name, a, b, mode, out_dtype, add=None, rider=None):
    if mode == "nn":
        (M, K), (K2, N) = a.shape, b.shape
    elif mode == "nt":
        (M, K), (N, K2) = a.shape, b.shape
    else:
        (K, M), (K2, N) = a.shape, b.shape
    assert K == K2, (name, a.shape, b.shape)
    tm = _pick(M, _MM_TILES)
    tn = _pick(N, _MM_TILES)
    if mode == "tn":
        tk = _pick(K, (1024, 512, 256))
    else:
        tk = K if K <= 2816 else _pick(K, _MM_TILES)
    nk = K // tk
    has_add = add is not None

    def body(*refs):
        if has_add:
            a_ref, b_ref, add_ref, o_ref, acc_ref = refs
        else:
            a_ref, b_ref, o_ref, acc_ref = refs
        k = pl.program_id(2)
        av = a_ref[...].astype(BF16)
        bv = b_ref[...].astype(BF16)
        if mode == "nn":
            p = jnp.dot(av, bv, preferred_element_type=F32)
        elif mode == "nt":
            p = lax.dot_general(av, bv, (((1,), (1,)), ((), ())), preferred_element_type=F32)
        else:
            p = lax.dot_general(av, bv, (((0,), (0,)), ((), ())), preferred_element_type=F32)

        def fin(v):
            if has_add:
                v = v + add_ref[...].astype(F32)
            o_ref[...] = v.astype(out_dtype)

        if nk == 1:
            fin(p)
        else:
            @pl.when(k == 0)
            def _():
                acc_ref[...] = p

            @pl.when(k > 0)
            def _():
                acc_ref[...] += p

            @pl.when(k == nk - 1)
            def _():
                fin(acc_ref[...])

    if mode == "nn":
        specs = [pl.BlockSpec((tm, tk), lambda i, j, k: (i, k)), pl.BlockSpec((tk, tn), lambda i, j, k: (k, j))]
    elif mode == "nt":
        specs = [pl.BlockSpec((tm, tk), lambda i, j, k: (i, k)), pl.BlockSpec((tn, tk), lambda i, j, k: (j, k))]
    else:
        specs = [pl.BlockSpec((tk, tm), lambda i, j, k: (k, i)), pl.BlockSpec((tk, tn), lambda i, j, k: (k, j))]
    args = [a, b]
    if has_add:
        specs.append(pl.BlockSpec((tm, tn), lambda i, j, k: (i, j)))
        args.append(add)
    acc_shape = (tm, tn) if nk > 1 else (8, 128)
    grid = (M // tm, N // tn, nk)
    out_spec = pl.BlockSpec((tm, tn), lambda i, j, k: (i, j))
    out_shape = jax.ShapeDtypeStruct((M, N), out_dtype)
    if rider is None:
        return pl.pallas_call(
            body, name=name, grid=grid, in_specs=specs, out_specs=out_spec, out_shape=out_shape,
            scratch_shapes=[pltpu.VMEM(acc_shape, F32)],
            compiler_params=_cparams(("parallel", "parallel", "arbitrary")),
        )(*args)
    n_in = len(args)

    def kernel_body(*refs):
        main, parts = rider.split(refs, n_in, 1, 1)
        i, j, k = pl.program_id(0), pl.program_id(1), pl.program_id(2)

        @pl.when((i == 0) & (j == 0) & (k == 0))
        def _():
            rider.start(*parts)

        if rider.mid is not None:
            flat = (grid[0] * grid[1] * grid[2] * MID_STEP_PCT) // 100
            mi, mj, mk = flat // (grid[1] * grid[2]), (flat // grid[2]) % grid[1], flat % grid[2]

            @pl.when((i == mi) & (j == mj) & (k == mk))
            def _():
                rider.mid(*parts)

        body(*main)

        @pl.when((i == grid[0] - 1) & (j == grid[1] - 1) & (k == grid[2] - 1))
        def _():
            rider.finish(*parts)

    return pl.pallas_call(
        kernel_body, name=name, grid=grid, in_specs=specs + [_ANY] * len(rider.operands()),
        out_specs=[out_spec] + [_ANY] * len(rider.out_shapes()), out_shape=[out_shape] + rider.out_shapes(),
        input_output_aliases=rider.aliases(n_in, 1), scratch_shapes=[pltpu.VMEM(acc_shape, F32)] + rider.sems,
        compiler_params=_cparams(("arbitrary", "arbitrary", "arbitrary")),
    )(*args, *rider.operands())


def _sigmoid(x):
    return 1.0 / (1.0 + jnp.exp(-x))


def _silu(x):
    return x * _sigmoid(x)


def _dsilu(x):
    s = _sigmoid(x)
    return s * (1.0 + x * (1.0 - s))


def _softplus(x):
    return jnp.maximum(x, 0.0) + jnp.log(1.0 + jnp.exp(-jnp.abs(x)))


_GELU_C = math.sqrt(2.0 / math.pi)


def _gelu(x):
    return 0.5 * x * (1.0 + jnp.tanh(_GELU_C * (x + 0.044715 * x * x * x)))


def _dgelu(x):
    t = jnp.tanh(_GELU_C * (x + 0.044715 * x * x * x))
    return 0.5 * (1.0 + t) + 0.5 * x * (1.0 - t * t) * _GELU_C * (1.0 + 3.0 * 0.044715 * x * x)


def _acc(ref, i, val):
    @pl.when(i == 0)
    def _():
        ref[...] = val

    @pl.when(i > 0)
    def _():
        ref[...] += val


def _rows(shape):
    return lax.broadcasted_iota(jnp.int32, shape, 0)


def _shift_down(x, k, halo8, first):
    r = pltpu.roll(x, k, 0)
    h = pltpu.roll(halo8, k, 0)
    h = jnp.where(first, 0.0, h)
    head = jnp.where(_rows(h.shape) < k, h, r[:8])
    if x.shape[0] == 8:
        return head
    return jnp.concatenate([head, r[8:]], axis=0)


def _shift_up(x, k, halo8, last):
    T = x.shape[0]
    r = pltpu.roll(x, T - k, 0)
    h = pltpu.roll(halo8, 8 - k, 0)
    h = jnp.where(last, 0.0, h)
    tail = jnp.where(_rows(h.shape) >= 8 - k, h, r[T - 8:])
    return jnp.concatenate([r[:T - 8], tail], axis=0)


def _ln_fwd(name, a, b, g, beta, tile):
    S, Dm = a.shape
    n = S // tile

    def body(a_ref, b_ref, g_ref, be_ref, y_ref, yb_ref, xh_ref, rs_ref):
        z = ALPHA * a_ref[...] + b_ref[...].astype(F32)
        mu = jnp.mean(z, axis=-1, keepdims=True)
        zc = z - mu
        var = jnp.mean(zc * zc, axis=-1, keepdims=True)
        rstd = lax.rsqrt(var + EPS)
        xh = zc * rstd
        y = xh * g_ref[...] + be_ref[...]
        y_ref[...] = y
        yb_ref[...] = y.astype(BF16)
        xh_ref[...] = xh
        rs_ref[...] = rstd

    return _pcall(name, body, n,
                  [("row", a, tile, Dm, 0), ("row", b, tile, Dm, 0), ("full", g), ("full", beta)],
                  [("row", S, Dm, F32, tile, Dm, 0), ("row", S, Dm, BF16, tile, Dm, 0), ("row", S, Dm, F32, tile, Dm, 0),
                   ("row", S, 1, F32, tile, 1, 0)])


def _ln_bwd(name, dys, coefs, xh, rstd, g, tile):
    S, Dm = xh.shape
    n = S // tile
    nd = len(dys)

    def body(*refs):
        dy_refs = refs[:nd]
        xh_ref, rs_ref, g_ref, dz_ref, dg_ref, db_ref = refs[nd:]
        i = pl.program_id(0)
        dy = coefs[0] * dy_refs[0][...].astype(F32)
        for k in range(1, nd):
            dy = dy + coefs[k] * dy_refs[k][...].astype(F32)
        xh_v = xh_ref[...]
        dxh = dy * g_ref[...]
        m1 = jnp.mean(dxh, axis=-1, keepdims=True)
        m2 = jnp.mean(dxh * xh_v, axis=-1, keepdims=True)
        dz_ref[...] = rs_ref[...] * (dxh - m1 - xh_v * m2)
        _acc(dg_ref, i, jnp.sum(dy * xh_v, axis=0, keepdims=True))
        _acc(db_ref, i, jnp.sum(dy, axis=0, keepdims=True))

    ins = [("row", d, tile, Dm, 0) for d in dys]
    ins += [("row", xh, tile, Dm, 0), ("row", rstd, tile, 1, 0), ("full", g)]
    return _pcall(name, body, n, ins,
                  [("row", S, Dm, F32, tile, Dm, 0), ("full", (1, Dm), F32), ("full", (1, Dm), F32)])


def _loss_fwd_bwd(name, y, target, tile):
    S, Dm = y.shape
    n = S // tile

    def body(y_ref, t_ref, dy_ref, l_ref):
        i = pl.program_id(0)
        err = y_ref[...] - t_ref[...]
        dy_ref[...] = err * (1.0 / Dm)
        part = jnp.sum(jnp.sum(err * err, axis=-1, keepdims=True), axis=0, keepdims=True) * (0.5 / Dm)
        _acc(l_ref, i, part)

    return _pcall(name, body, n, [("row", y, tile, Dm, 0), ("row", target, tile, Dm, 0)],
                  [("row", S, Dm, F32, tile, Dm, 0), ("full", (1, 1), F32)])


def _swiglu_fwd(name, gu, tile):
    S = gu.shape[0]
    n = S // tile

    def body(g_ref, u_ref, o_ref):
        o_ref[...] = (_silu(g_ref[...].astype(F32)) * u_ref[...].astype(F32)).astype(BF16)

    return _pcall(name, body, n, [("row", gu, tile, D_FF, 0), ("row", gu, tile, D_FF, 1)],
                  [("row", S, D_FF, BF16, tile, D_FF, 0)])[0]


def _swiglu_bwd(name, gu, dh, tile):
    S = gu.shape[0]
    n = S // tile

    def body(g_ref, u_ref, dh_ref, o_ref):
        gv = g_ref[...].astype(F32)
        uv = u_ref[...].astype(F32)
        dv = dh_ref[...].astype(F32)
        dg = dv * uv * _dsilu(gv)
        du = dv * _silu(gv)
        o_ref[...] = jnp.concatenate([dg, du], axis=1).astype(BF16)

    return _pcall(name, body, n, [("row", gu, tile, D_FF, 0), ("row", gu, tile, D_FF, 1), ("row", dh, tile, D_FF, 0)],
                  [("row", S, 2 * D_FF, BF16, tile, 2 * D_FF, 0)])[0]


def _merge_fwd(name, proj, b_gate, pl_, ps_, px_, tile):
    S = proj.shape[0]
    n = S // tile
    Dm = D_MODEL

    def body(l0, l1, l2, bg, p0, p1, p2, o_ref):
        bgv = bg[...]
        acc = _sigmoid(l0[...].astype(F32) + bgv[0:1]) * p0[...].astype(F32)
        acc = acc + _sigmoid(l1[...].astype(F32) + bgv[1:2]) * p1[...].astype(F32)
        acc = acc + _sigmoid(l2[...].astype(F32) + bgv[2:3]) * p2[...].astype(F32)
        o_ref[...] = acc.astype(BF16)

    ins = [("row", proj, tile, *CB_G0), ("row", proj, tile, *CB_G1), ("row", proj, tile, *CB_G2), ("full", b_gate),
           ("row", pl_, tile, Dm, 0), ("row", ps_, tile, Dm, 0), ("row", px_, tile, Dm, 0)]
    return _pcall(name, body, n, ins, [("row", S, Dm, BF16, tile, Dm, 0)])[0]


def _merge_bwd(name, proj, b_gate, pl_, ps_, px_, dmerged, tile):
    S = proj.shape[0]
    n = S // tile
    Dm = D_MODEL

    def body(l0, l1, l2, bg, p0, p1, p2, dm_ref, dproj_ref, d0, d1, d2, db0, db1, db2):
        i = pl.program_id(0)
        bgv = bg[...]
        dm = dm_ref[...].astype(F32)
        dls = []
        for k, (lr, pr, dr, dbr) in enumerate(((l0, p0, d0, db0), (l1, p1, d1, db1), (l2, p2, d2, db2))):
            gk = _sigmoid(lr[...].astype(F32) + bgv[k:k + 1])
            dr[...] = (dm * gk).astype(BF16)
            dl = dm * pr[...].astype(F32) * gk * (1.0 - gk)
            _acc(dbr, i, jnp.sum(dl, axis=0, keepdims=True))
            dls.append(dl)
        dproj_ref[...] = jnp.concatenate(dls, axis=1).astype(BF16)

    ins = [("row", proj, tile, *CB_G0), ("row", proj, tile, *CB_G1), ("row", proj, tile, *CB_G2), ("full", b_gate),
           ("row", pl_, tile, Dm, 0), ("row", ps_, tile, Dm, 0), ("row", px_, tile, Dm, 0), ("row", dmerged, tile, Dm, 0)]
    outs = [("row", S, N_PROJ, BF16, tile, *CB_LOGITS)] + [("row", S, Dm, BF16, tile, Dm, 0)] * 3 + [("full", (1, Dm), F32)] * 3
    return _pcall(name, body, n, ins, outs)


def _conv_taps(xf, halo8, first, w):
    out = xf * w[3:4]
    for k in (1, 2, 3):
        out = out + _shift_down(xf, k, halo8, first) * w[3 - k:4 - k]
    return out


def _conv_fwd(name, src, cb, w, b, act, out_dtype, tile):
    S = src.shape[0]
    C = cb[0]
    n = S // tile

    def body(x_ref, p_ref, w_ref, b_ref, o_ref):
        i = pl.program_id(0)
        xf = x_ref[...].astype(F32)
        halo8 = p_ref[...].astype(F32)[HALO - 8:]
        pre = _conv_taps(xf, halo8, i == 0, w_ref[...]) + b_ref[...]
        o_ref[...] = (_silu(pre) if act else pre).astype(out_dtype)

    return _pcall(name, body, n, [("row", src, tile, *cb), ("prev", src, tile, *cb), ("full", w), ("full", b)],
                  [("row", S, C, out_dtype, tile, C, 0)])[0]


def _conv_bwd(name, src, cb, w, b, act, dout, dproj, tile, rider=None):
    S = src.shape[0]
    C = cb[0]
    n = S // tile

    def body(x_ref, p_ref, xn_ref, w_ref, b_ref, d_ref, dn_ref, buf_ref, dx_ref, dw0, dw1, dw2, dw3, db_ref):
        i = pl.program_id(0)
        first, last = i == 0, i == n - 1
        xf = x_ref[...].astype(F32)
        halo8 = p_ref[...].astype(F32)[HALO - 8:]
        wv = w_ref[...]
        dv = d_ref[...].astype(F32)
        nx8 = dn_ref[...].astype(F32)[:8]
        xs = [xf] + [_shift_down(xf, k, halo8, first) for k in (1, 2, 3)]
        if act:
            bv = b_ref[...]
            pre = xs[0] * wv[3:4] + xs[1] * wv[2:3] + xs[2] * wv[1:2] + xs[3] * wv[0:1] + bv
            dv = dv * _dsilu(pre)
            xn8 = xn_ref[...].astype(F32)[:8]
            nx8 = nx8 * _dsilu(_conv_taps(xn8, xf[tile - 8:], False, wv) + bv)
        dx = dv * wv[3:4]
        for k in (1, 2, 3):
            dx = dx + _shift_up(dv, k, nx8, last) * wv[3 - k:4 - k]
        dx_ref[...] = dx.astype(BF16)
        for k, dwr in ((0, dw3), (1, dw2), (2, dw1), (3, dw0)):
            _acc(dwr, i, jnp.sum(dv * xs[k], axis=0, keepdims=True))
        _acc(db_ref, i, jnp.sum(dv, axis=0, keepdims=True))

    ins = [("row", src, tile, *cb), ("prev", src, tile, *cb), ("next", src, tile, *cb), ("full", w), ("full", b),
           ("row", dout, tile, C, 0), ("next", dout, tile, C, 0), ("any", dproj)]
    outs = [("row", S, N_PROJ, BF16, tile, *cb)] + [("full", (1, C), F32)] * 5
    return _pcall(name, body, n, ins, outs, aliases={7: 0}, rider=rider)


def _lru_gates(xc, wa_ref, wi_ref, ba, bi, lam):
    xb = xc.astype(BF16)
    pa, pi_ = [], []
    for nb in range(LRU_BLOCKS):
        sl = slice(nb * LRU_BLOCK, (nb + 1) * LRU_BLOCK)
        pa.append(jnp.dot(xb[:, sl], wa_ref[nb], preferred_element_type=F32))
        pi_.append(jnp.dot(xb[:, sl], wi_ref[nb], preferred_element_type=F32))
    r = _sigmoid(jnp.concatenate(pa, axis=1) + ba)
    ig = _sigmoid(jnp.concatenate(pi_, axis=1) + bi)
    sp = _softplus(-lam)
    a = jnp.exp(-LRU_C * r * sp)
    m = jnp.sqrt(1.0 - a * a)
    return xb, r, ig, sp, a, m


def _lru_fwd(name, xc, proj, wa, wi, ba, bi, lam, tile, rider=None):
    S = xc.shape[0]
    n = S // tile
    C = D_MODEL

    def body(xc_ref, gate_ref, wa_ref, wi_ref, ba_ref, bi_ref, lam_ref, y_ref, h_ref, carry):
        i = pl.program_id(0)

        @pl.when(i == 0)
        def _():
            carry[...] = jnp.zeros_like(carry)

        xcv = xc_ref[...]
        _, r, ig, sp, a, m = _lru_gates(xcv, wa_ref, wi_ref, ba_ref[...], bi_ref[...], lam_ref[...])
        u = m * (ig * xcv)
        rows = _rows(a.shape)
        d = 1
        while d < tile:
            keep = rows >= d
            a_s = jnp.where(keep, pltpu.roll(a, d, 0), 1.0)
            u_s = jnp.where(keep, pltpu.roll(u, d, 0), 0.0)
            u = a * u_s + u
            a = a * a_s
            d *= 2
        h = u + a * carry[0:1, :]
        h_ref[...] = h
        carry[0:1, :] = h_ref[pl.ds(tile - 1, 1), :]
        y_ref[...] = (_gelu(gate_ref[...].astype(F32)) * h).astype(BF16)

    ins = [("row", xc, tile, C, 0), ("row", proj, tile, *CB_LRU_GATE), ("full", wa), ("full", wi),
           ("full", ba), ("full", bi), ("full", lam)]
    return _pcall(name, body, n, ins, [("row", S, C, BF16, tile, C, 0), ("row", S, C, F32, tile, C, 0)],
                  scratch=[pltpu.VMEM((8, C), F32)], rider=rider)


def _lru_bwd(name, dy, xc, proj, h, wa, wi, ba, bi, lam, dproj, tile, rider=None):
    S = xc.shape[0]
    n = S // tile
    C = D_MODEL

    def body(dy_ref, xc_ref, gate_ref, h_ref, hp_ref, wa_ref, wi_ref, ba_ref, bi_ref, lam_ref, buf_ref,
             dg_ref, dxc_ref, dwa_ref, dwi_ref, dba_ref, dbi_ref, dlam_ref, carry):
        i = pl.program_id(0)
        first_tile = i == n - 1

        @pl.when(i == 0)
        def _():
            carry[...] = jnp.zeros_like(carry)

        xcv = xc_ref[...]
        lamv = lam_ref[...]
        xb, r, ig, sp, a, m = _lru_gates(xcv, wa_ref, wi_ref, ba_ref[...], bi_ref[...], lamv)
        hv = h_ref[...]
        gv = gate_ref[...].astype(F32)
        dyv = dy_ref[...].astype(F32)
        dg_ref[...] = (dyv * hv * _dgelu(gv)).astype(BF16)
        v = dyv * _gelu(gv)
        rows = _rows(a.shape)
        bcoef = jnp.where(rows == tile - 1, 1.0, pltpu.roll(a, tile - 1, 0))
        d = 1
        while d < tile:
            keep = rows < tile - d
            b_s = jnp.where(keep, pltpu.roll(bcoef, tile - d, 0), 1.0)
            v_s = jnp.where(keep, pltpu.roll(v, tile - d, 0), 0.0)
            v = v + bcoef * v_s
            bcoef = bcoef * b_s
            d *= 2
        dH = v + bcoef * carry[0:1, :]
        dxc_ref[...] = dH
        carry[0:1, :] = dxc_ref[pl.ds(0, 1), :] * a[0:1, :]
        halo8 = hp_ref[...][HALO - 8:]
        hprev = _shift_down(hv, 1, halo8, first_tile)
        da = dH * hprev
        ix = ig * xcv
        dm = dH * ix
        di = dH * m * xcv
        dxc = dH * m * ig
        da = da - dm * a / m
        dla = da * a
        dr = dla * (-LRU_C) * sp
        _acc(dlam_ref, i, jnp.sum(dla * (-LRU_C) * r, axis=0, keepdims=True) * (-_sigmoid(-lamv)))
        dpa = dr * r * (1.0 - r)
        dpi = di * ig * (1.0 - ig)
        _acc(dba_ref, i, jnp.sum(dpa, axis=0, keepdims=True))
        _acc(dbi_ref, i, jnp.sum(dpi, axis=0, keepdims=True))
        dpab, dpib = dpa.astype(BF16), dpi.astype(BF16)
        back = []
        for nb in range(LRU_BLOCKS):
            sl = slice(nb * LRU_BLOCK, (nb + 1) * LRU_BLOCK)
            back.append(lax.dot_general(dpab[:, sl], wa_ref[nb], (((1,), (1,)), ((), ())), preferred_element_type=F32)
                        + lax.dot_general(dpib[:, sl], wi_ref[nb], (((1,), (1,)), ((), ())), preferred_element_type=F32))
            ga = lax.dot_general(xb[:, sl], dpab[:, sl], (((0,), (0,)), ((), ())), preferred_element_type=F32)
            gi = lax.dot_general(xb[:, sl], dpib[:, sl], (((0,), (0,)), ((), ())), preferred_element_type=F32)

            @pl.when(i == 0)
            def _(ga=ga, gi=gi, nb=nb):
                dwa_ref[nb] = ga
                dwi_ref[nb] = gi

            @pl.when(i > 0)
            def _(ga=ga, gi=gi, nb=nb):
                dwa_ref[nb] += ga
                dwi_ref[nb] += gi

        dxc_ref[...] = dxc + jnp.concatenate(back, axis=1)

    ins = [("row", dy, tile, C, 0), ("row", xc, tile, C, 0), ("row", proj, tile, *CB_LRU_GATE), ("row", h, tile, C, 0),
           ("prev", h, tile, C, 0), ("full", wa), ("full", wi), ("full", ba), ("full", bi), ("full", lam), ("any", dproj)]
    outs = [("row", S, N_PROJ, BF16, tile, *CB_LRU_GATE), ("row", S, C, F32, tile, C, 0),
            ("full", (LRU_BLOCKS, LRU_BLOCK, LRU_BLOCK), F32), ("full", (LRU_BLOCKS, LRU_BLOCK, LRU_BLOCK), F32),
            ("full", (1, C), F32), ("full", (1, C), F32), ("full", (1, C), F32)]
    return _pcall(name, body, n, ins, outs, scratch=[pltpu.VMEM((8, C), F32)], reverse=True, aliases={10: 0}, rider=rider)


SSD_STEP = 4

def _split3(x):
    h = x.astype(BF16)
    r = x - h.astype(F32)
    m = r.astype(BF16)
    lo = (r - m.astype(F32)).astype(BF16)
    return h, m, lo


def _dot01_r(x, e):
    h = x.astype(BF16)
    m = (x - h.astype(F32)).astype(BF16)
    return jnp.dot(h, e, preferred_element_type=F32) + jnp.dot(m, e, preferred_element_type=F32)


def _dot01_l(e, x):
    h, m, lo = _split3(x)
    return (jnp.dot(e, h, preferred_element_type=F32) + jnp.dot(e, m, preferred_element_type=F32)
            + jnp.dot(e, lo, preferred_element_type=F32))


def _ssd_consts():
    hh = lax.broadcasted_iota(jnp.int32, (DT_PAD, D_SSD), 0)
    cc = lax.broadcasted_iota(jnp.int32, (DT_PAD, D_SSD), 1)
    e = (cc // SSD_HEAD_DIM == hh).astype(BF16)
    rows = SSD_STEP * CHUNK
    li = lax.broadcasted_iota(jnp.int32, (rows, rows), 0)
    si = lax.broadcasted_iota(jnp.int32, (rows, rows), 1)
    ltri = ((li >= si) & (li // CHUNK == si // CHUNK)).astype(BF16)
    l4 = lax.broadcasted_iota(jnp.int32, (CHUNK, 4 * CHUNK), 0)
    s4 = lax.broadcasted_iota(jnp.int32, (CHUNK, 4 * CHUNK), 1) % CHUNK
    itile = (l4 == s4).astype(F32)
    causal = (l4 >= s4).astype(F32)
    j4 = lax.broadcasted_iota(jnp.int32, (8, 4 * CHUNK), 0)
    c4 = lax.broadcasted_iota(jnp.int32, (8, 4 * CHUNK), 1) // CHUNK
    hmask = (j4 == c4).astype(F32)
    return e, e.T, ltri, ltri.T, itile, causal, hmask


def _ssd_chunk_common(xs_ref, bm_ref, cm_ref, dt_ref, dtb_ref, a_ref, e_ref, ltri_ref):
    xs = xs_ref[...].astype(F32)
    raw = dt_ref[...] + dtb_ref[...]
    dtv = _softplus(raw)
    da = dtv * a_ref[...]
    cs = _dot01_l(ltri_ref[...], da)
    e = e_ref[...]
    dte = _dot01_r(dtv, e)
    ce = _dot01_r(cs, e)
    xdt = xs * dte
    ecs = jnp.exp(ce)
    return xs, raw, dtv, cs, dte, ce, xdt, ecs


def _quad_terms(ce_q, cb4, itile, causal):
    cr = jnp.sum(ce_q * itile, axis=0, keepdims=True)
    seg = ce_q - cr
    dec = jnp.where(causal > 0.0, jnp.exp(jnp.minimum(seg, 0.0)), 0.0)
    return dec, cb4 * dec


def _block_diag4(xq, hmask):
    return jnp.concatenate([xq * hmask[j:j + 1].astype(xq.dtype) for j in range(4)], axis=0)


def _ssd_fwd(name, xbc, dt_raw, dt_bias, a_neg, d_exp, consts, rider=None):
    S = xbc.shape[0]
    nc = S // CHUNK
    e, et, ltri, ltri_t, itile, causal, hmask = consts

    def body(xs_ref, bm_ref, cm_ref, dt_ref, dtb_ref, a_ref, dex_ref, e_ref, ltri_ref, it_ref, ca_ref, hm_ref,
             y_ref, prev_ref, hst):
        i = pl.program_id(0)

        @pl.when(i == 0)
        def _():
            hst[...] = jnp.zeros_like(hst)

        xs, raw, dtv, cs, dte, ce_all, xdt_all, ecs_all = _ssd_chunk_common(
            xs_ref, bm_ref, cm_ref, dt_ref, dtb_ref, a_ref, e_ref, ltri_ref)
        itile_v, causal_v, hmask_v = it_ref[...], ca_ref[...], hm_ref[...]
        dskip_all = dex_ref[...] * xs
        for k in range(SSD_STEP):
            rs = slice(k * CHUNK, (k + 1) * CHUNK)
            ce, xdt, ecs, dskip = ce_all[rs], xdt_all[rs], ecs_all[rs], dskip_all[rs]
            cle = ce[CHUNK - 1:CHUNK, :]
            xdtb = xdt.astype(BF16)
            xst = (xdt * jnp.exp(cle - ce)).astype(BF16)
            ecl = jnp.exp(cle)
            bm = bm_ref[rs, :]
            cm = cm_ref[rs, :]
            for g in range(SSD_GROUPS):
                gs = slice(g * 512, (g + 1) * 512)
                ns = slice(g * SSD_STATE, (g + 1) * SSD_STATE)
                bm_g, cm_g = bm[:, ns], cm[:, ns]
                hprev = hst[g]
                hprev_b = hprev.astype(BF16)
                prev_ref[k, g] = hprev_b
                yoff = jnp.dot(cm_g, hprev_b, preferred_element_type=F32) * ecs[:, gs]
                st = lax.dot_general(bm_g, xst[:, gs], (((0,), (0,)), ((), ())), preferred_element_type=F32)
                hst[g] = hprev * ecl[:, gs] + st
                b4 = jnp.concatenate([bm_g] * 4, axis=0)
                cb4 = lax.dot_general(cm_g, b4, (((1,), (1,)), ((), ())), preferred_element_type=F32)
                for q in range(2):
                    cols = slice(g * 512 + q * 256, g * 512 + (q + 1) * 256)
                    _, mq = _quad_terms(ce[:, cols], cb4, itile_v, causal_v)
                    xbd = _block_diag4(xdtb[:, cols], hmask_v)
                    ydiag = jnp.dot(mq.astype(BF16), xbd, preferred_element_type=F32)
                    y_ref[rs, cols] = ydiag + yoff[:, q * 256:(q + 1) * 256] + dskip[:, cols]

    T = SSD_STEP * CHUNK
    ins = [("row", xbc, T, *CB_XS), ("row", xbc, T, *CB_BM), ("row", xbc, T, *CB_CM),
           ("row", dt_raw, T, DT_PAD, 0), ("full", dt_bias), ("full", a_neg), ("full", d_exp),
           ("full", e), ("full", ltri), ("full", itile), ("full", causal), ("full", hmask)]
    outs = [("row", S, D_SSD, F32, T, D_SSD, 0), ("lead", (nc, SSD_GROUPS, SSD_STATE, 512), BF16, SSD_STEP)]
    return _pcall(name, body, S // T, ins, outs, scratch=[pltpu.VMEM((SSD_GROUPS, SSD_STATE, 512), F32)], rider=rider)


def _ssd_bwd(name, xbc, dt_raw, dt_bias, a_neg, d_exp, prev, dy, consts, rider=None):
    S = xbc.shape[0]
    nc = S // CHUNK
    e, et, ltri, ltri_t, itile, causal, hmask = consts

    def body(xs_ref, bm_ref, cm_ref, dt_ref, dtb_ref, a_ref, dex_ref, prev_ref, dy_ref,
             e_ref, et_ref, ltri_ref, ltt_ref, it_ref, ca_ref, hm_ref,
             dx_ref, ddt_ref, da_ref, dbias_ref, dd_ref, dh, dce_ref, dxdt_ref):
        i = pl.program_id(0)

        @pl.when(i == 0)
        def _():
            dh[...] = jnp.zeros_like(dh)

        xs, raw, dtv, cs, dte, ce_all, xdt_all, ecs_all = _ssd_chunk_common(
            xs_ref, bm_ref, cm_ref, dt_ref, dtb_ref, a_ref, e_ref, ltri_ref)
        itile_v, causal_v, hmask_v = it_ref[...], ca_ref[...], hm_ref[...]
        dyv = dy_ref[...]
        last_row = _rows((CHUNK, 512)) == CHUNK - 1
        for k in reversed(range(SSD_STEP)):
            rs = slice(k * CHUNK, (k + 1) * CHUNK)
            ce, xdt, ecs = ce_all[rs], xdt_all[rs], ecs_all[rs]
            cle = ce[CHUNK - 1:CHUNK, :]
            dend = jnp.exp(cle - ce)
            xdtb = xdt.astype(BF16)
            xst = (xdt * dend).astype(BF16)
            ecl = jnp.exp(cle)
            bm = bm_ref[rs, :]
            cm = cm_ref[rs, :]
            for g in range(SSD_GROUPS):
                gs = slice(g * 512, (g + 1) * 512)
                ns = slice(g * SSD_STATE, (g + 1) * SSD_STATE)
                bm_g, cm_g = bm[:, ns], cm[:, ns]
                hprev_b = prev_ref[k, g]
                dhn = dh[g]
                dhn_b = dhn.astype(BF16)
                dy_g = dyv[rs, gs]
                ecs_g, dend_g, xdt_g, ecl_g = ecs[:, gs], dend[:, gs], xdt[:, gs], ecl[:, gs]
                z = jnp.dot(cm_g, hprev_b, preferred_element_type=F32)
                dz = dy_g * ecs_g
                dzb = dz.astype(BF16)
                dce_g = dz * z
                dcm_g = lax.dot_general(dzb, hprev_b, (((1,), (1,)), ((), ())), preferred_element_type=F32)
                dprev = lax.dot_general(cm_g, dzb, (((0,), (0,)), ((), ())), preferred_element_type=F32) + dhn * ecl_g
                dcl = jnp.sum(dhn * hprev_b.astype(F32), axis=0, keepdims=True) * ecl_g
                gmat = jnp.dot(bm_g, dhn_b, preferred_element_type=F32)
                dbm_g = lax.dot_general(xst[:, gs], dhn_b, (((1,), (1,)), ((), ())), preferred_element_type=F32)
                dxdt_g = gmat * dend_g
                t = gmat * xdt_g * dend_g
                dce_g = dce_g - t
                dcl = dcl + jnp.sum(t, axis=0, keepdims=True)
                dce_g = dce_g + jnp.where(last_row, dcl, 0.0)
                dh[g] = dprev
                b4 = jnp.concatenate([bm_g] * 4, axis=0)
                cb4 = lax.dot_general(cm_g, b4, (((1,), (1,)), ((), ())), preferred_element_type=F32)
                for q in range(2):
                    qs = slice(q * 256, (q + 1) * 256)
                    cols = slice(g * 512 + q * 256, g * 512 + (q + 1) * 256)
                    dec, mq = _quad_terms(ce[:, cols], cb4, itile_v, causal_v)
                    mqb = mq.astype(BF16)
                    xbd = _block_diag4(xdtb[:, cols], hmask_v)
                    dyq = dy_g[:, qs].astype(BF16)
                    dm = lax.dot_general(dyq, xbd, (((1,), (1,)), ((), ())), preferred_element_type=F32)
                    rmat = lax.dot_general(mqb, dyq, (((0,), (0,)), ((), ())), preferred_element_type=F32)
                    dxq = rmat[0:64] * hmask_v[0:1]
                    for j in range(1, 4):
                        dxq = dxq + rmat[64 * j:64 * (j + 1)] * hmask_v[j:j + 1]
                    tq = dm * dec
                    tqb = tq.astype(BF16)
                    dcm_g = dcm_g + jnp.dot(tqb, b4, preferred_element_type=F32)
                    rb = lax.dot_general(tqb, cm_g, (((0,), (0,)), ((), ())), preferred_element_type=F32)
                    dbm_g = dbm_g + rb[0:64] + rb[64:128] + rb[128:192] + rb[192:256]
                    dseg = tq * cb4
                    colsum = jnp.sum(dseg, axis=0, keepdims=True)
                    dce_ref[rs, cols] = dce_g[:, qs] + dseg - itile_v * colsum
                    dxdt_ref[rs, cols] = dxdt_g[:, qs] + dxq
                dx_ref[rs, D_SSD + g * SSD_STATE:D_SSD + (g + 1) * SSD_STATE] = dbm_g
                dx_ref[rs, D_SSD + D_BC + g * SSD_STATE:D_SSD + D_BC + (g + 1) * SSD_STATE] = dcm_g
        dxdt = dxdt_ref[...]
        dexv = dex_ref[...]
        dx_ref[:, 0:D_SSD] = dxdt * dte + dyv * dexv
        _acc(dd_ref, i, jnp.sum(dyv * xs, axis=0, keepdims=True))
        etv = et_ref[...]
        dcs = _dot01_r(dce_ref[...], etv)
        dda = _dot01_l(ltt_ref[...], dcs)
        av = a_ref[...]
        ddtv = dda * av + _dot01_r(dxdt * xs, etv)
        _acc(da_ref, i, jnp.sum(dda * dtv, axis=0, keepdims=True))
        draw = ddtv * _sigmoid(raw)
        ddt_ref[...] = draw.astype(BF16)
        _acc(dbias_ref, i, jnp.sum(draw, axis=0, keepdims=True))

    T = SSD_STEP * CHUNK
    ins = [("row", xbc, T, *CB_XS), ("row", xbc, T, *CB_BM), ("row", xbc, T, *CB_CM),
           ("row", dt_raw, T, DT_PAD, 0), ("full", dt_bias), ("full", a_neg), ("full", d_exp),
           ("lead", prev, SSD_STEP), ("row", dy, T, D_SSD, 0),
           ("full", e), ("full", et), ("full", ltri), ("full", ltri_t), ("full", itile), ("full", causal), ("full", hmask)]
    outs = [("row", S, D_XBC, F32, T, D_XBC, 0), ("row", S, DT_PAD, BF16, T, DT_PAD, 0),
            ("full", (1, DT_PAD), F32), ("full", (1, DT_PAD), F32), ("full", (1, D_SSD), F32)]
    scratch = [pltpu.VMEM((SSD_GROUPS, SSD_STATE, 512), F32), pltpu.VMEM((T, D_SSD), F32), pltpu.VMEM((T, D_SSD), F32)]
    return _pcall(name, body, S // T, ins, outs, scratch=scratch, reverse=True, rider=rider)


def _gate_norm_fwd(name, ycore, proj, norm_w, tile):
    S = ycore.shape[0]
    n = S // tile

    def body(y_ref, z_ref, w_ref, o_ref):
        y2 = y_ref[...] * _silu(z_ref[...].astype(F32))
        wv = w_ref[...]
        for g in range(SSD_GROUPS):
            gs = slice(g * 512, (g + 1) * 512)
            seg = y2[:, gs]
            r = lax.rsqrt(jnp.mean(seg * seg, axis=-1, keepdims=True) + EPS)
            o_ref[:, gs] = (seg * r * wv[:, gs]).astype(BF16)

    return _pcall(name, body, n, [("row", ycore, tile, D_SSD, 0), ("row", proj, tile, *CB_Z), ("full", norm_w)],
                  [("row", S, D_SSD, BF16, tile, D_SSD, 0)])[0]


def _gate_norm_bwd(name, dout, ycore, proj, norm_w, dproj, tile):
    S = ycore.shape[0]
    n = S // tile

    def body(do_ref, y_ref, z_ref, w_ref, buf_ref, dz_ref, dy_ref, dw_ref):
        i = pl.program_id(0)
        yv = y_ref[...]
        zv = z_ref[...].astype(F32)
        sz = _silu(zv)
        y2 = yv * sz
        dov = do_ref[...].astype(F32)
        wv = w_ref[...]
        dws, dy2s = [], []
        for g in range(SSD_GROUPS):
            gs = slice(g * 512, (g + 1) * 512)
            seg = y2[:, gs]
            r = lax.rsqrt(jnp.mean(seg * seg, axis=-1, keepdims=True) + EPS)
            yn = seg * r
            dws.append(jnp.sum(dov[:, gs] * yn, axis=0, keepdims=True))
            dyn = dov[:, gs] * wv[:, gs]
            dy2s.append(r * (dyn - yn * jnp.mean(dyn * yn, axis=-1, keepdims=True)))
        dy2 = jnp.concatenate(dy2s, axis=1)
        dy_ref[...] = dy2 * sz
        dz_ref[...] = (dy2 * yv * _dsilu(zv)).astype(BF16)
        _acc(dw_ref, i, jnp.concatenate(dws, axis=1))

    ins = [("row", dout, tile, D_SSD, 0), ("row", ycore, tile, D_SSD, 0), ("row", proj, tile, *CB_Z), ("full", norm_w),
           ("any", dproj)]
    outs = [("row", S, N_PROJ, BF16, tile, *CB_Z), ("row", S, D_SSD, F32, tile, D_SSD, 0), ("full", (1, D_SSD), F32)]
    return _pcall(name, body, n, ins, outs, aliases={4: 0})


_XA_SCALE = XA_HEAD_DIM ** -0.5
_NT = (((1,), (1,)), ((), ()))
_TN = (((0,), (0,)), ((), ()))


def _xa_probs(qh, kh):
    s = lax.dot_general(qh, kh, _NT, preferred_element_type=F32) * _XA_SCALE
    s = s - jnp.max(s, axis=-1, keepdims=True)
    p = jnp.exp(s)
    return p / jnp.sum(p, axis=-1, keepdims=True)


def _xa_fwd(name, proj, kv, tile):
    S = proj.shape[0]
    n = S // tile
    Dh = XA_HEAD_DIM

    def body(q_ref, kv_ref, o_ref):
        for hd in range(XA_HEADS):
            qh = q_ref[:, hd * Dh:(hd + 1) * Dh]
            kh = kv_ref[:, hd * Dh:(hd + 1) * Dh]
            vh = kv_ref[:, D_MODEL + hd * Dh:D_MODEL + (hd + 1) * Dh]
            p = _xa_probs(qh, kh)
            o_ref[:, hd * Dh:(hd + 1) * Dh] = jnp.dot(p.astype(BF16), vh, preferred_element_type=F32).astype(BF16)

    return _pcall(name, body, n, [("row", proj, tile, *CB_XA_Q), ("full", kv)],
                  [("row", S, D_MODEL, BF16, tile, D_MODEL, 0)])[0]


def _xa_bwd(name, proj, kv, dout, dproj, tile):
    S = proj.shape[0]
    n = S // tile
    Dh = XA_HEAD_DIM

    def body(q_ref, kv_ref, do_ref, buf_ref, dq_ref, dkv_ref):
        i = pl.program_id(0)
        for hd in range(XA_HEADS):
            ks_ = slice(hd * Dh, (hd + 1) * Dh)
            vs_ = slice(D_MODEL + hd * Dh, D_MODEL + (hd + 1) * Dh)
            qh = q_ref[:, ks_]
            kh = kv_ref[:, ks_]
            vh = kv_ref[:, vs_]
            doh = do_ref[:, ks_].astype(BF16)
            p = _xa_probs(qh, kh)
            pb = p.astype(BF16)
            dp = lax.dot_general(doh, vh, _NT, preferred_element_type=F32)
            dv = lax.dot_general(pb, doh, _TN, preferred_element_type=F32)
            ds = (p * (dp - jnp.sum(dp * p, axis=-1, keepdims=True)) * _XA_SCALE).astype(BF16)
            dq_ref[:, ks_] = jnp.dot(ds, kh, preferred_element_type=F32).astype(BF16)
            dk = lax.dot_general(ds, qh, _TN, preferred_element_type=F32)

            @pl.when(i == 0)
            def _(dk=dk, dv=dv, ks_=ks_, vs_=vs_):
                dkv_ref[:, ks_] = dk
                dkv_ref[:, vs_] = dv

            @pl.when(i > 0)
            def _(dk=dk, dv=dv, ks_=ks_, vs_=vs_):
                dkv_ref[:, ks_] += dk
                dkv_ref[:, vs_] += dv

    ins = [("row", proj, tile, *CB_XA_Q), ("full", kv), ("row", dout, tile, D_MODEL, 0), ("any", dproj)]
    outs = [("row", S, N_PROJ, BF16, tile, *CB_XA_Q), ("full", (N_MEM, 2 * D_MODEL), F32)]
    return _pcall(name, body, n, ins, outs, aliases={3: 0})


def _adamw(name, w, g, m, v):
    L, R, C = w.shape
    tile = _pick(R, [t for t in (256, 128, 64, 32, 16, 8) if t * C <= 128 * 2048])
    bc1 = 1.0 - ADAM_B1 ** ADAM_STEP
    bc2 = 1.0 - ADAM_B2 ** ADAM_STEP

    def body(w_ref, g_ref, m_ref, v_ref, d_ref, nm_ref, nv_ref):
        gv = g_ref[...]
        mn = ADAM_B1 * m_ref[...] + (1.0 - ADAM_B1) * gv
        vn = ADAM_B2 * v_ref[...] + (1.0 - ADAM_B2) * (gv * gv)
        nm_ref[...] = mn
        nv_ref[...] = vn
        d_ref[...] = -ADAM_LR * ((mn / bc1) / (jnp.sqrt(vn / bc2) + ADAM_EPS) + ADAM_WD * w_ref[...])

    spec = pl.BlockSpec((1, tile, C), lambda l, i: (l, i, 0))
    return pl.pallas_call(
        body, name=name, grid=(L, R // tile), in_specs=[spec] * 4, out_specs=[spec] * 3,
        out_shape=[jax.ShapeDtypeStruct((L, R, C), F32)] * 3,
        compiler_params=_cparams(("arbitrary", "arbitrary")),
    )(w, g, m, v)


def _axpy(name, coef, a, b, tile):
    S, C = a.shape
    n = S // tile

    def body(a_ref, b_ref, o_ref):
        o_ref[...] = coef * a_ref[...].astype(F32) + b_ref[...].astype(F32)

    return _pcall(name, body, n, [("row", a, tile, C, 0), ("row", b, tile, C, 0)], [("row", S, C, F32, tile, C, 0)])[0]


def _sum_terms(name, terms, out_dtype):
    R, C = terms[0].shape
    tile = _pick(R, (512, 256, 128, 64, 32, 16, 8))
    n = R // tile
    nt = len(terms)

    def body(*refs):
        vals = [r[...].astype(F32) for r in refs[:nt]]
        while len(vals) > 1:
            vals = [vals[k] + vals[k + 1] for k in range(0, len(vals), 2)]
        refs[nt][...] = vals[0].astype(out_dtype)

    return _pcall(name, body, n, [("row", t, tile, C, 0) for t in terms], [("row", R, C, out_dtype, tile, C, 0)])[0]


def _me():
    return lax.axis_index("x"), lax.axis_index("y"), lax.axis_index("c")


def _other_chips(x, y):
    return [(1 - x, y), (x, 1 - y), (1 - x, 1 - y)]


_ANY = pl.BlockSpec(memory_space=pl.ANY)


def _swap_sibling(name, src):
    def body(src_ref, out_ref, send_sem, recv_sem):
        x, y, c = _me()
        cp = pltpu.make_async_remote_copy(src_ref=src_ref, dst_ref=out_ref, send_sem=send_sem, recv_sem=recv_sem,
                                          device_id=(x, y, 1 - c), device_id_type=MESH)
        cp.start()
        cp.wait()

    return pl.pallas_call(
        body, name=name, in_specs=[_ANY], out_specs=_ANY, out_shape=jax.ShapeDtypeStruct(src.shape, src.dtype),
        scratch_shapes=[pltpu.SemaphoreType.DMA, pltpu.SemaphoreType.DMA],
    )(src)


W_IN_SHARD = N_IN // 4
W_IN_INNER = W_IN_SHARD - 8


def _win_rows(chip_x, chip_y):
    start = (2 * chip_x + chip_y) * W_IN_SHARD + 8 * chip_y
    return pl.ds(pl.multiple_of(start, 2 * 8), W_IN_INNER)


def _dma_sems(*shape):
    return [pltpu.SemaphoreType.DMA(shape), pltpu.SemaphoreType.DMA(shape)]


def _gather_rider(srcs, kinds, bufs, part):
    n = len(srcs)

    def copy(refs, i, k, cx, cy, to, own=False):
        src_refs, out_refs, _, (send_sems, recv_sems) = refs
        d = out_refs[i].at[2 * cx + cy] if kinds[i] == "blk" else out_refs[i].at[_win_rows(cx, cy)]
        return pltpu.make_async_remote_copy(src_ref=src_refs[i].at[part] if own else d, dst_ref=d,
                                            send_sem=send_sems.at[i, k], recv_sem=recv_sems.at[i, k],
                                            device_id=to, device_id_type=MESH)

    def start(*refs):
        x, y, c = _me()

        @pl.when(c == part)
        def _():
            for k, (cx, cy) in enumerate(_other_chips(x, y)):
                for i in range(n):
                    copy(refs, i, k, x, y, (cx, cy, part), own=True).start()

    def mid(*refs):
        x, y, c = _me()

        @pl.when(c == part)
        def _():
            for k, (cx, cy) in enumerate(_other_chips(x, y)):
                for i in range(n):
                    copy(refs, i, k, cx, cy, (x, y, part)).wait_recv()
                    copy(refs, i, 3 + k, cx, cy, (x, y, 1 - part)).start()

    def finish(*refs):
        x, y, c = _me()
        chips = _other_chips(x, y)

        @pl.when(c == part)
        def _():
            for k, (cx, cy) in enumerate(chips):
                for i in range(n):
                    copy(refs, i, k, x, y, (cx, cy, part), own=True).wait_send()
                    copy(refs, i, 3 + k, cx, cy, (x, y, 1 - part)).wait_send()

        @pl.when(c != part)
        def _():
            for k, (cx, cy) in enumerate(chips):
                for i in range(n):
                    copy(refs, i, 3 + k, cx, cy, (x, y, 1 - part)).wait_recv()

    return _Rider(srcs, bufs, [], _dma_sems(n, 6), start, finish, mid)


def _pair_swap_rider(gs, layer):
    n = len(gs)

    def copy(refs, i):
        in_refs, _, out_refs, (send_sems, recv_sems) = refs
        x, y, _c = _me()
        return pltpu.make_async_remote_copy(src_ref=in_refs[i], dst_ref=out_refs[i], send_sem=send_sems.at[i],
                                            recv_sem=recv_sems.at[i], device_id=(x, y, layer), device_id_type=MESH)

    def start(*refs):
        @pl.when(_me()[2] != layer)
        def _():
            for i in range(n):
                copy(refs, i).start()

    def finish(*refs):
        @pl.when(_me()[2] != layer)
        def _():
            for i in range(n):
                copy(refs, i).wait_send()

        @pl.when(_me()[2] == layer)
        def _():
            for i in range(n):
                copy(refs, i).wait_recv()

    return _Rider(gs, [], [jax.ShapeDtypeStruct(g.shape, g.dtype) for g in gs], _dma_sems(n), start, finish)


def _chip_send_rider(pairs, layer):
    n = len(pairs)

    def copies(refs):
        in_refs, _, out_refs, (send_sems, recv_sems) = refs
        x, y, _c = _me()
        return [pltpu.make_async_remote_copy(src_ref=in_refs[i].at[2 * cx + cy], dst_ref=out_refs[i].at[k],
                                             send_sem=send_sems.at[i, k], recv_sem=recv_sems.at[i, k],
                                             device_id=(cx, cy, layer), device_id_type=MESH)
                for k, (cx, cy) in enumerate(_other_chips(x, y)) for i in range(n)]

    def start(*refs):
        @pl.when(_me()[2] == layer)
        def _():
            for cp in copies(refs):
                cp.start()

    def finish(*refs):
        @pl.when(_me()[2] == layer)
        def _():
            for cp in copies(refs):
                cp.wait()

    outs = [jax.ShapeDtypeStruct((3,) + p.shape[1:], p.dtype) for p in pairs]
    return _Rider(pairs, [], outs, _dma_sems(n, 3), start, finish)


def _bcast_rider(srcs):
    n = len(srcs)

    def copies(refs):
        in_refs, _, out_refs, (send_sems, recv_sems) = refs
        x, y, c = _me()
        return [pltpu.make_async_remote_copy(src_ref=in_refs[i], dst_ref=out_refs[i].at[k], send_sem=send_sems.at[i, k],
                                             recv_sem=recv_sems.at[i, k], device_id=(cx, cy, c), device_id_type=MESH)
                for k, (cx, cy) in enumerate(_other_chips(x, y)) for i in range(n)]

    def start(*refs):
        for cp in copies(refs):
            cp.start()

    def finish(*refs):
        for cp in copies(refs):
            cp.wait()

    return _Rider(srcs, [], [jax.ShapeDtypeStruct((3,) + s.shape, s.dtype) for s in srcs], _dma_sems(n, 3), start, finish)


def _merge_riders(a, b):
    assert not a.alias and not b.alias and a.mid is None and b.mid is None

    def both(fa, fb):
        def run(r_in, r_alias, r_out, sems):
            fa(r_in[:len(a.ins)], (), r_out[:len(a.outs)], sems[:len(a.sems)])
            fb(r_in[len(a.ins):], (), r_out[len(a.outs):], sems[len(a.sems):])
        return run

    return _Rider(a.ins + b.ins, [], a.outs + b.outs, a.sems + b.sems, both(a.start, b.start), both(a.finish, b.finish))


def _join_parts_multi(name, bufs):
    n = len(bufs)

    def body(*refs):
        out_refs = refs[n:2 * n]
        send_sems, recv_sems = refs[2 * n:]
        x, y, c = _me()

        def copy(i, part):
            return pltpu.make_async_remote_copy(src_ref=out_refs[i].at[part], dst_ref=out_refs[i].at[part],
                                                send_sem=send_sems.at[i], recv_sem=recv_sems.at[i],
                                                device_id=(x, y, 1 - c), device_id_type=MESH)

        for i in range(n):
            copy(i, c).start()
        for i in range(n):
            copy(i, c).wait_send()
            copy(i, 1 - c).wait_recv()

    return pl.pallas_call(
        body, name=name, in_specs=[_ANY] * n, out_specs=[_ANY] * n,
        out_shape=[jax.ShapeDtypeStruct(b.shape, b.dtype) for b in bufs],
        input_output_aliases={i: i for i in range(n)},
        scratch_shapes=[pltpu.SemaphoreType.DMA((n,)), pltpu.SemaphoreType.DMA((n,))],
    )(*bufs)


def _col_tiles(R, C):
    tr = _pick(R, (512, 256, 128))
    if tr != R:
        return tr, C
    if R * C <= 512 * 1024:
        return R, C
    return R, _pick(C, (256, 128))


def _active(layer):
    return (_me()[2] == layer).astype(jnp.int32).reshape(1)


def _pair_sum(name, g, theirs, layer):
    n4, R, C = g.shape
    tr, tc = _col_tiles(R, C)

    def body(a_ref, g_ref, t_ref, o_ref, ob_ref):
        s = g_ref[...] + t_ref[...]
        o_ref[...] = s
        ob_ref[...] = s.astype(BF16)

    spec = pl.BlockSpec((1, tr, tc), lambda d, i, k, a: (d * a[0], i * a[0], k * a[0]))
    return pl.pallas_call(
        body, name=name,
        grid_spec=pltpu.PrefetchScalarGridSpec(num_scalar_prefetch=1, grid=(n4, R // tr, C // tc),
                                               in_specs=[spec, spec], out_specs=[spec, spec]),
        out_shape=[jax.ShapeDtypeStruct((n4, R, C), F32), jax.ShapeDtypeStruct((n4, R, C), BF16)],
        compiler_params=_cparams(("arbitrary", "arbitrary", "arbitrary")),
    )(_active(layer), g, theirs)


def _chip_sum(name, pair, got, j, layer, buf=None):
    _, R, C = pair.shape
    tr, tc = _col_tiles(R, C)

    def body(ja_ref, p_ref, g0, g1, g2, *rest):
        rest[-1][0] = (p_ref[0] + g2[0].astype(F32)) + (g0[0].astype(F32) + g1[0].astype(F32))

    def gspec(k):
        return pl.BlockSpec((1, tr, tc), lambda i, q, ja, k=k: (k, i * ja[1], q * ja[1]))

    in_specs = [pl.BlockSpec((1, tr, tc), lambda i, q, ja: (ja[0], i * ja[1], q * ja[1])), gspec(0), gspec(1), gspec(2)]
    args = [jnp.concatenate([j.reshape(1).astype(jnp.int32), _active(layer)]), pair, got, got, got]
    aliases = {}
    if buf is not None:
        in_specs.append(_ANY)
        args.append(buf)
        aliases = {5: 0}
    return pl.pallas_call(
        body, name=name,
        grid_spec=pltpu.PrefetchScalarGridSpec(
            num_scalar_prefetch=1, grid=(R // tr, C // tc), in_specs=in_specs,
            out_specs=pl.BlockSpec((1, tr, tc), lambda i, q, ja: (layer, i * ja[1], q * ja[1]))),
        out_shape=jax.ShapeDtypeStruct((2, R, C), F32), input_output_aliases=aliases,
        compiler_params=_cparams(("arbitrary", "arbitrary")),
    )(*args)


LANES = 1024
_BIG = (("w_in", "col"), ("ffn_w_in", "col"), ("mem_w_kv", "col"),
        ("w_br_lru", "row"), ("w_br_ssd", "row"), ("w_br_xa", "row"), ("w_out", "row"), ("ffn_w_down", "row"))
_SMALL_SHARDED = ("b_gate", "lru_conv_w", "ssd_conv_w")
_SMALL = ("b_gate", "lru_conv_w", "lru_conv_b", "lru_w_a", "lru_b_a", "lru_w_i", "lru_b_i", "lru_lambda",
          "ssd_conv_w", "ssd_conv_b", "ssd_dt_bias", "ssd_a_log", "ssd_d", "ssd_norm_w",
          "ln1_g", "ln1_b", "ln2_g", "ln2_b")
_W_NAMES = ("w_in", "b_gate", "lru_conv_w", "lru_conv_b", "lru_w_a", "lru_b_a", "lru_w_i", "lru_b_i", "lru_lambda",
            "ssd_conv_w", "ssd_conv_b", "ssd_dt_bias", "ssd_a_log", "ssd_d", "ssd_norm_w", "mem_w_kv", "w_br_lru",
            "w_br_ssd", "w_br_xa", "w_out", "ln1_g", "ln1_b", "ffn_w_in", "ffn_w_down", "ln2_g", "ln2_b")
_IN_ORDER = ((4096, 7168), (8224, 11296), (2048, 4096), (0, 1024), (1024, 2048), (7200, 8224))
_IN_DT = (7168, 7200)


def _flat_rows(parts, row_multiple):
    flat = jnp.concatenate([p.reshape(-1) for p in parts])
    rows = -(-flat.size // LANES)
    rows = -(-rows // row_multiple) * row_multiple
    return jnp.pad(flat, (0, rows * LANES - flat.size)).reshape(rows, LANES)


def _take_parts(flat, shapes):
    out, off = [], 0
    for shp in shapes:
        size = math.prod(shp)
        out.append(flat[off:off + size].reshape(shp))
        off += size
    return out


_GATHER_NAMES = ("w_in", "w_in_edge") + tuple(nm for nm, _ in _BIG[1:]) + _SMALL_SHARDED
_GATHER_KINDS = ("rows",) + ("blk",) * (len(_GATHER_NAMES) - 1)


def _gather_sources(w):
    x, y, _ = _me()
    j = 2 * x + y
    wt = jnp.swapaxes(w["w_in"], 1, 2)
    inner = lax.dynamic_slice_in_dim(wt, 8 * y, W_IN_INNER, axis=1).astype(BF16)
    edge = lax.dynamic_slice_in_dim(wt, (1 - y) * W_IN_INNER, 8, axis=1)
    srcs = [inner, edge] + [w[nm].astype(BF16) for nm, _ in _BIG[1:]] + [w[nm] for nm in _SMALL_SHARDED]
    bufs = []
    for l in range(DEPTH):
        row = []
        for s, kd in zip(srcs, _GATHER_KINDS):
            if kd == "blk":
                row.append(lax.dynamic_update_slice_in_dim(lax.empty((4,) + s.shape[1:], s.dtype), s[l][None], j, axis=0))
            else:
                row.append(lax.dynamic_update_slice_in_dim(lax.empty((N_IN, D_MODEL), s.dtype), s[l],
                                                           j * W_IN_SHARD + 8 * y, axis=0))
        bufs.append(row)
    return srcs, bufs


def _layer_weights(names, got):
    g = dict(zip(names, got))
    full = {}
    if "w_in" in g:
        wt_all, edges = g["w_in"], g["w_in_edge"]
        for a, b in ((0, 1), (2, 3)):
            tile = jnp.concatenate([edges[a], edges[b]], axis=0).astype(BF16)
            wt_all = lax.dynamic_update_slice_in_dim(wt_all, tile, b * W_IN_SHARD - 8, axis=0)
        full["w_re_t"] = jnp.concatenate([wt_all[lo:hi] for lo, hi in _IN_ORDER], axis=0)
        full["w_dt_t"] = jnp.pad(wt_all[_IN_DT[0]:_IN_DT[1]], ((0, DT_PAD - SSD_HEADS), (0, 0)))
    kinds = dict(_BIG)
    for nm in names:
        if nm in ("w_in", "w_in_edge"):
            continue
        _, r, cdim = g[nm].shape
        if kinds.get(nm, "col") == "col":
            full[nm] = jnp.moveaxis(g[nm], 0, 1).reshape(r, 4 * cdim)
        else:
            full[nm] = g[nm].reshape(4 * r, cdim)
    return full


def _per_chip(g, nm):
    r, cdim = g.shape
    if nm == "w_in" or dict(_BIG)[nm] == "row":
        return g.reshape(4, r // 4, cdim)
    return jnp.moveaxis(g.reshape(r, 4, cdim // 4), 1, 0)


def _layer_fwd(xin, mem2, W, P, consts, riders=None, late=None, xin_b=None):
    S = xin.shape[0]
    T, TX = min(256, S), min(512, S)
    xin_mm = xin if xin_b is None else xin_b
    sv = {"xin": xin, "xin_mm": xin_mm}
    riders = riders or {}
    rout = {}
    proj = _mm("proj", xin_mm, W["w_re_t"], "nt", BF16, rider=riders.get("proj"))
    if "proj" in riders:
        proj, rout["proj"] = proj[0], proj[1:]
    if late is not None:
        w_late, p_late = late(rout)
        W, P = {**W, **w_late}, {**P, **p_late}
    dt_raw = _mm("proj_dt", xin_mm, W["w_dt_t"], "nt", F32)
    xc = _conv_fwd("lru_conv", proj, CB_LRU_X, P["lru_conv_w"], P["lru_conv_b"], False, F32, T)
    res = _lru_fwd("lru_scan", xc, proj, P["lru_w_a"], P["lru_w_i"], P["lru_b_a"], P["lru_b_i"], P["lru_lambda"], T,
                   rider=riders.get("lru_scan"))
    y_lru, h = res[0], res[1]
    if "lru_scan" in riders:
        rout["lru_scan"] = res[2:]
    xact = _conv_fwd("ssd_conv", proj, CB_XBC, P["ssd_conv_w"], P["ssd_conv_b"], True, BF16, T)
    res = _ssd_fwd("ssd_scan", xact, dt_raw, P["dt_bias"], P["a_neg"], P["d_exp"], consts, rider=riders.get("ssd_scan"))
    ycore, prev = res[0], res[1]
    if "ssd_scan" in riders:
        rout["ssd_scan"] = res[2:]
    y_ssd = _gate_norm_fwd("ssd_norm", ycore, proj, P["ssd_norm_w"], TX)
    kv = _mm("mem_kv", mem2, W["w_kv"], "nn", BF16)
    y_xa = _xa_fwd("xattn", proj, kv, TX)
    p_l = _mm("br_lru", y_lru, W["w_l"], "nn", BF16)
    p_s = _mm("br_ssd", y_ssd, W["w_s"], "nn", BF16)
    p_x = _mm("br_xa", y_xa, W["w_x"], "nn", BF16)
    merged = _merge_fwd("merge", proj, P["b_gate"], p_l, p_s, p_x, TX)
    mix = _mm("mix_out", merged, W["w_o"], "nn", F32)
    x1, x1b, xh1, rs1 = _ln_fwd("ln_fwd", xin, mix, P["ln1_g"], P["ln1_b"], TX)
    gu = _mm("ffn_in", x1b, W["w_fi"], "nn", BF16, rider=riders.get("ffn_in"))
    if "ffn_in" in riders:
        gu, rout["ffn_in"] = gu[0], gu[1:]
    hmid = _swiglu_fwd("swiglu", gu, TX)
    f = _mm("ffn_down", hmid, W["w_fd"], "nn", F32)
    x2, x2b, xh2, rs2 = _ln_fwd("ln_fwd", x1, f, P["ln2_g"], P["ln2_b"], TX)
    sv.update(x1_mm=x1b, out_b=x2b)
    sv.update(proj=proj, dt_raw=dt_raw, xc=xc, h=h, y_lru=y_lru, xact=xact, ycore=ycore, prev=prev, y_ssd=y_ssd, kv=kv,
              y_xa=y_xa, p_l=p_l, p_s=p_s, p_x=p_x, merged=merged, x1=x1, xh1=xh1, rs1=rs1, gu=gu, hmid=hmid,
              xh2=xh2, rs2=rs2)
    return x2, sv, rout


def _layer_bwd(dys, coefs, sv, mem2, W, P, consts, riders=None):
    S = sv["xin"].shape[0]
    T, TX = min(256, S), min(512, S)
    proj = sv["proj"]
    g = {}
    rout = {}

    def ride(host):
        return riders[host](g, rout) if riders and host in riders else None

    dz2, g["ln2_g"], g["ln2_b"] = _ln_bwd("ln_bwd_%d" % len(dys), dys, coefs, sv["xh2"], sv["rs2"], P["ln2_g"], TX)
    dhmid = _mm("d_hmid", dz2, W["w_fd"], "nt", BF16)
    g["ffn_w_down"] = _mm("dw_ffn_down", sv["hmid"], dz2, "tn", F32)
    dgu = _swiglu_bwd("swiglu_bwd", sv["gu"], dhmid, T)
    dx1f = _mm("d_x1", dgu, W["w_fi"], "nt", F32)
    g["ffn_w_in"] = _mm("dw_ffn_in", sv["x1_mm"], dgu, "tn", F32)
    dz1, g["ln1_g"], g["ln1_b"] = _ln_bwd("ln_bwd_2", [dz2, dx1f], [ALPHA, 1.0], sv["xh1"], sv["rs1"], P["ln1_g"], TX)
    dmerged = _mm("d_merged", dz1, W["w_o"], "nt", BF16)
    g["w_out"] = _mm("dw_out", sv["merged"], dz1, "tn", F32)
    dproj, dpl, dps, dpx, dbg0, dbg1, dbg2 = _merge_bwd("merge_bwd", proj, P["b_gate"], sv["p_l"], sv["p_s"], sv["p_x"],
                                                         dmerged, TX)
    g["b_gate"] = jnp.concatenate([dbg0, dbg1, dbg2], axis=0)
    dy_lru = _mm("d_ylru", dpl, W["w_l"], "nt", BF16)
    g["w_br_lru"] = _mm("dw_br_lru", sv["y_lru"], dpl, "tn", F32)
    dy_ssd = _mm("d_yssd", dps, W["w_s"], "nt", BF16)
    g["w_br_ssd"] = _mm("dw_br_ssd", sv["y_ssd"], dps, "tn", F32)
    dy_xa = _mm("d_yxa", dpx, W["w_x"], "nt", BF16)
    g["w_br_xa"] = _mm("dw_br_xa", sv["y_xa"], dpx, "tn", F32)
    dproj, dkv = _xa_bwd("xattn_bwd", proj, sv["kv"], dy_xa, dproj, TX)
    g["mem_w_kv"] = _mm("dw_kv", mem2, dkv, "tn", F32)
    dproj, dycore, g["ssd_norm_w"] = _gate_norm_bwd("ssd_norm_bwd", dy_ssd, sv["ycore"], proj, P["ssd_norm_w"], dproj, T)
    res = _ssd_bwd("ssd_scan_bwd", sv["xact"], sv["dt_raw"], P["dt_bias"], P["a_neg"], P["d_exp"],
                   sv["prev"], dycore, consts, rider=ride("ssd_scan_bwd"))
    dxact, ddt, d_a, g_dtb, d_dexp = res[:5]
    rout["ssd_scan_bwd"] = res[5:]
    g["ssd_dt_bias"] = g_dtb[:, :SSD_HEADS]
    g["ssd_a_log"] = d_a[:, :SSD_HEADS] * P["a_neg"][:, :SSD_HEADS]
    g["ssd_d"] = jnp.sum(d_dexp.reshape(SSD_HEADS, SSD_HEAD_DIM), axis=-1)
    res = _conv_bwd("ssd_conv_bwd", proj, CB_XBC, P["ssd_conv_w"], P["ssd_conv_b"], True, dxact, dproj, T,
                    rider=ride("ssd_conv_bwd"))
    dproj, w0, w1, w2, w3, g["ssd_conv_b"] = res[:6]
    rout["ssd_conv_bwd"] = res[6:]
    g["ssd_conv_w"] = jnp.concatenate([w0, w1, w2, w3], axis=0)
    res = _lru_bwd("lru_scan_bwd", dy_lru, sv["xc"], proj, sv["h"], P["lru_w_a"], P["lru_w_i"], P["lru_b_a"], P["lru_b_i"],
                   P["lru_lambda"], dproj, T, rider=ride("lru_scan_bwd"))
    dproj, dxc, g["lru_w_a"], g["lru_w_i"], g["lru_b_a"], g["lru_b_i"], g["lru_lambda"] = res[:7]
    rout["lru_scan_bwd"] = res[7:]
    dproj, w0, w1, w2, w3, g["lru_conv_b"] = _conv_bwd("lru_conv_bwd", proj, CB_LRU_X, P["lru_conv_w"], P["lru_conv_b"], False,
                                                       dxc, dproj, T)
    g["lru_conv_w"] = jnp.concatenate([w0, w1, w2, w3], axis=0)
    xin = sv["xin_mm"]
    rd = ride("dw_in")
    dw_re_t = _mm("dw_in", dproj, xin, "tn", F32, rider=rd)
    if rd is not None:
        dw_re_t, rout["dw_in"] = dw_re_t[0], dw_re_t[1:]
    dw_dt_t = _mm("dw_in_dt", ddt, xin, "tn", F32)
    pieces = {rng: dw_re_t[off:off + rng[1] - rng[0]]
              for rng, off in zip(_IN_ORDER, (0, 3072, 6144, 8192, 9216, 10240))}
    pieces[_IN_DT] = dw_dt_t[:SSD_HEADS]
    g["w_in"] = jnp.concatenate([pieces[k] for k in sorted(pieces)], axis=0)
    rd = ride("d_xin")
    dxp = _mm("d_xin", dproj, W["w_re_t"], "nn", F32, rider=rd)
    if rd is not None:
        dxp, rout["d_xin"] = dxp[0], dxp[1:]
    dxs = _mm("d_xin_dt", ddt, W["w_dt_t"], "nn", F32, add=dxp)
    return [dz1, dxs], [ALPHA, 1.0], g, rout


def _step(a):
    x2d, mem2, target = a["x"][0], a["mem"][0], a["loss_target"][0]
    S = x2d.shape[0]
    T = min(256, S)
    xi, yi, ci = _me()
    j = 2 * xi + yi
    w = {nm: a[nm] for nm in _W_NAMES}
    consts = _ssd_consts()
    row = lambda v: v.reshape(1, -1)
    pad_h = lambda v: jnp.pad(v.reshape(1, -1), ((0, 0), (0, DT_PAD - SSD_HEADS)))

    w_keys = dict(w_re_t="w_re_t", w_dt_t="w_dt_t", w_fi="ffn_w_in", w_kv="mem_w_kv", w_l="w_br_lru", w_s="w_br_ssd",
                  w_x="w_br_xa", w_o="w_out", w_fd="ffn_w_down")

    def gathered_params(full):
        return ({k: full[v] for k, v in w_keys.items() if v in full}, {k: full[k] for k in _SMALL_SHARDED if k in full})

    def local_params(l):
        return dict(
            lru_conv_b=row(w["lru_conv_b"][l]), lru_w_a=w["lru_w_a"][l].astype(BF16), lru_w_i=w["lru_w_i"][l].astype(BF16),
            lru_b_a=row(w["lru_b_a"][l]), lru_b_i=row(w["lru_b_i"][l]), lru_lambda=row(w["lru_lambda"][l]),
            ssd_conv_b=row(w["ssd_conv_b"][l]), dt_bias=pad_h(w["ssd_dt_bias"][l]), a_neg=pad_h(-jnp.exp(w["ssd_a_log"][l])),
            d_exp=jnp.broadcast_to(w["ssd_d"][l][:, None], (SSD_HEADS, SSD_HEAD_DIM)).reshape(1, D_SSD),
            ssd_norm_w=row(w["ssd_norm_w"][l]), ln1_g=row(w["ln1_g"][l]), ln1_b=row(w["ln1_b"][l]),
            ln2_g=row(w["ln2_g"][l]), ln2_b=row(w["ln2_b"][l]))

    srcs, bufs = _gather_sources(w)
    names, kinds = list(_GATHER_NAMES), list(_GATHER_KINDS)
    groups = {"w_in": [0, 1], "mixer": [3, 4, 5, 6, 7, 9, 10, 11], "ffn": [2, 8]}

    def gather(group, layer):
        idx = groups[group]
        return _gather_rider([srcs[i] for i in idx], [kinds[i] for i in idx], [bufs[layer][i] for i in idx], layer)

    def weights_of(group, got):
        return _layer_weights([names[i] for i in groups[group]], got)

    rest = groups["mixer"] + groups["ffn"]
    got = _run_rider("gather_weights", gather("w_in", 0))
    W0, _ = gathered_params(weights_of("w_in", got))

    def late0(rout_):
        wl, pl_ = gathered_params(_layer_weights([names[i] for i in rest], rout_["proj"]))
        return wl, pl_

    rest_rider = _gather_rider([srcs[i] for i in rest], [kinds[i] for i in rest], [bufs[0][i] for i in rest], 0)
    x1, sv0, rout = _layer_fwd(x2d, mem2, W0, local_params(0), consts, late=late0, riders={
        "proj": rest_rider, "lru_scan": gather("mixer", 1), "ssd_scan": gather("w_in", 1), "ffn_in": gather("ffn", 1)})
    full1 = {**weights_of("w_in", rout["ssd_scan"]), **weights_of("mixer", rout["lru_scan"]), **weights_of("ffn", rout["ffn_in"])}
    W1, P1g = gathered_params(full1)
    P1 = {**local_params(1), **P1g}
    w0_late, p0_late = late0(rout)
    W0, P0 = {**W0, **w0_late}, {**local_params(0), **p0_late}
    xcur, sv1, _ = _layer_fwd(x1, mem2, W1, P1, consts, xin_b=sv0["out_b"])
    dy, loss_part = _loss_fwd_bwd("loss", xcur, target, min(512, S))
    loss = lax.psum(loss_part[0, 0], ("x", "y", "c"))

    big_names = [nm for nm, _ in _BIG]
    rest_names = big_names[1:]
    st = {}

    def swap_rider(key, names, layer):
        def build(g, rout_):
            st[key] = [_per_chip(g[nm], nm) for nm in names]
            return _pair_swap_rider(st[key], layer)
        return build

    def pair_sums(key, theirs, layer):
        return [_pair_sum("grads_pair_sum", g_, t_, layer) for g_, t_ in zip(st[key], theirs)]

    dys, coefs, g1, rout1 = _layer_bwd([dy], [1.0], sv1, mem2, W1, P1, consts,
                                       riders={"d_xin": swap_rider("g1", big_names, 1)})
    pairs1 = pair_sums("g1", rout1["d_xin"], 1)

    def send_rest0(g, rout_):
        st["pairs0"] = pair_sums("g0", rout_["lru_scan_bwd"], 0)
        return _chip_send_rider([pb for _, pb in st["pairs0"]], 0)

    dys, coefs, g0, rout0 = _layer_bwd(dys, coefs, sv0, mem2, W0, P0, consts, riders={
        "ssd_scan_bwd": lambda g, r: _chip_send_rider([pb for _, pb in pairs1[:1]], 1),
        "ssd_conv_bwd": lambda g, r: _chip_send_rider([pb for _, pb in pairs1[1:]], 1),
        "lru_scan_bwd": swap_rider("g0", rest_names, 0),
        "dw_in": send_rest0,
        "d_xin": swap_rider("g0_in", ["w_in"], 0)})
    grad_x = _axpy("grad_x", coefs[0], dys[0], dys[1], min(512, S))[None]
    pairs0_in = pair_sums("g0_in", rout0["d_xin"], 0)
    layer_grads = [g0, g1]
    stacked = {nm: jnp.stack([layer_grads[l][nm] for l in range(DEPTH)]) for nm in _SMALL}
    small_parts = [stacked[nm].reshape((DEPTH,) + tuple(sh)) for nm, sh in
                   ((nm, (3, D_MODEL) if nm == "b_gate" else (4, D_MODEL) if nm == "lru_conv_w" else
                     (4, D_XBC) if nm == "ssd_conv_w" else w[nm].shape[1:]) for nm in _SMALL)]
    small_buf = _flat_rows(small_parts, 8)
    small_pair = _sum_terms("small_pair_sum", [small_buf, _swap_sibling("small_pair_swap", small_buf)], F32)
    sent = _run_rider("grads_chip_send", _merge_riders(_chip_send_rider([pb for _, pb in pairs0_in], 0),
                                                        _bcast_rider([small_pair])))
    got0_in, small_got = sent[:1], sent[1]
    small_total = _sum_terms("small_chip_sum", [small_pair, small_got[2], small_got[0], small_got[1]], F32)
    small = dict(zip(_SMALL, _take_parts(small_total.reshape(-1), [p.shape for p in small_parts])))
    pairs0 = pairs0_in + st["pairs0"]
    gots0 = list(got0_in) + list(rout0["dw_in"])
    halves = []
    gots1 = list(rout0["ssd_scan_bwd"]) + list(rout0["ssd_conv_bwd"])
    for (p1, _), gt1, (p0, _), gt0 in zip(pairs1, gots1, pairs0, gots0):
        buf = _chip_sum("grads_chip_sum", p1, gt1, j, 1)
        halves.append(_chip_sum("grads_chip_sum", p0, gt0, j, 0, buf))
    big = dict(zip(big_names, _join_parts_multi("grads_join", halves)))
    big["w_in"] = jnp.swapaxes(big["w_in"], 1, 2)
    for nm in _SMALL_SHARDED:
        cs = w[nm].shape[2]
        small[nm] = lax.dynamic_slice_in_dim(small[nm], j * cs, cs, axis=2)
    grads = {**big, **small}

    delta, new_m, new_v = {}, {}, {}
    for nm, _ in _BIG:
        delta[nm], new_m[nm], new_v[nm] = _adamw("adamw_" + nm, w[nm], grads[nm], a["m_" + nm], a["v_" + nm])
    shapes = [w[nm].shape for nm in _SMALL]
    packs = [_flat_rows([src[nm] for nm in _SMALL], 8)[None] for src in
             (w, grads, {nm: a["m_" + nm] for nm in _SMALL}, {nm: a["v_" + nm] for nm in _SMALL})]
    d_, m_, v_ = _adamw("adamw_small", *packs)
    for dst, buf in ((delta, d_), (new_m, m_), (new_v, v_)):
        dst.update(zip(_SMALL, _take_parts(buf.reshape(-1), shapes)))

    outs = [loss, grad_x]
    for group in (grads, delta, new_m, new_v):
        outs += [group[nm] for nm in _W_NAMES]
    return tuple(outs)


def kernel(x, mem, w_in, b_gate, lru_conv_w, lru_conv_b, lru_w_a, lru_b_a, lru_w_i, lru_b_i, lru_lambda, ssd_conv_w, ssd_conv_b, ssd_dt_bias, ssd_a_log, ssd_d, ssd_norm_w, mem_w_kv, w_br_lru, w_br_ssd, w_br_xa, w_out, ln1_g, ln1_b, ffn_w_in, ffn_w_down, ln2_g, ln2_b, loss_target, m_w_in, m_b_gate, m_lru_conv_w, m_lru_conv_b, m_lru_w_a, m_lru_b_a, m_lru_w_i, m_lru_b_i, m_lru_lambda, m_ssd_conv_w, m_ssd_conv_b, m_ssd_dt_bias, m_ssd_a_log, m_ssd_d, m_ssd_norm_w, m_mem_w_kv, m_w_br_lru, m_w_br_ssd, m_w_br_xa, m_w_out, m_ln1_g, m_ln1_b, m_ffn_w_in, m_ffn_w_down, m_ln2_g, m_ln2_b, v_w_in, v_b_gate, v_lru_conv_w, v_lru_conv_b, v_lru_w_a, v_lru_b_a, v_lru_w_i, v_lru_b_i, v_lru_lambda, v_ssd_conv_w, v_ssd_conv_b, v_ssd_dt_bias, v_ssd_a_log, v_ssd_d, v_ssd_norm_w, v_mem_w_kv, v_w_br_lru, v_w_br_ssd, v_w_br_xa, v_w_out, v_ln1_g, v_ln1_b, v_ffn_w_in, v_ffn_w_down, v_ln2_g, v_ln2_b):
    return _step(dict(locals()))
```
